```python
import jax, jax.numpy as jnp
from jax import lax
import numpy as np

D_MODEL = 2048
BATCH = 8
SEQ = 2048
DEPTH = 4

CHUNK = 64
N_MEM = 256
N_MEM_HEADS = 4
MEM_HEAD_DIM = D_MODEL // N_MEM_HEADS
D_FF = ((8 * D_MODEL // 3 + 255) // 256) * 256
A_WIDTH = D_MODEL // 2
B_WIDTH = D_MODEL // 2
A_HEADS = 8
A_HEAD_DIM = A_WIDTH // A_HEADS
GMLP_BLOCK = 128
CONV_WIDTH = 31
POOL_WINDOWS = (2, 4, 8, 16)
C_GROUPS = len(POOL_WINDOWS)
C_GROUP_DIM = D_MODEL // C_GROUPS
N_EVEN = (DEPTH + 1) // 2
N_ODD = DEPTH // 2
EPS = 1e-6

kernel_name = "hybrid_gmlp_conformer_pool_encoder"


def rms_norm(x, g):
    x32 = x.astype(jnp.float32)
    y = x32 * lax.rsqrt(jnp.mean(x32 * x32, axis=-1, keepdims=True) + EPS)
    return (y * g.astype(jnp.float32)).astype(x.dtype)


def layer_norm(x, g, b):
    x32 = x.astype(jnp.float32)
    mu = jnp.mean(x32, axis=-1, keepdims=True)
    xc = x32 - mu
    var = jnp.mean(xc * xc, axis=-1, keepdims=True)
    y = xc * lax.rsqrt(var + EPS)
    return (y * g.astype(jnp.float32) + b.astype(jnp.float32)).astype(x.dtype)


def swiglu_ffn(h, w_gate, w_up, w_down):
    return (jax.nn.silu(h @ w_gate) * (h @ w_up)) @ w_down


def gmlp_spatial_gate(u, v, w_s, b_s, ln_g, ln_b):
    b, s, _ = u.shape
    v = layer_norm(v, ln_g, ln_b)
    pos = jnp.arange(GMLP_BLOCK)
    mask = (pos[None, :] // CHUNK) <= (pos[:, None] // CHUNK)
    w = jnp.where(mask[None], w_s, jnp.zeros_like(w_s))
    vb = v.reshape(b, s // GMLP_BLOCK, GMLP_BLOCK, A_HEADS, A_HEAD_DIM)
    sp = jnp.einsum('hpq,bnqhc->bnphc', w, vb) + b_s.T[None, None, :, :, None]
    return u * sp.reshape(b, s, A_WIDTH)


def conformer_conv(a, g, conv_w, conv_b, ln_g, ln_b):
    h = a * jax.nn.sigmoid(g)
    h = lax.conv_general_dilated(
        h, conv_w, window_strides=(1,), padding=[(CONV_WIDTH - 1, 0)],
        dimension_numbers=('NWC', 'WIO', 'NWC'), feature_group_count=B_WIDTH) + conv_b
    h = layer_norm(h, ln_g, ln_b)
    return jax.nn.silu(h)


def even_mixer(h, w_in, b_in, w_s, b_s, gln_g, gln_b, conv_w, conv_b, cln_g, cln_b, w_out, b_out):
    z = h @ w_in + b_in
    u_a, v_a, a_b, g_b = jnp.split(z, [A_WIDTH, 2 * A_WIDTH, 2 * A_WIDTH + B_WIDTH], axis=-1)
    y_a = gmlp_spatial_gate(jax.nn.gelu(u_a), jax.nn.gelu(v_a), w_s, b_s, gln_g, gln_b)
    y_b = conformer_conv(a_b, g_b, conv_w, conv_b, cln_g, cln_b)
    return jnp.concatenate([y_a, y_b], axis=-1) @ w_out + b_out


def multiscale_pool_mixer(h, w_c, b_c, scale):
    b, s, _ = h.shape
    h32 = h.astype(jnp.float32)
    cs = jnp.cumsum(h32, axis=1)
    t = jnp.arange(1, s + 1, dtype=jnp.float32)[None, :, None]
    outs = []
    for gi, win in enumerate(POOL_WINDOWS):
        sl = slice(gi * C_GROUP_DIM, (gi + 1) * C_GROUP_DIM)
        c = cs[..., sl]
        prev = jnp.pad(c, ((0, 0), (win, 0), (0, 0)))[:, :s]
        mean = (c - prev) / jnp.minimum(t, win)
        d_g = (mean - h32[..., sl]).astype(h.dtype)
        outs.append(d_g @ w_c[gi] + b_c[gi])
    return jnp.concatenate(outs, axis=-1) * scale


def memory_cross_attention(h, mem_n, wq, wk, wv, wo):
    b, s, _ = h.shape
    m = mem_n.shape[1]
    q = (h @ wq).reshape(b, s, N_MEM_HEADS, MEM_HEAD_DIM)
    k = (mem_n @ wk).reshape(b, m, N_MEM_HEADS, MEM_HEAD_DIM)
    v = (mem_n @ wv).reshape(b, m, N_MEM_HEADS, MEM_HEAD_DIM)
    scores = jnp.einsum('bshd,bmhd->bhsm', q, k).astype(jnp.float32) * (MEM_HEAD_DIM ** -0.5)
    p = jax.nn.softmax(scores, axis=-1).astype(v.dtype)
    o = jnp.einsum('bhsm,bmhd->bshd', p, v).reshape(b, s, D_MODEL)
    return o @ wo


def _fwd_setup_inputs(seed: int = 0) -> dict:
    key = jax.random.key(seed)
    ks = iter(jax.random.split(key, 48))

    def nrm(shape, scale):
        return jax.random.normal(next(ks), shape, jnp.float32) * scale

    def gain(shape):
        return 1.0 + nrm(shape, 0.1)

    D, F = D_MODEL, D_FF
    return {
        "x": nrm((BATCH, SEQ, D), 1.0),
        "mem": nrm((BATCH, N_MEM, D), 1.0),
        "norm_ffn1": gain((DEPTH, D)),
        "ffn1_gate": nrm((DEPTH, D, F), D ** -0.5),
        "ffn1_up": nrm((DEPTH, D, F), D ** -0.5),
        "ffn1_down": nrm((DEPTH, F, D), F ** -0.5),
        "norm_mix": gain((DEPTH, D)),
        "ab_w_in": nrm((N_EVEN, D, 2 * A_WIDTH + 2 * B_WIDTH), D ** -0.5),
        "ab_b_in": nrm((N_EVEN, 2 * A_WIDTH + 2 * B_WIDTH), 0.02),
        "gmlp_w_s": nrm((N_EVEN, A_HEADS, GMLP_BLOCK, GMLP_BLOCK), GMLP_BLOCK ** -0.5),
        "gmlp_b_s": gain((N_EVEN, A_HEADS, GMLP_BLOCK)),
        "gmlp_ln_g": gain((N_EVEN, A_WIDTH)),
        "gmlp_ln_b": nrm((N_EVEN, A_WIDTH), 0.02),
        "conv_w": nrm((N_EVEN, CONV_WIDTH, 1, B_WIDTH), CONV_WIDTH ** -0.5),
        "conv_b": nrm((N_EVEN, B_WIDTH), 0.02),
        "conv_ln_g": gain((N_EVEN, B_WIDTH)),
        "conv_ln_b": nrm((N_EVEN, B_WIDTH), 0.02),
        "ab_w_out": nrm((N_EVEN, A_WIDTH + B_WIDTH, D), (A_WIDTH + B_WIDTH) ** -0.5),
        "ab_b_out": nrm((N_EVEN, D), 0.02),
        "pool_w": nrm((N_ODD, C_GROUPS, C_GROUP_DIM, C_GROUP_DIM), C_GROUP_DIM ** -0.5),
        "pool_b": nrm((N_ODD, C_GROUPS, C_GROUP_DIM), 0.02),
        "pool_scale": 0.5 + nrm((N_ODD, D), 0.05),
        "norm_xq": gain((DEPTH, D)),
        "norm_xkv": gain((DEPTH, D)),
        "xattn_wq": nrm((DEPTH, D, D), D ** -0.5),
        "xattn_wk": nrm((DEPTH, D, D), D ** -0.5),
        "xattn_wv": nrm((DEPTH, D, D), D ** -0.5),
        "xattn_wo": nrm((DEPTH, D, D), D ** -0.5),
        "norm_ffn2": gain((DEPTH, D)),
        "ffn2_gate": nrm((DEPTH, D, F), D ** -0.5),
        "ffn2_up": nrm((DEPTH, D, F), D ** -0.5),
        "ffn2_down": nrm((DEPTH, F, D), F ** -0.5),
        "norm_final": gain((D,)),
    }


def _fwd_reference(x, mem, norm_ffn1, ffn1_gate, ffn1_up, ffn1_down, norm_mix,
              ab_w_in, ab_b_in, gmlp_w_s, gmlp_b_s, gmlp_ln_g, gmlp_ln_b,
              conv_w, conv_b, conv_ln_g, conv_ln_b, ab_w_out, ab_b_out,
              pool_w, pool_b, pool_scale, norm_xq, norm_xkv,
              xattn_wq, xattn_wk, xattn_wv, xattn_wo,
              norm_ffn2, ffn2_gate, ffn2_up, ffn2_down, norm_final):
    for l in range(DEPTH):
        h = rms_norm(x, norm_ffn1[l])
        x = x + 0.5 * swiglu_ffn(h, ffn1_gate[l], ffn1_up[l], ffn1_down[l])
        h = rms_norm(x, norm_mix[l])
        if l % 2 == 0:
            e = l // 2
            x = x + even_mixer(h, ab_w_in[e], ab_b_in[e], gmlp_w_s[e], gmlp_b_s[e],
                               gmlp_ln_g[e], gmlp_ln_b[e], conv_w[e], conv_b[e],
                               conv_ln_g[e], conv_ln_b[e], ab_w_out[e], ab_b_out[e])
        else:
            o = l // 2
            x = x + multiscale_pool_mixer(h, pool_w[o], pool_b[o], pool_scale[o])
        h = rms_norm(x, norm_xq[l])
        m = rms_norm(mem, norm_xkv[l])
        x = x + memory_cross_attention(h, m, xattn_wq[l], xattn_wk[l], xattn_wv[l], xattn_wo[l])
        h = rms_norm(x, norm_ffn2[l])
        x = x + 0.5 * swiglu_ffn(h, ffn2_gate[l], ffn2_up[l], ffn2_down[l])
    return rms_norm(x, norm_final)


import jax as _jax
import jax.numpy as _jnp

TWIN_FORMAT = 'train_step'
FWD_PARAMS = ['x', 'mem', 'norm_ffn1', 'ffn1_gate', 'ffn1_up', 'ffn1_down', 'norm_mix', 'ab_w_in', 'ab_b_in', 'gmlp_w_s', 'gmlp_b_s', 'gmlp_ln_g', 'gmlp_ln_b', 'conv_w', 'conv_b', 'conv_ln_g', 'conv_ln_b', 'ab_w_out', 'ab_b_out', 'pool_w', 'pool_b', 'pool_scale', 'norm_xq', 'norm_xkv', 'xattn_wq', 'xattn_wk', 'xattn_wv', 'xattn_wo', 'norm_ffn2', 'ffn2_gate', 'ffn2_up', 'ffn2_down', 'norm_final']
TWIN_WEIGHTS = ['norm_ffn1', 'ffn1_gate', 'ffn1_up', 'ffn1_down', 'norm_mix', 'ab_w_in', 'ab_b_in', 'gmlp_w_s', 'gmlp_b_s', 'gmlp_ln_g', 'gmlp_ln_b', 'conv_w', 'conv_b', 'conv_ln_g', 'conv_ln_b', 'ab_w_out', 'ab_b_out', 'pool_w', 'pool_b', 'pool_scale', 'norm_xq', 'norm_xkv', 'xattn_wq', 'xattn_wk', 'xattn_wv', 'xattn_wo', 'norm_ffn2', 'ffn2_gate', 'ffn2_up', 'ffn2_down', 'norm_final']
TWIN_DIFF_INPUT = 'x'
TWIN_INPUTS = ['x', 'mem', 'norm_ffn1', 'ffn1_gate', 'ffn1_up', 'ffn1_down', 'norm_mix', 'ab_w_in', 'ab_b_in', 'gmlp_w_s', 'gmlp_b_s', 'gmlp_ln_g', 'gmlp_ln_b', 'conv_w', 'conv_b', 'conv_ln_g', 'conv_ln_b', 'ab_w_out', 'ab_b_out', 'pool_w', 'pool_b', 'pool_scale', 'norm_xq', 'norm_xkv', 'xattn_wq', 'xattn_wk', 'xattn_wv', 'xattn_wo', 'norm_ffn2', 'ffn2_gate', 'ffn2_up', 'ffn2_down', 'norm_final', 'loss_target', 'm_norm_ffn1', 'm_ffn1_gate', 'm_ffn1_up', 'm_ffn1_down', 'm_norm_mix', 'm_ab_w_in', 'm_ab_b_in', 'm_gmlp_w_s', 'm_gmlp_b_s', 'm_gmlp_ln_g', 'm_gmlp_ln_b', 'm_conv_w', 'm_conv_b', 'm_conv_ln_g', 'm_conv_ln_b', 'm_ab_w_out', 'm_ab_b_out', 'm_pool_w', 'm_pool_b', 'm_pool_scale', 'm_norm_xq', 'm_norm_xkv', 'm_xattn_wq', 'm_xattn_wk', 'm_xattn_wv', 'm_xattn_wo', 'm_norm_ffn2', 'm_ffn2_gate', 'm_ffn2_up', 'm_ffn2_down', 'm_norm_final', 'v_norm_ffn1', 'v_ffn1_gate', 'v_ffn1_up', 'v_ffn1_down', 'v_norm_mix', 'v_ab_w_in', 'v_ab_b_in', 'v_gmlp_w_s', 'v_gmlp_b_s', 'v_gmlp_ln_g', 'v_gmlp_ln_b', 'v_conv_w', 'v_conv_b', 'v_conv_ln_g', 'v_conv_ln_b', 'v_ab_w_out', 'v_ab_b_out', 'v_pool_w', 'v_pool_b', 'v_pool_scale', 'v_norm_xq', 'v_norm_xkv', 'v_xattn_wq', 'v_xattn_wk', 'v_xattn_wv', 'v_xattn_wo', 'v_norm_ffn2', 'v_ffn2_gate', 'v_ffn2_up', 'v_ffn2_down', 'v_norm_final']
TWIN_OUTPUTS = ['loss', 'grad_x', 'grad_norm_ffn1', 'grad_ffn1_gate', 'grad_ffn1_up', 'grad_ffn1_down', 'grad_norm_mix', 'grad_ab_w_in', 'grad_ab_b_in', 'grad_gmlp_w_s', 'grad_gmlp_b_s', 'grad_gmlp_ln_g', 'grad_gmlp_ln_b', 'grad_conv_w', 'grad_conv_b', 'grad_conv_ln_g', 'grad_conv_ln_b', 'grad_ab_w_out', 'grad_ab_b_out', 'grad_pool_w', 'grad_pool_b', 'grad_pool_scale', 'grad_norm_xq', 'grad_norm_xkv', 'grad_xattn_wq', 'grad_xattn_wk', 'grad_xattn_wv', 'grad_xattn_wo', 'grad_norm_ffn2', 'grad_ffn2_gate', 'grad_ffn2_up', 'grad_ffn2_down', 'grad_norm_final', 'delta_norm_ffn1', 'delta_ffn1_gate', 'delta_ffn1_up', 'delta_ffn1_down', 'delta_norm_mix', 'delta_ab_w_in', 'delta_ab_b_in', 'delta_gmlp_w_s', 'delta_gmlp_b_s', 'delta_gmlp_ln_g', 'delta_gmlp_ln_b', 'delta_conv_w', 'delta_conv_b', 'delta_conv_ln_g', 'delta_conv_ln_b', 'delta_ab_w_out', 'delta_ab_b_out', 'delta_pool_w', 'delta_pool_b', 'delta_pool_scale', 'delta_norm_xq', 'delta_norm_xkv', 'delta_xattn_wq', 'delta_xattn_wk', 'delta_xattn_wv', 'delta_xattn_wo', 'delta_norm_ffn2', 'delta_ffn2_gate', 'delta_ffn2_up', 'delta_ffn2_down', 'delta_norm_final', 'new_m_norm_ffn1', 'new_m_ffn1_gate', 'new_m_ffn1_up', 'new_m_ffn1_down', 'new_m_norm_mix', 'new_m_ab_w_in', 'new_m_ab_b_in', 'new_m_gmlp_w_s', 'new_m_gmlp_b_s', 'new_m_gmlp_ln_g', 'new_m_gmlp_ln_b', 'new_m_conv_w', 'new_m_conv_b', 'new_m_conv_ln_g', 'new_m_conv_ln_b', 'new_m_ab_w_out', 'new_m_ab_b_out', 'new_m_pool_w', 'new_m_pool_b', 'new_m_pool_scale', 'new_m_norm_xq', 'new_m_norm_xkv', 'new_m_xattn_wq', 'new_m_xattn_wk', 'new_m_xattn_wv', 'new_m_xattn_wo', 'new_m_norm_ffn2', 'new_m_ffn2_gate', 'new_m_ffn2_up', 'new_m_ffn2_down', 'new_m_norm_final', 'new_v_norm_ffn1', 'new_v_ffn1_gate', 'new_v_ffn1_up', 'new_v_ffn1_down', 'new_v_norm_mix', 'new_v_ab_w_in', 'new_v_ab_b_in', 'new_v_gmlp_w_s', 'new_v_gmlp_b_s', 'new_v_gmlp_ln_g', 'new_v_gmlp_ln_b', 'new_v_conv_w', 'new_v_conv_b', 'new_v_conv_ln_g', 'new_v_conv_ln_b', 'new_v_ab_w_out', 'new_v_ab_b_out', 'new_v_pool_w', 'new_v_pool_b', 'new_v_pool_scale', 'new_v_norm_xq', 'new_v_norm_xkv', 'new_v_xattn_wq', 'new_v_xattn_wk', 'new_v_xattn_wv', 'new_v_xattn_wo', 'new_v_norm_ffn2', 'new_v_ffn2_gate', 'new_v_ffn2_up', 'new_v_ffn2_down', 'new_v_norm_final']
TWIN_LEAF_KINDS = {'loss': 'loss', 'grad_x': 'grad_x', 'grad_norm_ffn1': 'grad_w', 'grad_ffn1_gate': 'grad_w', 'grad_ffn1_up': 'grad_w', 'grad_ffn1_down': 'grad_w', 'grad_norm_mix': 'grad_w', 'grad_ab_w_in': 'grad_w', 'grad_ab_b_in': 'grad_w', 'grad_gmlp_w_s': 'grad_w', 'grad_gmlp_b_s': 'grad_w', 'grad_gmlp_ln_g': 'grad_w', 'grad_gmlp_ln_b': 'grad_w', 'grad_conv_w': 'grad_w', 'grad_conv_b': 'grad_w', 'grad_conv_ln_g': 'grad_w', 'grad_conv_ln_b': 'grad_w', 'grad_ab_w_out': 'grad_w', 'grad_ab_b_out': 'grad_w', 'grad_pool_w': 'grad_w', 'grad_pool_b': 'grad_w', 'grad_pool_scale': 'grad_w', 'grad_norm_xq': 'grad_w', 'grad_norm_xkv': 'grad_w', 'grad_xattn_wq': 'grad_w', 'grad_xattn_wk': 'grad_w', 'grad_xattn_wv': 'grad_w', 'grad_xattn_wo': 'grad_w', 'grad_norm_ffn2': 'grad_w', 'grad_ffn2_gate': 'grad_w', 'grad_ffn2_up': 'grad_w', 'grad_ffn2_down': 'grad_w', 'grad_norm_final': 'grad_w', 'delta_norm_ffn1': 'delta_w', 'delta_ffn1_gate': 'delta_w', 'delta_ffn1_up': 'delta_w', 'delta_ffn1_down': 'delta_w', 'delta_norm_mix': 'delta_w', 'delta_ab_w_in': 'delta_w', 'delta_ab_b_in': 'delta_w', 'delta_gmlp_w_s': 'delta_w', 'delta_gmlp_b_s': 'delta_w', 'delta_gmlp_ln_g': 'delta_w', 'delta_gmlp_ln_b': 'delta_w', 'delta_conv_w': 'delta_w', 'delta_conv_b': 'delta_w', 'delta_conv_ln_g': 'delta_w', 'delta_conv_ln_b': 'delta_w', 'delta_ab_w_out': 'delta_w', 'delta_ab_b_out': 'delta_w', 'delta_pool_w': 'delta_w', 'delta_pool_b': 'delta_w', 'delta_pool_scale': 'delta_w', 'delta_norm_xq': 'delta_w', 'delta_norm_xkv': 'delta_w', 'delta_xattn_wq': 'delta_w', 'delta_xattn_wk': 'delta_w', 'delta_xattn_wv': 'delta_w', 'delta_xattn_wo': 'delta_w', 'delta_norm_ffn2': 'delta_w', 'delta_ffn2_gate': 'delta_w', 'delta_ffn2_up': 'delta_w', 'delta_ffn2_down': 'delta_w', 'delta_norm_final': 'delta_w', 'new_m_norm_ffn1': 'new_m', 'new_m_ffn1_gate': 'new_m', 'new_m_ffn1_up': 'new_m', 'new_m_ffn1_down': 'new_m', 'new_m_norm_mix': 'new_m', 'new_m_ab_w_in': 'new_m', 'new_m_ab_b_in': 'new_m', 'new_m_gmlp_w_s': 'new_m', 'new_m_gmlp_b_s': 'new_m', 'new_m_gmlp_ln_g': 'new_m', 'new_m_gmlp_ln_b': 'new_m', 'new_m_conv_w': 'new_m', 'new_m_conv_b': 'new_m', 'new_m_conv_ln_g': 'new_m', 'new_m_conv_ln_b': 'new_m', 'new_m_ab_w_out': 'new_m', 'new_m_ab_b_out': 'new_m', 'new_m_pool_w': 'new_m', 'new_m_pool_b': 'new_m', 'new_m_pool_scale': 'new_m', 'new_m_norm_xq': 'new_m', 'new_m_norm_xkv': 'new_m', 'new_m_xattn_wq': 'new_m', 'new_m_xattn_wk': 'new_m', 'new_m_xattn_wv': 'new_m', 'new_m_xattn_wo': 'new_m', 'new_m_norm_ffn2': 'new_m', 'new_m_ffn2_gate': 'new_m', 'new_m_ffn2_up': 'new_m', 'new_m_ffn2_down': 'new_m', 'new_m_norm_final': 'new_m', 'new_v_norm_ffn1': 'new_v', 'new_v_ffn1_gate': 'new_v', 'new_v_ffn1_up': 'new_v', 'new_v_ffn1_down': 'new_v', 'new_v_norm_mix': 'new_v', 'new_v_ab_w_in': 'new_v', 'new_v_ab_b_in': 'new_v', 'new_v_gmlp_w_s': 'new_v', 'new_v_gmlp_b_s': 'new_v', 'new_v_gmlp_ln_g': 'new_v', 'new_v_gmlp_ln_b': 'new_v', 'new_v_conv_w': 'new_v', 'new_v_conv_b': 'new_v', 'new_v_conv_ln_g': 'new_v', 'new_v_conv_ln_b': 'new_v', 'new_v_ab_w_out': 'new_v', 'new_v_ab_b_out': 'new_v', 'new_v_pool_w': 'new_v', 'new_v_pool_b': 'new_v', 'new_v_pool_scale': 'new_v', 'new_v_norm_xq': 'new_v', 'new_v_norm_xkv': 'new_v', 'new_v_xattn_wq': 'new_v', 'new_v_xattn_wk': 'new_v', 'new_v_xattn_wv': 'new_v', 'new_v_xattn_wo': 'new_v', 'new_v_norm_ffn2': 'new_v', 'new_v_ffn2_gate': 'new_v', 'new_v_ffn2_up': 'new_v', 'new_v_ffn2_down': 'new_v', 'new_v_norm_final': 'new_v'}


def _forward(args):
    return _fwd_reference(*[args[k] for k in FWD_PARAMS])


def _output_shape():
    out = _jax.eval_shape(lambda: _forward(_fwd_setup_inputs(0)))
    return out.shape, out.dtype

N_MICROBATCH = 1
ADAM_LR = 0.001
ADAM_B1 = 0.9
ADAM_B2 = 0.999
ADAM_EPS = 1e-08
ADAM_WD = 0.01
ADAM_STEP = 10
PER_EXAMPLE_BATCH_AXIS = {'x': 0, 'mem': 0, 'loss_target': 0}
SHARED_INPUTS = []
_WEIGHT_DTYPES = {'norm_ffn1': _jnp.float32, 'ffn1_gate': _jnp.float32, 'ffn1_up': _jnp.float32, 'ffn1_down': _jnp.float32, 'norm_mix': _jnp.float32, 'ab_w_in': _jnp.float32, 'ab_b_in': _jnp.float32, 'gmlp_w_s': _jnp.float32, 'gmlp_b_s': _jnp.float32, 'gmlp_ln_g': _jnp.float32, 'gmlp_ln_b': _jnp.float32, 'conv_w': _jnp.float32, 'conv_b': _jnp.float32, 'conv_ln_g': _jnp.float32, 'conv_ln_b': _jnp.float32, 'ab_w_out': _jnp.float32, 'ab_b_out': _jnp.float32, 'pool_w': _jnp.float32, 'pool_b': _jnp.float32, 'pool_scale': _jnp.float32, 'norm_xq': _jnp.float32, 'norm_xkv': _jnp.float32, 'xattn_wq': _jnp.float32, 'xattn_wk': _jnp.float32, 'xattn_wv': _jnp.float32, 'xattn_wo': _jnp.float32, 'norm_ffn2': _jnp.float32, 'ffn2_gate': _jnp.float32, 'ffn2_up': _jnp.float32, 'ffn2_down': _jnp.float32, 'norm_final': _jnp.float32}
MOMENT_SCALE = {'norm_ffn1': 2.498897e-02, 'ffn1_gate': 1.078388e-02, 'ffn1_up': 1.056957e-02, 'ffn1_down': 1.753577e-02, 'norm_mix': 3.669638e-02, 'ab_w_in': 3.397120e-02, 'ab_b_in': 6.059710e-02, 'gmlp_w_s': 2.900534e-02, 'gmlp_b_s': 3.468321e-02, 'gmlp_ln_g': 2.960988e-02, 'gmlp_ln_b': 2.980916e-02, 'conv_w': 3.379913e-02, 'conv_b': 1.421422e-01, 'conv_ln_g': 6.431698e-02, 'conv_ln_b': 8.469832e-02, 'ab_w_out': 5.254149e-02, 'ab_b_out': 1.714883e-01, 'pool_w': 2.204736e-02, 'pool_b': 8.651884e-02, 'pool_scale': 1.725518e-01, 'norm_xq': 5.050534e-03, 'norm_xkv': 8.571583e-03, 'xattn_wq': 5.161606e-03, 'xattn_wk': 5.165740e-03, 'xattn_wv': 6.829078e-03, 'xattn_wo': 6.908840e-03, 'norm_ffn2': 2.000563e-02, 'ffn2_gate': 8.692576e-03, 'ffn2_up': 8.640303e-03, 'ffn2_down': 1.431862e-02, 'norm_final': 8.056664e+00}


def _to_microbatches(a, axis):
    t = _jnp.moveaxis(a, axis, 0)
    t = t.reshape((N_MICROBATCH, t.shape[0] // N_MICROBATCH) + t.shape[1:])
    return _jnp.moveaxis(t, 1, axis + 1)


def setup_inputs(seed: int = 0) -> dict:
    inp = _fwd_setup_inputs(seed)
    key = _jax.random.fold_in(_jax.random.key(seed), 7919)
    shape, _ = _output_shape()
    out = dict(inp)
    out["loss_target"] = _jax.random.normal(_jax.random.fold_in(key, 0), shape, _jnp.float32)
    for i, name in enumerate(TWIN_WEIGHTS):
        w = inp[name].astype(_jnp.float32)
        if MOMENT_SCALE is None:
            s = _jnp.sqrt(_jnp.mean(_jnp.square(w)) + 1e-30)
        else:
            s = MOMENT_SCALE[name]
        km, kv = _jax.random.split(_jax.random.fold_in(key, i + 1))
        out[name] = w
        out["m_" + name] = s * _jax.random.normal(km, w.shape, _jnp.float32)
        out["v_" + name] = (s * s) * _jax.random.uniform(kv, w.shape, _jnp.float32, 0.5, 1.5)
    if N_MICROBATCH > 1:
        for name, axis in PER_EXAMPLE_BATCH_AXIS.items():
            out[name] = _to_microbatches(out[name], axis)
    return {'x': out['x'], 'mem': out['mem'], 'norm_ffn1': out['norm_ffn1'], 'ffn1_gate': out['ffn1_gate'], 'ffn1_up': out['ffn1_up'], 'ffn1_down': out['ffn1_down'], 'norm_mix': out['norm_mix'], 'ab_w_in': out['ab_w_in'], 'ab_b_in': out['ab_b_in'], 'gmlp_w_s': out['gmlp_w_s'], 'gmlp_b_s': out['gmlp_b_s'], 'gmlp_ln_g': out['gmlp_ln_g'], 'gmlp_ln_b': out['gmlp_ln_b'], 'conv_w': out['conv_w'], 'conv_b': out['conv_b'], 'conv_ln_g': out['conv_ln_g'], 'conv_ln_b': out['conv_ln_b'], 'ab_w_out': out['ab_w_out'], 'ab_b_out': out['ab_b_out'], 'pool_w': out['pool_w'], 'pool_b': out['pool_b'], 'pool_scale': out['pool_scale'], 'norm_xq': out['norm_xq'], 'norm_xkv': out['norm_xkv'], 'xattn_wq': out['xattn_wq'], 'xattn_wk': out['xattn_wk'], 'xattn_wv': out['xattn_wv'], 'xattn_wo': out['xattn_wo'], 'norm_ffn2': out['norm_ffn2'], 'ffn2_gate': out['ffn2_gate'], 'ffn2_up': out['ffn2_up'], 'ffn2_down': out['ffn2_down'], 'norm_final': out['norm_final'], 'loss_target': out['loss_target'], 'm_norm_ffn1': out['m_norm_ffn1'], 'm_ffn1_gate': out['m_ffn1_gate'], 'm_ffn1_up': out['m_ffn1_up'], 'm_ffn1_down': out['m_ffn1_down'], 'm_norm_mix': out['m_norm_mix'], 'm_ab_w_in': out['m_ab_w_in'], 'm_ab_b_in': out['m_ab_b_in'], 'm_gmlp_w_s': out['m_gmlp_w_s'], 'm_gmlp_b_s': out['m_gmlp_b_s'], 'm_gmlp_ln_g': out['m_gmlp_ln_g'], 'm_gmlp_ln_b': out['m_gmlp_ln_b'], 'm_conv_w': out['m_conv_w'], 'm_conv_b': out['m_conv_b'], 'm_conv_ln_g': out['m_conv_ln_g'], 'm_conv_ln_b': out['m_conv_ln_b'], 'm_ab_w_out': out['m_ab_w_out'], 'm_ab_b_out': out['m_ab_b_out'], 'm_pool_w': out['m_pool_w'], 'm_pool_b': out['m_pool_b'], 'm_pool_scale': out['m_pool_scale'], 'm_norm_xq': out['m_norm_xq'], 'm_norm_xkv': out['m_norm_xkv'], 'm_xattn_wq': out['m_xattn_wq'], 'm_xattn_wk': out['m_xattn_wk'], 'm_xattn_wv': out['m_xattn_wv'], 'm_xattn_wo': out['m_xattn_wo'], 'm_norm_ffn2': out['m_norm_ffn2'], 'm_ffn2_gate': out['m_ffn2_gate'], 'm_ffn2_up': out['m_ffn2_up'], 'm_ffn2_down': out['m_ffn2_down'], 'm_norm_final': out['m_norm_final'], 'v_norm_ffn1': out['v_norm_ffn1'], 'v_ffn1_gate': out['v_ffn1_gate'], 'v_ffn1_up': out['v_ffn1_up'], 'v_ffn1_down': out['v_ffn1_down'], 'v_norm_mix': out['v_norm_mix'], 'v_ab_w_in': out['v_ab_w_in'], 'v_ab_b_in': out['v_ab_b_in'], 'v_gmlp_w_s': out['v_gmlp_w_s'], 'v_gmlp_b_s': out['v_gmlp_b_s'], 'v_gmlp_ln_g': out['v_gmlp_ln_g'], 'v_gmlp_ln_b': out['v_gmlp_ln_b'], 'v_conv_w': out['v_conv_w'], 'v_conv_b': out['v_conv_b'], 'v_conv_ln_g': out['v_conv_ln_g'], 'v_conv_ln_b': out['v_conv_ln_b'], 'v_ab_w_out': out['v_ab_w_out'], 'v_ab_b_out': out['v_ab_b_out'], 'v_pool_w': out['v_pool_w'], 'v_pool_b': out['v_pool_b'], 'v_pool_scale': out['v_pool_scale'], 'v_norm_xq': out['v_norm_xq'], 'v_norm_xkv': out['v_norm_xkv'], 'v_xattn_wq': out['v_xattn_wq'], 'v_xattn_wk': out['v_xattn_wk'], 'v_xattn_wv': out['v_xattn_wv'], 'v_xattn_wo': out['v_xattn_wo'], 'v_norm_ffn2': out['v_norm_ffn2'], 'v_ffn2_gate': out['v_ffn2_gate'], 'v_ffn2_up': out['v_ffn2_up'], 'v_ffn2_down': out['v_ffn2_down'], 'v_norm_final': out['v_norm_final']}


def _loss(weights, diff, rest, loss_target):
    with _jax.named_scope("forward"):
        args = {**rest, TWIN_DIFF_INPUT: diff, **{k: w.astype(_WEIGHT_DTYPES[k]) for k, w in weights.items()}}
        y = _forward(args)
    with _jax.named_scope("loss_head"):
        err = _jnp.square(y.astype(_jnp.float32) - loss_target)
        return 0.5 * _jnp.sum(_jnp.mean(err, axis=-1)) if err.ndim else 0.5 * err


def _adamw(w, g, m, v):
    m = ADAM_B1 * m + (1.0 - ADAM_B1) * g
    v = ADAM_B2 * v + (1.0 - ADAM_B2) * _jnp.square(g)
    m_hat = m / (1.0 - ADAM_B1 ** ADAM_STEP)
    v_hat = v / (1.0 - ADAM_B2 ** ADAM_STEP)
    delta = -ADAM_LR * (m_hat / (_jnp.sqrt(v_hat) + ADAM_EPS) + ADAM_WD * w)
    return delta, m, v


def reference(x, mem, norm_ffn1, ffn1_gate, ffn1_up, ffn1_down, norm_mix, ab_w_in, ab_b_in, gmlp_w_s, gmlp_b_s, gmlp_ln_g, gmlp_ln_b, conv_w, conv_b, conv_ln_g, conv_ln_b, ab_w_out, ab_b_out, pool_w, pool_b, pool_scale, norm_xq, norm_xkv, xattn_wq, xattn_wk, xattn_wv, xattn_wo, norm_ffn2, ffn2_gate, ffn2_up, ffn2_down, norm_final, loss_target, m_norm_ffn1, m_ffn1_gate, m_ffn1_up, m_ffn1_down, m_norm_mix, m_ab_w_in, m_ab_b_in, m_gmlp_w_s, m_gmlp_b_s, m_gmlp_ln_g, m_gmlp_ln_b, m_conv_w, m_conv_b, m_conv_ln_g, m_conv_ln_b, m_ab_w_out, m_ab_b_out, m_pool_w, m_pool_b, m_pool_scale, m_norm_xq, m_norm_xkv, m_xattn_wq, m_xattn_wk, m_xattn_wv, m_xattn_wo, m_norm_ffn2, m_ffn2_gate, m_ffn2_up, m_ffn2_down, m_norm_final, v_norm_ffn1, v_ffn1_gate, v_ffn1_up, v_ffn1_down, v_norm_mix, v_ab_w_in, v_ab_b_in, v_gmlp_w_s, v_gmlp_b_s, v_gmlp_ln_g, v_gmlp_ln_b, v_conv_w, v_conv_b, v_conv_ln_g, v_conv_ln_b, v_ab_w_out, v_ab_b_out, v_pool_w, v_pool_b, v_pool_scale, v_norm_xq, v_norm_xkv, v_xattn_wq, v_xattn_wk, v_xattn_wv, v_xattn_wo, v_norm_ffn2, v_ffn2_gate, v_ffn2_up, v_ffn2_down, v_norm_final):
    given = dict(x=x, mem=mem, norm_ffn1=norm_ffn1, ffn1_gate=ffn1_gate, ffn1_up=ffn1_up, ffn1_down=ffn1_down, norm_mix=norm_mix, ab_w_in=ab_w_in, ab_b_in=ab_b_in, gmlp_w_s=gmlp_w_s, gmlp_b_s=gmlp_b_s, gmlp_ln_g=gmlp_ln_g, gmlp_ln_b=gmlp_ln_b, conv_w=conv_w, conv_b=conv_b, conv_ln_g=conv_ln_g, conv_ln_b=conv_ln_b, ab_w_out=ab_w_out, ab_b_out=ab_b_out, pool_w=pool_w, pool_b=pool_b, pool_scale=pool_scale, norm_xq=norm_xq, norm_xkv=norm_xkv, xattn_wq=xattn_wq, xattn_wk=xattn_wk, xattn_wv=xattn_wv, xattn_wo=xattn_wo, norm_ffn2=norm_ffn2, ffn2_gate=ffn2_gate, ffn2_up=ffn2_up, ffn2_down=ffn2_down, norm_final=norm_final, loss_target=loss_target, m_norm_ffn1=m_norm_ffn1, m_ffn1_gate=m_ffn1_gate, m_ffn1_up=m_ffn1_up, m_ffn1_down=m_ffn1_down, m_norm_mix=m_norm_mix, m_ab_w_in=m_ab_w_in, m_ab_b_in=m_ab_b_in, m_gmlp_w_s=m_gmlp_w_s, m_gmlp_b_s=m_gmlp_b_s, m_gmlp_ln_g=m_gmlp_ln_g, m_gmlp_ln_b=m_gmlp_ln_b, m_conv_w=m_conv_w, m_conv_b=m_conv_b, m_conv_ln_g=m_conv_ln_g, m_conv_ln_b=m_conv_ln_b, m_ab_w_out=m_ab_w_out, m_ab_b_out=m_ab_b_out, m_pool_w=m_pool_w, m_pool_b=m_pool_b, m_pool_scale=m_pool_scale, m_norm_xq=m_norm_xq, m_norm_xkv=m_norm_xkv, m_xattn_wq=m_xattn_wq, m_xattn_wk=m_xattn_wk, m_xattn_wv=m_xattn_wv, m_xattn_wo=m_xattn_wo, m_norm_ffn2=m_norm_ffn2, m_ffn2_gate=m_ffn2_gate, m_ffn2_up=m_ffn2_up, m_ffn2_down=m_ffn2_down, m_norm_final=m_norm_final, v_norm_ffn1=v_norm_ffn1, v_ffn1_gate=v_ffn1_gate, v_ffn1_up=v_ffn1_up, v_ffn1_down=v_ffn1_down, v_norm_mix=v_norm_mix, v_ab_w_in=v_ab_w_in, v_ab_b_in=v_ab_b_in, v_gmlp_w_s=v_gmlp_w_s, v_gmlp_b_s=v_gmlp_b_s, v_gmlp_ln_g=v_gmlp_ln_g, v_gmlp_ln_b=v_gmlp_ln_b, v_conv_w=v_conv_w, v_conv_b=v_conv_b, v_conv_ln_g=v_conv_ln_g, v_conv_ln_b=v_conv_ln_b, v_ab_w_out=v_ab_w_out, v_ab_b_out=v_ab_b_out, v_pool_w=v_pool_w, v_pool_b=v_pool_b, v_pool_scale=v_pool_scale, v_norm_xq=v_norm_xq, v_norm_xkv=v_norm_xkv, v_xattn_wq=v_xattn_wq, v_xattn_wk=v_xattn_wk, v_xattn_wv=v_xattn_wv, v_xattn_wo=v_xattn_wo, v_norm_ffn2=v_norm_ffn2, v_ffn2_gate=v_ffn2_gate, v_ffn2_up=v_ffn2_up, v_ffn2_down=v_ffn2_down, v_norm_final=v_norm_final)
    weights = {n: given[n] for n in TWIN_WEIGHTS}
    shared = {n: given[n] for n in SHARED_INPUTS}
    per_example = {n: given[n] for n in ['x', 'mem']}
    grad_fn = _jax.value_and_grad(_loss, argnums=(0, 1))

    def one_microbatch(ex, loss_target):
        ex = dict(ex)
        diff = ex.pop(TWIN_DIFF_INPUT)
        return grad_fn(weights, diff, {**shared, **ex}, loss_target)

    if N_MICROBATCH == 1:
        loss, (grad_w, grad_x) = one_microbatch(per_example, given["loss_target"])
    else:
        def body(carry, xs):
            loss_sum, grad_sum = carry
            l_k, (gw_k, gx_k) = one_microbatch(xs[0], xs[1])
            with _jax.named_scope("update"):
                return (loss_sum + l_k, _jax.tree.map(_jnp.add, grad_sum, gw_k)), gx_k

        init = (_jnp.zeros((), _jnp.float32), _jax.tree.map(_jnp.zeros_like, weights))
        (loss, grad_w), grad_x = _jax.lax.scan(body, init, (per_example, given["loss_target"]))
    with _jax.named_scope("update"):
        delta_w, new_m, new_v = {}, {}, {}
        for n in TWIN_WEIGHTS:
            delta_w[n], new_m[n], new_v[n] = _adamw(weights[n], grad_w[n], given["m_" + n], given["v_" + n])
    return (loss, grad_x, *[grad_w[n] for n in TWIN_WEIGHTS], *[delta_w[n] for n in TWIN_WEIGHTS],
            *[new_m[n] for n in TWIN_WEIGHTS], *[new_v[n] for n in TWIN_WEIGHTS])
```

```python
import jax
import jax.numpy as jnp
from jax import lax
from jax.experimental import pallas as pl
from jax.experimental.pallas import tpu as pltpu

F32, BF16 = jnp.float32, jnp.bfloat16
EPS = 1e-6
N_MEM_HEADS = 4
A_HEADS = 8
GMLP_BLOCK = 128
CHUNK = 64
POOL_WINDOWS = (2, 4, 8, 16)
N_CHIPS = 4
N_DEV = 8
HALO = 32
LANES = 128
TM, TN, TK = 512, 1024, 512
ROW_TILE = 256
PACK_ROWS = 512
VMEM_LIMIT = 48 * 1024 * 1024
ADAM_LR, ADAM_B1, ADAM_B2, ADAM_EPS, ADAM_WD, ADAM_STEP = 0.001, 0.9, 0.999, 1e-08, 0.01, 10
MESH = pl.DeviceIdType.MESH
HBM = pl.BlockSpec(memory_space=pltpu.HBM)

WEIGHTS = ['norm_ffn1', 'ffn1_gate', 'ffn1_up', 'ffn1_down', 'norm_mix', 'ab_w_in', 'ab_b_in', 'gmlp_w_s',
           'gmlp_b_s', 'gmlp_ln_g', 'gmlp_ln_b', 'conv_w', 'conv_b', 'conv_ln_g', 'conv_ln_b', 'ab_w_out',
           'ab_b_out', 'pool_w', 'pool_b', 'pool_scale', 'norm_xq', 'norm_xkv', 'xattn_wq', 'xattn_wk',
           'xattn_wv', 'xattn_wo', 'norm_ffn2', 'ffn2_gate', 'ffn2_up', 'ffn2_down', 'norm_final']
BIG = ['ffn1_gate', 'ffn1_up', 'ffn1_down', 'ab_w_in', 'ab_w_out', 'pool_w', 'xattn_wq', 'xattn_wk', 'xattn_wv',
       'xattn_wo', 'ffn2_gate', 'ffn2_up', 'ffn2_down']
SMALL = [n for n in WEIGHTS if n not in BIG]
SMALL_SHARDED = ['conv_w', 'pool_b', 'pool_scale']

NN = (((1,), (0,)), ((), ()))
NT = (((1,), (1,)), ((), ()))
TN_ = (((0,), (0,)), ((), ()))
_DIMS = {'nn': NN, 'nt': NT, 'tn': TN_}


def _pick(n, pref, unit=LANES):
    if n <= pref:
        return n
    t = (pref // unit) * unit
    while t >= unit:
        if n % t == 0:
            return t
        t -= unit
    return n


def _sds(shape, dtype):
    return jax.ShapeDtypeStruct(tuple(shape), dtype)


def _mm(name, grid, ins, pairs, outs, acc_shapes, epilogue, extras=(), aliases=None):
    n_in, n_out = len(ins), len(outs)
    nk = grid[2]

    def body(*refs):
        in_refs, out_refs, acc_refs = refs[:n_in], refs[n_in:n_in + n_out], refs[n_in + n_out:]
        k = pl.program_id(2)

        @pl.when(k == 0)
        def _():
            for acc in acc_refs:
                acc[...] = jnp.zeros_like(acc)

        for ai, bi, mode, ci in pairs:
            a = in_refs[ai][...].astype(BF16)
            b = in_refs[bi][...].astype(BF16)
            acc_refs[ci][...] += lax.dot_general(a, b, _DIMS[mode], preferred_element_type=F32)

        @pl.when(k == nk - 1)
        def _():
            res = epilogue([acc[...] for acc in acc_refs], [in_refs[e][...] for e in extras])
            for o, r in zip(out_refs, res):
                o[...] = r.astype(o.dtype)

    res = pl.pallas_call(
        body, name=name, grid=grid,
        in_specs=[s for _, s in ins], out_specs=[s for _, s in outs], out_shape=[s for s, _ in outs],
        scratch_shapes=[pltpu.VMEM(s, F32) for s in acc_shapes],
        input_output_aliases=aliases or {},
        compiler_params=pltpu.CompilerParams(dimension_semantics=("parallel", "parallel", "arbitrary"),
                                             vmem_limit_bytes=VMEM_LIMIT),
    )(*[a for a, _ in ins])
    return res


def _bs(shape, fn):
    return pl.BlockSpec(shape, fn)


def _rowwise(name, fn, ins, row_outs, acc_outs, tile):
    T = next(a.shape[0] for k, a in ins if k == 'row')
    n = T // tile
    per = tile // HALO if tile % HALO == 0 else 1
    last = T // HALO - 1
    in_specs = []
    for kind, a in ins:
        if kind == 'row':
            in_specs.append(pl.BlockSpec((tile, a.shape[1]), lambda i: (i, 0)))
        elif kind == 'prev':
            in_specs.append(pl.BlockSpec((HALO, a.shape[1]), lambda i: (jnp.maximum(i * per - 1, 0), 0)))
        elif kind == 'next':
            in_specs.append(pl.BlockSpec((HALO, a.shape[1]), lambda i: (jnp.minimum((i + 1) * per, last), 0)))
        else:
            in_specs.append(pl.BlockSpec(a.shape, lambda i, nd=a.ndim: (0,) * nd))
    n_in, n_row = len(ins), len(row_outs)
    out_shape = [_sds((T, c), dt) for c, dt in row_outs] + [_sds(s, F32) for s in acc_outs]
    out_specs = [pl.BlockSpec((tile, c), lambda i: (i, 0)) for c, _ in row_outs]
    out_specs += [pl.BlockSpec(s, lambda i, nd=len(s): (0,) * nd) for s in acc_outs]
    kinds = [k for k, _ in ins]

    def body(*refs):
        i = pl.program_id(0)
        vals = [r if k == 'cref' else r[...] for k, r in zip(kinds, refs[:n_in])]
        ro, ao = fn(i, *vals)
        for r, v in zip(refs[n_in:n_in + n_row], ro):
            r[...] = v.astype(r.dtype)
        for r, v in zip(refs[n_in + n_row:], ao):
            @pl.when(i == 0)
            def _(r=r, v=v):
                r[...] = v

            @pl.when(i > 0)
            def _(r=r, v=v):
                r[...] += v

    return pl.pallas_call(
        body, name=name, grid=(n,), in_specs=in_specs, out_specs=out_specs, out_shape=out_shape,
        compiler_params=pltpu.CompilerParams(dimension_semantics=("arbitrary",), vmem_limit_bytes=VMEM_LIMIT),
    )(*[a for _, a in ins])


def _rms(x, g):
    return x * lax.rsqrt(jnp.mean(x * x, axis=-1, keepdims=True) + EPS) * g


def _ln(x, g, b):
    mu = jnp.mean(x, axis=-1, keepdims=True)
    xc = x - mu
    var = jnp.mean(xc * xc, axis=-1, keepdims=True)
    return xc * lax.rsqrt(var + EPS) * g + b


def _gelu(x):
    return 0.5 * x * (1.0 + jnp.tanh(0.7978845608028654 * (x + 0.044715 * (x * x * x))))


def _silu(x):
    return x * jax.nn.sigmoid(x)


def _glu(a, g):
    return a * jax.nn.sigmoid(g)


def _dot(a, b, mode):
    return lax.dot_general(a.astype(BF16), b.astype(BF16), _DIMS[mode], preferred_element_type=F32)


def _row_ids(i, tile, rows):
    return i * tile + lax.broadcasted_iota(jnp.int32, (rows, 1), 0)


def _mesh_pos():
    return lax.axis_index("x"), lax.axis_index("y"), lax.axis_index("c")


def _other_chips(x, y):
    return [(1 - x, y), (x, 1 - y), (1 - x, 1 - y)]


def _allgather(name, w):
    L = w.shape[0]
    lh = L // 2

    def body(w_ref, out_ref, send_sems, recv_sems, local_sem):
        x, y, c = _mesh_pos()
        me = 2 * x + y
        sibling = (x, y, 1 - c)
        mine, other = pl.ds(c * lh, lh), pl.ds((1 - c) * lh, lh)
        local = pltpu.make_async_copy(w_ref, out_ref.at[me], local_sem)
        local.start()
        chips = _other_chips(x, y)

        def copy(k, src, dst, to):
            return pltpu.make_async_remote_copy(src_ref=src, dst_ref=dst, send_sem=send_sems.at[k],
                                                recv_sem=recv_sems.at[k], device_id=to, device_id_type=MESH)

        first = [copy(k, w_ref.at[mine], out_ref.at[me, mine], (cx, cy, c)) for k, (cx, cy) in enumerate(chips)]
        for cp in first:
            cp.start()
        passed = []
        for k, (cx, cy) in enumerate(chips):
            landed = out_ref.at[2 * cx + cy, mine]
            copy(k, landed, landed, sibling).wait_recv()
            fw = copy(3 + k, landed, landed, sibling)
            fw.start()
            passed.append(fw)
        for k, (cx, cy) in enumerate(chips):
            landed = out_ref.at[2 * cx + cy, other]
            copy(3 + k, landed, landed, sibling).wait_recv()
        for cp in first + passed:
            cp.wait_send()
        local.wait()

    return pl.pallas_call(
        body, name=name, out_shape=_sds((N_CHIPS,) + w.shape, w.dtype), in_specs=[HBM], out_specs=HBM,
        scratch_shapes=[pltpu.SemaphoreType.DMA((6,)), pltpu.SemaphoreType.DMA((6,)), pltpu.SemaphoreType.DMA],
    )(w)


def _gather_all(name, buf):
    def body(b_ref, out_ref, send_sems, recv_sems, local_sem):
        x, y, c = _mesh_pos()
        me = 4 * x + 2 * y + c
        local = pltpu.make_async_copy(b_ref, out_ref.at[me], local_sem)
        local.start()
        peers = []
        for k in range(1, N_DEV):
            px = 1 - x if k & 4 else x
            py = 1 - y if k & 2 else y
            pc = 1 - c if k & 1 else c
            peers.append((px, py, pc))
        sends = []
        for k, peer in enumerate(peers):
            cp = pltpu.make_async_remote_copy(src_ref=b_ref, dst_ref=out_ref.at[me], send_sem=send_sems.at[k],
                                              recv_sem=recv_sems.at[k], device_id=peer, device_id_type=MESH)
            cp.start()
            sends.append(cp)
        for k, (px, py, pc) in enumerate(peers):
            slot = out_ref.at[4 * px + 2 * py + pc]
            pltpu.make_async_remote_copy(src_ref=slot, dst_ref=slot, send_sem=send_sems.at[k],
                                         recv_sem=recv_sems.at[k], device_id=(px, py, pc),
                                         device_id_type=MESH).wait_recv()
        for cp in sends:
            cp.wait_send()
        local.wait()

    return pl.pallas_call(
        body, name=name, out_shape=_sds((N_DEV,) + buf.shape, buf.dtype), in_specs=[HBM], out_specs=HBM,
        scratch_shapes=[pltpu.SemaphoreType.DMA((N_DEV - 1,)), pltpu.SemaphoreType.DMA((N_DEV - 1,)),
                        pltpu.SemaphoreType.DMA],
    )(buf)


def _swap_halves(name, g):
    lh = g.shape[1] // 2

    def body(g_ref, out_ref, send_sem, recv_sem):
        x, y, c = _mesh_pos()
        cp = pltpu.make_async_remote_copy(src_ref=g_ref.at[:, pl.ds((1 - c) * lh, lh)], dst_ref=out_ref,
                                          send_sem=send_sem, recv_sem=recv_sem, device_id=(x, y, 1 - c),
                                          device_id_type=MESH)
        cp.start()
        cp.wait()

    return pl.pallas_call(
        body, name=name, out_shape=_sds((g.shape[0], lh) + g.shape[2:], g.dtype), in_specs=[HBM], out_specs=HBM,
        scratch_shapes=[pltpu.SemaphoreType.DMA, pltpu.SemaphoreType.DMA],
    )(g)


def _chip_exchange(name, p):
    def body(p_ref, out_ref, send_sems, recv_sems):
        x, y, c = _mesh_pos()
        cps = []
        for k, (cx, cy) in enumerate(_other_chips(x, y)):
            cp = pltpu.make_async_remote_copy(src_ref=p_ref.at[2 * cx + cy], dst_ref=out_ref.at[k],
                                              send_sem=send_sems.at[k], recv_sem=recv_sems.at[k],
                                              device_id=(cx, cy, c), device_id_type=MESH)
            cp.start()
            cps.append(cp)
        for cp in cps:
            cp.wait_recv()
        for cp in cps:
            cp.wait_send()

    return pl.pallas_call(
        body, name=name, out_shape=_sds((3,) + p.shape[1:], p.dtype), in_specs=[HBM], out_specs=HBM,
        scratch_shapes=[pltpu.SemaphoreType.DMA((3,)), pltpu.SemaphoreType.DMA((3,))],
    )(p)


def _join_halves(name, r):
    lh = r.shape[0]

    def body(r_ref, out_ref, send_sem, recv_sem, local_sem):
        x, y, c = _mesh_pos()
        mine = pl.ds(c * lh, lh)
        local = pltpu.make_async_copy(r_ref, out_ref.at[mine], local_sem)
        local.start()
        cp = pltpu.make_async_remote_copy(src_ref=r_ref, dst_ref=out_ref.at[mine], send_sem=send_sem,
                                          recv_sem=recv_sem, device_id=(x, y, 1 - c), device_id_type=MESH)
        cp.start()
        theirs = out_ref.at[pl.ds((1 - c) * lh, lh)]
        pltpu.make_async_remote_copy(src_ref=theirs, dst_ref=theirs, send_sem=send_sem, recv_sem=recv_sem,
                                     device_id=(x, y, 1 - c), device_id_type=MESH).wait_recv()
        cp.wait_send()
        local.wait()

    return pl.pallas_call(
        body, name=name, out_shape=_sds((2 * lh,) + r.shape[1:], r.dtype), in_specs=[HBM], out_specs=HBM,
        scratch_shapes=[pltpu.SemaphoreType.DMA, pltpu.SemaphoreType.DMA, pltpu.SemaphoreType.DMA],
    )(r)


def _add_halves(name, g, recv, c):
    _, L, R, C = g.shape
    lh = L // 2
    tr = _pick(R, 512, 16)

    def body(s_ref, g_ref, a_ref, o_ref):
        o_ref[...] = (g_ref[...].astype(F32) + a_ref[...].astype(F32)).astype(o_ref.dtype)

    blk = (None, None, tr, C)
    grid_spec = pltpu.PrefetchScalarGridSpec(
        num_scalar_prefetch=1, grid=(N_CHIPS, lh, R // tr),
        in_specs=[pl.BlockSpec(blk, lambda j, l, r, s: (j, s[0] * lh + l, r, 0)),
                  pl.BlockSpec(blk, lambda j, l, r, s: (j, l, r, 0))],
        out_specs=pl.BlockSpec(blk, lambda j, l, r, s: (j, l, r, 0)))
    return pl.pallas_call(
        body, name=name, grid_spec=grid_spec, out_shape=_sds((N_CHIPS, lh, R, C), g.dtype),
        compiler_params=pltpu.CompilerParams(dimension_semantics=("arbitrary",) * 3, vmem_limit_bytes=VMEM_LIMIT),
    )(jnp.reshape(c, (1,)).astype(jnp.int32), g, recv)


def _sum_chips(name, p, recv, me):
    _, lh, R, C = p.shape
    tr = _pick(R, 512, 16)

    def body(s_ref, p_ref, r_ref, o_ref):
        acc = p_ref[...].astype(F32)
        for k in range(3):
            acc = acc + r_ref[k].astype(F32)
        o_ref[...] = acc

    grid_spec = pltpu.PrefetchScalarGridSpec(
        num_scalar_prefetch=1, grid=(lh, R // tr),
        in_specs=[pl.BlockSpec((None, None, tr, C), lambda l, r, s: (s[0], l, r, 0)),
                  pl.BlockSpec((3, None, tr, C), lambda l, r, s: (0, l, r, 0))],
        out_specs=pl.BlockSpec((None, tr, C), lambda l, r, s: (l, r, 0)))
    return pl.pallas_call(
        body, name=name, grid_spec=grid_spec, out_shape=_sds((lh, R, C), F32),
        compiler_params=pltpu.CompilerParams(dimension_semantics=("arbitrary",) * 2, vmem_limit_bytes=VMEM_LIMIT),
    )(jnp.reshape(me, (1,)).astype(jnp.int32), p, recv)


def _sum_slots(name, buf):
    _, n, _ = buf.shape

    def body(b_ref, o_ref):
        acc = b_ref[0]
        for k in range(1, N_DEV):
            acc = acc + b_ref[k]
        o_ref[...] = acc

    return pl.pallas_call(
        body, name=name, grid=(n // PACK_ROWS,), out_shape=_sds((n, LANES), F32),
        in_specs=[pl.BlockSpec((N_DEV, PACK_ROWS, LANES), lambda i: (0, i, 0))],
        out_specs=pl.BlockSpec((PACK_ROWS, LANES), lambda i: (i, 0)),
    )(buf)


def _adam_tile(i, w, g, m, v):
    m = ADAM_B1 * m + (1.0 - ADAM_B1) * g
    v = ADAM_B2 * v + (1.0 - ADAM_B2) * (g * g)
    m_hat = m / (1.0 - ADAM_B1 ** ADAM_STEP)
    v_hat = v / (1.0 - ADAM_B2 ** ADAM_STEP)
    delta = -ADAM_LR * (m_hat / (jnp.sqrt(v_hat) + ADAM_EPS) + ADAM_WD * w)
    return [delta, m, v], []


def _adam(name, w, g, m, v):
    rows, C = w.shape
    tile = _pick(rows, ROW_TILE, 8)
    return _rowwise(name, _adam_tile, [('row', w), ('row', g), ('row', m), ('row', v)], [(C, F32)] * 3, [], tile)


def _cast_bf16(name, w):
    rows, C = w.shape
    tile = _pick(rows, ROW_TILE, 16)
    return _rowwise(name, lambda i, a: ([a], []), [('row', w)], [(C, BF16)], [], tile)[0]


def _pack(arrs):
    flat = jnp.concatenate([a.reshape(-1).astype(F32) for a in arrs])
    unit = PACK_ROWS * LANES
    n = -(-flat.shape[0] // unit) * unit
    return jnp.pad(flat, (0, n - flat.shape[0])).reshape(-1, LANES)


def _unpack(buf, shapes):
    flat = buf.reshape(-1)
    out, off = [], 0
    for s in shapes:
        n = 1
        for d in s:
            n *= d
        out.append(flat[off:off + n].reshape(s))
        off += n
    return out


def _norm_fwd(name, x, g, tile):
    D = x.shape[1]
    return _rowwise(name, lambda i, xv, gv: ([_rms(xv, gv)], []), [('row', x), ('const', g.reshape(1, D))],
                    [(D, BF16)], [], tile)[0]


def _norm_bwd(name, x, g, dh, dxo, tile):
    D = x.shape[1]
    if dxo is None:
        def fn(i, xv, dhv, gv):
            _, vjp = jax.vjp(_rms, xv, gv)
            return [], [vjp(dhv)[1]]
        return _rowwise(name, fn, [('row', x), ('row', dh), ('const', g.reshape(1, D))], [], [(1, D)], tile)[0]

    def fn(i, xv, dhv, dxv, gv):
        _, vjp = jax.vjp(_rms, xv, gv)
        dx, dg = vjp(dhv)
        return [dxv + dx], [dg]
    return _rowwise(name, fn, [('row', x), ('row', dh), ('row', dxo), ('const', g.reshape(1, D))],
                    [(D, F32)], [(1, D)], tile)


def _ffn_fwd(tag, x, g, wg, wu, wd, l, tile):
    T, D = x.shape
    fs = wg.shape[3]
    F = N_CHIPS * fs
    tm, tk, tn = _pick(T, TM), _pick(D, TK), _pick(D, TN)
    h = _norm_fwd(f"{tag}_norm", x, g, tile)
    wspec = _bs((None, None, tk, fs), lambda i, j, k: (j, l, k, 0))
    ospec = _bs((tm, fs), lambda i, j, k: (i, j))

    def epi(accs, _):
        a, b = accs
        return [_silu(a) * b, a, b]

    s, a, b = _mm(f"{tag}_gateup", (T // tm, N_CHIPS, D // tk),
                  [(h, _bs((tm, tk), lambda i, j, k: (i, k))), (wg, wspec), (wu, wspec)],
                  [(0, 1, 'nn', 0), (0, 2, 'nn', 1)], [(_sds((T, F), BF16), ospec)] * 3, [(tm, fs)] * 2, epi)
    xo = _mm(f"{tag}_down", (T // tm, D // tn, N_CHIPS),
             [(s, _bs((tm, fs), lambda i, j, k: (i, k))), (wd, _bs((None, None, fs, tn), lambda i, j, k: (k, l, 0, j))),
              (x, _bs((tm, tn), lambda i, j, k: (i, j)))],
             [(0, 1, 'nn', 0)], [(_sds((T, D), F32), _bs((tm, tn), lambda i, j, k: (i, j)))], [(tm, tn)],
             lambda accs, ex: [ex[0] + 0.5 * accs[0]], extras=(2,))[0]
    return xo, (x, h, a, b, s)


def _ffn_bwd(tag, dxo, saved, g, wg, wu, wd, gg, gu, gd, l, tile):
    x, h, a, b, s = saved
    T, D = x.shape
    fs = wg.shape[3]
    F = N_CHIPS * fs
    tm, tk, tn = _pick(T, TM), _pick(D, TK), _pick(D, TN)
    tkt = _pick(T, TK)

    def epi_ds(accs, ex):
        ds = 0.5 * accs[0]
        av, bv = ex[0].astype(F32), ex[1].astype(F32)
        sig = jax.nn.sigmoid(av)
        return [ds * bv * (sig * (1.0 + av * (1.0 - sig))), ds * (av * sig)]

    tspec = _bs((tm, fs), lambda i, j, k: (i, j))
    da, db = _mm(f"{tag}_ds", (T // tm, N_CHIPS, D // tk),
                 [(dxo, _bs((tm, tk), lambda i, j, k: (i, k))),
                  (wd, _bs((None, None, fs, tk), lambda i, j, k: (j, l, 0, k))), (a, tspec), (b, tspec)],
                 [(0, 1, 'nt', 0)], [(_sds((T, F), BF16), tspec)] * 2, [(tm, fs)], epi_ds, extras=(2, 3))
    gd = _mm(f"{tag}_dwd", (N_CHIPS, D // tn, T // tkt),
             [(s, _bs((tkt, fs), lambda i, j, k: (k, i))), (dxo, _bs((tkt, tn), lambda i, j, k: (k, j))), (gd, HBM)],
             [(0, 1, 'tn', 0)], [(_sds(gd.shape, gd.dtype), _bs((None, None, fs, tn), lambda i, j, k: (i, l, 0, j)))],
             [(fs, tn)], lambda accs, _: [0.5 * accs[0]], aliases={2: 0})[0]
    tmd = _pick(D, TM)
    gspec = _bs((None, None, tmd, fs), lambda i, j, k: (j, l, i, 0))
    gg, gu = _mm(f"{tag}_dwgu", (D // tmd, N_CHIPS, T // tkt),
                 [(h, _bs((tkt, tmd), lambda i, j, k: (k, i))), (da, _bs((tkt, fs), lambda i, j, k: (k, j))),
                  (db, _bs((tkt, fs), lambda i, j, k: (k, j))), (gg, HBM), (gu, HBM)],
                 [(0, 1, 'tn', 0), (0, 2, 'tn', 1)],
                 [(_sds(gg.shape, gg.dtype), gspec), (_sds(gu.shape, gu.dtype), gspec)], [(tmd, fs)] * 2,
                 lambda accs, _: accs, aliases={3: 0, 4: 1})
    wtspec = _bs((None, None, tn, fs), lambda i, j, k: (k, l, j, 0))
    aspec = _bs((tm, fs), lambda i, j, k: (i, k))
    dh = _mm(f"{tag}_dh", (T // tm, D // tn, N_CHIPS),
             [(da, aspec), (wg, wtspec), (db, aspec), (wu, wtspec)], [(0, 1, 'nt', 0), (2, 3, 'nt', 0)],
             [(_sds((T, D), F32), _bs((tm, tn), lambda i, j, k: (i, j)))], [(tm, tn)], lambda accs, _: accs)[0]
    dx, dg = _norm_bwd(f"{tag}_norm_bwd", x, g, dh, dxo, tile)
    return dx, dg, gg, gu, gd


def _proj_rows(name, a, w, l, out_dtype, extras=(), epilogue=None):
    M, K = a.shape
    ks, N = w.shape[2], w.shape[3]
    tm, tn = _pick(M, TM), _pick(N, TN)
    ins = [(a, _bs((tm, ks), lambda i, j, k: (i, k))), (w, _bs((None, None, ks, tn), lambda i, j, k: (k, l, 0, j)))]
    for e in extras:
        if e.shape[0] == 1:
            ins.append((e, _bs((1, tn), lambda i, j, k: (0, j))))
        else:
            ins.append((e, _bs((tm, tn), lambda i, j, k: (i, j))))
    epi = epilogue or (lambda accs, ex: accs)
    return _mm(name, (M // tm, N // tn, N_CHIPS), ins, [(0, 1, 'nn', 0)],
               [(_sds((M, N), out_dtype), _bs((tm, tn), lambda i, j, k: (i, j)))], [(tm, tn)], epi,
               extras=tuple(range(2, 2 + len(extras))))[0]


def _proj_rows_t(name, pairs, l, out_dtype):
    dy0, w0 = pairs[0]
    M, N = dy0.shape
    ks = w0.shape[2]
    tm, tk = _pick(M, TM), _pick(N, TK)
    ins, prs = [], []
    for p, (dy, w) in enumerate(pairs):
        ins.append((dy, _bs((tm, tk), lambda i, j, k: (i, k))))
        ins.append((w, _bs((None, None, ks, tk), lambda i, j, k: (j, l, 0, k))))
        prs.append((2 * p, 2 * p + 1, 'nt', 0))
    return _mm(name, (M // tm, N_CHIPS, N // tk), ins, prs,
               [(_sds((M, N_CHIPS * ks), out_dtype), _bs((tm, ks), lambda i, j, k: (i, j)))], [(tm, ks)],
               lambda accs, _: accs)[0]


def _grad_rows(name, pairs, gbufs, l):
    a = pairs[0][0]
    T, K = a.shape
    N = pairs[0][1].shape[1]
    ks = K // N_CHIPS
    tn, tk = _pick(N, TN), _pick(T, TK)
    ins = [(a, _bs((tk, ks), lambda i, j, k: (k, i)))]
    prs = []
    for p, (_, dy) in enumerate(pairs):
        ins.append((dy, _bs((tk, tn), lambda i, j, k: (k, j))))
        prs.append((0, 1 + p, 'tn', p))
    n_in = len(ins)
    ins += [(gb, HBM) for gb in gbufs]
    gspec = _bs((None, None, ks, tn), lambda i, j, k: (i, l, 0, j))
    return _mm(name, (N_CHIPS, N // tn, T // tk), ins, prs, [(_sds(gb.shape, gb.dtype), gspec) for gb in gbufs],
               [(ks, tn)] * len(pairs), lambda accs, _: accs, aliases={n_in + p: p for p in range(len(gbufs))})


def _attn_fwd_tile(hd, scale):
    def fn(i, q, k, v):
        outs = []
        for h in range(N_MEM_HEADS):
            sl = slice(h * hd, (h + 1) * hd)
            s = lax.dot_general(q[:, sl], k[:, sl], NT, preferred_element_type=F32) * scale
            s = s - jnp.max(s, axis=-1, keepdims=True)
            p = jnp.exp(s)
            p = p / jnp.sum(p, axis=-1, keepdims=True)
            outs.append(lax.dot_general(p.astype(BF16), v[:, sl], NN, preferred_element_type=F32))
        return [jnp.concatenate(outs, axis=1)], []
    return fn


def _attn_bwd_tile(hd, scale):
    def fn(i, q, do, k, v):
        dqs, dks, dvs = [], [], []
        for h in range(N_MEM_HEADS):
            sl = slice(h * hd, (h + 1) * hd)
            qh, kh, vh, doh = q[:, sl], k[:, sl], v[:, sl], do[:, sl]
            s = lax.dot_general(qh, kh, NT, preferred_element_type=F32) * scale
            s = s - jnp.max(s, axis=-1, keepdims=True)
            p = jnp.exp(s)
            p = p / jnp.sum(p, axis=-1, keepdims=True)
            pb = p.astype(BF16)
            dvs.append(lax.dot_general(pb, doh, TN_, preferred_element_type=F32))
            dp = lax.dot_general(doh, vh, NT, preferred_element_type=F32)
            ds = (p * (dp - jnp.sum(dp * p, axis=-1, keepdims=True)) * scale).astype(BF16)
            dqs.append(lax.dot_general(ds, kh, NN, preferred_element_type=F32))
            dks.append(lax.dot_general(ds, qh, TN_, preferred_element_type=F32))
        return [jnp.concatenate(dqs, axis=1)], [jnp.concatenate(dks, axis=1), jnp.concatenate(dvs, axis=1)]
    return fn


def _attn_fwd(l, x, mem, gq, gkv, wq, wk, wv, wo, tile):
    T, D = x.shape
    M = mem.shape[0]
    hd = D // N_MEM_HEADS
    hq = _norm_fwd(f"xa{l}_normq", x, gq, tile)
    mn = _norm_fwd(f"xa{l}_normkv", mem, gkv, _pick(M, tile, 16))
    q = _proj_rows(f"xa{l}_q", hq, wq, l, BF16)
    k = _proj_rows(f"xa{l}_k", mn, wk, l, BF16)
    v = _proj_rows(f"xa{l}_v", mn, wv, l, BF16)
    o = _rowwise(f"xa{l}_attn", _attn_fwd_tile(hd, hd ** -0.5), [('row', q), ('const', k), ('const', v)],
                 [(D, BF16)], [], tile)[0]
    xo = _proj_rows(f"xa{l}_o", o, wo, l, F32, extras=(x,), epilogue=lambda accs, ex: [ex[0] + accs[0]])
    return xo, (x, hq, mn, q, k, v, o)


def _attn_bwd(l, dxo, saved, mem, gq, gkv, wq, wk, wv, wo, gwq, gwk, gwv, gwo, tile):
    x, hq, mn, q, k, v, o = saved
    T, D = x.shape
    M = mem.shape[0]
    hd = D // N_MEM_HEADS
    do = _proj_rows_t(f"xa{l}_do", [(dxo, wo)], l, BF16)
    gwo = _grad_rows(f"xa{l}_dwo", [(o, dxo)], [gwo], l)[0]
    dq, dk, dv = _rowwise(f"xa{l}_attn_bwd", _attn_bwd_tile(hd, hd ** -0.5),
                          [('row', q), ('row', do), ('const', k), ('const', v)], [(D, BF16)], [(M, D), (M, D)], tile)
    dhq = _proj_rows_t(f"xa{l}_dhq", [(dq, wq)], l, F32)
    gwq = _grad_rows(f"xa{l}_dwq", [(hq, dq)], [gwq], l)[0]
    dmn = _proj_rows_t(f"xa{l}_dmn", [(dk, wk), (dv, wv)], l, F32)
    gwk, gwv = _grad_rows(f"xa{l}_dwkv", [(mn, dk), (mn, dv)], [gwk, gwv], l)
    dx, dgq = _norm_bwd(f"xa{l}_normq_bwd", x, gq, dhq, dxo, tile)
    dgkv = _norm_bwd(f"xa{l}_normkv_bwd", mem, gkv, dmn, None, _pick(M, tile, 16))
    return dx, dgq, dgkv, gwq, gwk, gwv, gwo


def _chunk_mask():
    p = lax.broadcasted_iota(jnp.int32, (GMLP_BLOCK, GMLP_BLOCK), 0)
    q = lax.broadcasted_iota(jnp.int32, (GMLP_BLOCK, GMLP_BLOCK), 1)
    return (q // CHUNK) <= (p // CHUNK)


def _spatial_fwd(vn, ws_ref, bsf, mask, hd):
    vb = vn.astype(BF16)
    wsm = [jnp.where(mask, ws_ref[h], 0.0).astype(BF16) for h in range(A_HEADS)]
    rows = []
    for n in range(vn.shape[0] // GMLP_BLOCK):
        blk = vb[n * GMLP_BLOCK:(n + 1) * GMLP_BLOCK]
        cols = [lax.dot_general(wsm[h], blk[:, h * hd:(h + 1) * hd], NN, preferred_element_type=F32)
                for h in range(A_HEADS)]
        rows.append(jnp.concatenate(cols, axis=1) + bsf)
    return jnp.concatenate(rows, axis=0)


def _spatial_bwd(dsp, vn, ws_ref, mask, hd):
    vb, db16 = vn.astype(BF16), dsp.astype(BF16)
    wsm = [jnp.where(mask, ws_ref[h], 0.0).astype(BF16) for h in range(A_HEADS)]
    dws = [jnp.zeros((GMLP_BLOCK, GMLP_BLOCK), F32) for _ in range(A_HEADS)]
    dbs = jnp.zeros((GMLP_BLOCK, vn.shape[1]), F32)
    rows = []
    for n in range(vn.shape[0] // GMLP_BLOCK):
        sl = slice(n * GMLP_BLOCK, (n + 1) * GMLP_BLOCK)
        cols = []
        for h in range(A_HEADS):
            hs = slice(h * hd, (h + 1) * hd)
            cols.append(lax.dot_general(wsm[h], db16[sl, hs], TN_, preferred_element_type=F32))
            dws[h] = dws[h] + lax.dot_general(db16[sl, hs], vb[sl, hs], NT, preferred_element_type=F32)
        rows.append(jnp.concatenate(cols, axis=1))
        dbs = dbs + dsp[sl]
    dws = [jnp.where(mask, d, 0.0) for d in dws]
    return jnp.concatenate(rows, axis=0), dws, dbs


def _conv_taps(cat, cw_ref, kw, tile):
    acc = jnp.zeros((tile, cat.shape[1]), F32)
    for k in range(kw):
        sh = kw - 1 - k
        r = cat if sh == 0 else pltpu.roll(cat, sh, 0)
        acc = acc + r[HALO:] * cw_ref[k:k + 1, :]
    return acc


def _mix_fwd_tile(A, B, kw, tile):
    hd = A // A_HEADS

    def fn(i, z, zp, ws_ref, bsf, glg, glb, cw_ref, cb, clg, clb):
        mask = _chunk_mask()
        u = _gelu(z[:, :A])
        vn = _ln(_gelu(z[:, A:2 * A]), glg, glb)
        ya = u * _spatial_fwd(vn, ws_ref, bsf, mask, hd)
        hb = _glu(z[:, 2 * A:2 * A + B], z[:, 2 * A + B:])
        hp = jnp.where(i > 0, _glu(zp[:, 2 * A:2 * A + B], zp[:, 2 * A + B:]), 0.0)
        conv = _conv_taps(jnp.concatenate([hp, hb], axis=0), cw_ref, kw, tile) + cb
        yb = _silu(_ln(conv, clg, clb))
        return [jnp.concatenate([ya, yb], axis=1)], []
    return fn


def _mix_bwd1_tile(A, B, kw, tile):
    hd = A // A_HEADS

    def fn(i, z, zp, dy, dxo, ws_ref, bsf, glg, glb, cw_ref, cb, clg, clb):
        mask = _chunk_mask()
        dya, dyb = dy[:, :A], dy[:, A:]
        zu, zv = z[:, :A], z[:, A:2 * A]
        u, vjp_u = jax.vjp(_gelu, zu)
        vn, vjp_v = jax.vjp(lambda t, g, b: _ln(_gelu(t), g, b), zv, glg, glb)
        sp = _spatial_fwd(vn, ws_ref, bsf, mask, hd)
        dzu = vjp_u(dya * sp)[0]
        dvn, dws, dbs = _spatial_bwd(dya * u, vn, ws_ref, mask, hd)
        dzv, dglg, dglb = vjp_v(dvn)
        hb = _glu(z[:, 2 * A:2 * A + B], z[:, 2 * A + B:])
        hp = jnp.where(i > 0, _glu(zp[:, 2 * A:2 * A + B], zp[:, 2 * A + B:]), 0.0)
        cat = jnp.concatenate([hp, hb], axis=0)
        conv = _conv_taps(cat, cw_ref, kw, tile) + cb
        _, vjp_c = jax.vjp(lambda t, g, b: _silu(_ln(t, g, b)), conv, clg, clb)
        dconv, dclg, dclb = vjp_c(dyb)
        tap = lax.broadcasted_iota(jnp.int32, (HALO, 1), 0)
        dcw = jnp.zeros((HALO, B), F32)
        for k in range(kw):
            sh = kw - 1 - k
            r = cat if sh == 0 else pltpu.roll(cat, sh, 0)
            dcw = dcw + jnp.where(tap == k, jnp.sum(dconv * r[HALO:], axis=0, keepdims=True), 0.0)
        dcb = jnp.sum(dconv, axis=0, keepdims=True)
        dbo = jnp.sum(dxo, axis=0, keepdims=True)
        dws = jnp.concatenate([d[None] for d in dws], axis=0)
        return [jnp.concatenate([dzu, dzv], axis=1), dconv], [dws, dbs, dglg, dglb, dcw, dcb, dclg, dclb, dbo]
    return fn


def _mix_bwd2_tile(A, B, kw, tile, n_tiles):
    def fn(i, z, dza, dc, dcn, cw_ref):
        dcn = jnp.where(i < n_tiles - 1, dcn, 0.0)
        cat = jnp.concatenate([dc, dcn], axis=0)
        n = tile + HALO
        dhb = jnp.zeros((tile, B), F32)
        for k in range(kw):
            sh = kw - 1 - k
            r = cat if sh == 0 else pltpu.roll(cat, n - sh, 0)
            dhb = dhb + r[:tile] * cw_ref[k:k + 1, :]
        _, vjp_g = jax.vjp(_glu, z[:, 2 * A:2 * A + B], z[:, 2 * A + B:])
        da, dg = vjp_g(dhb)
        dz = jnp.concatenate([dza, da, dg], axis=1)
        return [dz], [jnp.sum(dz, axis=0, keepdims=True)]
    return fn


def _even_fwd(l, e, x, gm, p, W, tile):
    T, D = x.shape
    w_in, w_out = W['ab_w_in'], W['ab_w_out']
    zs = w_in.shape[3]
    Z = N_CHIPS * zs
    A = p['gmlp_ln_g'].shape[1]
    B = p['conv_b'].shape[1]
    kw = p['conv_w_full'].shape[1]
    tm, tk = _pick(T, TM), _pick(D, TK)
    h = _norm_fwd(f"mix{l}_norm", x, gm, tile)
    z = _mm(f"mix{l}_in", (T // tm, N_CHIPS, D // tk),
            [(h, _bs((tm, tk), lambda i, j, k: (i, k))), (w_in, _bs((None, None, tk, zs), lambda i, j, k: (j, e, k, 0))),
             (p['ab_b_in'][e].reshape(1, Z), _bs((1, zs), lambda i, j, k: (0, j)))],
            [(0, 1, 'nn', 0)], [(_sds((T, Z), F32), _bs((tm, zs), lambda i, j, k: (i, j)))], [(tm, zs)],
            lambda accs, ex: [accs[0] + ex[0]], extras=(2,))[0]
    consts = _even_consts(p, e, A, B, kw)
    ycat = _rowwise(f"mix{l}_mid", _mix_fwd_tile(A, B, kw, tile), [('row', z), ('prev', z)] + consts,
                    [(A + B, BF16)], [], tile)[0]
    xo = _proj_rows(f"mix{l}_out", ycat, w_out, e, F32, extras=(x, p['ab_b_out'][e].reshape(1, D)),
                    epilogue=lambda accs, ex: [ex[0] + accs[0] + ex[1]])
    return xo, (x, h, z, ycat)


def _even_consts(p, e, A, B, kw):
    hd = A // A_HEADS
    bsf = jnp.repeat(p['gmlp_b_s'][e].T, hd, axis=1)
    cw = jnp.pad(p['conv_w_full'][e], ((0, HALO - kw), (0, 0)))
    return [('cref', p['gmlp_w_s'][e]), ('const', bsf), ('const', p['gmlp_ln_g'][e].reshape(1, A)),
            ('const', p['gmlp_ln_b'][e].reshape(1, A)), ('cref', cw), ('const', p['conv_b'][e].reshape(1, B)),
            ('const', p['conv_ln_g'][e].reshape(1, B)), ('const', p['conv_ln_b'][e].reshape(1, B))]


def _even_bwd(l, e, dxo, saved, gm, p, W, G, tile):
    x, h, z, ycat = saved
    T, D = x.shape
    w_in, w_out = W['ab_w_in'], W['ab_w_out']
    zs = w_in.shape[3]
    Z = N_CHIPS * zs
    A = p['gmlp_ln_g'].shape[1]
    B = p['conv_b'].shape[1]
    kw = p['conv_w_full'].shape[1]
    hd = A // A_HEADS
    dycat = _proj_rows_t(f"mix{l}_dycat", [(dxo, w_out)], e, F32)
    G['ab_w_out'] = _grad_rows(f"mix{l}_dwout", [(ycat, dxo)], [G['ab_w_out']], e)[0]
    consts = _even_consts(p, e, A, B, kw)
    accs = [(A_HEADS, GMLP_BLOCK, GMLP_BLOCK), (GMLP_BLOCK, A), (1, A), (1, A), (HALO, B), (1, B), (1, B), (1, B),
            (1, D)]
    dza, dconv, dws, dbs, dglg, dglb, dcw, dcb, dclg, dclb, dbo = _rowwise(
        f"mix{l}_mid_bwd1", _mix_bwd1_tile(A, B, kw, tile),
        [('row', z), ('prev', z), ('row', dycat), ('row', dxo)] + consts, [(2 * A, F32), (B, F32)], accs, tile)
    dz, dbin = _rowwise(f"mix{l}_mid_bwd2", _mix_bwd2_tile(A, B, kw, tile, T // tile),
                        [('row', z), ('row', dza), ('row', dconv), ('next', dconv), consts[4]],
                        [(Z, BF16)], [(1, Z)], tile)
    tmd, tkt = _pick(D, TM), _pick(T, TK)
    G['ab_w_in'] = _mm(f"mix{l}_dwin", (D // tmd, N_CHIPS, T // tkt),
                       [(h, _bs((tkt, tmd), lambda i, j, k: (k, i))), (dz, _bs((tkt, zs), lambda i, j, k: (k, j))),
                        (G['ab_w_in'], HBM)], [(0, 1, 'tn', 0)],
                       [(_sds(G['ab_w_in'].shape, BF16), _bs((None, None, tmd, zs), lambda i, j, k: (j, e, i, 0)))],
                       [(tmd, zs)], lambda a, _: a, aliases={2: 0})[0]
    tm, tn = _pick(T, TM), _pick(D, TN)
    dh = _mm(f"mix{l}_dh", (T // tm, D // tn, N_CHIPS),
             [(dz, _bs((tm, zs), lambda i, j, k: (i, k))), (w_in, _bs((None, None, tn, zs), lambda i, j, k: (k, e, j, 0)))],
             [(0, 1, 'nt', 0)], [(_sds((T, D), F32), _bs((tm, tn), lambda i, j, k: (i, j)))], [(tm, tn)],
             lambda a, _: a)[0]
    dx, dgm = _norm_bwd(f"mix{l}_norm_bwd", x, gm, dh, dxo, tile)
    small = {'ab_b_in': dbin.reshape(Z), 'gmlp_w_s': dws, 'gmlp_b_s': dbs.reshape(GMLP_BLOCK, A_HEADS, hd).sum(-1).T,
             'gmlp_ln_g': dglg.reshape(A), 'gmlp_ln_b': dglb.reshape(A), 'conv_w': dcw[:kw], 'conv_b': dcb.reshape(B),
             'conv_ln_g': dclg.reshape(B), 'conv_ln_b': dclb.reshape(B), 'ab_b_out': dbo.reshape(D)}
    return dx, dgm, small


def _pool_counts(t, cg):
    return jnp.concatenate([jnp.broadcast_to(jnp.minimum(t + 1, w).astype(F32), (t.shape[0], cg))
                            for w in POOL_WINDOWS], axis=1)


def _window_sums(cat, cg, back):
    n = cat.shape[0]
    outs = []
    for gi, w in enumerate(POOL_WINDOWS):
        s = cat[:, gi * cg:(gi + 1) * cg]
        step = 1
        while step < w:
            s = s + pltpu.roll(s, step if back else n - step, 0)
            step *= 2
        outs.append(s)
    return jnp.concatenate(outs, axis=1)


def _pool_fwd_tile(D, tile):
    cg = D // len(POOL_WINDOWS)

    def fn(i, x, xp, g):
        h = _rms(x, g)
        hp = jnp.where(i > 0, _rms(xp, g), 0.0)
        sums = _window_sums(jnp.concatenate([hp, h], axis=0), cg, True)[HALO:]
        return [sums / _pool_counts(_row_ids(i, tile, tile), cg) - h], []
    return fn


def _pool_bwd_tile(D, tile, n_tiles):
    cg = D // len(POOL_WINDOWS)

    def fn(i, dd, ddn, x, dxo, g):
        e = dd / _pool_counts(_row_ids(i, tile, tile), cg)
        en = jnp.where(i < n_tiles - 1, ddn / _pool_counts(_row_ids(i + 1, tile, HALO), cg), 0.0)
        dh = _window_sums(jnp.concatenate([e, en], axis=0), cg, False)[:tile] - dd
        _, vjp = jax.vjp(_rms, x, g)
        dx, dg = vjp(dh)
        return [dxo + dx], [dg]
    return fn


def _odd_fwd(l, o, x, gm, p, W, tile):
    T, D = x.shape
    wc = W['pool_w']
    cg = wc.shape[3]
    cs = cg // N_CHIPS
    ng = len(POOL_WINDOWS)
    tm = _pick(T, TM)
    d = _rowwise(f"mix{l}_pool", _pool_fwd_tile(D, tile), [('row', x), ('prev', x), ('const', gm.reshape(1, D))],
                 [(D, BF16)], [], tile)[0]
    gspec = _bs((tm, cg), lambda i, j, k: (i, j))
    vspec = _bs((1, cg), lambda i, j, k: (0, j))

    def epi(accs, ex):
        pre = accs[0] + ex[0]
        return [ex[2] + pre * ex[1], pre]

    xo, pre = _mm(f"mix{l}_poolmm", (T // tm, ng, N_CHIPS),
                  [(d, _bs((tm, cs), lambda i, j, k: (i, j * N_CHIPS + k))),
                   (wc, _bs((None, None, cs, cg), lambda i, j, k: (k, o, j, 0))),
                   (p['pool_b_full'][o].reshape(1, D), vspec), (p['pool_scale_full'][o].reshape(1, D), vspec),
                   (x, gspec)],
                  [(0, 1, 'nn', 0)], [(_sds((T, D), F32), gspec)] * 2, [(tm, cg)], epi, extras=(2, 3, 4))
    return xo, (x, d, pre)


def _odd_bwd(l, o, dxo, saved, gm, p, W, G, tile):
    x, d, pre = saved
    T, D = x.shape
    wc = W['pool_w']
    cg = wc.shape[3]
    cs = cg // N_CHIPS
    ng = len(POOL_WINDOWS)
    tm, tkt = _pick(T, TM), _pick(T, TK)

    def fn(i, dxv, prev, sc):
        return [dxv * sc], [jnp.sum(dxv * prev, axis=0, keepdims=True), jnp.sum(dxv * sc, axis=0, keepdims=True)]

    do, dscale, dbc = _rowwise(f"mix{l}_pool_bwd1", fn,
                               [('row', dxo), ('row', pre), ('const', p['pool_scale_full'][o].reshape(1, D))],
                               [(D, BF16)], [(1, D), (1, D)], tile)
    nb = ng * N_CHIPS
    dd = _mm(f"mix{l}_pool_dd", (T // tm, nb, 1),
             [(do, _bs((tm, cg), lambda i, j, k: (i, j // N_CHIPS))),
              (wc, _bs((None, None, cs, cg), lambda i, j, k: (j % N_CHIPS, o, j // N_CHIPS, 0)))],
             [(0, 1, 'nt', 0)], [(_sds((T, D), F32), _bs((tm, cs), lambda i, j, k: (i, j)))], [(tm, cs)],
             lambda a, _: a)[0]
    G['pool_w'] = _mm(f"mix{l}_pool_dw", (nb, 1, T // tkt),
                      [(d, _bs((tkt, cs), lambda i, j, k: (k, i))), (do, _bs((tkt, cg), lambda i, j, k: (k, i // N_CHIPS))),
                       (G['pool_w'], HBM)], [(0, 1, 'tn', 0)],
                      [(_sds(G['pool_w'].shape, BF16),
                        _bs((None, None, cs, cg), lambda i, j, k: (i % N_CHIPS, o, i // N_CHIPS, 0)))],
                      [(cs, cg)], lambda a, _: a, aliases={2: 0})[0]
    dx, dgm = _rowwise(f"mix{l}_pool_bwd2", _pool_bwd_tile(D, tile, T // tile),
                       [('row', dd), ('next', dd), ('row', x), ('row', dxo), ('const', gm.reshape(1, D))],
                       [(D, F32)], [(1, D)], tile)
    small = {'pool_b': dbc.reshape(ng, cg), 'pool_scale': dscale.reshape(D)}
    return dx, dgm, small


def _final(x, g, tgt, tile):
    T, D = x.shape

    def fn(i, xv, tv, gv):
        y, vjp = jax.vjp(_rms, xv, gv)
        err = y - tv
        dx, dg = vjp(err / D)
        loss = 0.5 * jnp.sum(jnp.mean(err * err, axis=-1, keepdims=True), axis=0, keepdims=True)
        return [dx], [dg, jnp.broadcast_to(loss, (1, LANES))]

    dx, dg, loss = _rowwise("final", fn, [('row', x), ('row', tgt), ('const', g.reshape(1, D))],
                            [(D, F32)], [(1, D), (1, LANES)], tile)
    return dx, dg.reshape(D), loss[0, 0]


def _as3d(name, w):
    return w.reshape(w.shape[0], -1, w.shape[-1]) if name == 'pool_w' else w


def kernel(x, mem, norm_ffn1, ffn1_gate, ffn1_up, ffn1_down, norm_mix, ab_w_in, ab_b_in, gmlp_w_s, gmlp_b_s, gmlp_ln_g, gmlp_ln_b, conv_w, conv_b, conv_ln_g, conv_ln_b, ab_w_out, ab_b_out, pool_w, pool_b, pool_scale, norm_xq, norm_xkv, xattn_wq, xattn_wk, xattn_wv, xattn_wo, norm_ffn2, ffn2_gate, ffn2_up, ffn2_down, norm_final, loss_target, m_norm_ffn1, m_ffn1_gate, m_ffn1_up, m_ffn1_down, m_norm_mix, m_ab_w_in, m_ab_b_in, m_gmlp_w_s, m_gmlp_b_s, m_gmlp_ln_g, m_gmlp_ln_b, m_conv_w, m_conv_b, m_conv_ln_g, m_conv_ln_b, m_ab_w_out, m_ab_b_out, m_pool_w, m_pool_b, m_pool_scale, m_norm_xq, m_norm_xkv, m_xattn_wq, m_xattn_wk, m_xattn_wv, m_xattn_wo, m_norm_ffn2, m_ffn2_gate, m_ffn2_up, m_ffn2_down, m_norm_final, v_norm_ffn1, v_ffn1_gate, v_ffn1_up, v_ffn1_down, v_norm_mix, v_ab_w_in, v_ab_b_in, v_gmlp_w_s, v_gmlp_b_s, v_gmlp_ln_g, v_gmlp_ln_b, v_conv_w, v_conv_b, v_conv_ln_g, v_conv_ln_b, v_ab_w_out, v_ab_b_out, v_pool_w, v_pool_b, v_pool_scale, v_norm_xq, v_norm_xkv, v_xattn_wq, v_xattn_wk, v_xattn_wv, v_xattn_wo, v_norm_ffn2, v_ffn2_gate, v_ffn2_up, v_ffn2_down, v_norm_final):
    w = dict(zip(WEIGHTS, [norm_ffn1, ffn1_gate, ffn1_up, ffn1_down, norm_mix, ab_w_in, ab_b_in, gmlp_w_s, gmlp_b_s, gmlp_ln_g, gmlp_ln_b, conv_w, conv_b, conv_ln_g, conv_ln_b, ab_w_out, ab_b_out, pool_w, pool_b, pool_scale, norm_xq, norm_xkv, xattn_wq, xattn_wk, xattn_wv, xattn_wo, norm_ffn2, ffn2_gate, ffn2_up, ffn2_down, norm_final]))
    m = dict(zip(WEIGHTS, [m_norm_ffn1, m_ffn1_gate, m_ffn1_up, m_ffn1_down, m_norm_mix, m_ab_w_in, m_ab_b_in, m_gmlp_w_s, m_gmlp_b_s, m_gmlp_ln_g, m_gmlp_ln_b, m_conv_w, m_conv_b, m_conv_ln_g, m_conv_ln_b, m_ab_w_out, m_ab_b_out, m_pool_w, m_pool_b, m_pool_scale, m_norm_xq, m_norm_xkv, m_xattn_wq, m_xattn_wk, m_xattn_wv, m_xattn_wo, m_norm_ffn2, m_ffn2_gate, m_ffn2_up, m_ffn2_down, m_norm_final]))
    v = dict(zip(WEIGHTS, [v_norm_ffn1, v_ffn1_gate, v_ffn1_up, v_ffn1_down, v_norm_mix, v_ab_w_in, v_ab_b_in, v_gmlp_w_s, v_gmlp_b_s, v_gmlp_ln_g, v_gmlp_ln_b, v_conv_w, v_conv_b, v_conv_ln_g, v_conv_ln_b, v_ab_w_out, v_ab_b_out, v_pool_w, v_pool_b, v_pool_scale, v_norm_xq, v_norm_xkv, v_xattn_wq, v_xattn_wk, v_xattn_wv, v_xattn_wo, v_norm_ffn2, v_ffn2_gate, v_ffn2_up, v_ffn2_down, v_norm_final]))

    xs, mems, tgt = x[0], mem[0], loss_target[0]
    T, D = xs.shape
    L = norm_ffn1.shape[0]
    tile = _pick(T, ROW_TILE)
    cx, cy, cc = _mesh_pos()
    chip = 2 * cx + cy

    W = {}
    for n in BIG:
        w3 = _as3d(n, w[n])
        wb = _cast_bf16(f"cast_{n}", w3.reshape(-1, w3.shape[-1])).reshape(w3.shape)
        W[n] = _allgather(f"allgather_{n}", wb)
    sh_shapes = [w[n].shape for n in SMALL_SHARDED]
    slots = _gather_all("gather_small_shards", _pack([w[n] for n in SMALL_SHARDED]))
    per_chip = [_unpack(slots[2 * j], sh_shapes) for j in range(N_CHIPS)]
    full = {n: jnp.concatenate([per_chip[j][k] for j in range(N_CHIPS)], axis=-1) for k, n in enumerate(SMALL_SHARDED)}
    p = dict(w)
    p['conv_w_full'] = full['conv_w'].reshape(full['conv_w'].shape[0], full['conv_w'].shape[1], -1)
    p['pool_b_full'] = full['pool_b']
    p['pool_scale_full'] = full['pool_scale']

    saved = []
    xc = xs
    for l in range(L):
        s = {}
        xc, s['ffn1'] = _ffn_fwd(f"ffn1_{l}", xc, w['norm_ffn1'][l], W['ffn1_gate'], W['ffn1_up'], W['ffn1_down'], l, tile)
        if l % 2 == 0:
            xc, s['mix'] = _even_fwd(l, l // 2, xc, w['norm_mix'][l], p, W, tile)
        else:
            xc, s['mix'] = _odd_fwd(l, l // 2, xc, w['norm_mix'][l], p, W, tile)
        xc, s['xa'] = _attn_fwd(l, xc, mems, w['norm_xq'][l], w['norm_xkv'][l], W['xattn_wq'], W['xattn_wk'],
                                W['xattn_wv'], W['xattn_wo'], tile)
        xc, s['ffn2'] = _ffn_fwd(f"ffn2_{l}", xc, w['norm_ffn2'][l], W['ffn2_gate'], W['ffn2_up'], W['ffn2_down'], l, tile)
        saved.append(s)

    dx, g_final, loss_local = _final(xc, w['norm_final'], tgt, tile)
    loss = lax.psum(loss_local, ("x", "y", "c"))
    G = {n: lax.empty(W[n].shape, BF16) for n in BIG}
    gs = {n: [None] * w[n].shape[0] for n in SMALL if n != 'norm_final'}
    for l in reversed(range(L)):
        s = saved[l]
        dx, dg, G['ffn2_gate'], G['ffn2_up'], G['ffn2_down'] = _ffn_bwd(
            f"ffn2_{l}", dx, s['ffn2'], w['norm_ffn2'][l], W['ffn2_gate'], W['ffn2_up'], W['ffn2_down'],
            G['ffn2_gate'], G['ffn2_up'], G['ffn2_down'], l, tile)
        gs['norm_ffn2'][l] = dg.reshape(D)
        dx, dgq, dgkv, G['xattn_wq'], G['xattn_wk'], G['xattn_wv'], G['xattn_wo'] = _attn_bwd(
            l, dx, s['xa'], mems, w['norm_xq'][l], w['norm_xkv'][l], W['xattn_wq'], W['xattn_wk'], W['xattn_wv'],
            W['xattn_wo'], G['xattn_wq'], G['xattn_wk'], G['xattn_wv'], G['xattn_wo'], tile)
        gs['norm_xq'][l], gs['norm_xkv'][l] = dgq.reshape(D), dgkv.reshape(D)
        if l % 2 == 0:
            dx, dgm, small = _even_bwd(l, l // 2, dx, s['mix'], w['norm_mix'][l], p, W, G, tile)
        else:
            dx, dgm, small = _odd_bwd(l, l // 2, dx, s['mix'], w['norm_mix'][l], p, W, G, tile)
        for n, val in small.items():
            gs[n][l // 2] = val
        gs['norm_mix'][l] = dgm.reshape(D)
        dx, dg, G['ffn1_gate'], G['ffn1_up'], G['ffn1_down'] = _ffn_bwd(
            f"ffn1_{l}", dx, s['ffn1'], w['norm_ffn1'][l], W['ffn1_gate'], W['ffn1_up'], W['ffn1_down'],
            G['ffn1_gate'], G['ffn1_up'], G['ffn1_down'], l, tile)
        gs['norm_ffn1'][l] = dg.reshape(D)
    grad_x = dx[None]

    small_full = {n: jnp.stack(gs[n]) for n in gs}
    small_full['norm_final'] = g_final
    full_shapes = [small_full[n].shape for n in SMALL]
    summed = _sum_slots("sum_small", _gather_all("gather_small_grads", _pack([small_full[n] for n in SMALL])))
    g_small = dict(zip(SMALL, _unpack(summed, full_shapes)))
    for n in SMALL_SHARDED:
        width = w[n].shape[-1]
        g_small[n] = lax.dynamic_slice_in_dim(g_small[n], chip * width, width, axis=g_small[n].ndim - 1).reshape(w[n].shape)

    grads, delta, new_m, new_v = {}, {}, {}, {}
    for n in BIG:
        got = _swap_halves(f"rs_swap_{n}", G[n])
        part = _add_halves(f"rs_add_{n}", G[n], got, cc)
        parts = _chip_exchange(f"rs_chips_{n}", part)
        mine = _sum_chips(f"rs_sum_{n}", part, parts, chip)
        g3 = _join_halves(f"rs_join_{n}", mine)
        shape2 = (-1, g3.shape[-1])
        d2, m2, v2 = _adam(f"adam_{n}", _as3d(n, w[n]).reshape(shape2), g3.reshape(shape2),
                           _as3d(n, m[n]).reshape(shape2), _as3d(n, v[n]).reshape(shape2))
        grads[n], delta[n], new_m[n], new_v[n] = (t.reshape(w[n].shape) for t in (g3, d2, m2, v2))
    small_shapes = [w[n].shape for n in SMALL]
    d2, m2, v2 = _adam("adam_small", _pack([w[n] for n in SMALL]), _pack([g_small[n] for n in SMALL]),
                       _pack([m[n] for n in SMALL]), _pack([v[n] for n in SMALL]))
    for n, dn, mn_, vn_ in zip(SMALL, _unpack(d2, small_shapes), _unpack(m2, small_shapes), _unpack(v2, small_shapes)):
        grads[n], delta[n], new_m[n], new_v[n] = g_small[n].reshape(w[n].shape), dn, mn_, vn_

    return (loss, grad_x, *[grads[n] for n in WEIGHTS], *[delta[n] for n in WEIGHTS],
            *[new_m[n] for n in WEIGHTS], *[new_v[n] for n in WEIGHTS])
```

```python
import jax
import jax.numpy as jnp
from jax import lax
from jax.experimental import pallas as pl
from jax.experimental.pallas import tpu as pltpu

F32, BF16 = jnp.float32, jnp.bfloat16
EPS = 1e-6
N_MEM_HEADS = 4
A_HEADS = 8
GMLP_BLOCK = 128
CHUNK = 64
POOL_WINDOWS = (2, 4, 8, 16)
N_CHIPS = 4
N_DEV = 8
HALO = 32
LANES = 128
TM, TN, TK = 512, 1024, 512
ROW_TILE = 256
PACK_ROWS = 512
VMEM_LIMIT = 48 * 1024 * 1024
ADAM_LR, ADAM_B1, ADAM_B2, ADAM_EPS, ADAM_WD, ADAM_STEP = 0.001, 0.9, 0.999, 1e-08, 0.01, 10
MESH = pl.DeviceIdType.MESH
HBM = pl.BlockSpec(memory_space=pltpu.HBM)
SEM = pl.BlockSpec(memory_space=pltpu.SEMAPHORE)
ANY = pl.BlockSpec(memory_space=pl.ANY)
EFFECT = pltpu.SideEffectType.DATAFLOW_SIDE_EFFECTING

WEIGHTS = ['norm_ffn1', 'ffn1_gate', 'ffn1_up', 'ffn1_down', 'norm_mix', 'ab_w_in', 'ab_b_in', 'gmlp_w_s',
           'gmlp_b_s', 'gmlp_ln_g', 'gmlp_ln_b', 'conv_w', 'conv_b', 'conv_ln_g', 'conv_ln_b', 'ab_w_out',
           'ab_b_out', 'pool_w', 'pool_b', 'pool_scale', 'norm_xq', 'norm_xkv', 'xattn_wq', 'xattn_wk',
           'xattn_wv', 'xattn_wo', 'norm_ffn2', 'ffn2_gate', 'ffn2_up', 'ffn2_down', 'norm_final']
BIG = ['ffn1_gate', 'ffn1_up', 'ffn1_down', 'ab_w_in', 'ab_w_out', 'pool_w', 'xattn_wq', 'xattn_wk', 'xattn_wv',
       'xattn_wo', 'ffn2_gate', 'ffn2_up', 'ffn2_down']
EVEN_ONLY, ODD_ONLY = ['ab_w_in', 'ab_w_out'], ['pool_w']
SMALL = [n for n in WEIGHTS if n not in BIG]
SMALL_SHARDED = ['conv_w', 'pool_b', 'pool_scale']

NN = (((1,), (0,)), ((), ()))
NT = (((1,), (1,)), ((), ()))
TN_ = (((0,), (0,)), ((), ()))
_DIMS = {'nn': NN, 'nt': NT, 'tn': TN_}


def _pick(n, pref, unit=LANES):
    if n <= pref:
        return n
    t = (pref // unit) * unit
    while t >= unit:
        if n % t == 0:
            return t
        t -= unit
    return n


def _sds(shape, dtype):
    return jax.ShapeDtypeStruct(tuple(shape), dtype)


def _layer_names(l):
    mix = EVEN_ONLY if l % 2 == 0 else ODD_ONLY
    return ['ffn1_gate', 'ffn1_up', 'ffn1_down'] + mix + ['xattn_wq', 'xattn_wk', 'xattn_wv', 'xattn_wo',
                                                          'ffn2_gate', 'ffn2_up', 'ffn2_down']


def _stack_index(name, l):
    return l // 2 if name in EVEN_ONLY + ODD_ONLY else l


def _mm(name, grid, ins, pairs, outs, acc_shapes, epilogue, extras=()):
    n_in, n_out = len(ins), len(outs)
    nk = grid[2]

    def body(*refs):
        in_refs, out_refs, acc_refs = refs[:n_in], refs[n_in:n_in + n_out], refs[n_in + n_out:]
        k = pl.program_id(2)

        @pl.when(k == 0)
        def _():
            for acc in acc_refs:
                acc[...] = jnp.zeros_like(acc)

        for ai, bi, mode, ci in pairs:
            a = in_refs[ai][...].astype(BF16)
            b = in_refs[bi][...].astype(BF16)
            acc_refs[ci][...] += lax.dot_general(a, b, _DIMS[mode], preferred_element_type=F32)

        @pl.when(k == nk - 1)
        def _():
            res = epilogue([acc[...] for acc in acc_refs], [in_refs[e][...] for e in extras])
            for o, r in zip(out_refs, res):
                o[...] = r.astype(o.dtype)

    return pl.pallas_call(
        body, name=name, grid=grid,
        in_specs=[s for _, s in ins], out_specs=[s for _, s in outs], out_shape=[s for s, _ in outs],
        scratch_shapes=[pltpu.VMEM(s, F32) for s in acc_shapes],
        compiler_params=pltpu.CompilerParams(dimension_semantics=("parallel", "parallel", "arbitrary"),
                                             vmem_limit_bytes=VMEM_LIMIT),
    )(*[a for a, _ in ins])


def _bs(shape, fn):
    return pl.BlockSpec(shape, fn)


def _rowwise(name, fn, ins, row_outs, acc_outs, tile):
    T = next(a.shape[0] for k, a in ins if k == 'row')
    n = T // tile
    per = tile // HALO if tile % HALO == 0 else 1
    last = T // HALO - 1
    in_specs = []
    for kind, a in ins:
        if kind == 'row':
            in_specs.append(pl.BlockSpec((tile, a.shape[1]), lambda i: (i, 0)))
        elif kind == 'prev':
            in_specs.append(pl.BlockSpec((HALO, a.shape[1]), lambda i: (jnp.maximum(i * per - 1, 0), 0)))
        elif kind == 'next':
            in_specs.append(pl.BlockSpec((HALO, a.shape[1]), lambda i: (jnp.minimum((i + 1) * per, last), 0)))
        else:
            in_specs.append(pl.BlockSpec(a.shape, lambda i, nd=a.ndim: (0,) * nd))
    n_in, n_row = len(ins), len(row_outs)
    out_shape = [_sds((T, c), dt) for c, dt in row_outs] + [_sds(s, F32) for s in acc_outs]
    out_specs = [pl.BlockSpec((tile, c), lambda i: (i, 0)) for c, _ in row_outs]
    out_specs += [pl.BlockSpec(s, lambda i, nd=len(s): (0,) * nd) for s in acc_outs]
    kinds = [k for k, _ in ins]

    def body(*refs):
        i = pl.program_id(0)
        vals = [r if k == 'cref' else r[...] for k, r in zip(kinds, refs[:n_in])]
        ro, ao = fn(i, *vals)
        for r, v in zip(refs[n_in:n_in + n_row], ro):
            r[...] = v.astype(r.dtype)
        for r, v in zip(refs[n_in + n_row:], ao):
            @pl.when(i == 0)
            def _(r=r, v=v):
                r[...] = v

            @pl.when(i > 0)
            def _(r=r, v=v):
                r[...] += v

    return pl.pallas_call(
        body, name=name, grid=(n,), in_specs=in_specs, out_specs=out_specs, out_shape=out_shape,
        compiler_params=pltpu.CompilerParams(dimension_semantics=("arbitrary",), vmem_limit_bytes=VMEM_LIMIT),
    )(*[a for _, a in ins])


def _rms(x, g):
    return x * lax.rsqrt(jnp.mean(x * x, axis=-1, keepdims=True) + EPS) * g


def _ln(x, g, b):
    mu = jnp.mean(x, axis=-1, keepdims=True)
    xc = x - mu
    var = jnp.mean(xc * xc, axis=-1, keepdims=True)
    return xc * lax.rsqrt(var + EPS) * g + b


def _gelu(x):
    return 0.5 * x * (1.0 + jnp.tanh(0.7978845608028654 * (x + 0.044715 * (x * x * x))))


def _silu(x):
    return x * jax.nn.sigmoid(x)


def _glu(a, g):
    return a * jax.nn.sigmoid(g)


def _row_ids(i, tile, rows):
    return i * tile + lax.broadcasted_iota(jnp.int32, (rows, 1), 0)


def _mesh_pos():
    return lax.axis_index("x"), lax.axis_index("y"), lax.axis_index("c")


def _other_chips(x, y):
    return [(1 - x, y), (x, 1 - y), (1 - x, 1 - y)]


def _remote(src, dst, send_sems, recv_sems, s, to):
    return pltpu.make_async_remote_copy(src_ref=src, dst_ref=dst, send_sem=send_sems.at[s], recv_sem=recv_sems.at[s],
                                        device_id=to, device_id_type=MESH)


def _gather_copies(refs, send_sems, recv_sems):
    x, y, c = _mesh_pos()
    me = 2 * x + y
    out = []
    for t, ref in enumerate(refs):
        rh = ref.shape[1] // 2
        half = pl.ds(c * rh, rh)
        for k, (cx, cy) in enumerate(_other_chips(x, y)):
            mine, theirs = ref.at[me, half], ref.at[2 * cx + cy, half]
            out.append((_remote(mine, mine, send_sems, recv_sems, 3 * t + k, (cx, cy, c)),
                        _remote(theirs, theirs, send_sems, recv_sems, 3 * t + k, (cx, cy, c))))
    return out


def _exchange_copies(refs, send_sems, recv_sems):
    x, y, c = _mesh_pos()
    n = len(refs) // 2
    out = []
    for t in range(n):
        part, land = refs[t], refs[n + t]
        for k, (cx, cy) in enumerate(_other_chips(x, y)):
            out.append((_remote(part.at[2 * cx + cy], land.at[k], send_sems, recv_sems, 3 * t + k, (cx, cy, c)),
                        _remote(land.at[k], land.at[k], send_sems, recv_sems, 3 * t + k, (cx, cy, c))))
    return out


def _split_start(name, thru, after, n_sems, copies):
    n = len(thru)

    def body(*refs):
        outs = refs[n + 1:]
        send_sems, recv_sems, thru_refs, token = outs[0], outs[1], outs[2:2 + n], outs[2 + n]
        for send, _ in copies(thru_refs, send_sems, recv_sems):
            send.start()
        token[...] = jnp.zeros_like(token)

    res = pl.pallas_call(
        body, name=name,
        out_shape=(pltpu.SemaphoreType.DMA((n_sems,)), pltpu.SemaphoreType.DMA((n_sems,)),
                   *[pltpu.HBM(b.shape, b.dtype) for b in thru], _sds((8, LANES), F32)),
        in_specs=[HBM] * n + [ANY],
        out_specs=(SEM, SEM, *[HBM] * n, pl.BlockSpec(memory_space=pltpu.VMEM)),
        input_output_aliases={i: 2 + i for i in range(n)},
        compiler_params=pltpu.CompilerParams(has_side_effects=EFFECT),
    )(*[pltpu.with_memory_space_constraint(b, pltpu.HBM) for b in thru], after)
    return res[0], res[1], list(res[2:2 + n]), res[2 + n]


def _split_wait(name, thru, send_sems, recv_sems, after, copies):
    n = len(thru)

    def body(*refs):
        sends, recvs, outs = refs[n], refs[n + 1], refs[n + 3:]
        for send, arrival in copies(outs, sends, recvs):
            send.wait_send()
            arrival.wait_recv()

    res = pl.pallas_call(
        body, name=name, out_shape=tuple(pltpu.HBM(b.shape, b.dtype) for b in thru),
        in_specs=[HBM] * n + [SEM, SEM, ANY], out_specs=tuple([HBM] * n),
        input_output_aliases={i: i for i in range(n)},
        compiler_params=pltpu.CompilerParams(has_side_effects=EFFECT),
    )(*thru, send_sems, recv_sems, after)
    return list(res)


def _forward_halves(name, bufs):
    n = len(bufs)

    def body(*refs):
        outs, send_sems, recv_sems = refs[n:2 * n], refs[2 * n], refs[2 * n + 1]
        x, y, c = _mesh_pos()
        sibling = (x, y, 1 - c)
        sends, arrivals = [], []
        for t, ref in enumerate(outs):
            rh = ref.shape[1] // 2
            mine, other = pl.ds(c * rh, rh), pl.ds((1 - c) * rh, rh)
            for k, (cx, cy) in enumerate(_other_chips(x, y)):
                landed, coming = ref.at[2 * cx + cy, mine], ref.at[2 * cx + cy, other]
                cp = _remote(landed, landed, send_sems, recv_sems, 3 * t + k, sibling)
                cp.start()
                sends.append(cp)
                arrivals.append(_remote(coming, coming, send_sems, recv_sems, 3 * t + k, sibling))
        for a in arrivals:
            a.wait_recv()
        for cp in sends:
            cp.wait_send()

    res = pl.pallas_call(
        body, name=name, out_shape=tuple(_sds(b.shape, b.dtype) for b in bufs),
        in_specs=[HBM] * n, out_specs=tuple([HBM] * n), input_output_aliases={i: i for i in range(n)},
        scratch_shapes=[pltpu.SemaphoreType.DMA((3 * n,)), pltpu.SemaphoreType.DMA((3 * n,))],
    )(*bufs)
    return list(res)


def _swap_halves(name, gs):
    n = len(gs)

    def body(*refs):
        ins, outs, send_sems, recv_sems = refs[:n], refs[n:2 * n], refs[2 * n], refs[2 * n + 1]
        x, y, c = _mesh_pos()
        cps = []
        for t, (g_ref, o_ref) in enumerate(zip(ins, outs)):
            rh = g_ref.shape[1] // 2
            cp = _remote(g_ref.at[:, pl.ds((1 - c) * rh, rh)], o_ref, send_sems, recv_sems, t, (x, y, 1 - c))
            cp.start()
            cps.append(cp)
        for cp in cps:
            cp.wait_recv()
        for cp in cps:
            cp.wait_send()

    res = pl.pallas_call(
        body, name=name, out_shape=tuple(_sds((g.shape[0], g.shape[1] // 2, g.shape[2]), g.dtype) for g in gs),
        in_specs=[HBM] * n, out_specs=tuple([HBM] * n),
        scratch_shapes=[pltpu.SemaphoreType.DMA((n,)), pltpu.SemaphoreType.DMA((n,))],
    )(*gs)
    return list(res)


def _join_halves(name, gfulls, idx):
    n = len(gfulls)

    def body(*refs):
        outs, send_sems, recv_sems = refs[n:2 * n], refs[2 * n], refs[2 * n + 1]
        x, y, c = _mesh_pos()
        cps, arrivals = [], []
        for t, ref in enumerate(outs):
            rh = ref.shape[1] // 2
            mine, theirs = ref.at[idx[t], pl.ds(c * rh, rh)], ref.at[idx[t], pl.ds((1 - c) * rh, rh)]
            cp = _remote(mine, mine, send_sems, recv_sems, t, (x, y, 1 - c))
            cp.start()
            cps.append(cp)
            arrivals.append(_remote(theirs, theirs, send_sems, recv_sems, t, (x, y, 1 - c)))
        for a in arrivals:
            a.wait_recv()
        for cp in cps:
            cp.wait_send()

    res = pl.pallas_call(
        body, name=name, out_shape=tuple(_sds(g.shape, g.dtype) for g in gfulls),
        in_specs=[HBM] * n, out_specs=tuple([HBM] * n), input_output_aliases={i: i for i in range(n)},
        scratch_shapes=[pltpu.SemaphoreType.DMA((n,)), pltpu.SemaphoreType.DMA((n,))],
    )(*gfulls)
    return list(res)


def _gather_all(name, buf):
    def body(b_ref, out_ref, send_sems, recv_sems, local_sem):
        x, y, c = _mesh_pos()
        me = 4 * x + 2 * y + c
        local = pltpu.make_async_copy(b_ref, out_ref.at[me], local_sem)
        local.start()
        peers = []
        for k in range(1, N_DEV):
            peers.append((1 - x if k & 4 else x, 1 - y if k & 2 else y, 1 - c if k & 1 else c))
        sends = []
        for k, peer in enumerate(peers):
            cp = _remote(b_ref, out_ref.at[me], send_sems, recv_sems, k, peer)
            cp.start()
            sends.append(cp)
        for k, (px, py, pc) in enumerate(peers):
            slot = out_ref.at[4 * px + 2 * py + pc]
            _remote(slot, slot, send_sems, recv_sems, k, (px, py, pc)).wait_recv()
        for cp in sends:
            cp.wait_send()
        local.wait()

    return pl.pallas_call(
        body, name=name, out_shape=_sds((N_DEV,) + buf.shape, buf.dtype), in_specs=[HBM], out_specs=HBM,
        scratch_shapes=[pltpu.SemaphoreType.DMA((N_DEV - 1,)), pltpu.SemaphoreType.DMA((N_DEV - 1,)),
                        pltpu.SemaphoreType.DMA],
    )(buf)


def _scalars(*vals):
    return jnp.stack([jnp.asarray(v, jnp.int32) for v in vals])


def _cast_slab(name, w3, li, chip):
    _, R, C = w3.shape
    tr = _pick(R, ROW_TILE, 16)

    def body(s_ref, w_ref, o_ref):
        o_ref[...] = w_ref[...].astype(o_ref.dtype)

    grid_spec = pltpu.PrefetchScalarGridSpec(
        num_scalar_prefetch=1, grid=(R // tr,),
        in_specs=[pl.BlockSpec((None, tr, C), lambda r, s: (li, r, 0))],
        out_specs=pl.BlockSpec((None, tr, C), lambda r, s: (s[0], r, 0)))
    return pl.pallas_call(
        body, name=name, grid_spec=grid_spec, out_shape=_sds((N_CHIPS, R, C), BF16),
        compiler_params=pltpu.CompilerParams(dimension_semantics=("arbitrary",), vmem_limit_bytes=VMEM_LIMIT),
    )(_scalars(chip), w3)


def _add_halves(name, g, recv, c):
    _, R, C = g.shape
    rh = R // 2
    tr = _pick(rh, 512, 16)
    nr = rh // tr

    def body(s_ref, g_ref, a_ref, o_ref):
        o_ref[...] = (g_ref[...].astype(F32) + a_ref[...].astype(F32)).astype(o_ref.dtype)

    blk = (None, tr, C)
    grid_spec = pltpu.PrefetchScalarGridSpec(
        num_scalar_prefetch=1, grid=(N_CHIPS, nr),
        in_specs=[pl.BlockSpec(blk, lambda j, r, s: (j, s[0] * nr + r, 0)),
                  pl.BlockSpec(blk, lambda j, r, s: (j, r, 0))],
        out_specs=pl.BlockSpec(blk, lambda j, r, s: (j, r, 0)))
    return pl.pallas_call(
        body, name=name, grid_spec=grid_spec, out_shape=_sds((N_CHIPS, rh, C), g.dtype),
        compiler_params=pltpu.CompilerParams(dimension_semantics=("arbitrary",) * 2, vmem_limit_bytes=VMEM_LIMIT),
    )(_scalars(c), g, recv)


def _sum_into(name, p, recv, gfull, li, chip, c):
    _, rh, C = p.shape
    tr = _pick(rh, 512, 16)
    nr = rh // tr

    def body(s_ref, p_ref, r_ref, g_ref, o_ref):
        acc = p_ref[...].astype(F32)
        for k in range(3):
            acc = acc + r_ref[k].astype(F32)
        o_ref[...] = acc

    grid_spec = pltpu.PrefetchScalarGridSpec(
        num_scalar_prefetch=1, grid=(nr,),
        in_specs=[pl.BlockSpec((None, tr, C), lambda r, s: (s[0], r, 0)),
                  pl.BlockSpec((3, tr, C), lambda r, s: (0, r, 0)), HBM],
        out_specs=pl.BlockSpec((None, tr, C), lambda r, s: (li, s[1] * nr + r, 0)))
    return pl.pallas_call(
        body, name=name, grid_spec=grid_spec, out_shape=_sds(gfull.shape, F32), input_output_aliases={3: 0},
        compiler_params=pltpu.CompilerParams(dimension_semantics=("arbitrary",), vmem_limit_bytes=VMEM_LIMIT),
    )(_scalars(chip, c), p, recv, gfull)


def _sum_slots(name, buf):
    _, n, _ = buf.shape

    def body(b_ref, o_ref):
        acc = b_ref[0]
        for k in range(1, N_DEV):
            acc = acc + b_ref[k]
        o_ref[...] = acc

    return pl.pallas_call(
        body, name=name, grid=(n // PACK_ROWS,), out_shape=_sds((n, LANES), F32),
        in_specs=[pl.BlockSpec((N_DEV, PACK_ROWS, LANES), lambda i: (0, i, 0))],
        out_specs=pl.BlockSpec((PACK_ROWS, LANES), lambda i: (i, 0)),
    )(buf)


def _adam_tile(i, w, g, m, v):
    m = ADAM_B1 * m + (1.0 - ADAM_B1) * g
    v = ADAM_B2 * v + (1.0 - ADAM_B2) * (g * g)
    m_hat = m / (1.0 - ADAM_B1 ** ADAM_STEP)
    v_hat = v / (1.0 - ADAM_B2 ** ADAM_STEP)
    delta = -ADAM_LR * (m_hat / (jnp.sqrt(v_hat) + ADAM_EPS) + ADAM_WD * w)
    return [delta, m, v], []


def _adam(name, w, g, m, v):
    rows, C = w.shape
    tile = _pick(rows, ROW_TILE, 8)
    return _rowwise(name, _adam_tile, [('row', w), ('row', g), ('row', m), ('row', v)], [(C, F32)] * 3, [], tile)


def _pack(arrs):
    flat = jnp.concatenate([a.reshape(-1).astype(F32) for a in arrs])
    unit = PACK_ROWS * LANES
    n = -(-flat.shape[0] // unit) * unit
    return jnp.pad(flat, (0, n - flat.shape[0])).reshape(-1, LANES)


def _unpack(buf, shapes):
    flat = buf.reshape(-1)
    out, off = [], 0
    for s in shapes:
        n = 1
        for d in s:
            n *= d
        out.append(flat[off:off + n].reshape(s))
        off += n
    return out


def _norm_fwd(name, x, g, tile):
    D = x.shape[1]
    return _rowwise(name, lambda i, xv, gv: ([_rms(xv, gv)], []), [('row', x), ('const', g.reshape(1, D))],
                    [(D, BF16)], [], tile)[0]


def _norm_bwd(name, x, g, dh, dxo, tile):
    D = x.shape[1]
    if dxo is None:
        def fn(i, xv, dhv, gv):
            _, vjp = jax.vjp(_rms, xv, gv)
            return [], [vjp(dhv)[1]]
        return _rowwise(name, fn, [('row', x), ('row', dh), ('const', g.reshape(1, D))], [], [(1, D)], tile)[0]

    def fn(i, xv, dhv, dxv, gv):
        _, vjp = jax.vjp(_rms, xv, gv)
        dx, dg = vjp(dhv)
        return [dxv + dx], [dg]
    return _rowwise(name, fn, [('row', x), ('row', dh), ('row', dxo), ('const', g.reshape(1, D))],
                    [(D, F32)], [(1, D)], tile)


def _ffn_fwd(tag, x, g, wg, wu, wd, tile):
    T, D = x.shape
    fs = wg.shape[2]
    F = N_CHIPS * fs
    tm, tk, tn = _pick(T, TM), _pick(D, TK), _pick(D, TN)
    h = _norm_fwd(f"{tag}_norm", x, g, tile)
    wspec = _bs((None, tk, fs), lambda i, j, k: (j, k, 0))
    ospec = _bs((tm, fs), lambda i, j, k: (i, j))

    def epi(accs, _):
        a, b = accs
        return [_silu(a) * b, a, b]

    s, a, b = _mm(f"{tag}_gateup", (T // tm, N_CHIPS, D // tk),
                  [(h, _bs((tm, tk), lambda i, j, k: (i, k))), (wg, wspec), (wu, wspec)],
                  [(0, 1, 'nn', 0), (0, 2, 'nn', 1)], [(_sds((T, F), BF16), ospec)] * 3, [(tm, fs)] * 2, epi)
    xo = _mm(f"{tag}_down", (T // tm, D // tn, N_CHIPS),
             [(s, _bs((tm, fs), lambda i, j, k: (i, k))), (wd, _bs((None, fs, tn), lambda i, j, k: (k, 0, j))),
              (x, _bs((tm, tn), lambda i, j, k: (i, j)))],
             [(0, 1, 'nn', 0)], [(_sds((T, D), F32), _bs((tm, tn), lambda i, j, k: (i, j)))], [(tm, tn)],
             lambda accs, ex: [ex[0] + 0.5 * accs[0]], extras=(2,))[0]
    return xo, (x, h, a, b, s)


def _ffn_bwd(tag, dxo, saved, g, wg, wu, wd, tile):
    x, h, a, b, s = saved
    T, D = x.shape
    fs = wg.shape[2]
    F = N_CHIPS * fs
    tm, tk, tn = _pick(T, TM), _pick(D, TK), _pick(D, TN)
    tkt = _pick(T, TK)

    def epi_ds(accs, ex):
        ds = 0.5 * accs[0]
        av, bv = ex[0].astype(F32), ex[1].astype(F32)
        sig = jax.nn.sigmoid(av)
        return [ds * bv * (sig * (1.0 + av * (1.0 - sig))), ds * (av * sig)]

    tspec = _bs((tm, fs), lambda i, j, k: (i, j))
    da, db = _mm(f"{tag}_ds", (T // tm, N_CHIPS, D // tk),
                 [(dxo, _bs((tm, tk), lambda i, j, k: (i, k))),
                  (wd, _bs((None, fs, tk), lambda i, j, k: (j, 0, k))), (a, tspec), (b, tspec)],
                 [(0, 1, 'nt', 0)], [(_sds((T, F), BF16), tspec)] * 2, [(tm, fs)], epi_ds, extras=(2, 3))
    gd = _mm(f"{tag}_dwd", (N_CHIPS, D // tn, T // tkt),
             [(s, _bs((tkt, fs), lambda i, j, k: (k, i))), (dxo, _bs((tkt, tn), lambda i, j, k: (k, j)))],
             [(0, 1, 'tn', 0)], [(_sds(wd.shape, BF16), _bs((None, fs, tn), lambda i, j, k: (i, 0, j)))],
             [(fs, tn)], lambda accs, _: [0.5 * accs[0]])[0]
    tmd = _pick(D, TM)
    gspec = _bs((None, tmd, fs), lambda i, j, k: (j, i, 0))
    gg, gu = _mm(f"{tag}_dwgu", (D // tmd, N_CHIPS, T // tkt),
                 [(h, _bs((tkt, tmd), lambda i, j, k: (k, i))), (da, _bs((tkt, fs), lambda i, j, k: (k, j))),
                  (db, _bs((tkt, fs), lambda i, j, k: (k, j)))],
                 [(0, 1, 'tn', 0), (0, 2, 'tn', 1)],
                 [(_sds(wg.shape, BF16), gspec), (_sds(wu.shape, BF16), gspec)], [(tmd, fs)] * 2,
                 lambda accs, _: accs)
    wtspec = _bs((None, tn, fs), lambda i, j, k: (k, j, 0))
    aspec = _bs((tm, fs), lambda i, j, k: (i, k))
    dh = _mm(f"{tag}_dh", (T // tm, D // tn, N_CHIPS),
             [(da, aspec), (wg, wtspec), (db, aspec), (wu, wtspec)], [(0, 1, 'nt', 0), (2, 3, 'nt', 0)],
             [(_sds((T, D), F32), _bs((tm, tn), lambda i, j, k: (i, j)))], [(tm, tn)], lambda accs, _: accs)[0]
    dx, dg = _norm_bwd(f"{tag}_norm_bwd", x, g, dh, dxo, tile)
    return dx, dg, gg, gu, gd


def _proj_rows(name, a, w, out_dtype, extras=(), epilogue=None):
    M, K = a.shape
    ks, N = w.shape[1], w.shape[2]
    tm, tn = _pick(M, TM), _pick(N, TN)
    ins = [(a, _bs((tm, ks), lambda i, j, k: (i, k))), (w, _bs((None, ks, tn), lambda i, j, k: (k, 0, j)))]
    for e in extras:
        if e.shape[0] == 1:
            ins.append((e, _bs((1, tn), lambda i, j, k: (0, j))))
        else:
            ins.append((e, _bs((tm, tn), lambda i, j, k: (i, j))))
    epi = epilogue or (lambda accs, ex: accs)
    return _mm(name, (M // tm, N // tn, N_CHIPS), ins, [(0, 1, 'nn', 0)],
               [(_sds((M, N), out_dtype), _bs((tm, tn), lambda i, j, k: (i, j)))], [(tm, tn)], epi,
               extras=tuple(range(2, 2 + len(extras))))[0]


def _proj_rows_t(name, pairs, out_dtype):
    dy0, w0 = pairs[0]
    M, N = dy0.shape
    ks = w0.shape[1]
    tm, tk = _pick(M, TM), _pick(N, TK)
    ins, prs = [], []
    for p, (dy, w) in enumerate(pairs):
        ins.append((dy, _bs((tm, tk), lambda i, j, k: (i, k))))
        ins.append((w, _bs((None, ks, tk), lambda i, j, k: (j, 0, k))))
        prs.append((2 * p, 2 * p + 1, 'nt', 0))
    return _mm(name, (M // tm, N_CHIPS, N // tk), ins, prs,
               [(_sds((M, N_CHIPS * ks), out_dtype), _bs((tm, ks), lambda i, j, k: (i, j)))], [(tm, ks)],
               lambda accs, _: accs)[0]


def _grad_rows(name, pairs):
    a = pairs[0][0]
    T, K = a.shape
    N = pairs[0][1].shape[1]
    ks = K // N_CHIPS
    tn, tk = _pick(N, TN), _pick(T, TK)
    ins = [(a, _bs((tk, ks), lambda i, j, k: (k, i)))]
    prs = []
    for p, (_, dy) in enumerate(pairs):
        ins.append((dy, _bs((tk, tn), lambda i, j, k: (k, j))))
        prs.append((0, 1 + p, 'tn', p))
    gspec = _bs((None, ks, tn), lambda i, j, k: (i, 0, j))
    return _mm(name, (N_CHIPS, N // tn, T // tk), ins, prs, [(_sds((N_CHIPS, ks, N), BF16), gspec)] * len(pairs),
               [(ks, tn)] * len(pairs), lambda accs, _: accs)


def _softmax_rows(s):
    s = s - jnp.max(s, axis=-1, keepdims=True)
    p = jnp.exp(s)
    return p / jnp.sum(p, axis=-1, keepdims=True)


def _attn_fwd_tile(hd, scale):
    def fn(i, q, k, v):
        outs = []
        for h in range(N_MEM_HEADS):
            sl = slice(h * hd, (h + 1) * hd)
            p = _softmax_rows(lax.dot_general(q[:, sl], k[:, sl], NT, preferred_element_type=F32) * scale)
            outs.append(lax.dot_general(p.astype(BF16), v[:, sl], NN, preferred_element_type=F32))
        return [jnp.concatenate(outs, axis=1)], []
    return fn


def _attn_bwd_tile(hd, scale):
    def fn(i, q, do, k, v):
        dqs, dks, dvs = [], [], []
        for h in range(N_MEM_HEADS):
            sl = slice(h * hd, (h + 1) * hd)
            qh, kh, vh, doh = q[:, sl], k[:, sl], v[:, sl], do[:, sl]
            p = _softmax_rows(lax.dot_general(qh, kh, NT, preferred_element_type=F32) * scale)
            dvs.append(lax.dot_general(p.astype(BF16), doh, TN_, preferred_element_type=F32))
            dp = lax.dot_general(doh, vh, NT, preferred_element_type=F32)
            ds = (p * (dp - jnp.sum(dp * p, axis=-1, keepdims=True)) * scale).astype(BF16)
            dqs.append(lax.dot_general(ds, kh, NN, preferred_element_type=F32))
            dks.append(lax.dot_general(ds, qh, TN_, preferred_element_type=F32))
        return [jnp.concatenate(dqs, axis=1)], [jnp.concatenate(dks, axis=1), jnp.concatenate(dvs, axis=1)]
    return fn


def _attn_fwd(l, x, mem, gq, gkv, W, tile):
    T, D = x.shape
    M = mem.shape[0]
    hd = D // N_MEM_HEADS
    hq = _norm_fwd(f"xa{l}_normq", x, gq, tile)
    mn = _norm_fwd(f"xa{l}_normkv", mem, gkv, _pick(M, tile, 16))
    q = _proj_rows(f"xa{l}_q", hq, W['xattn_wq'], BF16)
    k = _proj_rows(f"xa{l}_k", mn, W['xattn_wk'], BF16)
    v = _proj_rows(f"xa{l}_v", mn, W['xattn_wv'], BF16)
    o = _rowwise(f"xa{l}_attn", _attn_fwd_tile(hd, hd ** -0.5), [('row', q), ('const', k), ('const', v)],
                 [(D, BF16)], [], tile)[0]
    xo = _proj_rows(f"xa{l}_o", o, W['xattn_wo'], F32, extras=(x,), epilogue=lambda accs, ex: [ex[0] + accs[0]])
    return xo, (x, hq, mn, q, k, v, o)


def _attn_bwd(l, dxo, saved, mem, gq, gkv, W, G, tile):
    x, hq, mn, q, k, v, o = saved
    T, D = x.shape
    M = mem.shape[0]
    hd = D // N_MEM_HEADS
    do = _proj_rows_t(f"xa{l}_do", [(dxo, W['xattn_wo'])], BF16)
    G['xattn_wo'] = _grad_rows(f"xa{l}_dwo", [(o, dxo)])[0]
    dq, dk, dv = _rowwise(f"xa{l}_attn_bwd", _attn_bwd_tile(hd, hd ** -0.5),
                          [('row', q), ('row', do), ('const', k), ('const', v)], [(D, BF16)], [(M, D), (M, D)], tile)
    dhq = _proj_rows_t(f"xa{l}_dhq", [(dq, W['xattn_wq'])], F32)
    G['xattn_wq'] = _grad_rows(f"xa{l}_dwq", [(hq, dq)])[0]
    dmn = _proj_rows_t(f"xa{l}_dmn", [(dk, W['xattn_wk']), (dv, W['xattn_wv'])], F32)
    G['xattn_wk'], G['xattn_wv'] = _grad_rows(f"xa{l}_dwkv", [(mn, dk), (mn, dv)])
    dx, dgq = _norm_bwd(f"xa{l}_normq_bwd", x, gq, dhq, dxo, tile)
    dgkv = _norm_bwd(f"xa{l}_normkv_bwd", mem, gkv, dmn, None, _pick(M, tile, 16))
    return dx, dgq, dgkv


def _chunk_mask():
    p = lax.broadcasted_iota(jnp.int32, (GMLP_BLOCK, GMLP_BLOCK), 0)
    q = lax.broadcasted_iota(jnp.int32, (GMLP_BLOCK, GMLP_BLOCK), 1)
    return (q // CHUNK) <= (p // CHUNK)


def _spatial_fwd(vn, ws_ref, bsf, mask, hd):
    vb = vn.astype(BF16)
    wsm = [jnp.where(mask, ws_ref[h], 0.0).astype(BF16) for h in range(A_HEADS)]
    rows = []
    for n in range(vn.shape[0] // GMLP_BLOCK):
        blk = vb[n * GMLP_BLOCK:(n + 1) * GMLP_BLOCK]
        cols = [lax.dot_general(wsm[h], blk[:, h * hd:(h + 1) * hd], NN, preferred_element_type=F32)
                for h in range(A_HEADS)]
        rows.append(jnp.concatenate(cols, axis=1) + bsf)
    return jnp.concatenate(rows, axis=0)


def _spatial_bwd(dsp, vn, ws_ref, mask, hd):
    vb, db16 = vn.astype(BF16), dsp.astype(BF16)
    wsm = [jnp.where(mask, ws_ref[h], 0.0).astype(BF16) for h in range(A_HEADS)]
    dws = [jnp.zeros((GMLP_BLOCK, GMLP_BLOCK), F32) for _ in range(A_HEADS)]
    dbs = jnp.zeros((GMLP_BLOCK, vn.shape[1]), F32)
    rows = []
    for n in range(vn.shape[0] // GMLP_BLOCK):
        sl = slice(n * GMLP_BLOCK, (n + 1) * GMLP_BLOCK)
        cols = []
        for h in range(A_HEADS):
            hs = slice(h * hd, (h + 1) * hd)
            cols.append(lax.dot_general(wsm[h], db16[sl, hs], TN_, preferred_element_type=F32))
            dws[h] = dws[h] + lax.dot_general(db16[sl, hs], vb[sl, hs], NT, preferred_element_type=F32)
        rows.append(jnp.concatenate(cols, axis=1))
        dbs = dbs + dsp[sl]
    dws = [jnp.where(mask, d, 0.0) for d in dws]
    return jnp.concatenate(rows, axis=0), dws, dbs


def _conv_taps(cat, cw_ref, kw, tile):
    acc = jnp.zeros((tile, cat.shape[1]), F32)
    for k in range(kw):
        sh = kw - 1 - k
        r = cat if sh == 0 else pltpu.roll(cat, sh, 0)
        acc = acc + r[HALO:] * cw_ref[k:k + 1, :]
    return acc


def _mix_fwd_tile(A, B, kw, tile):
    hd = A // A_HEADS

    def fn(i, z, zp, ws_ref, bsf, glg, glb, cw_ref, cb, clg, clb):
        mask = _chunk_mask()
        u = _gelu(z[:, :A])
        vn = _ln(_gelu(z[:, A:2 * A]), glg, glb)
        ya = u * _spatial_fwd(vn, ws_ref, bsf, mask, hd)
        hb = _glu(z[:, 2 * A:2 * A + B], z[:, 2 * A + B:])
        hp = jnp.where(i > 0, _glu(zp[:, 2 * A:2 * A + B], zp[:, 2 * A + B:]), 0.0)
        conv = _conv_taps(jnp.concatenate([hp, hb], axis=0), cw_ref, kw, tile) + cb
        yb = _silu(_ln(conv, clg, clb))
        return [jnp.concatenate([ya, yb], axis=1)], []
    return fn


def _mix_bwd1_tile(A, B, kw, tile):
    hd = A // A_HEADS

    def fn(i, z, zp, dy, dxo, ws_ref, bsf, glg, glb, cw_ref, cb, clg, clb):
        mask = _chunk_mask()
        dya, dyb = dy[:, :A], dy[:, A:]
        zu, zv = z[:, :A], z[:, A:2 * A]
        u, vjp_u = jax.vjp(_gelu, zu)
        vn, vjp_v = jax.vjp(lambda t, g, b: _ln(_gelu(t), g, b), zv, glg, glb)
        sp = _spatial_fwd(vn, ws_ref, bsf, mask, hd)
        dzu = vjp_u(dya * sp)[0]
        dvn, dws, dbs = _spatial_bwd(dya * u, vn, ws_ref, mask, hd)
        dzv, dglg, dglb = vjp_v(dvn)
        hb = _glu(z[:, 2 * A:2 * A + B], z[:, 2 * A + B:])
        hp = jnp.where(i > 0, _glu(zp[:, 2 * A:2 * A + B], zp[:, 2 * A + B:]), 0.0)
        cat = jnp.concatenate([hp, hb], axis=0)
        conv = _conv_taps(cat, cw_ref, kw, tile) + cb
        _, vjp_c = jax.vjp(lambda t, g, b: _silu(_ln(t, g, b)), conv, clg, clb)
        dconv, dclg, dclb = vjp_c(dyb)
        tap = lax.broadcasted_iota(jnp.int32, (HALO, 1), 0)
        dcw = jnp.zeros((HALO, B), F32)
        for k in range(kw):
            sh = kw - 1 - k
            r = cat if sh == 0 else pltpu.roll(cat, sh, 0)
            dcw = dcw + jnp.where(tap == k, jnp.sum(dconv * r[HALO:], axis=0, keepdims=True), 0.0)
        dcb = jnp.sum(dconv, axis=0, keepdims=True)
        dbo = jnp.sum(dxo, axis=0, keepdims=True)
        dws = jnp.concatenate([d[None] for d in dws], axis=0)
        return [jnp.concatenate([dzu, dzv], axis=1), dconv], [dws, dbs, dglg, dglb, dcw, dcb, dclg, dclb, dbo]
    return fn


def _mix_bwd2_tile(A, B, kw, tile, n_tiles):
    def fn(i, z, dza, dc, dcn, cw_ref):
        dcn = jnp.where(i < n_tiles - 1, dcn, 0.0)
        cat = jnp.concatenate([dc, dcn], axis=0)
        n = tile + HALO
        dhb = jnp.zeros((tile, B), F32)
        for k in range(kw):
            sh = kw - 1 - k
            r = cat if sh == 0 else pltpu.roll(cat, n - sh, 0)
            dhb = dhb + r[:tile] * cw_ref[k:k + 1, :]
        _, vjp_g = jax.vjp(_glu, z[:, 2 * A:2 * A + B], z[:, 2 * A + B:])
        da, dg = vjp_g(dhb)
        dz = jnp.concatenate([dza, da, dg], axis=1)
        return [dz], [jnp.sum(dz, axis=0, keepdims=True)]
    return fn


def _even_consts(p, e, A, B, kw):
    hd = A // A_HEADS
    bsf = jnp.repeat(p['gmlp_b_s'][e].T, hd, axis=1)
    cw = jnp.pad(p['conv_w_full'][e], ((0, HALO - kw), (0, 0)))
    return [('cref', p['gmlp_w_s'][e]), ('const', bsf), ('const', p['gmlp_ln_g'][e].reshape(1, A)),
            ('const', p['gmlp_ln_b'][e].reshape(1, A)), ('cref', cw), ('const', p['conv_b'][e].reshape(1, B)),
            ('const', p['conv_ln_g'][e].reshape(1, B)), ('const', p['conv_ln_b'][e].reshape(1, B))]


def _even_fwd(l, e, x, gm, p, W, tile):
    T, D = x.shape
    w_in, w_out = W['ab_w_in'], W['ab_w_out']
    zs = w_in.shape[2]
    Z = N_CHIPS * zs
    A = p['gmlp_ln_g'].shape[1]
    B = p['conv_b'].shape[1]
    kw = p['conv_w_full'].shape[1]
    tm, tk = _pick(T, TM), _pick(D, TK)
    h = _norm_fwd(f"mix{l}_norm", x, gm, tile)
    z = _mm(f"mix{l}_in", (T // tm, N_CHIPS, D // tk),
            [(h, _bs((tm, tk), lambda i, j, k: (i, k))), (w_in, _bs((None, tk, zs), lambda i, j, k: (j, k, 0))),
             (p['ab_b_in'][e].reshape(1, Z), _bs((1, zs), lambda i, j, k: (0, j)))],
            [(0, 1, 'nn', 0)], [(_sds((T, Z), F32), _bs((tm, zs), lambda i, j, k: (i, j)))], [(tm, zs)],
            lambda accs, ex: [accs[0] + ex[0]], extras=(2,))[0]
    consts = _even_consts(p, e, A, B, kw)
    ycat = _rowwise(f"mix{l}_mid", _mix_fwd_tile(A, B, kw, tile), [('row', z), ('prev', z)] + consts,
                    [(A + B, BF16)], [], tile)[0]
    xo = _proj_rows(f"mix{l}_out", ycat, w_out, F32, extras=(x, p['ab_b_out'][e].reshape(1, D)),
                    epilogue=lambda accs, ex: [ex[0] + accs[0] + ex[1]])
    return xo, (x, h, z, ycat)


def _even_bwd(l, e, dxo, saved, gm, p, W, G, tile):
    x, h, z, ycat = saved
    T, D = x.shape
    w_in, w_out = W['ab_w_in'], W['ab_w_out']
    zs = w_in.shape[2]
    Z = N_CHIPS * zs
    A = p['gmlp_ln_g'].shape[1]
    B = p['conv_b'].shape[1]
    kw = p['conv_w_full'].shape[1]
    hd = A // A_HEADS
    dycat = _proj_rows_t(f"mix{l}_dycat", [(dxo, w_out)], F32)
    G['ab_w_out'] = _grad_rows(f"mix{l}_dwout", [(ycat, dxo)])[0]
    consts = _even_consts(p, e, A, B, kw)
    accs = [(A_HEADS, GMLP_BLOCK, GMLP_BLOCK), (GMLP_BLOCK, A), (1, A), (1, A), (HALO, B), (1, B), (1, B), (1, B),
            (1, D)]
    dza, dconv, dws, dbs, dglg, dglb, dcw, dcb, dclg, dclb, dbo = _rowwise(
        f"mix{l}_mid_bwd1", _mix_bwd1_tile(A, B, kw, tile),
        [('row', z), ('prev', z), ('row', dycat), ('row', dxo)] + consts, [(2 * A, F32), (B, F32)], accs, tile)
    dz, dbin = _rowwise(f"mix{l}_mid_bwd2", _mix_bwd2_tile(A, B, kw, tile, T // tile),
                        [('row', z), ('row', dza), ('row', dconv), ('next', dconv), consts[4]],
                        [(Z, BF16)], [(1, Z)], tile)
    tmd, tkt = _pick(D, TM), _pick(T, TK)
    G['ab_w_in'] = _mm(f"mix{l}_dwin", (D // tmd, N_CHIPS, T // tkt),
                       [(h, _bs((tkt, tmd), lambda i, j, k: (k, i))), (dz, _bs((tkt, zs), lambda i, j, k: (k, j)))],
                       [(0, 1, 'tn', 0)],
                       [(_sds(w_in.shape, BF16), _bs((None, tmd, zs), lambda i, j, k: (j, i, 0)))],
                       [(tmd, zs)], lambda a, _: a)[0]
    tm, tn = _pick(T, TM), _pick(D, TN)
    dh = _mm(f"mix{l}_dh", (T // tm, D // tn, N_CHIPS),
             [(dz, _bs((tm, zs), lambda i, j, k: (i, k))), (w_in, _bs((None, tn, zs), lambda i, j, k: (k, j, 0)))],
             [(0, 1, 'nt', 0)], [(_sds((T, D), F32), _bs((tm, tn), lambda i, j, k: (i, j)))], [(tm, tn)],
             lambda a, _: a)[0]
    dx, dgm = _norm_bwd(f"mix{l}_norm_bwd", x, gm, dh, dxo, tile)
    small = {'ab_b_in': dbin.reshape(Z), 'gmlp_w_s': dws, 'gmlp_b_s': dbs.reshape(GMLP_BLOCK, A_HEADS, hd).sum(-1).T,
             'gmlp_ln_g': dglg.reshape(A), 'gmlp_ln_b': dglb.reshape(A), 'conv_w': dcw[:kw], 'conv_b': dcb.reshape(B),
             'conv_ln_g': dclg.reshape(B), 'conv_ln_b': dclb.reshape(B), 'ab_b_out': dbo.reshape(D)}
    return dx, dgm, small


def _pool_counts(t, cg):
    return jnp.concatenate([jnp.broadcast_to(jnp.minimum(t + 1, w).astype(F32), (t.shape[0], cg))
                            for w in POOL_WINDOWS], axis=1)


def _window_sums(cat, cg, back):
    n = cat.shape[0]
    outs = []
    for gi, w in enumerate(POOL_WINDOWS):
        s = cat[:, gi * cg:(gi + 1) * cg]
        step = 1
        while step < w:
            s = s + pltpu.roll(s, step if back else n - step, 0)
            step *= 2
        outs.append(s)
    return jnp.concatenate(outs, axis=1)


def _pool_fwd_tile(D, tile):
    cg = D // len(POOL_WINDOWS)

    def fn(i, x, xp, g):
        h = _rms(x, g)
        hp = jnp.where(i > 0, _rms(xp, g), 0.0)
        sums = _window_sums(jnp.concatenate([hp, h], axis=0), cg, True)[HALO:]
        return [sums / _pool_counts(_row_ids(i, tile, tile), cg) - h], []
    return fn


def _pool_bwd_tile(D, tile, n_tiles):
    cg = D // len(POOL_WINDOWS)

    def fn(i, dd, ddn, x, dxo, g):
        e = dd / _pool_counts(_row_ids(i, tile, tile), cg)
        en = jnp.where(i < n_tiles - 1, ddn / _pool_counts(_row_ids(i + 1, tile, HALO), cg), 0.0)
        dh = _window_sums(jnp.concatenate([e, en], axis=0), cg, False)[:tile] - dd
        _, vjp = jax.vjp(_rms, x, g)
        dx, dg = vjp(dh)
        return [dxo + dx], [dg]
    return fn


def _odd_fwd(l, o, x, gm, p, W, tile):
    T, D = x.shape
    wc = W['pool_w']
    cg = wc.shape[2]
    cs = cg // N_CHIPS
    ng = len(POOL_WINDOWS)
    tm = _pick(T, TM)
    d = _rowwise(f"mix{l}_pool", _pool_fwd_tile(D, tile), [('row', x), ('prev', x), ('const', gm.reshape(1, D))],
                 [(D, BF16)], [], tile)[0]
    gspec = _bs((tm, cg), lambda i, j, k: (i, j))
    vspec = _bs((1, cg), lambda i, j, k: (0, j))

    def epi(accs, ex):
        pre = accs[0] + ex[0]
        return [ex[2] + pre * ex[1], pre]

    xo, pre = _mm(f"mix{l}_poolmm", (T // tm, ng, N_CHIPS),
                  [(d, _bs((tm, cs), lambda i, j, k: (i, j * N_CHIPS + k))),
                   (wc, _bs((None, cs, cg), lambda i, j, k: (k, j, 0))),
                   (p['pool_b_full'][o].reshape(1, D), vspec), (p['pool_scale_full'][o].reshape(1, D), vspec),
                   (x, gspec)],
                  [(0, 1, 'nn', 0)], [(_sds((T, D), F32), gspec)] * 2, [(tm, cg)], epi, extras=(2, 3, 4))
    return xo, (x, d, pre)


def _odd_bwd(l, o, dxo, saved, gm, p, W, G, tile):
    x, d, pre = saved
    T, D = x.shape
    wc = W['pool_w']
    cg = wc.shape[2]
    cs = cg // N_CHIPS
    ng = len(POOL_WINDOWS)
    tm, tkt = _pick(T, TM), _pick(T, TK)

    def fn(i, dxv, prev, sc):
        return [dxv * sc], [jnp.sum(dxv * prev, axis=0, keepdims=True), jnp.sum(dxv * sc, axis=0, keepdims=True)]

    do, dscale, dbc = _rowwise(f"mix{l}_pool_bwd1", fn,
                               [('row', dxo), ('row', pre), ('const', p['pool_scale_full'][o].reshape(1, D))],
                               [(D, BF16)], [(1, D), (1, D)], tile)
    nb = ng * N_CHIPS
    dd = _mm(f"mix{l}_pool_dd", (T // tm, nb, 1),
             [(do, _bs((tm, cg), lambda i, j, k: (i, j // N_CHIPS))),
              (wc, _bs((None, cs, cg), lambda i, j, k: (j % N_CHIPS, j // N_CHIPS, 0)))],
             [(0, 1, 'nt', 0)], [(_sds((T, D), F32), _bs((tm, cs), lambda i, j, k: (i, j)))], [(tm, cs)],
             lambda a, _: a)[0]
    G['pool_w'] = _mm(f"mix{l}_pool_dw", (nb, 1, T // tkt),
                      [(d, _bs((tkt, cs), lambda i, j, k: (k, i))), (do, _bs((tkt, cg), lambda i, j, k: (k, i // N_CHIPS)))],
                      [(0, 1, 'tn', 0)],
                      [(_sds(wc.shape, BF16), _bs((None, cs, cg), lambda i, j, k: (i % N_CHIPS, i // N_CHIPS, 0)))],
                      [(cs, cg)], lambda a, _: a)[0]
    dx, dgm = _rowwise(f"mix{l}_pool_bwd2", _pool_bwd_tile(D, tile, T // tile),
                       [('row', dd), ('next', dd), ('row', x), ('row', dxo), ('const', gm.reshape(1, D))],
                       [(D, F32)], [(1, D)], tile)
    small = {'pool_b': dbc.reshape(ng, cg), 'pool_scale': dscale.reshape(D)}
    return dx, dgm, small


def _final(x, g, tgt, tile):
    T, D = x.shape

    def fn(i, xv, tv, gv):
        y, vjp = jax.vjp(_rms, xv, gv)
        err = y - tv
        dx, dg = vjp(err / D)
        loss = 0.5 * jnp.sum(jnp.mean(err * err, axis=-1, keepdims=True), axis=0, keepdims=True)
        return [dx], [dg, jnp.broadcast_to(loss, (1, LANES))]

    dx, dg, loss = _rowwise("final", fn, [('row', x), ('row', tgt), ('const', g.reshape(1, D))],
                            [(D, F32)], [(1, D), (1, LANES)], tile)
    return dx, dg.reshape(D), loss[0, 0]


def _as3d(name, w):
    return w.reshape(w.shape[0], -1, w.shape[-1]) if name == 'pool_w' else w


def kernel(x, mem, norm_ffn1, ffn1_gate, ffn1_up, ffn1_down, norm_mix, ab_w_in, ab_b_in, gmlp_w_s, gmlp_b_s, gmlp_ln_g, gmlp_ln_b, conv_w, conv_b, conv_ln_g, conv_ln_b, ab_w_out, ab_b_out, pool_w, pool_b, pool_scale, norm_xq, norm_xkv, xattn_wq, xattn_wk, xattn_wv, xattn_wo, norm_ffn2, ffn2_gate, ffn2_up, ffn2_down, norm_final, loss_target, m_norm_ffn1, m_ffn1_gate, m_ffn1_up, m_ffn1_down, m_norm_mix, m_ab_w_in, m_ab_b_in, m_gmlp_w_s, m_gmlp_b_s, m_gmlp_ln_g, m_gmlp_ln_b, m_conv_w, m_conv_b, m_conv_ln_g, m_conv_ln_b, m_ab_w_out, m_ab_b_out, m_pool_w, m_pool_b, m_pool_scale, m_norm_xq, m_norm_xkv, m_xattn_wq, m_xattn_wk, m_xattn_wv, m_xattn_wo, m_norm_ffn2, m_ffn2_gate, m_ffn2_up, m_ffn2_down, m_norm_final, v_norm_ffn1, v_ffn1_gate, v_ffn1_up, v_ffn1_down, v_norm_mix, v_ab_w_in, v_ab_b_in, v_gmlp_w_s, v_gmlp_b_s, v_gmlp_ln_g, v_gmlp_ln_b, v_conv_w, v_conv_b, v_conv_ln_g, v_conv_ln_b, v_ab_w_out, v_ab_b_out, v_pool_w, v_pool_b, v_pool_scale, v_norm_xq, v_norm_xkv, v_xattn_wq, v_xattn_wk, v_xattn_wv, v_xattn_wo, v_norm_ffn2, v_ffn2_gate, v_ffn2_up, v_ffn2_down, v_norm_final):
    w = dict(zip(WEIGHTS, [norm_ffn1, ffn1_gate, ffn1_up, ffn1_down, norm_mix, ab_w_in, ab_b_in, gmlp_w_s, gmlp_b_s, gmlp_ln_g, gmlp_ln_b, conv_w, conv_b, conv_ln_g, conv_ln_b, ab_w_out, ab_b_out, pool_w, pool_b, pool_scale, norm_xq, norm_xkv, xattn_wq, xattn_wk, xattn_wv, xattn_wo, norm_ffn2, ffn2_gate, ffn2_up, ffn2_down, norm_final]))
    m = dict(zip(WEIGHTS, [m_norm_ffn1, m_ffn1_gate, m_ffn1_up, m_ffn1_down, m_norm_mix, m_ab_w_in, m_ab_b_in, m_gmlp_w_s, m_gmlp_b_s, m_gmlp_ln_g, m_gmlp_ln_b, m_conv_w, m_conv_b, m_conv_ln_g, m_conv_ln_b, m_ab_w_out, m_ab_b_out, m_pool_w, m_pool_b, m_pool_scale, m_norm_xq, m_norm_xkv, m_xattn_wq, m_xattn_wk, m_xattn_wv, m_xattn_wo, m_norm_ffn2, m_ffn2_gate, m_ffn2_up, m_ffn2_down, m_norm_final]))
    v = dict(zip(WEIGHTS, [v_norm_ffn1, v_ffn1_gate, v_ffn1_up, v_ffn1_down, v_norm_mix, v_ab_w_in, v_ab_b_in, v_gmlp_w_s, v_gmlp_b_s, v_gmlp_ln_g, v_gmlp_ln_b, v_conv_w, v_conv_b, v_conv_ln_g, v_conv_ln_b, v_ab_w_out, v_ab_b_out, v_pool_w, v_pool_b, v_pool_scale, v_norm_xq, v_norm_xkv, v_xattn_wq, v_xattn_wk, v_xattn_wv, v_xattn_wo, v_norm_ffn2, v_ffn2_gate, v_ffn2_up, v_ffn2_down, v_norm_final]))

    xs, mems, tgt = x[0], mem[0], loss_target[0]
    T, D = xs.shape
    L = norm_ffn1.shape[0]
    tile = _pick(T, ROW_TILE)
    cx, cy, cc = _mesh_pos()
    chip = 2 * cx + cy
    w3 = {n: _as3d(n, w[n]) for n in BIG}
    names = [_layer_names(l) for l in range(L)]

    sh_shapes = [w[n].shape for n in SMALL_SHARDED]
    slots = _gather_all("gather_small_shards", _pack([w[n] for n in SMALL_SHARDED]))
    per_chip = [_unpack(slots[2 * j], sh_shapes) for j in range(N_CHIPS)]
    full = {n: jnp.concatenate([per_chip[j][k] for j in range(N_CHIPS)], axis=-1) for k, n in enumerate(SMALL_SHARDED)}
    p = dict(w)
    p['conv_w_full'] = full['conv_w'].reshape(full['conv_w'].shape[0], full['conv_w'].shape[1], -1)
    p['pool_b_full'] = full['pool_b']
    p['pool_scale_full'] = full['pool_scale']

    token = jnp.zeros((8, LANES), F32)
    in_flight = []
    for l in range(L):
        slabs = [_cast_slab(f"cast_{n}_{l}", w3[n], _stack_index(n, l), chip) for n in names[l]]
        sends, arrivals, slabs, token = _split_start(f"gather_start_{l}", slabs, token, 3 * len(slabs), _gather_copies)
        in_flight.append((sends, arrivals, slabs))

    saved, Wl = [], []
    xc = xs + token[0, 0]
    for l in range(L):
        sends, arrivals, slabs = in_flight[l]
        slabs = _split_wait(f"gather_wait_{l}", slabs, sends, arrivals, xc, _gather_copies)
        W = dict(zip(names[l], _forward_halves(f"gather_fwd_{l}", slabs)))
        Wl.append(W)
        s = {}
        xc, s['ffn1'] = _ffn_fwd(f"ffn1_{l}", xc, w['norm_ffn1'][l], W['ffn1_gate'], W['ffn1_up'], W['ffn1_down'], tile)
        if l % 2 == 0:
            xc, s['mix'] = _even_fwd(l, l // 2, xc, w['norm_mix'][l], p, W, tile)
        else:
            xc, s['mix'] = _odd_fwd(l, l // 2, xc, w['norm_mix'][l], p, W, tile)
        xc, s['xa'] = _attn_fwd(l, xc, mems, w['norm_xq'][l], w['norm_xkv'][l], W, tile)
        xc, s['ffn2'] = _ffn_fwd(f"ffn2_{l}", xc, w['norm_ffn2'][l], W['ffn2_gate'], W['ffn2_up'], W['ffn2_down'], tile)
        saved.append(s)

    dx, g_final, loss_local = _final(xc, w['norm_final'], tgt, tile)
    loss = lax.psum(loss_local, ("x", "y", "c"))
    gfull = {n: lax.empty(w3[n].shape, F32) for n in BIG}
    gs = {n: [None] * w[n].shape[0] for n in SMALL if n != 'norm_final'}

    def finish_exchange(pending, after):
        l, sends, arrivals, thru = pending
        ns = names[l]
        thru = _split_wait(f"rs_wait_{l}", thru, sends, arrivals, after, _exchange_copies)
        parts, lands = thru[:len(ns)], thru[len(ns):]
        for n, part, land in zip(ns, parts, lands):
            gfull[n] = _sum_into(f"rs_sum_{n}_{l}", part, land, gfull[n], _stack_index(n, l), chip, cc)
        joined = _join_halves(f"rs_join_{l}", [gfull[n] for n in ns], [_stack_index(n, l) for n in ns])
        gfull.update(zip(ns, joined))

    pending = None
    for l in reversed(range(L)):
        s, W, G = saved[l], Wl[l], {}
        dx, dg, G['ffn2_gate'], G['ffn2_up'], G['ffn2_down'] = _ffn_bwd(
            f"ffn2_{l}", dx, s['ffn2'], w['norm_ffn2'][l], W['ffn2_gate'], W['ffn2_up'], W['ffn2_down'], tile)
        gs['norm_ffn2'][l] = dg.reshape(D)
        dx, dgq, dgkv = _attn_bwd(l, dx, s['xa'], mems, w['norm_xq'][l], w['norm_xkv'][l], W, G, tile)
        gs['norm_xq'][l], gs['norm_xkv'][l] = dgq.reshape(D), dgkv.reshape(D)
        if l % 2 == 0:
            dx, dgm, small = _even_bwd(l, l // 2, dx, s['mix'], w['norm_mix'][l], p, W, G, tile)
        else:
            dx, dgm, small = _odd_bwd(l, l // 2, dx, s['mix'], w['norm_mix'][l], p, W, G, tile)
        for n, val in small.items():
            gs[n][l // 2] = val
        gs['norm_mix'][l] = dgm.reshape(D)
        dx, dg, G['ffn1_gate'], G['ffn1_up'], G['ffn1_down'] = _ffn_bwd(
            f"ffn1_{l}", dx, s['ffn1'], w['norm_ffn1'][l], W['ffn1_gate'], W['ffn1_up'], W['ffn1_down'], tile)
        gs['norm_ffn1'][l] = dg.reshape(D)
        if pending is not None:
            finish_exchange(pending, dx)
        grads_l = [G[n] for n in names[l]]
        got = _swap_halves(f"rs_swap_{l}", grads_l)
        parts = [_add_halves(f"rs_add_{n}_{l}", g, r, cc) for n, g, r in zip(names[l], grads_l, got)]
        lands = [lax.empty((3,) + part.shape[1:], BF16) for part in parts]
        sends, arrivals, thru, token = _split_start(f"rs_start_{l}", parts + lands, dx, 3 * len(parts), _exchange_copies)
        dx = dx + token[0, 0]
        pending = (l, sends, arrivals, thru)
    finish_exchange(pending, dx)
    grad_x = dx[None]

    small_full = {n: jnp.stack(gs[n]) for n in gs}
    small_full['norm_final'] = g_final
    full_shapes = [small_full[n].shape for n in SMALL]
    summed = _sum_slots("sum_small", _gather_all("gather_small_grads", _pack([small_full[n] for n in SMALL])))
    g_small = dict(zip(SMALL, _unpack(summed, full_shapes)))
    for n in SMALL_SHARDED:
        width = w[n].shape[-1]
        g_small[n] = lax.dynamic_slice_in_dim(g_small[n], chip * width, width, axis=g_small[n].ndim - 1).reshape(w[n].shape)

    grads, delta, new_m, new_v = {}, {}, {}, {}
    for n in BIG:
        g3 = gfull[n]
        shape2 = (-1, g3.shape[-1])
        d2, m2, v2 = _adam(f"adam_{n}", w3[n].reshape(shape2), g3.reshape(shape2),
                           _as3d(n, m[n]).reshape(shape2), _as3d(n, v[n]).reshape(shape2))
        grads[n], delta[n], new_m[n], new_v[n] = (t.reshape(w[n].shape) for t in (g3, d2, m2, v2))
    small_shapes = [w[n].shape for n in SMALL]
    d2, m2, v2 = _adam("adam_small", _pack([w[n] for n in SMALL]), _pack([g_small[n] for n in SMALL]),
                       _pack([m[n] for n in SMALL]), _pack([v[n] for n in SMALL]))
    for n, dn, mn_, vn_ in zip(SMALL, _unpack(d2, small_shapes), _unpack(m2, small_shapes), _unpack(v2, small_shapes)):
        grads[n], delta[n], new_m[n], new_v[n] = g_small[n].reshape(w[n].shape), dn, mn_, vn_

    return (loss, grad_x, *[grads[n] for n in WEIGHTS], *[delta[n] for n in WEIGHTS],
            *[new_m[n] for n in WEIGHTS], *[new_v[n] for n in WEIGHTS])
```

```python
import jax
import jax.numpy as jnp
from jax import lax
from jax.experimental import pallas as pl
from jax.experimental.pallas import tpu as pltpu

F32, BF16 = jnp.float32, jnp.bfloat16
EPS = 1e-6
N_MEM_HEADS = 4
A_HEADS = 8
GMLP_BLOCK = 128
CHUNK = 64
POOL_WINDOWS = (2, 4, 8, 16)
N_CHIPS = 4
N_DEV = 8
HALO = 32
LANES = 128
TM, TN, TK = 512, 1024, 512
TM_BIG, TM_MID = 1024, 512
TN_BIG, TN_MID, TN_SMALL = 1024, 512, 256
EPI_ROWS = 256
ROW_TILE = 256
PACK_ROWS = 512
VMEM_LIMIT = 48 * 1024 * 1024
ADAM_LR, ADAM_B1, ADAM_B2, ADAM_EPS, ADAM_WD, ADAM_STEP = 0.001, 0.9, 0.999, 1e-08, 0.01, 10
MESH = pl.DeviceIdType.MESH
HBM = pl.BlockSpec(memory_space=pltpu.HBM)
SEM = pl.BlockSpec(memory_space=pltpu.SEMAPHORE)
ANY = pl.BlockSpec(memory_space=pl.ANY)
EFFECT = pltpu.SideEffectType.DATAFLOW_SIDE_EFFECTING

WEIGHTS = ['norm_ffn1', 'ffn1_gate', 'ffn1_up', 'ffn1_down', 'norm_mix', 'ab_w_in', 'ab_b_in', 'gmlp_w_s',
           'gmlp_b_s', 'gmlp_ln_g', 'gmlp_ln_b', 'conv_w', 'conv_b', 'conv_ln_g', 'conv_ln_b', 'ab_w_out',
           'ab_b_out', 'pool_w', 'pool_b', 'pool_scale', 'norm_xq', 'norm_xkv', 'xattn_wq', 'xattn_wk',
           'xattn_wv', 'xattn_wo', 'norm_ffn2', 'ffn2_gate', 'ffn2_up', 'ffn2_down', 'norm_final']
BIG = ['ffn1_gate', 'ffn1_up', 'ffn1_down', 'ab_w_in', 'ab_w_out', 'pool_w', 'xattn_wq', 'xattn_wk', 'xattn_wv',
       'xattn_wo', 'ffn2_gate', 'ffn2_up', 'ffn2_down']
EVEN_ONLY, ODD_ONLY = ['ab_w_in', 'ab_w_out'], ['pool_w']
SMALL = [n for n in WEIGHTS if n not in BIG]
SMALL_SHARDED = ['conv_w', 'pool_b', 'pool_scale']

NN = (((1,), (0,)), ((), ()))
NT = (((1,), (1,)), ((), ()))
TN_ = (((0,), (0,)), ((), ()))
_DIMS = {'nn': NN, 'nt': NT, 'tn': TN_}


def _pick(n, pref, unit=LANES):
    if n <= pref:
        return n
    t = (pref // unit) * unit
    while t >= unit:
        if n % t == 0:
            return t
        t -= unit
    return n


def _sds(shape, dtype):
    return jax.ShapeDtypeStruct(tuple(shape), dtype)


def _layer_names(l):
    mix = EVEN_ONLY if l % 2 == 0 else ODD_ONLY
    return ['ffn1_gate', 'ffn1_up', 'ffn1_down'] + mix + ['xattn_wq', 'xattn_wk', 'xattn_wv', 'xattn_wo',
                                                          'ffn2_gate', 'ffn2_up', 'ffn2_down']


def _stack_index(name, l):
    return l // 2 if name in EVEN_ONLY + ODD_ONLY else l


def _mm(name, grid, ins, pairs, outs, acc_shapes, epilogue, extras=()):
    n_in, n_out = len(ins), len(outs)
    nk = grid[2]

    def body(*refs):
        in_refs, out_refs, acc_refs = refs[:n_in], refs[n_in:n_in + n_out], refs[n_in + n_out:]
        k = pl.program_id(2)

        @pl.when(k == 0)
        def _():
            for acc in acc_refs:
                acc[...] = jnp.zeros_like(acc)

        for ai, bi, mode, ci in pairs:
            a = in_refs[ai][...].astype(BF16)
            b = in_refs[bi][...].astype(BF16)
            acc_refs[ci][...] += lax.dot_general(a, b, _DIMS[mode], preferred_element_type=F32)

        @pl.when(k == nk - 1)
        def _():
            res = epilogue([acc[...] for acc in acc_refs], [in_refs[e][...] for e in extras])
            for o, r in zip(out_refs, res):
                o[...] = r.astype(o.dtype)

    return pl.pallas_call(
        body, name=name, grid=grid,
        in_specs=[s for _, s in ins], out_specs=[s for _, s in outs], out_shape=[s for s, _ in outs],
        scratch_shapes=[pltpu.VMEM(s, F32) for s in acc_shapes],
        compiler_params=pltpu.CompilerParams(dimension_semantics=("parallel", "parallel", "arbitrary"),
                                             vmem_limit_bytes=VMEM_LIMIT),
    )(*[a for a, _ in ins])


def _bs(shape, fn):
    return pl.BlockSpec(shape, fn)


def _mm1(name, grid, ins, outs, compute):
    n_in = len(ins)

    def body(*refs):
        compute(refs[:n_in], refs[n_in:])

    return pl.pallas_call(
        body, name=name, grid=grid,
        in_specs=[s for _, s in ins], out_specs=[s for _, s in outs], out_shape=[s for s, _ in outs],
        compiler_params=pltpu.CompilerParams(dimension_semantics=("parallel", "parallel"),
                                             vmem_limit_bytes=VMEM_LIMIT),
    )(*[a for a, _ in ins])


def _dot(a, b, mode):
    return lax.dot_general(a.astype(BF16), b.astype(BF16), _DIMS[mode], preferred_element_type=F32)


def _row_chunks(rows):
    step = min(rows, EPI_ROWS)
    return [slice(r, r + step) for r in range(0, rows, step)]


def _rowwise(name, fn, ins, row_outs, acc_outs, tile):
    T = next(a.shape[0] for k, a in ins if k == 'row')
    n = T // tile
    per = tile // HALO if tile % HALO == 0 else 1
    last = T // HALO - 1
    in_specs = []
    for kind, a in ins:
        if kind == 'row':
            in_specs.append(pl.BlockSpec((tile, a.shape[1]), lambda i: (i, 0)))
        elif kind == 'prev':
            in_specs.append(pl.BlockSpec((HALO, a.shape[1]), lambda i: (jnp.maximum(i * per - 1, 0), 0)))
        elif kind == 'next':
            in_specs.append(pl.BlockSpec((HALO, a.shape[1]), lambda i: (jnp.minimum((i + 1) * per, last), 0)))
        else:
            in_specs.append(pl.BlockSpec(a.shape, lambda i, nd=a.ndim: (0,) * nd))
    n_in, n_row = len(ins), len(row_outs)
    out_shape = [_sds((T, c), dt) for c, dt in row_outs] + [_sds(s, F32) for s in acc_outs]
    out_specs = [pl.BlockSpec((tile, c), lambda i: (i, 0)) for c, _ in row_outs]
    out_specs += [pl.BlockSpec(s, lambda i, nd=len(s): (0,) * nd) for s in acc_outs]
    kinds = [k for k, _ in ins]

    def body(*refs):
        i = pl.program_id(0)
        vals = [r if k == 'cref' else r[...] for k, r in zip(kinds, refs[:n_in])]
        ro, ao = fn(i, *vals)
        for r, v in zip(refs[n_in:n_in + n_row], ro):
            r[...] = v.astype(r.dtype)
        for r, v in zip(refs[n_in + n_row:], ao):
            @pl.when(i == 0)
            def _(r=r, v=v):
                r[...] = v

            @pl.when(i > 0)
            def _(r=r, v=v):
                r[...] += v

    return pl.pallas_call(
        body, name=name, grid=(n,), in_specs=in_specs, out_specs=out_specs, out_shape=out_shape,
        compiler_params=pltpu.CompilerParams(dimension_semantics=("arbitrary",), vmem_limit_bytes=VMEM_LIMIT),
    )(*[a for _, a in ins])


def _rms(x, g):
    return x * lax.rsqrt(jnp.mean(x * x, axis=-1, keepdims=True) + EPS) * g


def _ln(x, g, b):
    mu = jnp.mean(x, axis=-1, keepdims=True)
    xc = x - mu
    var = jnp.mean(xc * xc, axis=-1, keepdims=True)
    return xc * lax.rsqrt(var + EPS) * g + b


def _gelu(x):
    return 0.5 * x * (1.0 + jnp.tanh(0.7978845608028654 * (x + 0.044715 * (x * x * x))))


def _silu(x):
    return x * jax.nn.sigmoid(x)


def _glu(a, g):
    return a * jax.nn.sigmoid(g)


def _row_ids(i, tile, rows):
    return i * tile + lax.broadcasted_iota(jnp.int32, (rows, 1), 0)


def _mesh_pos():
    return lax.axis_index("x"), lax.axis_index("y"), lax.axis_index("c")


def _other_chips(x, y):
    return [(1 - x, y), (x, 1 - y), (1 - x, 1 - y)]


def _remote(src, dst, send_sems, recv_sems, s, to):
    return pltpu.make_async_remote_copy(src_ref=src, dst_ref=dst, send_sem=send_sems.at[s], recv_sem=recv_sems.at[s],
                                        device_id=to, device_id_type=MESH)


def _gather_copies(refs, send_sems, recv_sems):
    x, y, c = _mesh_pos()
    me = 2 * x + y
    out = []
    for t, ref in enumerate(refs):
        rh = ref.shape[1] // 2
        half = pl.ds(c * rh, rh)
        for k, (cx, cy) in enumerate(_other_chips(x, y)):
            mine, theirs = ref.at[me, half], ref.at[2 * cx + cy, half]
            out.append((_remote(mine, mine, send_sems, recv_sems, 3 * t + k, (cx, cy, c)),
                        _remote(theirs, theirs, send_sems, recv_sems, 3 * t + k, (cx, cy, c))))
    return out


def _exchange_copies(refs, send_sems, recv_sems):
    x, y, c = _mesh_pos()
    n = len(refs) // 2
    out = []
    for t in range(n):
        part, land = refs[t], refs[n + t]
        for k, (cx, cy) in enumerate(_other_chips(x, y)):
            out.append((_remote(part.at[2 * cx + cy], land.at[k], send_sems, recv_sems, 3 * t + k, (cx, cy, c)),
                        _remote(land.at[k], land.at[k], send_sems, recv_sems, 3 * t + k, (cx, cy, c))))
    return out


def _split_start(name, thru, after, n_sems, copies):
    n = len(thru)

    def body(*refs):
        outs = refs[n + 1:]
        send_sems, recv_sems, thru_refs, token = outs[0], outs[1], outs[2:2 + n], outs[2 + n]
        for send, _ in copies(thru_refs, send_sems, recv_sems):
            send.start()
        token[...] = jnp.zeros_like(token)

    res = pl.pallas_call(
        body, name=name,
        out_shape=(pltpu.SemaphoreType.DMA((n_sems,)), pltpu.SemaphoreType.DMA((n_sems,)),
                   *[pltpu.HBM(b.shape, b.dtype) for b in thru], _sds((8, LANES), F32)),
        in_specs=[HBM] * n + [ANY],
        out_specs=(SEM, SEM, *[HBM] * n, pl.BlockSpec(memory_space=pltpu.VMEM)),
        input_output_aliases={i: 2 + i for i in range(n)},
        compiler_params=pltpu.CompilerParams(has_side_effects=EFFECT),
    )(*[pltpu.with_memory_space_constraint(b, pltpu.HBM) for b in thru], after)
    return res[0], res[1], list(res[2:2 + n]), res[2 + n]


def _split_wait(name, thru, send_sems, recv_sems, after, copies):
    n = len(thru)

    def body(*refs):
        sends, recvs, outs = refs[n], refs[n + 1], refs[n + 3:]
        for send, arrival in copies(outs, sends, recvs):
            send.wait_send()
            arrival.wait_recv()

    res = pl.pallas_call(
        body, name=name, out_shape=tuple(pltpu.HBM(b.shape, b.dtype) for b in thru),
        in_specs=[HBM] * n + [SEM, SEM, ANY], out_specs=tuple([HBM] * n),
        input_output_aliases={i: i for i in range(n)},
        compiler_params=pltpu.CompilerParams(has_side_effects=EFFECT),
    )(*thru, send_sems, recv_sems, after)
    return list(res)


def _forward_halves(name, bufs):
    n = len(bufs)

    def body(*refs):
        outs, send_sems, recv_sems = refs[n:2 * n], refs[2 * n], refs[2 * n + 1]
        x, y, c = _mesh_pos()
        sibling = (x, y, 1 - c)
        sends, arrivals = [], []
        for t, ref in enumerate(outs):
            rh = ref.shape[1] // 2
            mine, other = pl.ds(c * rh, rh), pl.ds((1 - c) * rh, rh)
            for k, (cx, cy) in enumerate(_other_chips(x, y)):
                landed, coming = ref.at[2 * cx + cy, mine], ref.at[2 * cx + cy, other]
                cp = _remote(landed, landed, send_sems, recv_sems, 3 * t + k, sibling)
                cp.start()
                sends.append(cp)
                arrivals.append(_remote(coming, coming, send_sems, recv_sems, 3 * t + k, sibling))
        for a in arrivals:
            a.wait_recv()
        for cp in sends:
            cp.wait_send()

    res = pl.pallas_call(
        body, name=name, out_shape=tuple(_sds(b.shape, b.dtype) for b in bufs),
        in_specs=[HBM] * n, out_specs=tuple([HBM] * n), input_output_aliases={i: i for i in range(n)},
        scratch_shapes=[pltpu.SemaphoreType.DMA((3 * n,)), pltpu.SemaphoreType.DMA((3 * n,))],
    )(*bufs)
    return list(res)


def _swap_halves(name, gs):
    n = len(gs)

    def body(*refs):
        ins, outs, send_sems, recv_sems = refs[:n], refs[n:2 * n], refs[2 * n], refs[2 * n + 1]
        x, y, c = _mesh_pos()
        cps = []
        for t, (g_ref, o_ref) in enumerate(zip(ins, outs)):
            rh = g_ref.shape[1] // 2
            cp = _remote(g_ref.at[:, pl.ds((1 - c) * rh, rh)], o_ref, send_sems, recv_sems, t, (x, y, 1 - c))
            cp.start()
            cps.append(cp)
        for cp in cps:
            cp.wait_recv()
        for cp in cps:
            cp.wait_send()

    res = pl.pallas_call(
        body, name=name, out_shape=tuple(_sds((g.shape[0], g.shape[1] // 2, g.shape[2]), g.dtype) for g in gs),
        in_specs=[HBM] * n, out_specs=tuple([HBM] * n),
        scratch_shapes=[pltpu.SemaphoreType.DMA((n,)), pltpu.SemaphoreType.DMA((n,))],
    )(*gs)
    return list(res)


def _join_halves(name, gfulls, idx):
    n = len(gfulls)

    def body(*refs):
        outs, send_sems, recv_sems = refs[n:2 * n], refs[2 * n], refs[2 * n + 1]
        x, y, c = _mesh_pos()
        cps, arrivals = [], []
        for t, ref in enumerate(outs):
            rh = ref.shape[1] // 2
            mine, theirs = ref.at[idx[t], pl.ds(c * rh, rh)], ref.at[idx[t], pl.ds((1 - c) * rh, rh)]
            cp = _remote(mine, mine, send_sems, recv_sems, t, (x, y, 1 - c))
            cp.start()
            cps.append(cp)
            arrivals.append(_remote(theirs, theirs, send_sems, recv_sems, t, (x, y, 1 - c)))
        for a in arrivals:
            a.wait_recv()
        for cp in cps:
            cp.wait_send()

    res = pl.pallas_call(
        body, name=name, out_shape=tuple(_sds(g.shape, g.dtype) for g in gfulls),
        in_specs=[HBM] * n, out_specs=tuple([HBM] * n), input_output_aliases={i: i for i in range(n)},
        scratch_shapes=[pltpu.SemaphoreType.DMA((n,)), pltpu.SemaphoreType.DMA((n,))],
    )(*gfulls)
    return list(res)


def _gather_all(name, buf):
    def body(b_ref, out_ref, send_sems, recv_sems, local_sem):
        x, y, c = _mesh_pos()
        me = 4 * x + 2 * y + c
        local = pltpu.make_async_copy(b_ref, out_ref.at[me], local_sem)
        local.start()
        peers = []
        for k in range(1, N_DEV):
            peers.append((1 - x if k & 4 else x, 1 - y if k & 2 else y, 1 - c if k & 1 else c))
        sends = []
        for k, peer in enumerate(peers):
            cp = _remote(b_ref, out_ref.at[me], send_sems, recv_sems, k, peer)
            cp.start()
            sends.append(cp)
        for k, (px, py, pc) in enumerate(peers):
            slot = out_ref.at[4 * px + 2 * py + pc]
            _remote(slot, slot, send_sems, recv_sems, k, (px, py, pc)).wait_recv()
        for cp in sends:
            cp.wait_send()
        local.wait()

    return pl.pallas_call(
        body, name=name, out_shape=_sds((N_DEV,) + buf.shape, buf.dtype), in_specs=[HBM], out_specs=HBM,
        scratch_shapes=[pltpu.SemaphoreType.DMA((N_DEV - 1,)), pltpu.SemaphoreType.DMA((N_DEV - 1,)),
                        pltpu.SemaphoreType.DMA],
    )(buf)


def _scalars(*vals):
    return jnp.stack([jnp.asarray(v, jnp.int32) for v in vals])


def _cast_slab(name, w3, li, chip):
    _, R, C = w3.shape
    tr = _pick(R, ROW_TILE, 16)

    def body(s_ref, w_ref, o_ref):
        o_ref[...] = w_ref[...].astype(o_ref.dtype)

    grid_spec = pltpu.PrefetchScalarGridSpec(
        num_scalar_prefetch=1, grid=(R // tr,),
        in_specs=[pl.BlockSpec((None, tr, C), lambda r, s: (li, r, 0))],
        out_specs=pl.BlockSpec((None, tr, C), lambda r, s: (s[0], r, 0)))
    return pl.pallas_call(
        body, name=name, grid_spec=grid_spec, out_shape=_sds((N_CHIPS, R, C), BF16),
        compiler_params=pltpu.CompilerParams(dimension_semantics=("arbitrary",), vmem_limit_bytes=VMEM_LIMIT),
    )(_scalars(chip), w3)


def _add_halves(name, g, recv, c):
    _, R, C = g.shape
    rh = R // 2
    tr = _pick(rh, 512, 16)
    nr = rh // tr

    def body(s_ref, g_ref, a_ref, o_ref):
        o_ref[...] = (g_ref[...].astype(F32) + a_ref[...].astype(F32)).astype(o_ref.dtype)

    blk = (None, tr, C)
    grid_spec = pltpu.PrefetchScalarGridSpec(
        num_scalar_prefetch=1, grid=(N_CHIPS, nr),
        in_specs=[pl.BlockSpec(blk, lambda j, r, s: (j, s[0] * nr + r, 0)),
                  pl.BlockSpec(blk, lambda j, r, s: (j, r, 0))],
        out_specs=pl.BlockSpec(blk, lambda j, r, s: (j, r, 0)))
    return pl.pallas_call(
        body, name=name, grid_spec=grid_spec, out_shape=_sds((N_CHIPS, rh, C), g.dtype),
        compiler_params=pltpu.CompilerParams(dimension_semantics=("arbitrary",) * 2, vmem_limit_bytes=VMEM_LIMIT),
    )(_scalars(c), g, recv)


def _sum_into(name, p, recv, gfull, li, chip, c):
    _, rh, C = p.shape
    tr = _pick(rh, 512, 16)
    nr = rh // tr

    def body(s_ref, p_ref, r_ref, g_ref, o_ref):
        acc = p_ref[...].astype(F32)
        for k in range(3):
            acc = acc + r_ref[k].astype(F32)
        o_ref[...] = acc

    grid_spec = pltpu.PrefetchScalarGridSpec(
        num_scalar_prefetch=1, grid=(nr,),
        in_specs=[pl.BlockSpec((None, tr, C), lambda r, s: (s[0], r, 0)),
                  pl.BlockSpec((3, tr, C), lambda r, s: (0, r, 0)), HBM],
        out_specs=pl.BlockSpec((None, tr, C), lambda r, s: (li, s[1] * nr + r, 0)))
    return pl.pallas_call(
        body, name=name, grid_spec=grid_spec, out_shape=_sds(gfull.shape, F32), input_output_aliases={3: 0},
        compiler_params=pltpu.CompilerParams(dimension_semantics=("arbitrary",), vmem_limit_bytes=VMEM_LIMIT),
    )(_scalars(chip, c), p, recv, gfull)


def _sum_slots(name, buf):
    _, n, _ = buf.shape

    def body(b_ref, o_ref):
        acc = b_ref[0]
        for k in range(1, N_DEV):
            acc = acc + b_ref[k]
        o_ref[...] = acc

    return pl.pallas_call(
        body, name=name, grid=(n // PACK_ROWS,), out_shape=_sds((n, LANES), F32),
        in_specs=[pl.BlockSpec((N_DEV, PACK_ROWS, LANES), lambda i: (0, i, 0))],
        out_specs=pl.BlockSpec((PACK_ROWS, LANES), lambda i: (i, 0)),
    )(buf)


def _adam_tile(i, w, g, m, v):
    m = ADAM_B1 * m + (1.0 - ADAM_B1) * g
    v = ADAM_B2 * v + (1.0 - ADAM_B2) * (g * g)
    m_hat = m / (1.0 - ADAM_B1 ** ADAM_STEP)
    v_hat = v / (1.0 - ADAM_B2 ** ADAM_STEP)
    delta = -ADAM_LR * (m_hat / (jnp.sqrt(v_hat) + ADAM_EPS) + ADAM_WD * w)
    return [delta, m, v], []


def _adam(name, w, g, m, v):
    rows, C = w.shape
    tile = _pick(rows, ROW_TILE, 8)
    return _rowwise(name, _adam_tile, [('row', w), ('row', g), ('row', m), ('row', v)], [(C, F32)] * 3, [], tile)


def _pack(arrs):
    flat = jnp.concatenate([a.reshape(-1).astype(F32) for a in arrs])
    unit = PACK_ROWS * LANES
    n = -(-flat.shape[0] // unit) * unit
    return jnp.pad(flat, (0, n - flat.shape[0])).reshape(-1, LANES)


def _unpack(buf, shapes):
    flat = buf.reshape(-1)
    out, off = [], 0
    for s in shapes:
        n = 1
        for d in s:
            n *= d
        out.append(flat[off:off + n].reshape(s))
        off += n
    return out


def _norm_fwd(name, x, g, tile):
    D = x.shape[1]
    return _rowwise(name, lambda i, xv, gv: ([_rms(xv, gv)], []), [('row', x), ('const', g.reshape(1, D))],
                    [(D, BF16)], [], tile)[0]


def _norm_bwd(name, x, g, dh, dxo, tile):
    D = x.shape[1]
    if dxo is None:
        def fn(i, xv, dhv, gv):
            _, vjp = jax.vjp(_rms, xv, gv)
            return [], [vjp(dhv)[1]]
        return _rowwise(name, fn, [('row', x), ('row', dh), ('const', g.reshape(1, D))], [], [(1, D)], tile)[0]

    def fn(i, xv, dhv, dxv, gv):
        _, vjp = jax.vjp(_rms, xv, gv)
        dx, dg = vjp(dhv)
        return [dxv + dx, dxv + dx], [dg]
    return _rowwise(name, fn, [('row', x), ('row', dh), ('row', dxo), ('const', g.reshape(1, D))],
                    [(D, F32), (D, BF16)], [(1, D)], tile)


def _ffn_fwd(tag, x, g, wg, wu, wd, tile):
    T, D = x.shape
    fs = wg.shape[2]
    F = N_CHIPS * fs
    tm, tn = _pick(T, TM_BIG), _pick(D, TN_SMALL)
    h = _norm_fwd(f"{tag}_norm", x, g, tile)
    hspec = _bs((tm, D), lambda j, i: (i, 0))
    wspec = _bs((None, D, fs), lambda j, i: (j, 0, 0))
    ospec = _bs((tm, fs), lambda j, i: (i, j))

    def gate(ins, outs):
        outs[0][...] = _dot(ins[0][...], ins[1][...], 'nn').astype(BF16)

    a = _mm1(f"{tag}_gate", (N_CHIPS, T // tm), [(h, hspec), (wg, wspec)], [(_sds((T, F), BF16), ospec)], gate)[0]

    def up(ins, outs):
        bv = _dot(ins[0][...], ins[1][...], 'nn')
        for rows in _row_chunks(tm):
            bb = bv[rows]
            outs[0][rows, :] = bb.astype(BF16)
            outs[1][rows, :] = (_silu(ins[2][rows, :].astype(F32)) * bb).astype(BF16)

    b, s = _mm1(f"{tag}_up", (N_CHIPS, T // tm), [(h, hspec), (wu, wspec), (a, ospec)],
                [(_sds((T, F), BF16), ospec)] * 2, up)

    def down(ins, outs):
        outs[0][...] = ins[2][...] + 0.5 * _dot(ins[0][...], ins[1][...].reshape(F, tn), 'nn')

    xspec = _bs((tm, tn), lambda i, j: (i, j))
    xo = _mm1(f"{tag}_down", (T // tm, D // tn),
              [(s, _bs((tm, F), lambda i, j: (i, 0))), (wd, _bs((N_CHIPS, fs, tn), lambda i, j: (0, 0, j))), (x, xspec)],
              [(_sds((T, D), F32), xspec)], down)[0]
    return xo, (x, h, a, b, s)


def _ffn_bwd(tag, dxo, dxb, saved, g, wg, wu, wd, tile):
    x, h, a, b, s = saved
    T, D = x.shape
    fs = wg.shape[2]
    F = N_CHIPS * fs
    tm = _pick(T, TM_MID)
    tspec = _bs((tm, fs), lambda j, i: (i, j))

    def ds_fn(ins, outs):
        d = _dot(ins[0][...], ins[1][...], 'nt')
        for rows in _row_chunks(tm):
            ds = 0.5 * d[rows]
            av, bv = ins[2][rows, :].astype(F32), ins[3][rows, :].astype(F32)
            sig = jax.nn.sigmoid(av)
            outs[0][rows, :] = (ds * bv * (sig * (1.0 + av * (1.0 - sig)))).astype(BF16)
            outs[1][rows, :] = (ds * (av * sig)).astype(BF16)

    da, db = _mm1(f"{tag}_ds", (N_CHIPS, T // tm),
                  [(dxb, _bs((tm, D), lambda j, i: (i, 0))), (wd, _bs((None, fs, D), lambda j, i: (j, 0, 0))),
                   (a, tspec), (b, tspec)], [(_sds((T, F), BF16), tspec)] * 2, ds_fn)

    tn = _pick(D, TN_BIG)

    def dwd_fn(ins, outs):
        outs[0][...] = (0.5 * _dot(ins[0][...], ins[1][...], 'tn')).astype(BF16)

    gd = _mm1(f"{tag}_dwd", (N_CHIPS, D // tn),
              [(s, _bs((T, fs), lambda i, j: (0, i))), (dxb, _bs((T, tn), lambda i, j: (0, j)))],
              [(_sds(wd.shape, BF16), _bs((None, fs, tn), lambda i, j: (i, 0, j)))], dwd_fn)[0]

    tmd = _pick(D, TM_BIG)

    def dw_fn(ins, outs):
        outs[0][...] = _dot(ins[0][...], ins[1][...], 'tn').astype(BF16)

    def dw(name, dy):
        return _mm1(name, (N_CHIPS, D // tmd),
                    [(h, _bs((T, tmd), lambda j, i: (0, i))), (dy, _bs((T, fs), lambda j, i: (0, j)))],
                    [(_sds(wg.shape, BF16), _bs((None, tmd, fs), lambda j, i: (j, i, 0)))], dw_fn)[0]

    gg, gu = dw(f"{tag}_dwg", da), dw(f"{tag}_dwu", db)

    tn3 = _pick(D, TN_SMALL)

    def dh_fn(ins, outs):
        acc = None
        for j in range(N_CHIPS):
            cols = slice(j * fs, (j + 1) * fs)
            t = _dot(ins[0][:, cols], ins[1][j], 'nt') + _dot(ins[2][:, cols], ins[3][j], 'nt')
            acc = t if acc is None else acc + t
        outs[0][...] = acc

    aspec = _bs((tm, F), lambda i, j: (i, 0))
    wtspec = _bs((N_CHIPS, tn3, fs), lambda i, j: (0, j, 0))
    dh = _mm1(f"{tag}_dh", (T // tm, D // tn3), [(da, aspec), (wg, wtspec), (db, aspec), (wu, wtspec)],
              [(_sds((T, D), F32), _bs((tm, tn3), lambda i, j: (i, j)))], dh_fn)[0]
    dx, dxb2, dg = _norm_bwd(f"{tag}_norm_bwd", x, g, dh, dxo, tile)
    return dx, dxb2, dg, gg, gu, gd


def _proj_rows(name, a, w, out_dtype, extras=(), epilogue=None):
    M, K = a.shape
    ks, N = w.shape[1], w.shape[2]
    tm, tn = _pick(M, TM_BIG), _pick(N, TN_MID)
    ins = [(a, _bs((tm, K), lambda i, j: (i, 0))), (w, _bs((N_CHIPS, ks, tn), lambda i, j: (0, 0, j)))]
    for e in extras:
        if e.shape[0] == 1:
            ins.append((e, _bs((1, tn), lambda i, j: (0, j))))
        else:
            ins.append((e, _bs((tm, tn), lambda i, j: (i, j))))

    def fn(refs, outs):
        acc = _dot(refs[0][...], refs[1][...].reshape(K, tn), 'nn')
        if epilogue is not None:
            acc = epilogue(acc, [r[...] for r in refs[2:]])
        outs[0][...] = acc.astype(out_dtype)

    return _mm1(name, (M // tm, N // tn), ins, [(_sds((M, N), out_dtype), _bs((tm, tn), lambda i, j: (i, j)))], fn)[0]


def _proj_rows_t(name, pairs, out_dtype):
    dy0, w0 = pairs[0]
    M, N = dy0.shape
    ks = w0.shape[1]
    tm = _pick(M, TM_BIG)
    ins = []
    for dy, w in pairs:
        ins.append((dy, _bs((tm, N), lambda i, j: (i, 0))))
        ins.append((w, _bs((None, ks, N), lambda i, j: (j, 0, 0))))

    def fn(refs, outs):
        acc = _dot(refs[0][...], refs[1][...], 'nt')
        for p in range(1, len(pairs)):
            acc = acc + _dot(refs[2 * p][...], refs[2 * p + 1][...], 'nt')
        outs[0][...] = acc.astype(out_dtype)

    return _mm1(name, (M // tm, N_CHIPS), ins,
                [(_sds((M, N_CHIPS * ks), out_dtype), _bs((tm, ks), lambda i, j: (i, j)))], fn)[0]


def _grad_rows(name, a, dys):
    T, K = a.shape
    N = dys[0].shape[1]
    ks = K // N_CHIPS
    tn = _pick(N, TN_BIG)
    ins = [(a, _bs((T, ks), lambda i, j: (0, i)))] + [(dy, _bs((T, tn), lambda i, j: (0, j))) for dy in dys]

    def fn(refs, outs):
        av = refs[0][...]
        for p in range(len(dys)):
            outs[p][...] = _dot(av, refs[1 + p][...], 'tn').astype(BF16)

    gspec = _bs((None, ks, tn), lambda i, j: (i, 0, j))
    return _mm1(name, (N_CHIPS, N // tn), ins, [(_sds((N_CHIPS, ks, N), BF16), gspec)] * len(dys), fn)


def _softmax_rows(s):
    s = s - jnp.max(s, axis=-1, keepdims=True)
    p = jnp.exp(s)
    return p / jnp.sum(p, axis=-1, keepdims=True)


def _attn_fwd_tile(hd, scale):
    def fn(i, q, k, v):
        outs = []
        for h in range(N_MEM_HEADS):
            sl = slice(h * hd, (h + 1) * hd)
            p = _softmax_rows(lax.dot_general(q[:, sl], k[:, sl], NT, preferred_element_type=F32) * scale)
            outs.append(lax.dot_general(p.astype(BF16), v[:, sl], NN, preferred_element_type=F32))
        return [jnp.concatenate(outs, axis=1)], []
    return fn


def _attn_bwd_tile(hd, scale):
    def fn(i, q, do, k, v):
        dqs, dks, dvs = [], [], []
        for h in range(N_MEM_HEADS):
            sl = slice(h * hd, (h + 1) * hd)
            qh, kh, vh, doh = q[:, sl], k[:, sl], v[:, sl], do[:, sl]
            p = _softmax_rows(lax.dot_general(qh, kh, NT, preferred_element_type=F32) * scale)
            dvs.append(lax.dot_general(p.astype(BF16), doh, TN_, preferred_element_type=F32))
            dp = lax.dot_general(doh, vh, NT, preferred_element_type=F32)
            ds = (p * (dp - jnp.sum(dp * p, axis=-1, keepdims=True)) * scale).astype(BF16)
            dqs.append(lax.dot_general(ds, kh, NN, preferred_element_type=F32))
            dks.append(lax.dot_general(ds, qh, TN_, preferred_element_type=F32))
        return [jnp.concatenate(dqs, axis=1)], [jnp.concatenate(dks, axis=1), jnp.concatenate(dvs, axis=1)]
    return fn


def _attn_fwd(l, x, mem, gq, gkv, W, tile):
    T, D = x.shape
    M = mem.shape[0]
    hd = D // N_MEM_HEADS
    hq = _norm_fwd(f"xa{l}_normq", x, gq, tile)
    mn = _norm_fwd(f"xa{l}_normkv", mem, gkv, _pick(M, tile, 16))
    q = _proj_rows(f"xa{l}_q", hq, W['xattn_wq'], BF16)
    k = _proj_rows(f"xa{l}_k", mn, W['xattn_wk'], BF16)
    v = _proj_rows(f"xa{l}_v", mn, W['xattn_wv'], BF16)
    o = _rowwise(f"xa{l}_attn", _attn_fwd_tile(hd, hd ** -0.5), [('row', q), ('const', k), ('const', v)],
                 [(D, BF16)], [], tile)[0]
    xo = _proj_rows(f"xa{l}_o", o, W['xattn_wo'], F32, extras=(x,), epilogue=lambda acc, ex: ex[0] + acc)
    return xo, (x, hq, mn, q, k, v, o)


def _attn_bwd(l, dxo, dxb, saved, mem, gq, gkv, W, G, tile):
    x, hq, mn, q, k, v, o = saved
    T, D = x.shape
    M = mem.shape[0]
    hd = D // N_MEM_HEADS
    do = _proj_rows_t(f"xa{l}_do", [(dxb, W['xattn_wo'])], BF16)
    G['xattn_wo'] = _grad_rows(f"xa{l}_dwo", o, [dxb])[0]
    dq, dk, dv = _rowwise(f"xa{l}_attn_bwd", _attn_bwd_tile(hd, hd ** -0.5),
                          [('row', q), ('row', do), ('const', k), ('const', v)], [(D, BF16)], [(M, D), (M, D)], tile)
    dhq = _proj_rows_t(f"xa{l}_dhq", [(dq, W['xattn_wq'])], F32)
    G['xattn_wq'] = _grad_rows(f"xa{l}_dwq", hq, [dq])[0]
    dmn = _proj_rows_t(f"xa{l}_dmn", [(dk, W['xattn_wk']), (dv, W['xattn_wv'])], F32)
    G['xattn_wk'], G['xattn_wv'] = _grad_rows(f"xa{l}_dwkv", mn, [dk, dv])
    dx, dxb2, dgq = _norm_bwd(f"xa{l}_normq_bwd", x, gq, dhq, dxo, tile)
    dgkv = _norm_bwd(f"xa{l}_normkv_bwd", mem, gkv, dmn, None, _pick(M, tile, 16))
    return dx, dxb2, dgq, dgkv


def _chunk_mask():
    p = lax.broadcasted_iota(jnp.int32, (GMLP_BLOCK, GMLP_BLOCK), 0)
    q = lax.broadcasted_iota(jnp.int32, (GMLP_BLOCK, GMLP_BLOCK), 1)
    return (q // CHUNK) <= (p // CHUNK)


def _spatial_fwd(vn, ws_ref, bsf, mask, hd):
    vb = vn.astype(BF16)
    wsm = [jnp.where(mask, ws_ref[h], 0.0).astype(BF16) for h in range(A_HEADS)]
    rows = []
    for n in range(vn.shape[0] // GMLP_BLOCK):
        blk = vb[n * GMLP_BLOCK:(n + 1) * GMLP_BLOCK]
        cols = [lax.dot_general(wsm[h], blk[:, h * hd:(h + 1) * hd], NN, preferred_element_type=F32)
                for h in range(A_HEADS)]
        rows.append(jnp.concatenate(cols, axis=1) + bsf)
    return jnp.concatenate(rows, axis=0)


def _spatial_bwd(dsp, vn, ws_ref, mask, hd):
    vb, db16 = vn.astype(BF16), dsp.astype(BF16)
    wsm = [jnp.where(mask, ws_ref[h], 0.0).astype(BF16) for h in range(A_HEADS)]
    dws = [jnp.zeros((GMLP_BLOCK, GMLP_BLOCK), F32) for _ in range(A_HEADS)]
    dbs = jnp.zeros((GMLP_BLOCK, vn.shape[1]), F32)
    rows = []
    for n in range(vn.shape[0] // GMLP_BLOCK):
        sl = slice(n * GMLP_BLOCK, (n + 1) * GMLP_BLOCK)
        cols = []
        for h in range(A_HEADS):
            hs = slice(h * hd, (h + 1) * hd)
            cols.append(lax.dot_general(wsm[h], db16[sl, hs], TN_, preferred_element_type=F32))
            dws[h] = dws[h] + lax.dot_general(db16[sl, hs], vb[sl, hs], NT, preferred_element_type=F32)
        rows.append(jnp.concatenate(cols, axis=1))
        dbs = dbs + dsp[sl]
    dws = [jnp.where(mask, d, 0.0) for d in dws]
    return jnp.concatenate(rows, axis=0), dws, dbs


def _conv_taps(cat, cw_ref, kw, tile):
    acc = jnp.zeros((tile, cat.shape[1]), F32)
    for k in range(kw):
        sh = kw - 1 - k
        r = cat if sh == 0 else pltpu.roll(cat, sh, 0)
        acc = acc + r[HALO:] * cw_ref[k:k + 1, :]
    return acc


def _mix_fwd_tile(A, B, kw, tile):
    hd = A // A_HEADS

    def fn(i, z, zp, ws_ref, bsf, glg, glb, cw_ref, cb, clg, clb):
        mask = _chunk_mask()
        u = _gelu(z[:, :A])
        vn = _ln(_gelu(z[:, A:2 * A]), glg, glb)
        ya = u * _spatial_fwd(vn, ws_ref, bsf, mask, hd)
        hb = _glu(z[:, 2 * A:2 * A + B], z[:, 2 * A + B:])
        hp = jnp.where(i > 0, _glu(zp[:, 2 * A:2 * A + B], zp[:, 2 * A + B:]), 0.0)
        conv = _conv_taps(jnp.concatenate([hp, hb], axis=0), cw_ref, kw, tile) + cb
        yb = _silu(_ln(conv, clg, clb))
        return [jnp.concatenate([ya, yb], axis=1)], []
    return fn


def _mix_bwd1_tile(A, B, kw, tile):
    hd = A // A_HEADS

    def fn(i, z, zp, dy, dxo, ws_ref, bsf, glg, glb, cw_ref, cb, clg, clb):
        mask = _chunk_mask()
        dya, dyb = dy[:, :A], dy[:, A:]
        zu, zv = z[:, :A], z[:, A:2 * A]
        u, vjp_u = jax.vjp(_gelu, zu)
        vn, vjp_v = jax.vjp(lambda t, g, b: _ln(_gelu(t), g, b), zv, glg, glb)
        sp = _spatial_fwd(vn, ws_ref, bsf, mask, hd)
        dzu = vjp_u(dya * sp)[0]
        dvn, dws, dbs = _spatial_bwd(dya * u, vn, ws_ref, mask, hd)
        dzv, dglg, dglb = vjp_v(dvn)
        hb = _glu(z[:, 2 * A:2 * A + B], z[:, 2 * A + B:])
        hp = jnp.where(i > 0, _glu(zp[:, 2 * A:2 * A + B], zp[:, 2 * A + B:]), 0.0)
        cat = jnp.concatenate([hp, hb], axis=0)
        conv = _conv_taps(cat, cw_ref, kw, tile) + cb
        _, vjp_c = jax.vjp(lambda t, g, b: _silu(_ln(t, g, b)), conv, clg, clb)
        dconv, dclg, dclb = vjp_c(dyb)
        tap = lax.broadcasted_iota(jnp.int32, (HALO, 1), 0)
        dcw = jnp.zeros((HALO, B), F32)
        for k in range(kw):
            sh = kw - 1 - k
            r = cat if sh == 0 else pltpu.roll(cat, sh, 0)
            dcw = dcw + jnp.where(tap == k, jnp.sum(dconv * r[HALO:], axis=0, keepdims=True), 0.0)
        dcb = jnp.sum(dconv, axis=0, keepdims=True)
        dbo = jnp.sum(dxo, axis=0, keepdims=True)
        dws = jnp.concatenate([d[None] for d in dws], axis=0)
        return [jnp.concatenate([dzu, dzv], axis=1), dconv], [dws, dbs, dglg, dglb, dcw, dcb, dclg, dclb, dbo]
    return fn


def _mix_bwd2_tile(A, B, kw, tile, n_tiles):
    def fn(i, z, dza, dc, dcn, cw_ref):
        dcn = jnp.where(i < n_tiles - 1, dcn, 0.0)
        cat = jnp.concatenate([dc, dcn], axis=0)
        n = tile + HALO
        dhb = jnp.zeros((tile, B), F32)
        for k in range(kw):
            sh = kw - 1 - k
            r = cat if sh == 0 else pltpu.roll(cat, n - sh, 0)
            dhb = dhb + r[:tile] * cw_ref[k:k + 1, :]
        _, vjp_g = jax.vjp(_glu, z[:, 2 * A:2 * A + B], z[:, 2 * A + B:])
        da, dg = vjp_g(dhb)
        dz = jnp.concatenate([dza, da, dg], axis=1)
        return [dz], [jnp.sum(dz, axis=0, keepdims=True)]
    return fn


def _even_consts(p, e, A, B, kw):
    hd = A // A_HEADS
    bsf = jnp.repeat(p['gmlp_b_s'][e].T, hd, axis=1)
    cw = jnp.pad(p['conv_w_full'][e], ((0, HALO - kw), (0, 0)))
    return [('cref', p['gmlp_w_s'][e]), ('const', bsf), ('const', p['gmlp_ln_g'][e].reshape(1, A)),
            ('const', p['gmlp_ln_b'][e].reshape(1, A)), ('cref', cw), ('const', p['conv_b'][e].reshape(1, B)),
            ('const', p['conv_ln_g'][e].reshape(1, B)), ('const', p['conv_ln_b'][e].reshape(1, B))]


def _even_fwd(l, e, x, gm, p, W, tile):
    T, D = x.shape
    w_in, w_out = W['ab_w_in'], W['ab_w_out']
    zs = w_in.shape[2]
    Z = N_CHIPS * zs
    A = p['gmlp_ln_g'].shape[1]
    B = p['conv_b'].shape[1]
    kw = p['conv_w_full'].shape[1]
    tm = _pick(T, TM_BIG)
    h = _norm_fwd(f"mix{l}_norm", x, gm, tile)

    def in_fn(refs, outs):
        outs[0][...] = _dot(refs[0][...], refs[1][...], 'nn') + refs[2][...]

    z = _mm1(f"mix{l}_in", (N_CHIPS, T // tm),
             [(h, _bs((tm, D), lambda j, i: (i, 0))), (w_in, _bs((None, D, zs), lambda j, i: (j, 0, 0))),
              (p['ab_b_in'][e].reshape(1, Z), _bs((1, zs), lambda j, i: (0, j)))],
             [(_sds((T, Z), F32), _bs((tm, zs), lambda j, i: (i, j)))], in_fn)[0]
    consts = _even_consts(p, e, A, B, kw)
    ycat = _rowwise(f"mix{l}_mid", _mix_fwd_tile(A, B, kw, tile), [('row', z), ('prev', z)] + consts,
                    [(A + B, BF16)], [], tile)[0]
    xo = _proj_rows(f"mix{l}_out", ycat, w_out, F32, extras=(x, p['ab_b_out'][e].reshape(1, D)),
                    epilogue=lambda acc, ex: ex[0] + acc + ex[1])
    return xo, (x, h, z, ycat)


def _even_bwd(l, e, dxo, dxb, saved, gm, p, W, G, tile):
    x, h, z, ycat = saved
    T, D = x.shape
    w_in, w_out = W['ab_w_in'], W['ab_w_out']
    zs = w_in.shape[2]
    Z = N_CHIPS * zs
    A = p['gmlp_ln_g'].shape[1]
    B = p['conv_b'].shape[1]
    kw = p['conv_w_full'].shape[1]
    hd = A // A_HEADS
    dycat = _proj_rows_t(f"mix{l}_dycat", [(dxb, w_out)], F32)
    G['ab_w_out'] = _grad_rows(f"mix{l}_dwout", ycat, [dxb])[0]
    consts = _even_consts(p, e, A, B, kw)
    accs = [(A_HEADS, GMLP_BLOCK, GMLP_BLOCK), (GMLP_BLOCK, A), (1, A), (1, A), (HALO, B), (1, B), (1, B), (1, B),
            (1, D)]
    dza, dconv, dws, dbs, dglg, dglb, dcw, dcb, dclg, dclb, dbo = _rowwise(
        f"mix{l}_mid_bwd1", _mix_bwd1_tile(A, B, kw, tile),
        [('row', z), ('prev', z), ('row', dycat), ('row', dxo)] + consts, [(2 * A, F32), (B, F32)], accs, tile)
    dz, dbin = _rowwise(f"mix{l}_mid_bwd2", _mix_bwd2_tile(A, B, kw, tile, T // tile),
                        [('row', z), ('row', dza), ('row', dconv), ('next', dconv), consts[4]],
                        [(Z, BF16)], [(1, Z)], tile)
    tmd = _pick(D, TM_BIG)

    def dwin_fn(refs, outs):
        outs[0][...] = _dot(refs[0][...], refs[1][...], 'tn').astype(BF16)

    G['ab_w_in'] = _mm1(f"mix{l}_dwin", (N_CHIPS, D // tmd),
                        [(h, _bs((T, tmd), lambda j, i: (0, i))), (dz, _bs((T, zs), lambda j, i: (0, j)))],
                        [(_sds(w_in.shape, BF16), _bs((None, tmd, zs), lambda j, i: (j, i, 0)))], dwin_fn)[0]
    tm, tn = _pick(T, TM_MID), _pick(D, TN_SMALL)

    def dh_fn(refs, outs):
        acc = None
        for j in range(N_CHIPS):
            t = _dot(refs[0][:, j * zs:(j + 1) * zs], refs[1][j], 'nt')
            acc = t if acc is None else acc + t
        outs[0][...] = acc

    dh = _mm1(f"mix{l}_dh", (T // tm, D // tn),
              [(dz, _bs((tm, Z), lambda i, j: (i, 0))), (w_in, _bs((N_CHIPS, tn, zs), lambda i, j: (0, j, 0)))],
              [(_sds((T, D), F32), _bs((tm, tn), lambda i, j: (i, j)))], dh_fn)[0]
    dx, dxb2, dgm = _norm_bwd(f"mix{l}_norm_bwd", x, gm, dh, dxo, tile)
    small = {'ab_b_in': dbin.reshape(Z), 'gmlp_w_s': dws, 'gmlp_b_s': dbs.reshape(GMLP_BLOCK, A_HEADS, hd).sum(-1).T,
             'gmlp_ln_g': dglg.reshape(A), 'gmlp_ln_b': dglb.reshape(A), 'conv_w': dcw[:kw], 'conv_b': dcb.reshape(B),
             'conv_ln_g': dclg.reshape(B), 'conv_ln_b': dclb.reshape(B), 'ab_b_out': dbo.reshape(D)}
    return dx, dxb2, dgm, small


def _pool_counts(t, cg):
    return jnp.concatenate([jnp.broadcast_to(jnp.minimum(t + 1, w).astype(F32), (t.shape[0], cg))
                            for w in POOL_WINDOWS], axis=1)


def _window_sums(cat, cg, back):
    n = cat.shape[0]
    outs = []
    for gi, w in enumerate(POOL_WINDOWS):
        s = cat[:, gi * cg:(gi + 1) * cg]
        step = 1
        while step < w:
            s = s + pltpu.roll(s, step if back else n - step, 0)
            step *= 2
        outs.append(s)
    return jnp.concatenate(outs, axis=1)


def _pool_fwd_tile(D, tile):
    cg = D // len(POOL_WINDOWS)

    def fn(i, x, xp, g):
        h = _rms(x, g)
        hp = jnp.where(i > 0, _rms(xp, g), 0.0)
        sums = _window_sums(jnp.concatenate([hp, h], axis=0), cg, True)[HALO:]
        return [sums / _pool_counts(_row_ids(i, tile, tile), cg) - h], []
    return fn


def _pool_bwd_tile(D, tile, n_tiles):
    cg = D // len(POOL_WINDOWS)

    def fn(i, dd, ddn, x, dxo, g):
        e = dd / _pool_counts(_row_ids(i, tile, tile), cg)
        en = jnp.where(i < n_tiles - 1, ddn / _pool_counts(_row_ids(i + 1, tile, HALO), cg), 0.0)
        dh = _window_sums(jnp.concatenate([e, en], axis=0), cg, False)[:tile] - dd
        _, vjp = jax.vjp(_rms, x, g)
        dx, dg = vjp(dh)
        return [dxo + dx, dxo + dx], [dg]
    return fn


def _odd_fwd(l, o, x, gm, p, W, tile):
    T, D = x.shape
    wc = W['pool_w']
    cg = wc.shape[2]
    cs = cg // N_CHIPS
    ng = len(POOL_WINDOWS)
    tm = _pick(T, TM)
    d = _rowwise(f"mix{l}_pool", _pool_fwd_tile(D, tile), [('row', x), ('prev', x), ('const', gm.reshape(1, D))],
                 [(D, BF16)], [], tile)[0]
    gspec = _bs((tm, cg), lambda i, j, k: (i, j))
    vspec = _bs((1, cg), lambda i, j, k: (0, j))

    def epi(accs, ex):
        pre = accs[0] + ex[0]
        return [ex[2] + pre * ex[1], pre]

    xo, pre = _mm(f"mix{l}_poolmm", (T // tm, ng, N_CHIPS),
                  [(d, _bs((tm, cs), lambda i, j, k: (i, j * N_CHIPS + k))),
                   (wc, _bs((None, cs, cg), lambda i, j, k: (k, j, 0))),
                   (p['pool_b_full'][o].reshape(1, D), vspec), (p['pool_scale_full'][o].reshape(1, D), vspec),
                   (x, gspec)],
                  [(0, 1, 'nn', 0)], [(_sds((T, D), F32), gspec)] * 2, [(tm, cg)], epi, extras=(2, 3, 4))
    return xo, (x, d, pre)


def _odd_bwd(l, o, dxo, dxb, saved, gm, p, W, G, tile):
    x, d, pre = saved
    T, D = x.shape
    wc = W['pool_w']
    cg = wc.shape[2]
    cs = cg // N_CHIPS
    ng = len(POOL_WINDOWS)
    tm, tkt = _pick(T, TM), _pick(T, TK)

    def fn(i, dxv, prev, sc):
        return [dxv * sc], [jnp.sum(dxv * prev, axis=0, keepdims=True), jnp.sum(dxv * sc, axis=0, keepdims=True)]

    do, dscale, dbc = _rowwise(f"mix{l}_pool_bwd1", fn,
                               [('row', dxo), ('row', pre), ('const', p['pool_scale_full'][o].reshape(1, D))],
                               [(D, BF16)], [(1, D), (1, D)], tile)
    nb = ng * N_CHIPS
    dd = _mm(f"mix{l}_pool_dd", (T // tm, nb, 1),
             [(do, _bs((tm, cg), lambda i, j, k: (i, j // N_CHIPS))),
              (wc, _bs((None, cs, cg), lambda i, j, k: (j % N_CHIPS, j // N_CHIPS, 0)))],
             [(0, 1, 'nt', 0)], [(_sds((T, D), F32), _bs((tm, cs), lambda i, j, k: (i, j)))], [(tm, cs)],
             lambda a, _: a)[0]
    G['pool_w'] = _mm(f"mix{l}_pool_dw", (nb, 1, T // tkt),
                      [(d, _bs((tkt, cs), lambda i, j, k: (k, i))), (do, _bs((tkt, cg), lambda i, j, k: (k, i // N_CHIPS)))],
                      [(0, 1, 'tn', 0)],
                      [(_sds(wc.shape, BF16), _bs((None, cs, cg), lambda i, j, k: (i % N_CHIPS, i // N_CHIPS, 0)))],
                      [(cs, cg)], lambda a, _: a)[0]
    dx, dxb2, dgm = _rowwise(f"mix{l}_pool_bwd2", _pool_bwd_tile(D, tile, T // tile),
                             [('row', dd), ('next', dd), ('row', x), ('row', dxo), ('const', gm.reshape(1, D))],
                             [(D, F32), (D, BF16)], [(1, D)], tile)
    small = {'pool_b': dbc.reshape(ng, cg), 'pool_scale': dscale.reshape(D)}
    return dx, dxb2, dgm, small


def _final(x, g, tgt, tile):
    T, D = x.shape

    def fn(i, xv, tv, gv):
        y, vjp = jax.vjp(_rms, xv, gv)
        err = y - tv
        dx, dg = vjp(err / D)
        loss = 0.5 * jnp.sum(jnp.mean(err * err, axis=-1, keepdims=True), axis=0, keepdims=True)
        return [dx, dx], [dg, jnp.broadcast_to(loss, (1, LANES))]

    dx, dxb, dg, loss = _rowwise("final", fn, [('row', x), ('row', tgt), ('const', g.reshape(1, D))],
                                 [(D, F32), (D, BF16)], [(1, D), (1, LANES)], tile)
    return dx, dxb, dg.reshape(D), loss[0, 0]


def _as3d(name, w):
    return w.reshape(w.shape[0], -1, w.shape[-1]) if name == 'pool_w' else w


def kernel(x, mem, norm_ffn1, ffn1_gate, ffn1_up, ffn1_down, norm_mix, ab_w_in, ab_b_in, gmlp_w_s, gmlp_b_s, gmlp_ln_g, gmlp_ln_b, conv_w, conv_b, conv_ln_g, conv_ln_b, ab_w_out, ab_b_out, pool_w, pool_b, pool_scale, norm_xq, norm_xkv, xattn_wq, xattn_wk, xattn_wv, xattn_wo, norm_ffn2, ffn2_gate, ffn2_up, ffn2_down, norm_final, loss_target, m_norm_ffn1, m_ffn1_gate, m_ffn1_up, m_ffn1_down, m_norm_mix, m_ab_w_in, m_ab_b_in, m_gmlp_w_s, m_gmlp_b_s, m_gmlp_ln_g, m_gmlp_ln_b, m_conv_w, m_conv_b, m_conv_ln_g, m_conv_ln_b, m_ab_w_out, m_ab_b_out, m_pool_w, m_pool_b, m_pool_scale, m_norm_xq, m_norm_xkv, m_xattn_wq, m_xattn_wk, m_xattn_wv, m_xattn_wo, m_norm_ffn2, m_ffn2_gate, m_ffn2_up, m_ffn2_down, m_norm_final, v_norm_ffn1, v_ffn1_gate, v_ffn1_up, v_ffn1_down, v_norm_mix, v_ab_w_in, v_ab_b_in, v_gmlp_w_s, v_gmlp_b_s, v_gmlp_ln_g, v_gmlp_ln_b, v_conv_w, v_conv_b, v_conv_ln_g, v_conv_ln_b, v_ab_w_out, v_ab_b_out, v_pool_w, v_pool_b, v_pool_scale, v_norm_xq, v_norm_xkv, v_xattn_wq, v_xattn_wk, v_xattn_wv, v_xattn_wo, v_norm_ffn2, v_ffn2_gate, v_ffn2_up, v_ffn2_down, v_norm_final):
    w = dict(zip(WEIGHTS, [norm_ffn1, ffn1_gate, ffn1_up, ffn1_down, norm_mix, ab_w_in, ab_b_in, gmlp_w_s, gmlp_b_s, gmlp_ln_g, gmlp_ln_b, conv_w, conv_b, conv_ln_g, conv_ln_b, ab_w_out, ab_b_out, pool_w, pool_b, pool_scale, norm_xq, norm_xkv, xattn_wq, xattn_wk, xattn_wv, xattn_wo, norm_ffn2, ffn2_gate, ffn2_up, ffn2_down, norm_final]))
    m = dict(zip(WEIGHTS, [m_norm_ffn1, m_ffn1_gate, m_ffn1_up, m_ffn1_down, m_norm_mix, m_ab_w_in, m_ab_b_in, m_gmlp_w_s, m_gmlp_b_s, m_gmlp_ln_g, m_gmlp_ln_b, m_conv_w, m_conv_b, m_conv_ln_g, m_conv_ln_b, m_ab_w_out, m_ab_b_out, m_pool_w, m_pool_b, m_pool_scale, m_norm_xq, m_norm_xkv, m_xattn_wq, m_xattn_wk, m_xattn_wv, m_xattn_wo, m_norm_ffn2, m_ffn2_gate, m_ffn2_up, m_ffn2_down, m_norm_final]))
    v = dict(zip(WEIGHTS, [v_norm_ffn1, v_ffn1_gate, v_ffn1_up, v_ffn1_down, v_norm_mix, v_ab_w_in, v_ab_b_in, v_gmlp_w_s, v_gmlp_b_s, v_gmlp_ln_g, v_gmlp_ln_b, v_conv_w, v_conv_b, v_conv_ln_g, v_conv_ln_b, v_ab_w_out, v_ab_b_out, v_pool_w, v_pool_b, v_pool_scale, v_norm_xq, v_norm_xkv, v_xattn_wq, v_xattn_wk, v_xattn_wv, v_xattn_wo, v_norm_ffn2, v_ffn2_gate, v_ffn2_up, v_ffn2_down, v_norm_final]))

    xs, mems, tgt = x[0], mem[0], loss_target[0]
    T, D = xs.shape
    L = norm_ffn1.shape[0]
    tile = _pick(T, ROW_TILE)
    cx, cy, cc = _mesh_pos()
    chip = 2 * cx + cy
    w3 = {n: _as3d(n, w[n]) for n in BIG}
    names = [_layer_names(l) for l in range(L)]

    sh_shapes = [w[n].shape for n in SMALL_SHARDED]
    slots = _gather_all("gather_small_shards", _pack([w[n] for n in SMALL_SHARDED]))
    per_chip = [_unpack(slots[2 * j], sh_shapes) for j in range(N_CHIPS)]
    full = {n: jnp.concatenate([per_chip[j][k] for j in range(N_CHIPS)], axis=-1) for k, n in enumerate(SMALL_SHARDED)}
    p = dict(w)
    p['conv_w_full'] = full['conv_w'].reshape(full['conv_w'].shape[0], full['conv_w'].shape[1], -1)
    p['pool_b_full'] = full['pool_b']
    p['pool_scale_full'] = full['pool_scale']

    token = slots
    in_flight = []
    for l in range(L):
        slabs = [_cast_slab(f"cast_{n}_{l}", w3[n], _stack_index(n, l), chip) for n in names[l]]
        sends, arrivals, slabs, token = _split_start(f"gather_start_{l}", slabs, token, 3 * len(slabs), _gather_copies)
        in_flight.append((sends, arrivals, slabs))

    saved, Wl = [], []
    xc = xs + token[0, 0]
    for l in range(L):
        sends, arrivals, slabs = in_flight[l]
        slabs = _split_wait(f"gather_wait_{l}", slabs, sends, arrivals, xc, _gather_copies)
        W = dict(zip(names[l], _forward_halves(f"gather_fwd_{l}", slabs)))
        Wl.append(W)
        s = {}
        xc, s['ffn1'] = _ffn_fwd(f"ffn1_{l}", xc, w['norm_ffn1'][l], W['ffn1_gate'], W['ffn1_up'], W['ffn1_down'], tile)
        if l % 2 == 0:
            xc, s['mix'] = _even_fwd(l, l // 2, xc, w['norm_mix'][l], p, W, tile)
        else:
            xc, s['mix'] = _odd_fwd(l, l // 2, xc, w['norm_mix'][l], p, W, tile)
        xc, s['xa'] = _attn_fwd(l, xc, mems, w['norm_xq'][l], w['norm_xkv'][l], W, tile)
        xc, s['ffn2'] = _ffn_fwd(f"ffn2_{l}", xc, w['norm_ffn2'][l], W['ffn2_gate'], W['ffn2_up'], W['ffn2_down'], tile)
        saved.append(s)

    dx, dxb, g_final, loss_local = _final(xc, w['norm_final'], tgt, tile)
    loss = lax.psum(loss_local, ("x", "y", "c"))
    gfull = {n: lax.empty(w3[n].shape, F32) for n in BIG}
    gs = {n: [None] * w[n].shape[0] for n in SMALL if n != 'norm_final'}

    def finish_exchange(pending, after):
        l, sends, arrivals, thru = pending
        ns = names[l]
        thru = _split_wait(f"rs_wait_{l}", thru, sends, arrivals, after, _exchange_copies)
        parts, lands = thru[:len(ns)], thru[len(ns):]
        for n, part, land in zip(ns, parts, lands):
            gfull[n] = _sum_into(f"rs_sum_{n}_{l}", part, land, gfull[n], _stack_index(n, l), chip, cc)
        joined = _join_halves(f"rs_join_{l}", [gfull[n] for n in ns], [_stack_index(n, l) for n in ns])
        gfull.update(zip(ns, joined))

    pending = None
    for l in reversed(range(L)):
        s, W, G = saved[l], Wl[l], {}
        dx, dxb, dg, G['ffn2_gate'], G['ffn2_up'], G['ffn2_down'] = _ffn_bwd(
            f"ffn2_{l}", dx, dxb, s['ffn2'], w['norm_ffn2'][l], W['ffn2_gate'], W['ffn2_up'], W['ffn2_down'], tile)
        gs['norm_ffn2'][l] = dg.reshape(D)
        dx, dxb, dgq, dgkv = _attn_bwd(l, dx, dxb, s['xa'], mems, w['norm_xq'][l], w['norm_xkv'][l], W, G, tile)
        gs['norm_xq'][l], gs['norm_xkv'][l] = dgq.reshape(D), dgkv.reshape(D)
        if l % 2 == 0:
            dx, dxb, dgm, small = _even_bwd(l, l // 2, dx, dxb, s['mix'], w['norm_mix'][l], p, W, G, tile)
        else:
            dx, dxb, dgm, small = _odd_bwd(l, l // 2, dx, dxb, s['mix'], w['norm_mix'][l], p, W, G, tile)
        for n, val in small.items():
            gs[n][l // 2] = val
        gs['norm_mix'][l] = dgm.reshape(D)
        dx, dxb, dg, G['ffn1_gate'], G['ffn1_up'], G['ffn1_down'] = _ffn_bwd(
            f"ffn1_{l}", dx, dxb, s['ffn1'], w['norm_ffn1'][l], W['ffn1_gate'], W['ffn1_up'], W['ffn1_down'], tile)
        gs['norm_ffn1'][l] = dg.reshape(D)
        if pending is not None:
            finish_exchange(pending, dx)
        grads_l = [G[n] for n in names[l]]
        got = _swap_halves(f"rs_swap_{l}", grads_l)
        parts = [_add_halves(f"rs_add_{n}_{l}", g, r, cc) for n, g, r in zip(names[l], grads_l, got)]
        lands = [lax.empty((3,) + part.shape[1:], BF16) for part in parts]
        sends, arrivals, thru, token = _split_start(f"rs_start_{l}", parts + lands, dx, 3 * len(parts), _exchange_copies)
        dx, dxb = dx + token[0, 0], dxb + token[0, 0].astype(BF16)
        pending = (l, sends, arrivals, thru)
    finish_exchange(pending, dx)
    grad_x = dx[None]

    small_full = {n: jnp.stack(gs[n]) for n in gs}
    small_full['norm_final'] = g_final
    full_shapes = [small_full[n].shape for n in SMALL]
    summed = _sum_slots("sum_small", _gather_all("gather_small_grads", _pack([small_full[n] for n in SMALL])))
    g_small = dict(zip(SMALL, _unpack(summed, full_shapes)))
    for n in SMALL_SHARDED:
        width = w[n].shape[-1]
        g_small[n] = lax.dynamic_slice_in_dim(g_small[n], chip * width, width, axis=g_small[n].ndim - 1).reshape(w[n].shape)

    grads, delta, new_m, new_v = {}, {}, {}, {}
    for n in BIG:
        g3 = gfull[n]
        shape2 = (-1, g3.shape[-1])
        d2, m2, v2 = _adam(f"adam_{n}", w3[n].reshape(shape2), g3.reshape(shape2),
                           _as3d(n, m[n]).reshape(shape2), _as3d(n, v[n]).reshape(shape2))
        grads[n], delta[n], new_m[n], new_v[n] = (t.reshape(w[n].shape) for t in (g3, d2, m2, v2))
    small_shapes = [w[n].shape for n in SMALL]
    d2, m2, v2 = _adam("adam_small", _pack([w[n] for n in SMALL]), _pack([g_small[n] for n in SMALL]),
                       _pack([m[n] for n in SMALL]), _pack([v[n] for n in SMALL]))
    for n, dn, mn_, vn_ in zip(SMALL, _unpack(d2, small_shapes), _unpack(m2, small_shapes), _unpack(v2, small_shapes)):
        grads[n], delta[n], new_m[n], new_v[n] = g_small[n].reshape(w[n].shape), dn, mn_, vn_

    return (loss, grad_x, *[grads[n] for n in WEIGHTS], *[delta[n] for n in WEIGHTS],
            *[new_m[n] for n in WEIGHTS], *[new_v[n] for n in WEIGHTS])
```

```python
import jax
import jax.numpy as jnp
from jax import lax
from jax.experimental import pallas as pl
from jax.experimental.pallas import tpu as pltpu

F32, BF16 = jnp.float32, jnp.bfloat16
EPS = 1e-6
N_MEM_HEADS = 4
A_HEADS = 8
GMLP_BLOCK = 128
CHUNK = 64
POOL_WINDOWS = (2, 4, 8, 16)
N_CHIPS = 4
N_DEV = 8
HALO = 32
LANES = 128
TM, TN, TK = 512, 1024, 512
TM_BIG, TM_MID = 1024, 512
TN_BIG, TN_MID, TN_SMALL = 1024, 512, 256
EPI_ROWS = 256
EXCHANGES_IN_FLIGHT = 2
ROW_TILE = 256
PACK_ROWS = 512
VMEM_LIMIT = 48 * 1024 * 1024
ADAM_LR, ADAM_B1, ADAM_B2, ADAM_EPS, ADAM_WD, ADAM_STEP = 0.001, 0.9, 0.999, 1e-08, 0.01, 10
MESH = pl.DeviceIdType.MESH
HBM = pl.BlockSpec(memory_space=pltpu.HBM)
SEM = pl.BlockSpec(memory_space=pltpu.SEMAPHORE)
ANY = pl.BlockSpec(memory_space=pl.ANY)
EFFECT = pltpu.SideEffectType.DATAFLOW_SIDE_EFFECTING

WEIGHTS = ['norm_ffn1', 'ffn1_gate', 'ffn1_up', 'ffn1_down', 'norm_mix', 'ab_w_in', 'ab_b_in', 'gmlp_w_s',
           'gmlp_b_s', 'gmlp_ln_g', 'gmlp_ln_b', 'conv_w', 'conv_b', 'conv_ln_g', 'conv_ln_b', 'ab_w_out',
           'ab_b_out', 'pool_w', 'pool_b', 'pool_scale', 'norm_xq', 'norm_xkv', 'xattn_wq', 'xattn_wk',
           'xattn_wv', 'xattn_wo', 'norm_ffn2', 'ffn2_gate', 'ffn2_up', 'ffn2_down', 'norm_final']
BIG = ['ffn1_gate', 'ffn1_up', 'ffn1_down', 'ab_w_in', 'ab_w_out', 'pool_w', 'xattn_wq', 'xattn_wk', 'xattn_wv',
       'xattn_wo', 'ffn2_gate', 'ffn2_up', 'ffn2_down']
EVEN_ONLY, ODD_ONLY = ['ab_w_in', 'ab_w_out'], ['pool_w']
SMALL = [n for n in WEIGHTS if n not in BIG]
SMALL_SHARDED = ['conv_w', 'pool_b', 'pool_scale']

NN = (((1,), (0,)), ((), ()))
NT = (((1,), (1,)), ((), ()))
TN_ = (((0,), (0,)), ((), ()))
_DIMS = {'nn': NN, 'nt': NT, 'tn': TN_}


def _pick(n, pref, unit=LANES):
    if n <= pref:
        return n
    t = (pref // unit) * unit
    while t >= unit:
        if n % t == 0:
            return t
        t -= unit
    return n


def _sds(shape, dtype):
    return jax.ShapeDtypeStruct(tuple(shape), dtype)


def _layer_names(l):
    mix = EVEN_ONLY if l % 2 == 0 else ODD_ONLY
    return ['ffn1_gate', 'ffn1_up', 'ffn1_down'] + mix + ['xattn_wq', 'xattn_wk', 'xattn_wv', 'xattn_wo',
                                                          'ffn2_gate', 'ffn2_up', 'ffn2_down']


def _stack_index(name, l):
    return l // 2 if name in EVEN_ONLY + ODD_ONLY else l


def _mm(name, grid, ins, pairs, outs, acc_shapes, epilogue, extras=()):
    n_in, n_out = len(ins), len(outs)
    nk = grid[2]

    def body(*refs):
        in_refs, out_refs, acc_refs = refs[:n_in], refs[n_in:n_in + n_out], refs[n_in + n_out:]
        k = pl.program_id(2)

        @pl.when(k == 0)
        def _():
            for acc in acc_refs:
                acc[...] = jnp.zeros_like(acc)

        for ai, bi, mode, ci in pairs:
            a = in_refs[ai][...].astype(BF16)
            b = in_refs[bi][...].astype(BF16)
            acc_refs[ci][...] += lax.dot_general(a, b, _DIMS[mode], preferred_element_type=F32)

        @pl.when(k == nk - 1)
        def _():
            res = epilogue([acc[...] for acc in acc_refs], [in_refs[e][...] for e in extras])
            for o, r in zip(out_refs, res):
                o[...] = r.astype(o.dtype)

    return pl.pallas_call(
        body, name=name, grid=grid,
        in_specs=[s for _, s in ins], out_specs=[s for _, s in outs], out_shape=[s for s, _ in outs],
        scratch_shapes=[pltpu.VMEM(s, F32) for s in acc_shapes],
        compiler_params=pltpu.CompilerParams(dimension_semantics=("parallel", "parallel", "arbitrary"),
                                             vmem_limit_bytes=VMEM_LIMIT),
    )(*[a for a, _ in ins])


def _bs(shape, fn):
    return pl.BlockSpec(shape, fn)


def _mm1(name, grid, ins, outs, compute):
    n_in = len(ins)

    def body(*refs):
        compute(refs[:n_in], refs[n_in:])

    return pl.pallas_call(
        body, name=name, grid=grid,
        in_specs=[s for _, s in ins], out_specs=[s for _, s in outs], out_shape=[s for s, _ in outs],
        compiler_params=pltpu.CompilerParams(dimension_semantics=("parallel", "parallel"),
                                             vmem_limit_bytes=VMEM_LIMIT),
    )(*[a for a, _ in ins])


def _dot(a, b, mode):
    return lax.dot_general(a.astype(BF16), b.astype(BF16), _DIMS[mode], preferred_element_type=F32)


def _row_chunks(rows):
    step = min(rows, EPI_ROWS)
    return [slice(r, r + step) for r in range(0, rows, step)]


def _rowwise(name, fn, ins, row_outs, acc_outs, tile):
    T = next(a.shape[0] for k, a in ins if k == 'row')
    n = T // tile
    per = tile // HALO if tile % HALO == 0 else 1
    last = T // HALO - 1
    in_specs = []
    for kind, a in ins:
        if kind == 'row':
            in_specs.append(pl.BlockSpec((tile, a.shape[1]), lambda i: (i, 0)))
        elif kind == 'prev':
            in_specs.append(pl.BlockSpec((HALO, a.shape[1]), lambda i: (jnp.maximum(i * per - 1, 0), 0)))
        elif kind == 'next':
            in_specs.append(pl.BlockSpec((HALO, a.shape[1]), lambda i: (jnp.minimum((i + 1) * per, last), 0)))
        else:
            in_specs.append(pl.BlockSpec(a.shape, lambda i, nd=a.ndim: (0,) * nd))
    n_in, n_row = len(ins), len(row_outs)
    out_shape = [_sds((T, c), dt) for c, dt in row_outs] + [_sds(s, F32) for s in acc_outs]
    out_specs = [pl.BlockSpec((tile, c), lambda i: (i, 0)) for c, _ in row_outs]
    out_specs += [pl.BlockSpec(s, lambda i, nd=len(s): (0,) * nd) for s in acc_outs]
    kinds = [k for k, _ in ins]

    def body(*refs):
        i = pl.program_id(0)
        vals = [r if k == 'cref' else r[...] for k, r in zip(kinds, refs[:n_in])]
        ro, ao = fn(i, *vals)
        for r, v in zip(refs[n_in:n_in + n_row], ro):
            r[...] = v.astype(r.dtype)
        for r, v in zip(refs[n_in + n_row:], ao):
            @pl.when(i == 0)
            def _(r=r, v=v):
                r[...] = v

            @pl.when(i > 0)
            def _(r=r, v=v):
                r[...] += v

    return pl.pallas_call(
        body, name=name, grid=(n,), in_specs=in_specs, out_specs=out_specs, out_shape=out_shape,
        compiler_params=pltpu.CompilerParams(dimension_semantics=("arbitrary",), vmem_limit_bytes=VMEM_LIMIT),
    )(*[a for _, a in ins])


def _rms(x, g):
    return x * lax.rsqrt(jnp.mean(x * x, axis=-1, keepdims=True) + EPS) * g


def _ln(x, g, b):
    mu = jnp.mean(x, axis=-1, keepdims=True)
    xc = x - mu
    var = jnp.mean(xc * xc, axis=-1, keepdims=True)
    return xc * lax.rsqrt(var + EPS) * g + b


def _gelu(x):
    return 0.5 * x * (1.0 + jnp.tanh(0.7978845608028654 * (x + 0.044715 * (x * x * x))))


def _silu(x):
    return x * jax.nn.sigmoid(x)


def _glu(a, g):
    return a * jax.nn.sigmoid(g)


def _row_ids(i, tile, rows):
    return i * tile + lax.broadcasted_iota(jnp.int32, (rows, 1), 0)


def _mesh_pos():
    return lax.axis_index("x"), lax.axis_index("y"), lax.axis_index("c")


def _other_chips(x, y):
    return [(1 - x, y), (x, 1 - y), (1 - x, 1 - y)]


def _remote(src, dst, send_sems, recv_sems, s, to):
    return pltpu.make_async_remote_copy(src_ref=src, dst_ref=dst, send_sem=send_sems.at[s], recv_sem=recv_sems.at[s],
                                        device_id=to, device_id_type=MESH)


def _gather_copies(refs, send_sems, recv_sems):
    x, y, c = _mesh_pos()
    me = 2 * x + y
    out = []
    for t, ref in enumerate(refs):
        rh = ref.shape[1] // 2
        half = pl.ds(c * rh, rh)
        for k, (cx, cy) in enumerate(_other_chips(x, y)):
            mine, theirs = ref.at[me, half], ref.at[2 * cx + cy, half]
            out.append((_remote(mine, mine, send_sems, recv_sems, 3 * t + k, (cx, cy, c)),
                        _remote(theirs, theirs, send_sems, recv_sems, 3 * t + k, (cx, cy, c))))
    return out


def _exchange_copies(refs, send_sems, recv_sems):
    x, y, c = _mesh_pos()
    n = len(refs) // 2
    out = []
    for t in range(n):
        part, land = refs[t], refs[n + t]
        for k, (cx, cy) in enumerate(_other_chips(x, y)):
            out.append((_remote(part.at[2 * cx + cy], land.at[k], send_sems, recv_sems, 3 * t + k, (cx, cy, c)),
                        _remote(land.at[k], land.at[k], send_sems, recv_sems, 3 * t + k, (cx, cy, c))))
    return out


def _split_start(name, thru, after, n_sems, copies, carry=()):
    n, nc = len(thru), len(carry)
    both = list(thru) + list(carry)

    def body(*refs):
        outs = refs[n + nc + 1:]
        send_sems, recv_sems, thru_refs, token = outs[0], outs[1], outs[2:2 + n], outs[2 + n + nc]
        for send, _ in copies(thru_refs, send_sems, recv_sems):
            send.start()
        token[...] = jnp.zeros_like(token)

    res = pl.pallas_call(
        body, name=name,
        out_shape=(pltpu.SemaphoreType.DMA((n_sems,)), pltpu.SemaphoreType.DMA((n_sems,)),
                   *[pltpu.HBM(b.shape, b.dtype) for b in both], _sds((8, LANES), F32)),
        in_specs=[HBM] * (n + nc) + [ANY],
        out_specs=(SEM, SEM, *[HBM] * (n + nc), pl.BlockSpec(memory_space=pltpu.VMEM)),
        input_output_aliases={i: 2 + i for i in range(n + nc)},
        compiler_params=pltpu.CompilerParams(has_side_effects=EFFECT),
    )(*[pltpu.with_memory_space_constraint(b, pltpu.HBM) for b in both], after)
    return res[0], res[1], list(res[2:2 + n]), list(res[2 + n:2 + n + nc]), res[2 + n + nc]


def _split_wait(name, thru, send_sems, recv_sems, after, copies):
    n = len(thru)

    def body(*refs):
        sends, recvs, outs = refs[n], refs[n + 1], refs[n + 3:]
        for send, arrival in copies(outs, sends, recvs):
            send.wait_send()
            arrival.wait_recv()

    res = pl.pallas_call(
        body, name=name, out_shape=tuple(pltpu.HBM(b.shape, b.dtype) for b in thru),
        in_specs=[HBM] * n + [SEM, SEM, ANY], out_specs=tuple([HBM] * n),
        input_output_aliases={i: i for i in range(n)},
        compiler_params=pltpu.CompilerParams(has_side_effects=EFFECT),
    )(*thru, send_sems, recv_sems, after)
    return list(res)


def _forward_halves(name, bufs):
    n = len(bufs)

    def body(*refs):
        outs, send_sems, recv_sems = refs[n:2 * n], refs[2 * n], refs[2 * n + 1]
        x, y, c = _mesh_pos()
        sibling = (x, y, 1 - c)
        sends, arrivals = [], []
        for t, ref in enumerate(outs):
            rh = ref.shape[1] // 2
            mine, other = pl.ds(c * rh, rh), pl.ds((1 - c) * rh, rh)
            for k, (cx, cy) in enumerate(_other_chips(x, y)):
                landed, coming = ref.at[2 * cx + cy, mine], ref.at[2 * cx + cy, other]
                cp = _remote(landed, landed, send_sems, recv_sems, 3 * t + k, sibling)
                cp.start()
                sends.append(cp)
                arrivals.append(_remote(coming, coming, send_sems, recv_sems, 3 * t + k, sibling))
        for a in arrivals:
            a.wait_recv()
        for cp in sends:
            cp.wait_send()

    res = pl.pallas_call(
        body, name=name, out_shape=tuple(_sds(b.shape, b.dtype) for b in bufs),
        in_specs=[HBM] * n, out_specs=tuple([HBM] * n), input_output_aliases={i: i for i in range(n)},
        scratch_shapes=[pltpu.SemaphoreType.DMA((3 * n,)), pltpu.SemaphoreType.DMA((3 * n,))],
    )(*bufs)
    return list(res)


def _swap_halves(name, gs):
    n = len(gs)

    def body(*refs):
        ins, outs, send_sems, recv_sems = refs[:n], refs[n:2 * n], refs[2 * n], refs[2 * n + 1]
        x, y, c = _mesh_pos()
        cps = []
        for t, (g_ref, o_ref) in enumerate(zip(ins, outs)):
            rh = g_ref.shape[1] // 2
            cp = _remote(g_ref.at[:, pl.ds((1 - c) * rh, rh)], o_ref, send_sems, recv_sems, t, (x, y, 1 - c))
            cp.start()
            cps.append(cp)
        for cp in cps:
            cp.wait_recv()
        for cp in cps:
            cp.wait_send()

    res = pl.pallas_call(
        body, name=name, out_shape=tuple(_sds((g.shape[0], g.shape[1] // 2, g.shape[2]), g.dtype) for g in gs),
        in_specs=[HBM] * n, out_specs=tuple([HBM] * n),
        scratch_shapes=[pltpu.SemaphoreType.DMA((n,)), pltpu.SemaphoreType.DMA((n,))],
    )(*gs)
    return list(res)


def _join_halves(name, gfulls, idx):
    n = len(gfulls)

    def body(*refs):
        outs, send_sems, recv_sems = refs[n:2 * n], refs[2 * n], refs[2 * n + 1]
        x, y, c = _mesh_pos()
        cps, arrivals = [], []
        for t, ref in enumerate(outs):
            rh = ref.shape[1] // 2
            mine, theirs = ref.at[idx[t], pl.ds(c * rh, rh)], ref.at[idx[t], pl.ds((1 - c) * rh, rh)]
            cp = _remote(mine, mine, send_sems, recv_sems, t, (x, y, 1 - c))
            cp.start()
            cps.append(cp)
            arrivals.append(_remote(theirs, theirs, send_sems, recv_sems, t, (x, y, 1 - c)))
        for a in arrivals:
            a.wait_recv()
        for cp in cps:
            cp.wait_send()

    res = pl.pallas_call(
        body, name=name, out_shape=tuple(_sds(g.shape, g.dtype) for g in gfulls),
        in_specs=[HBM] * n, out_specs=tuple([HBM] * n), input_output_aliases={i: i for i in range(n)},
        scratch_shapes=[pltpu.SemaphoreType.DMA((n,)), pltpu.SemaphoreType.DMA((n,))],
    )(*gfulls)
    return list(res)


def _gather_all(name, buf, after):
    def body(b_ref, after_ref, out_ref, send_sems, recv_sems, local_sem):
        x, y, c = _mesh_pos()
        me = 4 * x + 2 * y + c
        local = pltpu.make_async_copy(b_ref, out_ref.at[me], local_sem)
        local.start()
        peers = []
        for k in range(1, N_DEV):
            peers.append((1 - x if k & 4 else x, 1 - y if k & 2 else y, 1 - c if k & 1 else c))
        sends = []
        for k, peer in enumerate(peers):
            cp = _remote(b_ref, out_ref.at[me], send_sems, recv_sems, k, peer)
            cp.start()
            sends.append(cp)
        for k, (px, py, pc) in enumerate(peers):
            slot = out_ref.at[4 * px + 2 * py + pc]
            _remote(slot, slot, send_sems, recv_sems, k, (px, py, pc)).wait_recv()
        for cp in sends:
            cp.wait_send()
        local.wait()

    return pl.pallas_call(
        body, name=name, out_shape=_sds((N_DEV,) + buf.shape, buf.dtype), in_specs=[HBM, ANY], out_specs=HBM,
        scratch_shapes=[pltpu.SemaphoreType.DMA((N_DEV - 1,)), pltpu.SemaphoreType.DMA((N_DEV - 1,)),
                        pltpu.SemaphoreType.DMA],
    )(buf, after)


def _scalars(*vals):
    return jnp.stack([jnp.asarray(v, jnp.int32) for v in vals])


def _cast_slab(name, w3, li, chip):
    _, R, C = w3.shape
    tr = _pick(R, ROW_TILE, 16)

    def body(s_ref, w_ref, o_ref):
        o_ref[...] = w_ref[...].astype(o_ref.dtype)

    grid_spec = pltpu.PrefetchScalarGridSpec(
        num_scalar_prefetch=1, grid=(R // tr,),
        in_specs=[pl.BlockSpec((None, tr, C), lambda r, s: (li, r, 0))],
        out_specs=pl.BlockSpec((None, tr, C), lambda r, s: (s[0], r, 0)))
    return pl.pallas_call(
        body, name=name, grid_spec=grid_spec, out_shape=_sds((N_CHIPS, R, C), BF16),
        compiler_params=pltpu.CompilerParams(dimension_semantics=("arbitrary",), vmem_limit_bytes=VMEM_LIMIT),
    )(_scalars(chip), w3)


def _add_halves(name, g, recv, c):
    _, R, C = g.shape
    rh = R // 2
    tr = _pick(rh, 512, 16)
    nr = rh // tr

    def body(s_ref, g_ref, a_ref, o_ref):
        o_ref[...] = (g_ref[...].astype(F32) + a_ref[...].astype(F32)).astype(o_ref.dtype)

    blk = (None, tr, C)
    grid_spec = pltpu.PrefetchScalarGridSpec(
        num_scalar_prefetch=1, grid=(N_CHIPS, nr),
        in_specs=[pl.BlockSpec(blk, lambda j, r, s: (j, s[0] * nr + r, 0)),
                  pl.BlockSpec(blk, lambda j, r, s: (j, r, 0))],
        out_specs=pl.BlockSpec(blk, lambda j, r, s: (j, r, 0)))
    return pl.pallas_call(
        body, name=name, grid_spec=grid_spec, out_shape=_sds((N_CHIPS, rh, C), g.dtype),
        compiler_params=pltpu.CompilerParams(dimension_semantics=("arbitrary",) * 2, vmem_limit_bytes=VMEM_LIMIT),
    )(_scalars(c), g, recv)


def _sum_into(name, p, recv, gfull, li, chip, c):
    _, rh, C = p.shape
    tr = _pick(rh, 512, 16)
    nr = rh // tr

    def body(s_ref, p_ref, r_ref, g_ref, o_ref):
        acc = p_ref[...].astype(F32)
        for k in range(3):
            acc = acc + r_ref[k].astype(F32)
        o_ref[...] = acc

    grid_spec = pltpu.PrefetchScalarGridSpec(
        num_scalar_prefetch=1, grid=(nr,),
        in_specs=[pl.BlockSpec((None, tr, C), lambda r, s: (s[0], r, 0)),
                  pl.BlockSpec((3, tr, C), lambda r, s: (0, r, 0)), HBM],
        out_specs=pl.BlockSpec((None, tr, C), lambda r, s: (li, s[1] * nr + r, 0)))
    return pl.pallas_call(
        body, name=name, grid_spec=grid_spec, out_shape=_sds(gfull.shape, F32), input_output_aliases={3: 0},
        compiler_params=pltpu.CompilerParams(dimension_semantics=("arbitrary",), vmem_limit_bytes=VMEM_LIMIT),
    )(_scalars(chip, c), p, recv, gfull)


def _sum_slots(name, buf):
    _, n, _ = buf.shape

    def body(b_ref, o_ref):
        acc = b_ref[0]
        for k in range(1, N_DEV):
            acc = acc + b_ref[k]
        o_ref[...] = acc

    return pl.pallas_call(
        body, name=name, grid=(n // PACK_ROWS,), out_shape=_sds((n, LANES), F32),
        in_specs=[pl.BlockSpec((N_DEV, PACK_ROWS, LANES), lambda i: (0, i, 0))],
        out_specs=pl.BlockSpec((PACK_ROWS, LANES), lambda i: (i, 0)),
    )(buf)


def _adam_tile(i, w, g, m, v):
    m = ADAM_B1 * m + (1.0 - ADAM_B1) * g
    v = ADAM_B2 * v + (1.0 - ADAM_B2) * (g * g)
    m_hat = m / (1.0 - ADAM_B1 ** ADAM_STEP)
    v_hat = v / (1.0 - ADAM_B2 ** ADAM_STEP)
    delta = -ADAM_LR * (m_hat / (jnp.sqrt(v_hat) + ADAM_EPS) + ADAM_WD * w)
    return [delta, m, v], []


def _adam(name, w, g, m, v):
    rows, C = w.shape
    tile = _pick(rows, ROW_TILE, 8)
    return _rowwise(name, _adam_tile, [('row', w), ('row', g), ('row', m), ('row', v)], [(C, F32)] * 3, [], tile)


def _adam_rows(name, w, g, m, v, lo, hi, unit, prev=None, after=None):
    rows, C = w.shape
    tile = _pick(unit, ROW_TILE, 8)
    first = lo // tile
    spec = pl.BlockSpec((tile, C), lambda i: (i + first, 0))
    ins, in_specs = [w, g, m, v], [spec] * 4
    if prev is not None:
        ins, in_specs = ins + list(prev), in_specs + [ANY] * 4
    if after is not None:
        ins, in_specs = ins + [after], in_specs + [ANY]

    def body(*refs):
        n_in = len(ins)
        outs, _ = _adam_tile(0, *[r[...] for r in refs[:4]])
        refs[n_in][...] = refs[1][...]
        for o, val in zip(refs[n_in + 1:n_in + 4], outs):
            o[...] = val

    return pl.pallas_call(
        body, name=name, grid=((hi - lo) // tile,), in_specs=in_specs, out_specs=[spec] * 4,
        out_shape=[_sds((rows, C), F32)] * 4,
        input_output_aliases={4 + k: k for k in range(4)} if prev is not None else {},
        compiler_params=pltpu.CompilerParams(dimension_semantics=("arbitrary",), vmem_limit_bytes=VMEM_LIMIT),
    )(*ins)


def _pack(arrs):
    flat = jnp.concatenate([a.reshape(-1).astype(F32) for a in arrs])
    unit = PACK_ROWS * LANES
    n = -(-flat.shape[0] // unit) * unit
    return jnp.pad(flat, (0, n - flat.shape[0])).reshape(-1, LANES)


def _unpack(buf, shapes):
    flat = buf.reshape(-1)
    out, off = [], 0
    for s in shapes:
        n = 1
        for d in s:
            n *= d
        out.append(flat[off:off + n].reshape(s))
        off += n
    return out


def _norm_fwd(name, x, g, tile):
    D = x.shape[1]
    return _rowwise(name, lambda i, xv, gv: ([_rms(xv, gv)], []), [('row', x), ('const', g.reshape(1, D))],
                    [(D, BF16)], [], tile)[0]


def _norm_bwd(name, x, g, dh, dxo, tile):
    D = x.shape[1]
    if dxo is None:
        def fn(i, xv, dhv, gv):
            _, vjp = jax.vjp(_rms, xv, gv)
            return [], [vjp(dhv)[1]]
        return _rowwise(name, fn, [('row', x), ('row', dh), ('const', g.reshape(1, D))], [], [(1, D)], tile)[0]

    def fn(i, xv, dhv, dxv, gv):
        _, vjp = jax.vjp(_rms, xv, gv)
        dx, dg = vjp(dhv)
        return [dxv + dx, dxv + dx], [dg]
    return _rowwise(name, fn, [('row', x), ('row', dh), ('row', dxo), ('const', g.reshape(1, D))],
                    [(D, F32), (D, BF16)], [(1, D)], tile)


def _ffn_fwd(tag, x, g, wg, wu, wd, tile):
    T, D = x.shape
    fs = wg.shape[2]
    F = N_CHIPS * fs
    tm, tn = _pick(T, TM_BIG), _pick(D, TN_SMALL)
    h = _norm_fwd(f"{tag}_norm", x, g, tile)
    hspec = _bs((tm, D), lambda j, i: (i, 0))
    wspec = _bs((None, D, fs), lambda j, i: (j, 0, 0))
    ospec = _bs((tm, fs), lambda j, i: (i, j))

    def gate(ins, outs):
        outs[0][...] = _dot(ins[0][...], ins[1][...], 'nn').astype(BF16)

    a = _mm1(f"{tag}_gate", (N_CHIPS, T // tm), [(h, hspec), (wg, wspec)], [(_sds((T, F), BF16), ospec)], gate)[0]

    def up(ins, outs):
        bv = _dot(ins[0][...], ins[1][...], 'nn')
        for rows in _row_chunks(tm):
            bb = bv[rows]
            outs[0][rows, :] = bb.astype(BF16)
            outs[1][rows, :] = (_silu(ins[2][rows, :].astype(F32)) * bb).astype(BF16)

    b, s = _mm1(f"{tag}_up", (N_CHIPS, T // tm), [(h, hspec), (wu, wspec), (a, ospec)],
                [(_sds((T, F), BF16), ospec)] * 2, up)

    def down(ins, outs):
        outs[0][...] = ins[2][...] + 0.5 * _dot(ins[0][...], ins[1][...].reshape(F, tn), 'nn')

    xspec = _bs((tm, tn), lambda i, j: (i, j))
    xo = _mm1(f"{tag}_down", (T // tm, D // tn),
              [(s, _bs((tm, F), lambda i, j: (i, 0))), (wd, _bs((N_CHIPS, fs, tn), lambda i, j: (0, 0, j))), (x, xspec)],
              [(_sds((T, D), F32), xspec)], down)[0]
    return xo, (x, h, a, b, s)


def _ffn_bwd(tag, dxo, dxb, saved, g, wg, wu, wd, tile):
    x, h, a, b, s = saved
    T, D = x.shape
    fs = wg.shape[2]
    F = N_CHIPS * fs
    tm = _pick(T, TM_MID)
    tspec = _bs((tm, fs), lambda j, i: (i, j))

    def ds_fn(ins, outs):
        d = _dot(ins[0][...], ins[1][...], 'nt')
        for rows in _row_chunks(tm):
            ds = 0.5 * d[rows]
            av, bv = ins[2][rows, :].astype(F32), ins[3][rows, :].astype(F32)
            sig = jax.nn.sigmoid(av)
            outs[0][rows, :] = (ds * bv * (sig * (1.0 + av * (1.0 - sig)))).astype(BF16)
            outs[1][rows, :] = (ds * (av * sig)).astype(BF16)

    da, db = _mm1(f"{tag}_ds", (N_CHIPS, T // tm),
                  [(dxb, _bs((tm, D), lambda j, i: (i, 0))), (wd, _bs((None, fs, D), lambda j, i: (j, 0, 0))),
                   (a, tspec), (b, tspec)], [(_sds((T, F), BF16), tspec)] * 2, ds_fn)

    tn = _pick(D, TN_BIG)

    def dwd_fn(ins, outs):
        outs[0][...] = (0.5 * _dot(ins[0][...], ins[1][...], 'tn')).astype(BF16)

    gd = _mm1(f"{tag}_dwd", (N_CHIPS, D // tn),
              [(s, _bs((T, fs), lambda i, j: (0, i))), (dxb, _bs((T, tn), lambda i, j: (0, j)))],
              [(_sds(wd.shape, BF16), _bs((None, fs, tn), lambda i, j: (i, 0, j)))], dwd_fn)[0]

    tmd = _pick(D, TM_BIG)

    def dw_fn(ins, outs):
        outs[0][...] = _dot(ins[0][...], ins[1][...], 'tn').astype(BF16)

    def dw(name, dy):
        return _mm1(name, (N_CHIPS, D // tmd),
                    [(h, _bs((T, tmd), lambda j, i: (0, i))), (dy, _bs((T, fs), lambda j, i: (0, j)))],
                    [(_sds(wg.shape, BF16), _bs((None, tmd, fs), lambda j, i: (j, i, 0)))], dw_fn)[0]

    gg, gu = dw(f"{tag}_dwg", da), dw(f"{tag}_dwu", db)

    tn3 = _pick(D, TN_SMALL)

    def dh_fn(ins, outs):
        acc = None
        for j in range(N_CHIPS):
            cols = slice(j * fs, (j + 1) * fs)
            t = _dot(ins[0][:, cols], ins[1][j], 'nt') + _dot(ins[2][:, cols], ins[3][j], 'nt')
            acc = t if acc is None else acc + t
        outs[0][...] = acc

    aspec = _bs((tm, F), lambda i, j: (i, 0))
    wtspec = _bs((N_CHIPS, tn3, fs), lambda i, j: (0, j, 0))
    dh = _mm1(f"{tag}_dh", (T // tm, D // tn3), [(da, aspec), (wg, wtspec), (db, aspec), (wu, wtspec)],
              [(_sds((T, D), F32), _bs((tm, tn3), lambda i, j: (i, j)))], dh_fn)[0]
    dx, dxb2, dg = _norm_bwd(f"{tag}_norm_bwd", x, g, dh, dxo, tile)
    return dx, dxb2, dg, gg, gu, gd


def _proj_rows(name, a, w, out_dtype, extras=(), epilogue=None):
    M, K = a.shape
    ks, N = w.shape[1], w.shape[2]
    tm, tn = _pick(M, TM_BIG), _pick(N, TN_MID)
    ins = [(a, _bs((tm, K), lambda i, j: (i, 0))), (w, _bs((N_CHIPS, ks, tn), lambda i, j: (0, 0, j)))]
    for e in extras:
        if e.shape[0] == 1:
            ins.append((e, _bs((1, tn), lambda i, j: (0, j))))
        else:
            ins.append((e, _bs((tm, tn), lambda i, j: (i, j))))

    def fn(refs, outs):
        acc = _dot(refs[0][...], refs[1][...].reshape(K, tn), 'nn')
        if epilogue is not None:
            acc = epilogue(acc, [r[...] for r in refs[2:]])
        outs[0][...] = acc.astype(out_dtype)

    return _mm1(name, (M // tm, N // tn), ins, [(_sds((M, N), out_dtype), _bs((tm, tn), lambda i, j: (i, j)))], fn)[0]


def _proj_rows_t(name, pairs, out_dtype):
    dy0, w0 = pairs[0]
    M, N = dy0.shape
    ks = w0.shape[1]
    tm = _pick(M, TM_BIG)
    ins = []
    for dy, w in pairs:
        ins.append((dy, _bs((tm, N), lambda i, j: (i, 0))))
        ins.append((w, _bs((None, ks, N), lambda i, j: (j, 0, 0))))

    def fn(refs, outs):
        acc = _dot(refs[0][...], refs[1][...], 'nt')
        for p in range(1, len(pairs)):
            acc = acc + _dot(refs[2 * p][...], refs[2 * p + 1][...], 'nt')
        outs[0][...] = acc.astype(out_dtype)

    return _mm1(name, (M // tm, N_CHIPS), ins,
                [(_sds((M, N_CHIPS * ks), out_dtype), _bs((tm, ks), lambda i, j: (i, j)))], fn)[0]


def _grad_rows(name, a, dys):
    T, K = a.shape
    N = dys[0].shape[1]
    ks = K // N_CHIPS
    tn = _pick(N, TN_BIG)
    ins = [(a, _bs((T, ks), lambda i, j: (0, i)))] + [(dy, _bs((T, tn), lambda i, j: (0, j))) for dy in dys]

    def fn(refs, outs):
        av = refs[0][...]
        for p in range(len(dys)):
            outs[p][...] = _dot(av, refs[1 + p][...], 'tn').astype(BF16)

    gspec = _bs((None, ks, tn), lambda i, j: (i, 0, j))
    return _mm1(name, (N_CHIPS, N // tn), ins, [(_sds((N_CHIPS, ks, N), BF16), gspec)] * len(dys), fn)


def _softmax_rows(s):
    s = s - jnp.max(s, axis=-1, keepdims=True)
    p = jnp.exp(s)
    return p / jnp.sum(p, axis=-1, keepdims=True)


def _attn_fwd_tile(hd, scale):
    def fn(i, q, k, v):
        outs = []
        for h in range(N_MEM_HEADS):
            sl = slice(h * hd, (h + 1) * hd)
            p = _softmax_rows(lax.dot_general(q[:, sl], k[:, sl], NT, preferred_element_type=F32) * scale)
            outs.append(lax.dot_general(p.astype(BF16), v[:, sl], NN, preferred_element_type=F32))
        return [jnp.concatenate(outs, axis=1)], []
    return fn


def _attn_bwd_tile(hd, scale):
    def fn(i, q, do, k, v):
        dqs, dks, dvs = [], [], []
        for h in range(N_MEM_HEADS):
            sl = slice(h * hd, (h + 1) * hd)
            qh, kh, vh, doh = q[:, sl], k[:, sl], v[:, sl], do[:, sl]
            p = _softmax_rows(lax.dot_general(qh, kh, NT, preferred_element_type=F32) * scale)
            dvs.append(lax.dot_general(p.astype(BF16), doh, TN_, preferred_element_type=F32))
            dp = lax.dot_general(doh, vh, NT, preferred_element_type=F32)
            ds = (p * (dp - jnp.sum(dp * p, axis=-1, keepdims=True)) * scale).astype(BF16)
            dqs.append(lax.dot_general(ds, kh, NN, preferred_element_type=F32))
            dks.append(lax.dot_general(ds, qh, TN_, preferred_element_type=F32))
        return [jnp.concatenate(dqs, axis=1)], [jnp.concatenate(dks, axis=1), jnp.concatenate(dvs, axis=1)]
    return fn


def _attn_fwd(l, x, mem, gq, gkv, W, tile):
    T, D = x.shape
    M = mem.shape[0]
    hd = D // N_MEM_HEADS
    hq = _norm_fwd(f"xa{l}_normq", x, gq, tile)
    mn = _norm_fwd(f"xa{l}_normkv", mem, gkv, _pick(M, tile, 16))
    q = _proj_rows(f"xa{l}_q", hq, W['xattn_wq'], BF16)
    k = _proj_rows(f"xa{l}_k", mn, W['xattn_wk'], BF16)
    v = _proj_rows(f"xa{l}_v", mn, W['xattn_wv'], BF16)
    o = _rowwise(f"xa{l}_attn", _attn_fwd_tile(hd, hd ** -0.5), [('row', q), ('const', k), ('const', v)],
                 [(D, BF16)], [], tile)[0]
    xo = _proj_rows(f"xa{l}_o", o, W['xattn_wo'], F32, extras=(x,), epilogue=lambda acc, ex: ex[0] + acc)
    return xo, (x, hq, mn, q, k, v, o)


def _attn_bwd(l, dxo, dxb, saved, mem, gq, gkv, W, G, tile):
    x, hq, mn, q, k, v, o = saved
    T, D = x.shape
    M = mem.shape[0]
    hd = D // N_MEM_HEADS
    do = _proj_rows_t(f"xa{l}_do", [(dxb, W['xattn_wo'])], BF16)
    G['xattn_wo'] = _grad_rows(f"xa{l}_dwo", o, [dxb])[0]
    dq, dk, dv = _rowwise(f"xa{l}_attn_bwd", _attn_bwd_tile(hd, hd ** -0.5),
                          [('row', q), ('row', do), ('const', k), ('const', v)], [(D, BF16)], [(M, D), (M, D)], tile)
    dhq = _proj_rows_t(f"xa{l}_dhq", [(dq, W['xattn_wq'])], F32)
    G['xattn_wq'] = _grad_rows(f"xa{l}_dwq", hq, [dq])[0]
    dmn = _proj_rows_t(f"xa{l}_dmn", [(dk, W['xattn_wk']), (dv, W['xattn_wv'])], F32)
    G['xattn_wk'], G['xattn_wv'] = _grad_rows(f"xa{l}_dwkv", mn, [dk, dv])
    dx, dxb2, dgq = _norm_bwd(f"xa{l}_normq_bwd", x, gq, dhq, dxo, tile)
    dgkv = _norm_bwd(f"xa{l}_normkv_bwd", mem, gkv, dmn, None, _pick(M, tile, 16))
    return dx, dxb2, dgq, dgkv


def _chunk_mask():
    p = lax.broadcasted_iota(jnp.int32, (GMLP_BLOCK, GMLP_BLOCK), 0)
    q = lax.broadcasted_iota(jnp.int32, (GMLP_BLOCK, GMLP_BLOCK), 1)
    return (q // CHUNK) <= (p // CHUNK)


def _spatial_fwd(vn, ws_ref, bsf, mask, hd):
    vb = vn.astype(BF16)
    wsm = [jnp.where(mask, ws_ref[h], 0.0).astype(BF16) for h in range(A_HEADS)]
    rows = []
    for n in range(vn.shape[0] // GMLP_BLOCK):
        blk = vb[n * GMLP_BLOCK:(n + 1) * GMLP_BLOCK]
        cols = [lax.dot_general(wsm[h], blk[:, h * hd:(h + 1) * hd], NN, preferred_element_type=F32)
                for h in range(A_HEADS)]
        rows.append(jnp.concatenate(cols, axis=1) + bsf)
    return jnp.concatenate(rows, axis=0)


def _spatial_bwd(dsp, vn, ws_ref, mask, hd):
    vb, db16 = vn.astype(BF16), dsp.astype(BF16)
    wsm = [jnp.where(mask, ws_ref[h], 0.0).astype(BF16) for h in range(A_HEADS)]
    dws = [jnp.zeros((GMLP_BLOCK, GMLP_BLOCK), F32) for _ in range(A_HEADS)]
    dbs = jnp.zeros((GMLP_BLOCK, vn.shape[1]), F32)
    rows = []
    for n in range(vn.shape[0] // GMLP_BLOCK):
        sl = slice(n * GMLP_BLOCK, (n + 1) * GMLP_BLOCK)
        cols = []
        for h in range(A_HEADS):
            hs = slice(h * hd, (h + 1) * hd)
            cols.append(lax.dot_general(wsm[h], db16[sl, hs], TN_, preferred_element_type=F32))
            dws[h] = dws[h] + lax.dot_general(db16[sl, hs], vb[sl, hs], NT, preferred_element_type=F32)
        rows.append(jnp.concatenate(cols, axis=1))
        dbs = dbs + dsp[sl]
    dws = [jnp.where(mask, d, 0.0) for d in dws]
    return jnp.concatenate(rows, axis=0), dws, dbs


def _conv_taps(cat, cw_ref, kw, tile):
    acc = jnp.zeros((tile, cat.shape[1]), F32)
    for k in range(kw):
        sh = kw - 1 - k
        r = cat if sh == 0 else pltpu.roll(cat, sh, 0)
        acc = acc + r[HALO:] * cw_ref[k:k + 1, :]
    return acc


def _mix_fwd_tile(A, B, kw, tile):
    hd = A // A_HEADS

    def fn(i, z, zp, ws_ref, bsf, glg, glb, cw_ref, cb, clg, clb):
        mask = _chunk_mask()
        u = _gelu(z[:, :A])
        vn = _ln(_gelu(z[:, A:2 * A]), glg, glb)
        ya = u * _spatial_fwd(vn, ws_ref, bsf, mask, hd)
        hb = _glu(z[:, 2 * A:2 * A + B], z[:, 2 * A + B:])
        hp = jnp.where(i > 0, _glu(zp[:, 2 * A:2 * A + B], zp[:, 2 * A + B:]), 0.0)
        conv = _conv_taps(jnp.concatenate([hp, hb], axis=0), cw_ref, kw, tile) + cb
        yb = _silu(_ln(conv, clg, clb))
        return [jnp.concatenate([ya, yb], axis=1)], []
    return fn


def _mix_bwd1_tile(A, B, kw, tile):
    hd = A // A_HEADS

    def fn(i, z, zp, dy, dxo, ws_ref, bsf, glg, glb, cw_ref, cb, clg, clb):
        mask = _chunk_mask()
        dya, dyb = dy[:, :A], dy[:, A:]
        zu, zv = z[:, :A], z[:, A:2 * A]
        u, vjp_u = jax.vjp(_gelu, zu)
        vn, vjp_v = jax.vjp(lambda t, g, b: _ln(_gelu(t), g, b), zv, glg, glb)
        sp = _spatial_fwd(vn, ws_ref, bsf, mask, hd)
        dzu = vjp_u(dya * sp)[0]
        dvn, dws, dbs = _spatial_bwd(dya * u, vn, ws_ref, mask, hd)
        dzv, dglg, dglb = vjp_v(dvn)
        hb = _glu(z[:, 2 * A:2 * A + B], z[:, 2 * A + B:])
        hp = jnp.where(i > 0, _glu(zp[:, 2 * A:2 * A + B], zp[:, 2 * A + B:]), 0.0)
        cat = jnp.concatenate([hp, hb], axis=0)
        conv = _conv_taps(cat, cw_ref, kw, tile) + cb
        _, vjp_c = jax.vjp(lambda t, g, b: _silu(_ln(t, g, b)), conv, clg, clb)
        dconv, dclg, dclb = vjp_c(dyb)
        tap = lax.broadcasted_iota(jnp.int32, (HALO, 1), 0)
        dcw = jnp.zeros((HALO, B), F32)
        for k in range(kw):
            sh = kw - 1 - k
            r = cat if sh == 0 else pltpu.roll(cat, sh, 0)
            dcw = dcw + jnp.where(tap == k, jnp.sum(dconv * r[HALO:], axis=0, keepdims=True), 0.0)
        dcb = jnp.sum(dconv, axis=0, keepdims=True)
        dbo = jnp.sum(dxo, axis=0, keepdims=True)
        dws = jnp.concatenate([d[None] for d in dws], axis=0)
        return [jnp.concatenate([dzu, dzv], axis=1), dconv], [dws, dbs, dglg, dglb, dcw, dcb, dclg, dclb, dbo]
    return fn


def _mix_bwd2_tile(A, B, kw, tile, n_tiles):
    def fn(i, z, dza, dc, dcn, cw_ref):
        dcn = jnp.where(i < n_tiles - 1, dcn, 0.0)
        cat = jnp.concatenate([dc, dcn], axis=0)
        n = tile + HALO
        dhb = jnp.zeros((tile, B), F32)
        for k in range(kw):
            sh = kw - 1 - k
            r = cat if sh == 0 else pltpu.roll(cat, n - sh, 0)
            dhb = dhb + r[:tile] * cw_ref[k:k + 1, :]
        _, vjp_g = jax.vjp(_glu, z[:, 2 * A:2 * A + B], z[:, 2 * A + B:])
        da, dg = vjp_g(dhb)
        dz = jnp.concatenate([dza, da, dg], axis=1)
        return [dz], [jnp.sum(dz, axis=0, keepdims=True)]
    return fn


def _even_consts(p, e, A, B, kw):
    hd = A // A_HEADS
    bsf = jnp.repeat(p['gmlp_b_s'][e].T, hd, axis=1)
    cw = jnp.pad(p['conv_w_full'][e], ((0, HALO - kw), (0, 0)))
    return [('cref', p['gmlp_w_s'][e]), ('const', bsf), ('const', p['gmlp_ln_g'][e].reshape(1, A)),
            ('const', p['gmlp_ln_b'][e].reshape(1, A)), ('cref', cw), ('const', p['conv_b'][e].reshape(1, B)),
            ('const', p['conv_ln_g'][e].reshape(1, B)), ('const', p['conv_ln_b'][e].reshape(1, B))]


def _even_fwd(l, e, x, gm, p, W, tile):
    T, D = x.shape
    w_in, w_out = W['ab_w_in'], W['ab_w_out']
    zs = w_in.shape[2]
    Z = N_CHIPS * zs
    A = p['gmlp_ln_g'].shape[1]
    B = p['conv_b'].shape[1]
    kw = p['conv_w_full'].shape[1]
    tm = _pick(T, TM_BIG)
    h = _norm_fwd(f"mix{l}_norm", x, gm, tile)

    def in_fn(refs, outs):
        outs[0][...] = _dot(refs[0][...], refs[1][...], 'nn') + refs[2][...]

    z = _mm1(f"mix{l}_in", (N_CHIPS, T // tm),
             [(h, _bs((tm, D), lambda j, i: (i, 0))), (w_in, _bs((None, D, zs), lambda j, i: (j, 0, 0))),
              (p['ab_b_in'][e].reshape(1, Z), _bs((1, zs), lambda j, i: (0, j)))],
             [(_sds((T, Z), F32), _bs((tm, zs), lambda j, i: (i, j)))], in_fn)[0]
    consts = _even_consts(p, e, A, B, kw)
    ycat = _rowwise(f"mix{l}_mid", _mix_fwd_tile(A, B, kw, tile), [('row', z), ('prev', z)] + consts,
                    [(A + B, BF16)], [], tile)[0]
    xo = _proj_rows(f"mix{l}_out", ycat, w_out, F32, extras=(x, p['ab_b_out'][e].reshape(1, D)),
                    epilogue=lambda acc, ex: ex[0] + acc + ex[1])
    return xo, (x, h, z, ycat)


def _even_bwd(l, e, dxo, dxb, saved, gm, p, W, G, tile):
    x, h, z, ycat = saved
    T, D = x.shape
    w_in, w_out = W['ab_w_in'], W['ab_w_out']
    zs = w_in.shape[2]
    Z = N_CHIPS * zs
    A = p['gmlp_ln_g'].shape[1]
    B = p['conv_b'].shape[1]
    kw = p['conv_w_full'].shape[1]
    hd = A // A_HEADS
    dycat = _proj_rows_t(f"mix{l}_dycat", [(dxb, w_out)], F32)
    G['ab_w_out'] = _grad_rows(f"mix{l}_dwout", ycat, [dxb])[0]
    consts = _even_consts(p, e, A, B, kw)
    accs = [(A_HEADS, GMLP_BLOCK, GMLP_BLOCK), (GMLP_BLOCK, A), (1, A), (1, A), (HALO, B), (1, B), (1, B), (1, B),
            (1, D)]
    dza, dconv, dws, dbs, dglg, dglb, dcw, dcb, dclg, dclb, dbo = _rowwise(
        f"mix{l}_mid_bwd1", _mix_bwd1_tile(A, B, kw, tile),
        [('row', z), ('prev', z), ('row', dycat), ('row', dxo)] + consts, [(2 * A, F32), (B, F32)], accs, tile)
    dz, dbin = _rowwise(f"mix{l}_mid_bwd2", _mix_bwd2_tile(A, B, kw, tile, T // tile),
                        [('row', z), ('row', dza), ('row', dconv), ('next', dconv), consts[4]],
                        [(Z, BF16)], [(1, Z)], tile)
    tmd = _pick(D, TM_BIG)

    def dwin_fn(refs, outs):
        outs[0][...] = _dot(refs[0][...], refs[1][...], 'tn').astype(BF16)

    G['ab_w_in'] = _mm1(f"mix{l}_dwin", (N_CHIPS, D // tmd),
                        [(h, _bs((T, tmd), lambda j, i: (0, i))), (dz, _bs((T, zs), lambda j, i: (0, j)))],
                        [(_sds(w_in.shape, BF16), _bs((None, tmd, zs), lambda j, i: (j, i, 0)))], dwin_fn)[0]
    tm, tn = _pick(T, TM_MID), _pick(D, TN_SMALL)

    def dh_fn(refs, outs):
        acc = None
        for j in range(N_CHIPS):
            t = _dot(refs[0][:, j * zs:(j + 1) * zs], refs[1][j], 'nt')
            acc = t if acc is None else acc + t
        outs[0][...] = acc

    dh = _mm1(f"mix{l}_dh", (T // tm, D // tn),
              [(dz, _bs((tm, Z), lambda i, j: (i, 0))), (w_in, _bs((N_CHIPS, tn, zs), lambda i, j: (0, j, 0)))],
              [(_sds((T, D), F32), _bs((tm, tn), lambda i, j: (i, j)))], dh_fn)[0]
    dx, dxb2, dgm = _norm_bwd(f"mix{l}_norm_bwd", x, gm, dh, dxo, tile)
    small = {'ab_b_in': dbin.reshape(Z), 'gmlp_w_s': dws, 'gmlp_b_s': dbs.reshape(GMLP_BLOCK, A_HEADS, hd).sum(-1).T,
             'gmlp_ln_g': dglg.reshape(A), 'gmlp_ln_b': dglb.reshape(A), 'conv_w': dcw[:kw], 'conv_b': dcb.reshape(B),
             'conv_ln_g': dclg.reshape(B), 'conv_ln_b': dclb.reshape(B), 'ab_b_out': dbo.reshape(D)}
    return dx, dxb2, dgm, small


def _pool_counts(t, cg):
    return jnp.concatenate([jnp.broadcast_to(jnp.minimum(t + 1, w).astype(F32), (t.shape[0], cg))
                            for w in POOL_WINDOWS], axis=1)


def _window_sums(cat, cg, back):
    n = cat.shape[0]
    outs = []
    for gi, w in enumerate(POOL_WINDOWS):
        s = cat[:, gi * cg:(gi + 1) * cg]
        step = 1
        while step < w:
            s = s + pltpu.roll(s, step if back else n - step, 0)
            step *= 2
        outs.append(s)
    return jnp.concatenate(outs, axis=1)


def _pool_fwd_tile(D, tile):
    cg = D // len(POOL_WINDOWS)

    def fn(i, x, xp, g):
        h = _rms(x, g)
        hp = jnp.where(i > 0, _rms(xp, g), 0.0)
        sums = _window_sums(jnp.concatenate([hp, h], axis=0), cg, True)[HALO:]
        return [sums / _pool_counts(_row_ids(i, tile, tile), cg) - h], []
    return fn


def _pool_bwd_tile(D, tile, n_tiles):
    cg = D // len(POOL_WINDOWS)

    def fn(i, dd, ddn, x, dxo, g):
        e = dd / _pool_counts(_row_ids(i, tile, tile), cg)
        en = jnp.where(i < n_tiles - 1, ddn / _pool_counts(_row_ids(i + 1, tile, HALO), cg), 0.0)
        dh = _window_sums(jnp.concatenate([e, en], axis=0), cg, False)[:tile] - dd
        _, vjp = jax.vjp(_rms, x, g)
        dx, dg = vjp(dh)
        return [dxo + dx, dxo + dx], [dg]
    return fn


def _odd_fwd(l, o, x, gm, p, W, tile):
    T, D = x.shape
    wc = W['pool_w']
    cg = wc.shape[2]
    cs = cg // N_CHIPS
    ng = len(POOL_WINDOWS)
    tm = _pick(T, TM)
    d = _rowwise(f"mix{l}_pool", _pool_fwd_tile(D, tile), [('row', x), ('prev', x), ('const', gm.reshape(1, D))],
                 [(D, BF16)], [], tile)[0]
    gspec = _bs((tm, cg), lambda i, j, k: (i, j))
    vspec = _bs((1, cg), lambda i, j, k: (0, j))

    def epi(accs, ex):
        pre = accs[0] + ex[0]
        return [ex[2] + pre * ex[1], pre]

    xo, pre = _mm(f"mix{l}_poolmm", (T // tm, ng, N_CHIPS),
                  [(d, _bs((tm, cs), lambda i, j, k: (i, j * N_CHIPS + k))),
                   (wc, _bs((None, cs, cg), lambda i, j, k: (k, j, 0))),
                   (p['pool_b_full'][o].reshape(1, D), vspec), (p['pool_scale_full'][o].reshape(1, D), vspec),
                   (x, gspec)],
                  [(0, 1, 'nn', 0)], [(_sds((T, D), F32), gspec)] * 2, [(tm, cg)], epi, extras=(2, 3, 4))
    return xo, (x, d, pre)


def _odd_bwd(l, o, dxo, dxb, saved, gm, p, W, G, tile):
    x, d, pre = saved
    T, D = x.shape
    wc = W['pool_w']
    cg = wc.shape[2]
    cs = cg // N_CHIPS
    ng = len(POOL_WINDOWS)
    tm, tkt = _pick(T, TM), _pick(T, TK)

    def fn(i, dxv, prev, sc):
        return [dxv * sc], [jnp.sum(dxv * prev, axis=0, keepdims=True), jnp.sum(dxv * sc, axis=0, keepdims=True)]

    do, dscale, dbc = _rowwise(f"mix{l}_pool_bwd1", fn,
                               [('row', dxo), ('row', pre), ('const', p['pool_scale_full'][o].reshape(1, D))],
                               [(D, BF16)], [(1, D), (1, D)], tile)
    nb = ng * N_CHIPS
    dd = _mm(f"mix{l}_pool_dd", (T // tm, nb, 1),
             [(do, _bs((tm, cg), lambda i, j, k: (i, j // N_CHIPS))),
              (wc, _bs((None, cs, cg), lambda i, j, k: (j % N_CHIPS, j // N_CHIPS, 0)))],
             [(0, 1, 'nt', 0)], [(_sds((T, D), F32), _bs((tm, cs), lambda i, j, k: (i, j)))], [(tm, cs)],
             lambda a, _: a)[0]
    G['pool_w'] = _mm(f"mix{l}_pool_dw", (nb, 1, T // tkt),
                      [(d, _bs((tkt, cs), lambda i, j, k: (k, i))), (do, _bs((tkt, cg), lambda i, j, k: (k, i // N_CHIPS)))],
                      [(0, 1, 'tn', 0)],
                      [(_sds(wc.shape, BF16), _bs((None, cs, cg), lambda i, j, k: (i % N_CHIPS, i // N_CHIPS, 0)))],
                      [(cs, cg)], lambda a, _: a)[0]
    dx, dxb2, dgm = _rowwise(f"mix{l}_pool_bwd2", _pool_bwd_tile(D, tile, T // tile),
                             [('row', dd), ('next', dd), ('row', x), ('row', dxo), ('const', gm.reshape(1, D))],
                             [(D, F32), (D, BF16)], [(1, D)], tile)
    small = {'pool_b': dbc.reshape(ng, cg), 'pool_scale': dscale.reshape(D)}
    return dx, dxb2, dgm, small


def _final(x, g, tgt, tile):
    T, D = x.shape

    def fn(i, xv, tv, gv):
        y, vjp = jax.vjp(_rms, xv, gv)
        err = y - tv
        dx, dg = vjp(err / D)
        loss = 0.5 * jnp.sum(jnp.mean(err * err, axis=-1, keepdims=True), axis=0, keepdims=True)
        return [dx, dx], [dg, jnp.broadcast_to(loss, (1, LANES))]

    dx, dxb, dg, loss = _rowwise("final", fn, [('row', x), ('row', tgt), ('const', g.reshape(1, D))],
                                 [(D, F32), (D, BF16)], [(1, D), (1, LANES)], tile)
    return dx, dxb, dg.reshape(D), loss[0, 0]


def _as3d(name, w):
    return w.reshape(w.shape[0], -1, w.shape[-1]) if name == 'pool_w' else w


def kernel(x, mem, norm_ffn1, ffn1_gate, ffn1_up, ffn1_down, norm_mix, ab_w_in, ab_b_in, gmlp_w_s, gmlp_b_s, gmlp_ln_g, gmlp_ln_b, conv_w, conv_b, conv_ln_g, conv_ln_b, ab_w_out, ab_b_out, pool_w, pool_b, pool_scale, norm_xq, norm_xkv, xattn_wq, xattn_wk, xattn_wv, xattn_wo, norm_ffn2, ffn2_gate, ffn2_up, ffn2_down, norm_final, loss_target, m_norm_ffn1, m_ffn1_gate, m_ffn1_up, m_ffn1_down, m_norm_mix, m_ab_w_in, m_ab_b_in, m_gmlp_w_s, m_gmlp_b_s, m_gmlp_ln_g, m_gmlp_ln_b, m_conv_w, m_conv_b, m_conv_ln_g, m_conv_ln_b, m_ab_w_out, m_ab_b_out, m_pool_w, m_pool_b, m_pool_scale, m_norm_xq, m_norm_xkv, m_xattn_wq, m_xattn_wk, m_xattn_wv, m_xattn_wo, m_norm_ffn2, m_ffn2_gate, m_ffn2_up, m_ffn2_down, m_norm_final, v_norm_ffn1, v_ffn1_gate, v_ffn1_up, v_ffn1_down, v_norm_mix, v_ab_w_in, v_ab_b_in, v_gmlp_w_s, v_gmlp_b_s, v_gmlp_ln_g, v_gmlp_ln_b, v_conv_w, v_conv_b, v_conv_ln_g, v_conv_ln_b, v_ab_w_out, v_ab_b_out, v_pool_w, v_pool_b, v_pool_scale, v_norm_xq, v_norm_xkv, v_xattn_wq, v_xattn_wk, v_xattn_wv, v_xattn_wo, v_norm_ffn2, v_ffn2_gate, v_ffn2_up, v_ffn2_down, v_norm_final):
    w = dict(zip(WEIGHTS, [norm_ffn1, ffn1_gate, ffn1_up, ffn1_down, norm_mix, ab_w_in, ab_b_in, gmlp_w_s, gmlp_b_s, gmlp_ln_g, gmlp_ln_b, conv_w, conv_b, conv_ln_g, conv_ln_b, ab_w_out, ab_b_out, pool_w, pool_b, pool_scale, norm_xq, norm_xkv, xattn_wq, xattn_wk, xattn_wv, xattn_wo, norm_ffn2, ffn2_gate, ffn2_up, ffn2_down, norm_final]))
    m = dict(zip(WEIGHTS, [m_norm_ffn1, m_ffn1_gate, m_ffn1_up, m_ffn1_down, m_norm_mix, m_ab_w_in, m_ab_b_in, m_gmlp_w_s, m_gmlp_b_s, m_gmlp_ln_g, m_gmlp_ln_b, m_conv_w, m_conv_b, m_conv_ln_g, m_conv_ln_b, m_ab_w_out, m_ab_b_out, m_pool_w, m_pool_b, m_pool_scale, m_norm_xq, m_norm_xkv, m_xattn_wq, m_xattn_wk, m_xattn_wv, m_xattn_wo, m_norm_ffn2, m_ffn2_gate, m_ffn2_up, m_ffn2_down, m_norm_final]))
    v = dict(zip(WEIGHTS, [v_norm_ffn1, v_ffn1_gate, v_ffn1_up, v_ffn1_down, v_norm_mix, v_ab_w_in, v_ab_b_in, v_gmlp_w_s, v_gmlp_b_s, v_gmlp_ln_g, v_gmlp_ln_b, v_conv_w, v_conv_b, v_conv_ln_g, v_conv_ln_b, v_ab_w_out, v_ab_b_out, v_pool_w, v_pool_b, v_pool_scale, v_norm_xq, v_norm_xkv, v_xattn_wq, v_xattn_wk, v_xattn_wv, v_xattn_wo, v_norm_ffn2, v_ffn2_gate, v_ffn2_up, v_ffn2_down, v_norm_final]))

    xs, mems, tgt = x[0], mem[0], loss_target[0]
    T, D = xs.shape
    L = norm_ffn1.shape[0]
    tile = _pick(T, ROW_TILE)
    cx, cy, cc = _mesh_pos()
    chip = 2 * cx + cy
    w3 = {n: _as3d(n, w[n]) for n in BIG}
    names = [_layer_names(l) for l in range(L)]

    sh_shapes = [w[n].shape for n in SMALL_SHARDED]
    slots = _gather_all("gather_small_shards", _pack([w[n] for n in SMALL_SHARDED]), jnp.zeros((8, LANES), F32))
    per_chip = [_unpack(slots[2 * j], sh_shapes) for j in range(N_CHIPS)]
    full = {n: jnp.concatenate([per_chip[j][k] for j in range(N_CHIPS)], axis=-1) for k, n in enumerate(SMALL_SHARDED)}
    p = dict(w)
    p['conv_w_full'] = full['conv_w'].reshape(full['conv_w'].shape[0], full['conv_w'].shape[1], -1)
    p['pool_b_full'] = full['pool_b']
    p['pool_scale_full'] = full['pool_scale']

    token = slots
    in_flight = []
    for l in range(L):
        slabs = [_cast_slab(f"cast_{n}_{l}", w3[n], _stack_index(n, l), chip) for n in names[l]]
        sends, arrivals, slabs, _, token = _split_start(f"gather_start_{l}", slabs, token, 3 * len(slabs), _gather_copies)
        in_flight.append((sends, arrivals, slabs))

    saved, Wl = [], []
    xc = xs + token[0, 0]
    for l in range(L):
        sends, arrivals, slabs = in_flight[l]
        slabs = _split_wait(f"gather_wait_{l}", slabs, sends, arrivals, xc, _gather_copies)
        W = dict(zip(names[l], _forward_halves(f"gather_fwd_{l}", slabs)))
        Wl.append(W)
        s = {}
        xc, s['ffn1'] = _ffn_fwd(f"ffn1_{l}", xc, w['norm_ffn1'][l], W['ffn1_gate'], W['ffn1_up'], W['ffn1_down'], tile)
        if l % 2 == 0:
            xc, s['mix'] = _even_fwd(l, l // 2, xc, w['norm_mix'][l], p, W, tile)
        else:
            xc, s['mix'] = _odd_fwd(l, l // 2, xc, w['norm_mix'][l], p, W, tile)
        xc, s['xa'] = _attn_fwd(l, xc, mems, w['norm_xq'][l], w['norm_xkv'][l], W, tile)
        xc, s['ffn2'] = _ffn_fwd(f"ffn2_{l}", xc, w['norm_ffn2'][l], W['ffn2_gate'], W['ffn2_up'], W['ffn2_down'], tile)
        saved.append(s)

    dx, dxb, g_final, loss_local = _final(xc, w['norm_final'], tgt, tile)
    loss = lax.psum(loss_local, ("x", "y", "c"))
    gfull = {n: lax.empty(w3[n].shape, F32) for n in BIG}
    gs = {n: [None] * w[n].shape[0] for n in SMALL if n != 'norm_final'}

    def finish_exchange(pending, after):
        tag, l, ns, sends, arrivals, thru = pending
        thru = _split_wait(f"rs_wait_{tag}", thru, sends, arrivals, after, _exchange_copies)
        parts, lands = thru[:len(ns)], thru[len(ns):]
        for n, part, land in zip(ns, parts, lands):
            gfull[n] = _sum_into(f"rs_sum_{n}_{l}", part, land, gfull[n], _stack_index(n, l), chip, cc)
        joined = _join_halves(f"rs_join_{tag}", [gfull[n] for n in ns], [_stack_index(n, l) for n in ns])
        gfull.update(zip(ns, joined))

    def start_exchange(tag, l, ns, G, pending, dx, dxb):
        if len(pending) >= EXCHANGES_IN_FLIGHT:
            finish_exchange(pending.pop(0), dx)
        grads_g = [G[n] for n in ns]
        got = _swap_halves(f"rs_swap_{tag}", grads_g)
        parts = [_add_halves(f"rs_add_{n}_{l}", g, r, cc) for n, g, r in zip(ns, grads_g, got)]
        lands = [lax.empty((3,) + part.shape[1:], BF16) for part in parts]
        sends, arrivals, thru, (dx, dxb), tok = _split_start(
            f"rs_start_{tag}", parts + lands, jnp.zeros((8, LANES), F32), 3 * len(parts), _exchange_copies,
            carry=(dx, dxb))
        pending.append((tag, l, ns, sends, arrivals, thru))
        return dx, dxb, tok

    pending = []
    for l in reversed(range(L)):
        first = [n for n in names[l] if n.startswith(('ffn2', 'xattn'))]
        second = [n for n in names[l] if n not in first]
        s, W, G = saved[l], Wl[l], {}
        dx, dxb, dg, G['ffn2_gate'], G['ffn2_up'], G['ffn2_down'] = _ffn_bwd(
            f"ffn2_{l}", dx, dxb, s['ffn2'], w['norm_ffn2'][l], W['ffn2_gate'], W['ffn2_up'], W['ffn2_down'], tile)
        gs['norm_ffn2'][l] = dg.reshape(D)
        dx, dxb, dgq, dgkv = _attn_bwd(l, dx, dxb, s['xa'], mems, w['norm_xq'][l], w['norm_xkv'][l], W, G, tile)
        gs['norm_xq'][l], gs['norm_xkv'][l] = dgq.reshape(D), dgkv.reshape(D)
        dx, dxb, _ = start_exchange(f"a{l}", l, first, G, pending, dx, dxb)
        if l % 2 == 0:
            dx, dxb, dgm, small = _even_bwd(l, l // 2, dx, dxb, s['mix'], w['norm_mix'][l], p, W, G, tile)
        else:
            dx, dxb, dgm, small = _odd_bwd(l, l // 2, dx, dxb, s['mix'], w['norm_mix'][l], p, W, G, tile)
        for n, val in small.items():
            gs[n][l // 2] = val
        gs['norm_mix'][l] = dgm.reshape(D)
        dx, dxb, dg, G['ffn1_gate'], G['ffn1_up'], G['ffn1_down'] = _ffn_bwd(
            f"ffn1_{l}", dx, dxb, s['ffn1'], w['norm_ffn1'][l], W['ffn1_gate'], W['ffn1_up'], W['ffn1_down'], tile)
        gs['norm_ffn1'][l] = dg.reshape(D)
        dx, dxb, tok = start_exchange(f"b{l}", l, second, G, pending, dx, dxb)
    grad_x = dx[None]

    def flat2(n, t):
        t3 = _as3d(n, t)
        return t3.reshape(-1, t3.shape[-1])

    early, dep = {}, []
    for n in BIG:
        R = w3[n].shape[1]
        lo = 0 if n in ODD_ONLY else R
        early[n] = _adam_rows(f"adam_early_{n}", flat2(n, w[n]), flat2(n, gfull[n]), flat2(n, m[n]), flat2(n, v[n]),
                              lo, w3[n].shape[0] * R, R, after=tok)
        dep.append(early[n][1][-1, 0])
    dep = jnp.stack(dep)

    small_full = {n: jnp.stack(gs[n]) for n in gs}
    small_full['norm_final'] = g_final
    full_shapes = [small_full[n].shape for n in SMALL]
    summed = _sum_slots("sum_small", _gather_all("gather_small_grads", _pack([small_full[n] for n in SMALL]), dep))
    g_small = dict(zip(SMALL, _unpack(summed, full_shapes)))
    for n in SMALL_SHARDED:
        width = w[n].shape[-1]
        g_small[n] = lax.dynamic_slice_in_dim(g_small[n], chip * width, width, axis=g_small[n].ndim - 1).reshape(w[n].shape)

    for group in pending:
        finish_exchange(group, summed)
    grads, delta, new_m, new_v = {}, {}, {}, {}
    for n in BIG:
        outs = early[n]
        if n not in ODD_ONLY:
            R = w3[n].shape[1]
            outs = _adam_rows(f"adam_late_{n}", flat2(n, w[n]), flat2(n, gfull[n]), flat2(n, m[n]), flat2(n, v[n]),
                              0, R, R, prev=early[n])
        grads[n], delta[n], new_m[n], new_v[n] = (t.reshape(w[n].shape) for t in outs)
    small_shapes = [w[n].shape for n in SMALL]
    d2, m2, v2 = _adam("adam_small", _pack([w[n] for n in SMALL]), _pack([g_small[n] for n in SMALL]),
                       _pack([m[n] for n in SMALL]), _pack([v[n] for n in SMALL]))
    for n, dn, mn_, vn_ in zip(SMALL, _unpack(d2, small_shapes), _unpack(m2, small_shapes), _unpack(v2, small_shapes)):
        grads[n], delta[n], new_m[n], new_v[n] = g_small[n].reshape(w[n].shape), dn, mn_, vn_

    return (loss, grad_x, *[grads[n] for n in WEIGHTS], *[delta[n] for n in WEIGHTS],
            *[new_m[n] for n in WEIGHTS], *[new_v[n] for n in WEIGHTS])
```

```python
import jax
import jax.numpy as jnp
from jax import lax
from jax.experimental import pallas as pl
from jax.experimental.pallas import tpu as pltpu

F32, BF16 = jnp.float32, jnp.bfloat16
EPS = 1e-6
N_MEM_HEADS = 4
A_HEADS = 8
GMLP_BLOCK = 128
CHUNK = 64
POOL_WINDOWS = (2, 4, 8, 16)
N_CHIPS = 4
N_DEV = 8
HALO = 32
LANES = 128
TM, TN, TK = 512, 1024, 512
TM_BIG, TM_MID = 1024, 512
TN_BIG, TN_MID, TN_SMALL = 1024, 512, 256
EPI_ROWS = 256
EXCHANGES_IN_FLIGHT = 2
ROW_TILE = 256
PACK_ROWS = 512
VMEM_LIMIT = 48 * 1024 * 1024
ADAM_LR, ADAM_B1, ADAM_B2, ADAM_EPS, ADAM_WD, ADAM_STEP = 0.001, 0.9, 0.999, 1e-08, 0.01, 10
MESH = pl.DeviceIdType.MESH
HBM = pl.BlockSpec(memory_space=pltpu.HBM)
SEM = pl.BlockSpec(memory_space=pltpu.SEMAPHORE)
ANY = pl.BlockSpec(memory_space=pl.ANY)
EFFECT = pltpu.SideEffectType.DATAFLOW_SIDE_EFFECTING

WEIGHTS = ['norm_ffn1', 'ffn1_gate', 'ffn1_up', 'ffn1_down', 'norm_mix', 'ab_w_in', 'ab_b_in', 'gmlp_w_s',
           'gmlp_b_s', 'gmlp_ln_g', 'gmlp_ln_b', 'conv_w', 'conv_b', 'conv_ln_g', 'conv_ln_b', 'ab_w_out',
           'ab_b_out', 'pool_w', 'pool_b', 'pool_scale', 'norm_xq', 'norm_xkv', 'xattn_wq', 'xattn_wk',
           'xattn_wv', 'xattn_wo', 'norm_ffn2', 'ffn2_gate', 'ffn2_up', 'ffn2_down', 'norm_final']
BIG = ['ffn1_gate', 'ffn1_up', 'ffn1_down', 'ab_w_in', 'ab_w_out', 'pool_w', 'xattn_wq', 'xattn_wk', 'xattn_wv',
       'xattn_wo', 'ffn2_gate', 'ffn2_up', 'ffn2_down']
EVEN_ONLY, ODD_ONLY = ['ab_w_in', 'ab_w_out'], ['pool_w']
SMALL = [n for n in WEIGHTS if n not in BIG]
SMALL_SHARDED = ['conv_w', 'pool_b', 'pool_scale']

NN = (((1,), (0,)), ((), ()))
NT = (((1,), (1,)), ((), ()))
TN_ = (((0,), (0,)), ((), ()))
_DIMS = {'nn': NN, 'nt': NT, 'tn': TN_}


def _pick(n, pref, unit=LANES):
    if n <= pref:
        return n
    t = (pref // unit) * unit
    while t >= unit:
        if n % t == 0:
            return t
        t -= unit
    return n


def _sds(shape, dtype):
    return jax.ShapeDtypeStruct(tuple(shape), dtype)


def _layer_names(l):
    mix = EVEN_ONLY if l % 2 == 0 else ODD_ONLY
    return ['ffn1_gate', 'ffn1_up', 'ffn1_down'] + mix + ['xattn_wq', 'xattn_wk', 'xattn_wv', 'xattn_wo',
                                                          'ffn2_gate', 'ffn2_up', 'ffn2_down']


def _stack_index(name, l):
    return l // 2 if name in EVEN_ONLY + ODD_ONLY else l


def _mm(name, grid, ins, pairs, outs, acc_shapes, epilogue, extras=()):
    n_in, n_out = len(ins), len(outs)
    nk = grid[2]

    def body(*refs):
        in_refs, out_refs, acc_refs = refs[:n_in], refs[n_in:n_in + n_out], refs[n_in + n_out:]
        k = pl.program_id(2)

        @pl.when(k == 0)
        def _():
            for acc in acc_refs:
                acc[...] = jnp.zeros_like(acc)

        for ai, bi, mode, ci in pairs:
            a = in_refs[ai][...].astype(BF16)
            b = in_refs[bi][...].astype(BF16)
            acc_refs[ci][...] += lax.dot_general(a, b, _DIMS[mode], preferred_element_type=F32)

        @pl.when(k == nk - 1)
        def _():
            res = epilogue([acc[...] for acc in acc_refs], [in_refs[e][...] for e in extras])
            for o, r in zip(out_refs, res):
                o[...] = r.astype(o.dtype)

    return pl.pallas_call(
        body, name=name, grid=grid,
        in_specs=[s for _, s in ins], out_specs=[s for _, s in outs], out_shape=[s for s, _ in outs],
        scratch_shapes=[pltpu.VMEM(s, F32) for s in acc_shapes],
        compiler_params=pltpu.CompilerParams(dimension_semantics=("parallel", "parallel", "arbitrary"),
                                             vmem_limit_bytes=VMEM_LIMIT),
    )(*[a for a, _ in ins])


def _bs(shape, fn):
    return pl.BlockSpec(shape, fn)


def _mm1(name, grid, ins, outs, compute):
    n_in = len(ins)

    def body(*refs):
        compute(refs[:n_in], refs[n_in:])

    return pl.pallas_call(
        body, name=name, grid=grid,
        in_specs=[s for _, s in ins], out_specs=[s for _, s in outs], out_shape=[s for s, _ in outs],
        compiler_params=pltpu.CompilerParams(dimension_semantics=("parallel", "parallel"),
                                             vmem_limit_bytes=VMEM_LIMIT),
    )(*[a for a, _ in ins])


def _dot(a, b, mode):
    return lax.dot_general(a.astype(BF16), b.astype(BF16), _DIMS[mode], preferred_element_type=F32)


def _row_chunks(rows):
    step = min(rows, EPI_ROWS)
    return [slice(r, r + step) for r in range(0, rows, step)]


def _rowwise(name, fn, ins, row_outs, acc_outs, tile):
    T = next(a.shape[0] for k, a in ins if k == 'row')
    n = T // tile
    per = tile // HALO if tile % HALO == 0 else 1
    last = T // HALO - 1
    in_specs = []
    for kind, a in ins:
        if kind == 'row':
            in_specs.append(pl.BlockSpec((tile, a.shape[1]), lambda i: (i, 0)))
        elif kind == 'prev':
            in_specs.append(pl.BlockSpec((HALO, a.shape[1]), lambda i: (jnp.maximum(i * per - 1, 0), 0)))
        elif kind == 'next':
            in_specs.append(pl.BlockSpec((HALO, a.shape[1]), lambda i: (jnp.minimum((i + 1) * per, last), 0)))
        else:
            in_specs.append(pl.BlockSpec(a.shape, lambda i, nd=a.ndim: (0,) * nd))
    n_in, n_row = len(ins), len(row_outs)
    out_shape = [_sds((T, c), dt) for c, dt in row_outs] + [_sds(s, F32) for s in acc_outs]
    out_specs = [pl.BlockSpec((tile, c), lambda i: (i, 0)) for c, _ in row_outs]
    out_specs += [pl.BlockSpec(s, lambda i, nd=len(s): (0,) * nd) for s in acc_outs]
    kinds = [k for k, _ in ins]

    def body(*refs):
        i = pl.program_id(0)
        vals = [r if k == 'cref' else r[...] for k, r in zip(kinds, refs[:n_in])]
        ro, ao = fn(i, *vals)
        for r, v in zip(refs[n_in:n_in + n_row], ro):
            r[...] = v.astype(r.dtype)
        for r, v in zip(refs[n_in + n_row:], ao):
            @pl.when(i == 0)
            def _(r=r, v=v):
                r[...] = v

            @pl.when(i > 0)
            def _(r=r, v=v):
                r[...] += v

    return pl.pallas_call(
        body, name=name, grid=(n,), in_specs=in_specs, out_specs=out_specs, out_shape=out_shape,
        compiler_params=pltpu.CompilerParams(dimension_semantics=("arbitrary",), vmem_limit_bytes=VMEM_LIMIT),
    )(*[a for _, a in ins])


def _rms(x, g):
    return x * lax.rsqrt(jnp.mean(x * x, axis=-1, keepdims=True) + EPS) * g


def _ln(x, g, b):
    mu = jnp.mean(x, axis=-1, keepdims=True)
    xc = x - mu
    var = jnp.mean(xc * xc, axis=-1, keepdims=True)
    return xc * lax.rsqrt(var + EPS) * g + b


def _gelu(x):
    return 0.5 * x * (1.0 + jnp.tanh(0.7978845608028654 * (x + 0.044715 * (x * x * x))))


def _silu(x):
    return x * jax.nn.sigmoid(x)


def _glu(a, g):
    return a * jax.nn.sigmoid(g)


def _row_ids(i, tile, rows):
    return i * tile + lax.broadcasted_iota(jnp.int32, (rows, 1), 0)


def _mesh_pos():
    return lax.axis_index("x"), lax.axis_index("y"), lax.axis_index("c")


def _other_chips(x, y):
    return [(1 - x, y), (x, 1 - y), (1 - x, 1 - y)]


def _remote(src, dst, send_sems, recv_sems, s, to):
    return pltpu.make_async_remote_copy(src_ref=src, dst_ref=dst, send_sem=send_sems.at[s], recv_sem=recv_sems.at[s],
                                        device_id=to, device_id_type=MESH)


def _gather_near_copies(refs, send_sems, recv_sems):
    x, y, c = _mesh_pos()
    me = 2 * x + y
    out = []
    for t, ref in enumerate(refs):
        rh = ref.shape[1] // 2
        half = pl.ds(c * rh, rh)
        for k, (cx, cy) in enumerate(_other_chips(x, y)[:2]):
            mine, theirs = ref.at[me, half], ref.at[2 * cx + cy, half]
            out.append((_remote(mine, mine, send_sems, recv_sems, 2 * t + k, (cx, cy, c)),
                        _remote(theirs, theirs, send_sems, recv_sems, 2 * t + k, (cx, cy, c))))
    return out


def _gather_far_copies(refs, send_sems, recv_sems):
    x, y, c = _mesh_pos()
    xn, yn, diag = 2 * (1 - x) + y, 2 * x + (1 - y), 2 * (1 - x) + (1 - y)
    out = []
    for t, ref in enumerate(refs):
        rq = ref.shape[1] // 4
        q0, q1 = pl.ds(2 * c * rq, rq), pl.ds((2 * c + 1) * rq, rq)
        out.append((_remote(ref.at[yn, q1], ref.at[yn, q1], send_sems, recv_sems, 2 * t, (1 - x, y, c)),
                    _remote(ref.at[diag, q1], ref.at[diag, q1], send_sems, recv_sems, 2 * t, (1 - x, y, c))))
        out.append((_remote(ref.at[xn, q0], ref.at[xn, q0], send_sems, recv_sems, 2 * t + 1, (x, 1 - y, c)),
                    _remote(ref.at[diag, q0], ref.at[diag, q0], send_sems, recv_sems, 2 * t + 1, (x, 1 - y, c))))
    return out


def _exchange_copies(refs, send_sems, recv_sems):
    x, y, c = _mesh_pos()
    n = len(refs) // 2
    out = []
    for t in range(n):
        part, land = refs[t], refs[n + t]
        for k, (cx, cy) in enumerate(_other_chips(x, y)):
            out.append((_remote(part.at[2 * cx + cy], land.at[k], send_sems, recv_sems, 3 * t + k, (cx, cy, c)),
                        _remote(land.at[k], land.at[k], send_sems, recv_sems, 3 * t + k, (cx, cy, c))))
    return out


def _split_start(name, thru, after, n_sems, copies, carry=()):
    n, nc = len(thru), len(carry)
    both = list(thru) + list(carry)

    def body(*refs):
        outs = refs[n + nc + 1:]
        send_sems, recv_sems, thru_refs, token = outs[0], outs[1], outs[2:2 + n], outs[2 + n + nc]
        for send, _ in copies(thru_refs, send_sems, recv_sems):
            send.start()
        token[...] = jnp.zeros_like(token)

    res = pl.pallas_call(
        body, name=name,
        out_shape=(pltpu.SemaphoreType.DMA((n_sems,)), pltpu.SemaphoreType.DMA((n_sems,)),
                   *[pltpu.HBM(b.shape, b.dtype) for b in both], _sds((8, LANES), F32)),
        in_specs=[HBM] * (n + nc) + [ANY],
        out_specs=(SEM, SEM, *[HBM] * (n + nc), pl.BlockSpec(memory_space=pltpu.VMEM)),
        input_output_aliases={i: 2 + i for i in range(n + nc)},
        compiler_params=pltpu.CompilerParams(has_side_effects=EFFECT),
    )(*[pltpu.with_memory_space_constraint(b, pltpu.HBM) for b in both], after)
    return res[0], res[1], list(res[2:2 + n]), list(res[2 + n:2 + n + nc]), res[2 + n + nc]


def _split_wait(name, thru, send_sems, recv_sems, after, copies):
    n = len(thru)

    def body(*refs):
        sends, recvs, outs = refs[n], refs[n + 1], refs[n + 3:]
        for send, arrival in copies(outs, sends, recvs):
            send.wait_send()
            arrival.wait_recv()

    res = pl.pallas_call(
        body, name=name, out_shape=tuple(pltpu.HBM(b.shape, b.dtype) for b in thru),
        in_specs=[HBM] * n + [SEM, SEM, ANY], out_specs=tuple([HBM] * n),
        input_output_aliases={i: i for i in range(n)},
        compiler_params=pltpu.CompilerParams(has_side_effects=EFFECT),
    )(*thru, send_sems, recv_sems, after)
    return list(res)


def _forward_halves(name, bufs):
    n = len(bufs)

    def body(*refs):
        outs, send_sems, recv_sems = refs[n:2 * n], refs[2 * n], refs[2 * n + 1]
        x, y, c = _mesh_pos()
        sibling = (x, y, 1 - c)
        sends, arrivals = [], []
        for t, ref in enumerate(outs):
            rh = ref.shape[1] // 2
            mine, other = pl.ds(c * rh, rh), pl.ds((1 - c) * rh, rh)
            for k, (cx, cy) in enumerate(_other_chips(x, y)):
                landed, coming = ref.at[2 * cx + cy, mine], ref.at[2 * cx + cy, other]
                cp = _remote(landed, landed, send_sems, recv_sems, 3 * t + k, sibling)
                cp.start()
                sends.append(cp)
                arrivals.append(_remote(coming, coming, send_sems, recv_sems, 3 * t + k, sibling))
        for a in arrivals:
            a.wait_recv()
        for cp in sends:
            cp.wait_send()

    res = pl.pallas_call(
        body, name=name, out_shape=tuple(_sds(b.shape, b.dtype) for b in bufs),
        in_specs=[HBM] * n, out_specs=tuple([HBM] * n), input_output_aliases={i: i for i in range(n)},
        scratch_shapes=[pltpu.SemaphoreType.DMA((3 * n,)), pltpu.SemaphoreType.DMA((3 * n,))],
    )(*bufs)
    return list(res)


def _swap_halves(name, gs):
    n = len(gs)

    def body(*refs):
        ins, outs, send_sems, recv_sems = refs[:n], refs[n:2 * n], refs[2 * n], refs[2 * n + 1]
        x, y, c = _mesh_pos()
        cps = []
        for t, (g_ref, o_ref) in enumerate(zip(ins, outs)):
            rh = g_ref.shape[1] // 2
            cp = _remote(g_ref.at[:, pl.ds((1 - c) * rh, rh)], o_ref, send_sems, recv_sems, t, (x, y, 1 - c))
            cp.start()
            cps.append(cp)
        for cp in cps:
            cp.wait_recv()
        for cp in cps:
            cp.wait_send()

    res = pl.pallas_call(
        body, name=name, out_shape=tuple(_sds((g.shape[0], g.shape[1] // 2, g.shape[2]), g.dtype) for g in gs),
        in_specs=[HBM] * n, out_specs=tuple([HBM] * n),
        scratch_shapes=[pltpu.SemaphoreType.DMA((n,)), pltpu.SemaphoreType.DMA((n,))],
    )(*gs)
    return list(res)


def _join_halves(name, gfulls, idx):
    n = len(gfulls)

    def body(*refs):
        outs, send_sems, recv_sems = refs[n:2 * n], refs[2 * n], refs[2 * n + 1]
        x, y, c = _mesh_pos()
        cps, arrivals = [], []
        for t, ref in enumerate(outs):
            rh = ref.shape[1] // 2
            mine, theirs = ref.at[idx[t], pl.ds(c * rh, rh)], ref.at[idx[t], pl.ds((1 - c) * rh, rh)]
            cp = _remote(mine, mine, send_sems, recv_sems, t, (x, y, 1 - c))
            cp.start()
            cps.append(cp)
            arrivals.append(_remote(theirs, theirs, send_sems, recv_sems, t, (x, y, 1 - c)))
        for a in arrivals:
            a.wait_recv()
        for cp in cps:
            cp.wait_send()

    res = pl.pallas_call(
        body, name=name, out_shape=tuple(_sds(g.shape, g.dtype) for g in gfulls),
        in_specs=[HBM] * n, out_specs=tuple([HBM] * n), input_output_aliases={i: i for i in range(n)},
        scratch_shapes=[pltpu.SemaphoreType.DMA((n,)), pltpu.SemaphoreType.DMA((n,))],
    )(*gfulls)
    return list(res)


def _gather_all(name, buf, after):
    def body(b_ref, after_ref, out_ref, send_sems, recv_sems, local_sem):
        x, y, c = _mesh_pos()
        me = 4 * x + 2 * y + c
        local = pltpu.make_async_copy(b_ref, out_ref.at[me], local_sem)
        local.start()
        peers = []
        for k in range(1, N_DEV):
            peers.append((1 - x if k & 4 else x, 1 - y if k & 2 else y, 1 - c if k & 1 else c))
        sends = []
        for k, peer in enumerate(peers):
            cp = _remote(b_ref, out_ref.at[me], send_sems, recv_sems, k, peer)
            cp.start()
            sends.append(cp)
        for k, (px, py, pc) in enumerate(peers):
            slot = out_ref.at[4 * px + 2 * py + pc]
            _remote(slot, slot, send_sems, recv_sems, k, (px, py, pc)).wait_recv()
        for cp in sends:
            cp.wait_send()
        local.wait()

    return pl.pallas_call(
        body, name=name, out_shape=_sds((N_DEV,) + buf.shape, buf.dtype), in_specs=[HBM, ANY], out_specs=HBM,
        scratch_shapes=[pltpu.SemaphoreType.DMA((N_DEV - 1,)), pltpu.SemaphoreType.DMA((N_DEV - 1,)),
                        pltpu.SemaphoreType.DMA],
    )(buf, after)


def _scalars(*vals):
    return jnp.stack([jnp.asarray(v, jnp.int32) for v in vals])


def _cast_slab(name, w3, li, chip):
    _, R, C = w3.shape
    tr = _pick(R, ROW_TILE, 16)

    def body(s_ref, w_ref, o_ref):
        o_ref[...] = w_ref[...].astype(o_ref.dtype)

    grid_spec = pltpu.PrefetchScalarGridSpec(
        num_scalar_prefetch=1, grid=(R // tr,),
        in_specs=[pl.BlockSpec((None, tr, C), lambda r, s: (li, r, 0))],
        out_specs=pl.BlockSpec((None, tr, C), lambda r, s: (s[0], r, 0)))
    return pl.pallas_call(
        body, name=name, grid_spec=grid_spec, out_shape=_sds((N_CHIPS, R, C), BF16),
        compiler_params=pltpu.CompilerParams(dimension_semantics=("arbitrary",), vmem_limit_bytes=VMEM_LIMIT),
    )(_scalars(chip), w3)


def _add_halves(name, g, recv, c):
    _, R, C = g.shape
    rh = R // 2
    tr = _pick(rh, 512, 16)
    nr = rh // tr

    def body(s_ref, g_ref, a_ref, o_ref):
        o_ref[...] = (g_ref[...].astype(F32) + a_ref[...].astype(F32)).astype(o_ref.dtype)

    blk = (None, tr, C)
    grid_spec = pltpu.PrefetchScalarGridSpec(
        num_scalar_prefetch=1, grid=(N_CHIPS, nr),
        in_specs=[pl.BlockSpec(blk, lambda j, r, s: (j, s[0] * nr + r, 0)),
                  pl.BlockSpec(blk, lambda j, r, s: (j, r, 0))],
        out_specs=pl.BlockSpec(blk, lambda j, r, s: (j, r, 0)))
    return pl.pallas_call(
        body, name=name, grid_spec=grid_spec, out_shape=_sds((N_CHIPS, rh, C), g.dtype),
        compiler_params=pltpu.CompilerParams(dimension_semantics=("arbitrary",) * 2, vmem_limit_bytes=VMEM_LIMIT),
    )(_scalars(c), g, recv)


def _sum_into(name, p, recv, gfull, li, chip, c):
    _, rh, C = p.shape
    tr = _pick(rh, 512, 16)
    nr = rh // tr

    def body(s_ref, p_ref, r_ref, g_ref, o_ref):
        acc = p_ref[...].astype(F32)
        for k in range(3):
            acc = acc + r_ref[k].astype(F32)
        o_ref[...] = acc

    grid_spec = pltpu.PrefetchScalarGridSpec(
        num_scalar_prefetch=1, grid=(nr,),
        in_specs=[pl.BlockSpec((None, tr, C), lambda r, s: (s[0], r, 0)),
                  pl.BlockSpec((3, tr, C), lambda r, s: (0, r, 0)), HBM],
        out_specs=pl.BlockSpec((None, tr, C), lambda r, s: (li, s[1] * nr + r, 0)))
    return pl.pallas_call(
        body, name=name, grid_spec=grid_spec, out_shape=_sds(gfull.shape, F32), input_output_aliases={3: 0},
        compiler_params=pltpu.CompilerParams(dimension_semantics=("arbitrary",), vmem_limit_bytes=VMEM_LIMIT),
    )(_scalars(chip, c), p, recv, gfull)


def _sum_slots(name, buf):
    _, n, _ = buf.shape

    def body(b_ref, o_ref):
        acc = b_ref[0]
        for k in range(1, N_DEV):
            acc = acc + b_ref[k]
        o_ref[...] = acc

    return pl.pallas_call(
        body, name=name, grid=(n // PACK_ROWS,), out_shape=_sds((n, LANES), F32),
        in_specs=[pl.BlockSpec((N_DEV, PACK_ROWS, LANES), lambda i: (0, i, 0))],
        out_specs=pl.BlockSpec((PACK_ROWS, LANES), lambda i: (i, 0)),
    )(buf)


def _adam_tile(i, w, g, m, v):
    m = ADAM_B1 * m + (1.0 - ADAM_B1) * g
    v = ADAM_B2 * v + (1.0 - ADAM_B2) * (g * g)
    m_hat = m / (1.0 - ADAM_B1 ** ADAM_STEP)
    v_hat = v / (1.0 - ADAM_B2 ** ADAM_STEP)
    delta = -ADAM_LR * (m_hat / (jnp.sqrt(v_hat) + ADAM_EPS) + ADAM_WD * w)
    return [delta, m, v], []


def _adam(name, w, g, m, v):
    rows, C = w.shape
    tile = _pick(rows, ROW_TILE, 8)
    return _rowwise(name, _adam_tile, [('row', w), ('row', g), ('row', m), ('row', v)], [(C, F32)] * 3, [], tile)


def _adam_rows(name, w, g, m, v, lo, hi, unit, prev=None, after=None):
    rows, C = w.shape
    tile = _pick(unit, ROW_TILE, 8)
    first = lo // tile
    spec = pl.BlockSpec((tile, C), lambda i: (i + first, 0))
    ins, in_specs = [w, g, m, v], [spec] * 4
    if prev is not None:
        ins, in_specs = ins + list(prev), in_specs + [ANY] * 4
    if after is not None:
        ins, in_specs = ins + [after], in_specs + [ANY]

    def body(*refs):
        n_in = len(ins)
        outs, _ = _adam_tile(0, *[r[...] for r in refs[:4]])
        refs[n_in][...] = refs[1][...]
        for o, val in zip(refs[n_in + 1:n_in + 4], outs):
            o[...] = val

    return pl.pallas_call(
        body, name=name, grid=((hi - lo) // tile,), in_specs=in_specs, out_specs=[spec] * 4,
        out_shape=[_sds((rows, C), F32)] * 4,
        input_output_aliases={4 + k: k for k in range(4)} if prev is not None else {},
        compiler_params=pltpu.CompilerParams(dimension_semantics=("arbitrary",), vmem_limit_bytes=VMEM_LIMIT),
    )(*ins)


def _pack(arrs):
    flat = jnp.concatenate([a.reshape(-1).astype(F32) for a in arrs])
    unit = PACK_ROWS * LANES
    n = -(-flat.shape[0] // unit) * unit
    return jnp.pad(flat, (0, n - flat.shape[0])).reshape(-1, LANES)


def _unpack(buf, shapes):
    flat = buf.reshape(-1)
    out, off = [], 0
    for s in shapes:
        n = 1
        for d in s:
            n *= d
        out.append(flat[off:off + n].reshape(s))
        off += n
    return out


def _norm_fwd(name, x, g, tile):
    D = x.shape[1]
    return _rowwise(name, lambda i, xv, gv: ([_rms(xv, gv)], []), [('row', x), ('const', g.reshape(1, D))],
                    [(D, BF16)], [], tile)[0]


def _norm_bwd(name, x, g, dh, dxo, tile):
    D = x.shape[1]
    if dxo is None:
        def fn(i, xv, dhv, gv):
            _, vjp = jax.vjp(_rms, xv, gv)
            return [], [vjp(dhv)[1]]
        return _rowwise(name, fn, [('row', x), ('row', dh), ('const', g.reshape(1, D))], [], [(1, D)], tile)[0]

    def fn(i, xv, dhv, dxv, gv):
        _, vjp = jax.vjp(_rms, xv, gv)
        dx, dg = vjp(dhv)
        return [dxv + dx, dxv + dx], [dg]
    return _rowwise(name, fn, [('row', x), ('row', dh), ('row', dxo), ('const', g.reshape(1, D))],
                    [(D, F32), (D, BF16)], [(1, D)], tile)


def _ffn_fwd(tag, x, g, wg, wu, wd, tile):
    T, D = x.shape
    fs = wg.shape[2]
    F = N_CHIPS * fs
    tm, tn = _pick(T, TM_BIG), _pick(D, TN_SMALL)
    h = _norm_fwd(f"{tag}_norm", x, g, tile)
    hspec = _bs((tm, D), lambda j, i: (i, 0))
    wspec = _bs((None, D, fs), lambda j, i: (j, 0, 0))
    ospec = _bs((tm, fs), lambda j, i: (i, j))

    def gate(ins, outs):
        outs[0][...] = _dot(ins[0][...], ins[1][...], 'nn').astype(BF16)

    a = _mm1(f"{tag}_gate", (N_CHIPS, T // tm), [(h, hspec), (wg, wspec)], [(_sds((T, F), BF16), ospec)], gate)[0]

    def up(ins, outs):
        bv = _dot(ins[0][...], ins[1][...], 'nn')
        for rows in _row_chunks(tm):
            bb = bv[rows]
            outs[0][rows, :] = bb.astype(BF16)
            outs[1][rows, :] = (_silu(ins[2][rows, :].astype(F32)) * bb).astype(BF16)

    b, s = _mm1(f"{tag}_up", (N_CHIPS, T // tm), [(h, hspec), (wu, wspec), (a, ospec)],
                [(_sds((T, F), BF16), ospec)] * 2, up)

    def down(ins, outs):
        outs[0][...] = ins[2][...] + 0.5 * _dot(ins[0][...], ins[1][...].reshape(F, tn), 'nn')

    xspec = _bs((tm, tn), lambda i, j: (i, j))
    xo = _mm1(f"{tag}_down", (T // tm, D // tn),
              [(s, _bs((tm, F), lambda i, j: (i, 0))), (wd, _bs((N_CHIPS, fs, tn), lambda i, j: (0, 0, j))), (x, xspec)],
              [(_sds((T, D), F32), xspec)], down)[0]
    return xo, (x, h, a, b, s)


def _ffn_bwd(tag, dxo, dxb, saved, g, wg, wu, wd, tile):
    x, h, a, b, s = saved
    T, D = x.shape
    fs = wg.shape[2]
    F = N_CHIPS * fs
    tm = _pick(T, TM_MID)
    tspec = _bs((tm, fs), lambda j, i: (i, j))

    def ds_fn(ins, outs):
        d = _dot(ins[0][...], ins[1][...], 'nt')
        for rows in _row_chunks(tm):
            ds = 0.5 * d[rows]
            av, bv = ins[2][rows, :].astype(F32), ins[3][rows, :].astype(F32)
            sig = jax.nn.sigmoid(av)
            outs[0][rows, :] = (ds * bv * (sig * (1.0 + av * (1.0 - sig)))).astype(BF16)
            outs[1][rows, :] = (ds * (av * sig)).astype(BF16)

    da, db = _mm1(f"{tag}_ds", (N_CHIPS, T // tm),
                  [(dxb, _bs((tm, D), lambda j, i: (i, 0))), (wd, _bs((None, fs, D), lambda j, i: (j, 0, 0))),
                   (a, tspec), (b, tspec)], [(_sds((T, F), BF16), tspec)] * 2, ds_fn)

    tn = _pick(D, TN_BIG)

    def dwd_fn(ins, outs):
        outs[0][...] = (0.5 * _dot(ins[0][...], ins[1][...], 'tn')).astype(BF16)

    gd = _mm1(f"{tag}_dwd", (N_CHIPS, D // tn),
              [(s, _bs((T, fs), lambda i, j: (0, i))), (dxb, _bs((T, tn), lambda i, j: (0, j)))],
              [(_sds(wd.shape, BF16), _bs((None, fs, tn), lambda i, j: (i, 0, j)))], dwd_fn)[0]

    tmd = _pick(D, TM_BIG)

    def dw_fn(ins, outs):
        outs[0][...] = _dot(ins[0][...], ins[1][...], 'tn').astype(BF16)

    def dw(name, dy):
        return _mm1(name, (N_CHIPS, D // tmd),
                    [(h, _bs((T, tmd), lambda j, i: (0, i))), (dy, _bs((T, fs), lambda j, i: (0, j)))],
                    [(_sds(wg.shape, BF16), _bs((None, tmd, fs), lambda j, i: (j, i, 0)))], dw_fn)[0]

    gg, gu = dw(f"{tag}_dwg", da), dw(f"{tag}_dwu", db)

    tn3 = _pick(D, TN_SMALL)

    def dh_fn(ins, outs):
        acc = None
        for j in range(N_CHIPS):
            cols = slice(j * fs, (j + 1) * fs)
            t = _dot(ins[0][:, cols], ins[1][j], 'nt') + _dot(ins[2][:, cols], ins[3][j], 'nt')
            acc = t if acc is None else acc + t
        outs[0][...] = acc

    aspec = _bs((tm, F), lambda i, j: (i, 0))
    wtspec = _bs((N_CHIPS, tn3, fs), lambda i, j: (0, j, 0))
    dh = _mm1(f"{tag}_dh", (T // tm, D // tn3), [(da, aspec), (wg, wtspec), (db, aspec), (wu, wtspec)],
              [(_sds((T, D), F32), _bs((tm, tn3), lambda i, j: (i, j)))], dh_fn)[0]
    dx, dxb2, dg = _norm_bwd(f"{tag}_norm_bwd", x, g, dh, dxo, tile)
    return dx, dxb2, dg, gg, gu, gd


def _proj_rows(name, a, w, out_dtype, extras=(), epilogue=None):
    M, K = a.shape
    ks, N = w.shape[1], w.shape[2]
    tm, tn = _pick(M, TM_BIG), _pick(N, TN_MID)
    ins = [(a, _bs((tm, K), lambda i, j: (i, 0))), (w, _bs((N_CHIPS, ks, tn), lambda i, j: (0, 0, j)))]
    for e in extras:
        if e.shape[0] == 1:
            ins.append((e, _bs((1, tn), lambda i, j: (0, j))))
        else:
            ins.append((e, _bs((tm, tn), lambda i, j: (i, j))))

    def fn(refs, outs):
        acc = _dot(refs[0][...], refs[1][...].reshape(K, tn), 'nn')
        if epilogue is not None:
            acc = epilogue(acc, [r[...] for r in refs[2:]])
        outs[0][...] = acc.astype(out_dtype)

    return _mm1(name, (M // tm, N // tn), ins, [(_sds((M, N), out_dtype), _bs((tm, tn), lambda i, j: (i, j)))], fn)[0]


def _proj_rows_t(name, pairs, out_dtype):
    dy0, w0 = pairs[0]
    M, N = dy0.shape
    ks = w0.shape[1]
    tm = _pick(M, TM_BIG)
    ins = []
    for dy, w in pairs:
        ins.append((dy, _bs((tm, N), lambda i, j: (i, 0))))
        ins.append((w, _bs((None, ks, N), lambda i, j: (j, 0, 0))))

    def fn(refs, outs):
        acc = _dot(refs[0][...], refs[1][...], 'nt')
        for p in range(1, len(pairs)):
            acc = acc + _dot(refs[2 * p][...], refs[2 * p + 1][...], 'nt')
        outs[0][...] = acc.astype(out_dtype)

    return _mm1(name, (M // tm, N_CHIPS), ins,
                [(_sds((M, N_CHIPS * ks), out_dtype), _bs((tm, ks), lambda i, j: (i, j)))], fn)[0]


def _grad_rows(name, a, dys):
    T, K = a.shape
    N = dys[0].shape[1]
    ks = K // N_CHIPS
    tn = _pick(N, TN_BIG)
    ins = [(a, _bs((T, ks), lambda i, j: (0, i)))] + [(dy, _bs((T, tn), lambda i, j: (0, j))) for dy in dys]

    def fn(refs, outs):
        av = refs[0][...]
        for p in range(len(dys)):
            outs[p][...] = _dot(av, refs[1 + p][...], 'tn').astype(BF16)

    gspec = _bs((None, ks, tn), lambda i, j: (i, 0, j))
    return _mm1(name, (N_CHIPS, N // tn), ins, [(_sds((N_CHIPS, ks, N), BF16), gspec)] * len(dys), fn)


def _softmax_rows(s):
    s = s - jnp.max(s, axis=-1, keepdims=True)
    p = jnp.exp(s)
    return p / jnp.sum(p, axis=-1, keepdims=True)


def _attn_fwd_tile(hd, scale):
    def fn(i, q, k, v):
        outs = []
        for h in range(N_MEM_HEADS):
            sl = slice(h * hd, (h + 1) * hd)
            p = _softmax_rows(lax.dot_general(q[:, sl], k[:, sl], NT, preferred_element_type=F32) * scale)
            outs.append(lax.dot_general(p.astype(BF16), v[:, sl], NN, preferred_element_type=F32))
        return [jnp.concatenate(outs, axis=1)], []
    return fn


def _attn_bwd_tile(hd, scale):
    def fn(i, q, do, k, v):
        dqs, dks, dvs = [], [], []
        for h in range(N_MEM_HEADS):
            sl = slice(h * hd, (h + 1) * hd)
            qh, kh, vh, doh = q[:, sl], k[:, sl], v[:, sl], do[:, sl]
            p = _softmax_rows(lax.dot_general(qh, kh, NT, preferred_element_type=F32) * scale)
            dvs.append(lax.dot_general(p.astype(BF16), doh, TN_, preferred_element_type=F32))
            dp = lax.dot_general(doh, vh, NT, preferred_element_type=F32)
            ds = (p * (dp - jnp.sum(dp * p, axis=-1, keepdims=True)) * scale).astype(BF16)
            dqs.append(lax.dot_general(ds, kh, NN, preferred_element_type=F32))
            dks.append(lax.dot_general(ds, qh, TN_, preferred_element_type=F32))
        return [jnp.concatenate(dqs, axis=1)], [jnp.concatenate(dks, axis=1), jnp.concatenate(dvs, axis=1)]
    return fn


def _attn_fwd(l, x, mem, gq, gkv, W, tile):
    T, D = x.shape
    M = mem.shape[0]
    hd = D // N_MEM_HEADS
    hq = _norm_fwd(f"xa{l}_normq", x, gq, tile)
    mn = _norm_fwd(f"xa{l}_normkv", mem, gkv, _pick(M, tile, 16))
    q = _proj_rows(f"xa{l}_q", hq, W['xattn_wq'], BF16)
    k = _proj_rows(f"xa{l}_k", mn, W['xattn_wk'], BF16)
    v = _proj_rows(f"xa{l}_v", mn, W['xattn_wv'], BF16)
    o = _rowwise(f"xa{l}_attn", _attn_fwd_tile(hd, hd ** -0.5), [('row', q), ('const', k), ('const', v)],
                 [(D, BF16)], [], tile)[0]
    xo = _proj_rows(f"xa{l}_o", o, W['xattn_wo'], F32, extras=(x,), epilogue=lambda acc, ex: ex[0] + acc)
    return xo, (x, hq, mn, q, k, v, o)


def _attn_bwd(l, dxo, dxb, saved, mem, gq, gkv, W, G, tile):
    x, hq, mn, q, k, v, o = saved
    T, D = x.shape
    M = mem.shape[0]
    hd = D // N_MEM_HEADS
    do = _proj_rows_t(f"xa{l}_do", [(dxb, W['xattn_wo'])], BF16)
    G['xattn_wo'] = _grad_rows(f"xa{l}_dwo", o, [dxb])[0]
    dq, dk, dv = _rowwise(f"xa{l}_attn_bwd", _attn_bwd_tile(hd, hd ** -0.5),
                          [('row', q), ('row', do), ('const', k), ('const', v)], [(D, BF16)], [(M, D), (M, D)], tile)
    dhq = _proj_rows_t(f"xa{l}_dhq", [(dq, W['xattn_wq'])], F32)
    G['xattn_wq'] = _grad_rows(f"xa{l}_dwq", hq, [dq])[0]
    dmn = _proj_rows_t(f"xa{l}_dmn", [(dk, W['xattn_wk']), (dv, W['xattn_wv'])], F32)
    G['xattn_wk'], G['xattn_wv'] = _grad_rows(f"xa{l}_dwkv", mn, [dk, dv])
    dx, dxb2, dgq = _norm_bwd(f"xa{l}_normq_bwd", x, gq, dhq, dxo, tile)
    dgkv = _norm_bwd(f"xa{l}_normkv_bwd", mem, gkv, dmn, None, _pick(M, tile, 16))
    return dx, dxb2, dgq, dgkv


def _chunk_mask():
    p = lax.broadcasted_iota(jnp.int32, (GMLP_BLOCK, GMLP_BLOCK), 0)
    q = lax.broadcasted_iota(jnp.int32, (GMLP_BLOCK, GMLP_BLOCK), 1)
    return (q // CHUNK) <= (p // CHUNK)


def _spatial_fwd(vn, ws_ref, bsf, mask, hd):
    vb = vn.astype(BF16)
    wsm = [jnp.where(mask, ws_ref[h], 0.0).astype(BF16) for h in range(A_HEADS)]
    rows = []
    for n in range(vn.shape[0] // GMLP_BLOCK):
        blk = vb[n * GMLP_BLOCK:(n + 1) * GMLP_BLOCK]
        cols = [lax.dot_general(wsm[h], blk[:, h * hd:(h + 1) * hd], NN, preferred_element_type=F32)
                for h in range(A_HEADS)]
        rows.append(jnp.concatenate(cols, axis=1) + bsf)
    return jnp.concatenate(rows, axis=0)


def _spatial_bwd(dsp, vn, ws_ref, mask, hd):
    vb, db16 = vn.astype(BF16), dsp.astype(BF16)
    wsm = [jnp.where(mask, ws_ref[h], 0.0).astype(BF16) for h in range(A_HEADS)]
    dws = [jnp.zeros((GMLP_BLOCK, GMLP_BLOCK), F32) for _ in range(A_HEADS)]
    dbs = jnp.zeros((GMLP_BLOCK, vn.shape[1]), F32)
    rows = []
    for n in range(vn.shape[0] // GMLP_BLOCK):
        sl = slice(n * GMLP_BLOCK, (n + 1) * GMLP_BLOCK)
        cols = []
        for h in range(A_HEADS):
            hs = slice(h * hd, (h + 1) * hd)
            cols.append(lax.dot_general(wsm[h], db16[sl, hs], TN_, preferred_element_type=F32))
            dws[h] = dws[h] + lax.dot_general(db16[sl, hs], vb[sl, hs], NT, preferred_element_type=F32)
        rows.append(jnp.concatenate(cols, axis=1))
        dbs = dbs + dsp[sl]
    dws = [jnp.where(mask, d, 0.0) for d in dws]
    return jnp.concatenate(rows, axis=0), dws, dbs


def _conv_taps(cat, cw_ref, kw, tile):
    acc = jnp.zeros((tile, cat.shape[1]), F32)
    for k in range(kw):
        sh = kw - 1 - k
        r = cat if sh == 0 else pltpu.roll(cat, sh, 0)
        acc = acc + r[HALO:] * cw_ref[k:k + 1, :]
    return acc


def _mix_fwd_tile(A, B, kw, tile):
    hd = A // A_HEADS

    def fn(i, z, zp, ws_ref, bsf, glg, glb, cw_ref, cb, clg, clb):
        mask = _chunk_mask()
        u = _gelu(z[:, :A])
        vn = _ln(_gelu(z[:, A:2 * A]), glg, glb)
        ya = u * _spatial_fwd(vn, ws_ref, bsf, mask, hd)
        hb = _glu(z[:, 2 * A:2 * A + B], z[:, 2 * A + B:])
        hp = jnp.where(i > 0, _glu(zp[:, 2 * A:2 * A + B], zp[:, 2 * A + B:]), 0.0)
        conv = _conv_taps(jnp.concatenate([hp, hb], axis=0), cw_ref, kw, tile) + cb
        yb = _silu(_ln(conv, clg, clb))
        return [jnp.concatenate([ya, yb], axis=1)], []
    return fn


def _mix_bwd1_tile(A, B, kw, tile):
    hd = A // A_HEADS

    def fn(i, z, zp, dy, dxo, ws_ref, bsf, glg, glb, cw_ref, cb, clg, clb):
        mask = _chunk_mask()
        dya, dyb = dy[:, :A], dy[:, A:]
        zu, zv = z[:, :A], z[:, A:2 * A]
        u, vjp_u = jax.vjp(_gelu, zu)
        vn, vjp_v = jax.vjp(lambda t, g, b: _ln(_gelu(t), g, b), zv, glg, glb)
        sp = _spatial_fwd(vn, ws_ref, bsf, mask, hd)
        dzu = vjp_u(dya * sp)[0]
        dvn, dws, dbs = _spatial_bwd(dya * u, vn, ws_ref, mask, hd)
        dzv, dglg, dglb = vjp_v(dvn)
        hb = _glu(z[:, 2 * A:2 * A + B], z[:, 2 * A + B:])
        hp = jnp.where(i > 0, _glu(zp[:, 2 * A:2 * A + B], zp[:, 2 * A + B:]), 0.0)
        cat = jnp.concatenate([hp, hb], axis=0)
        conv = _conv_taps(cat, cw_ref, kw, tile) + cb
        _, vjp_c = jax.vjp(lambda t, g, b: _silu(_ln(t, g, b)), conv, clg, clb)
        dconv, dclg, dclb = vjp_c(dyb)
        tap = lax.broadcasted_iota(jnp.int32, (HALO, 1), 0)
        dcw = jnp.zeros((HALO, B), F32)
        for k in range(kw):
            sh = kw - 1 - k
            r = cat if sh == 0 else pltpu.roll(cat, sh, 0)
            dcw = dcw + jnp.where(tap == k, jnp.sum(dconv * r[HALO:], axis=0, keepdims=True), 0.0)
        dcb = jnp.sum(dconv, axis=0, keepdims=True)
        dbo = jnp.sum(dxo, axis=0, keepdims=True)
        dws = jnp.concatenate([d[None] for d in dws], axis=0)
        return [jnp.concatenate([dzu, dzv], axis=1), dconv], [dws, dbs, dglg, dglb, dcw, dcb, dclg, dclb, dbo]
    return fn


def _mix_bwd2_tile(A, B, kw, tile, n_tiles):
    def fn(i, z, dza, dc, dcn, cw_ref):
        dcn = jnp.where(i < n_tiles - 1, dcn, 0.0)
        cat = jnp.concatenate([dc, dcn], axis=0)
        n = tile + HALO
        dhb = jnp.zeros((tile, B), F32)
        for k in range(kw):
            sh = kw - 1 - k
            r = cat if sh == 0 else pltpu.roll(cat, n - sh, 0)
            dhb = dhb + r[:tile] * cw_ref[k:k + 1, :]
        _, vjp_g = jax.vjp(_glu, z[:, 2 * A:2 * A + B], z[:, 2 * A + B:])
        da, dg = vjp_g(dhb)
        dz = jnp.concatenate([dza, da, dg], axis=1)
        return [dz], [jnp.sum(dz, axis=0, keepdims=True)]
    return fn


def _even_consts(p, e, A, B, kw):
    hd = A // A_HEADS
    bsf = jnp.repeat(p['gmlp_b_s'][e].T, hd, axis=1)
    cw = jnp.pad(p['conv_w_full'][e], ((0, HALO - kw), (0, 0)))
    return [('cref', p['gmlp_w_s'][e]), ('const', bsf), ('const', p['gmlp_ln_g'][e].reshape(1, A)),
            ('const', p['gmlp_ln_b'][e].reshape(1, A)), ('cref', cw), ('const', p['conv_b'][e].reshape(1, B)),
            ('const', p['conv_ln_g'][e].reshape(1, B)), ('const', p['conv_ln_b'][e].reshape(1, B))]


def _even_fwd(l, e, x, gm, p, W, tile):
    T, D = x.shape
    w_in, w_out = W['ab_w_in'], W['ab_w_out']
    zs = w_in.shape[2]
    Z = N_CHIPS * zs
    A = p['gmlp_ln_g'].shape[1]
    B = p['conv_b'].shape[1]
    kw = p['conv_w_full'].shape[1]
    tm = _pick(T, TM_BIG)
    h = _norm_fwd(f"mix{l}_norm", x, gm, tile)

    def in_fn(refs, outs):
        outs[0][...] = _dot(refs[0][...], refs[1][...], 'nn') + refs[2][...]

    z = _mm1(f"mix{l}_in", (N_CHIPS, T // tm),
             [(h, _bs((tm, D), lambda j, i: (i, 0))), (w_in, _bs((None, D, zs), lambda j, i: (j, 0, 0))),
              (p['ab_b_in'][e].reshape(1, Z), _bs((1, zs), lambda j, i: (0, j)))],
             [(_sds((T, Z), F32), _bs((tm, zs), lambda j, i: (i, j)))], in_fn)[0]
    consts = _even_consts(p, e, A, B, kw)
    ycat = _rowwise(f"mix{l}_mid", _mix_fwd_tile(A, B, kw, tile), [('row', z), ('prev', z)] + consts,
                    [(A + B, BF16)], [], tile)[0]
    xo = _proj_rows(f"mix{l}_out", ycat, w_out, F32, extras=(x, p['ab_b_out'][e].reshape(1, D)),
                    epilogue=lambda acc, ex: ex[0] + acc + ex[1])
    return xo, (x, h, z, ycat)


def _even_bwd(l, e, dxo, dxb, saved, gm, p, W, G, tile):
    x, h, z, ycat = saved
    T, D = x.shape
    w_in, w_out = W['ab_w_in'], W['ab_w_out']
    zs = w_in.shape[2]
    Z = N_CHIPS * zs
    A = p['gmlp_ln_g'].shape[1]
    B = p['conv_b'].shape[1]
    kw = p['conv_w_full'].shape[1]
    hd = A // A_HEADS
    dycat = _proj_rows_t(f"mix{l}_dycat", [(dxb, w_out)], F32)
    G['ab_w_out'] = _grad_rows(f"mix{l}_dwout", ycat, [dxb])[0]
    consts = _even_consts(p, e, A, B, kw)
    accs = [(A_HEADS, GMLP_BLOCK, GMLP_BLOCK), (GMLP_BLOCK, A), (1, A), (1, A), (HALO, B), (1, B), (1, B), (1, B),
            (1, D)]
    dza, dconv, dws, dbs, dglg, dglb, dcw, dcb, dclg, dclb, dbo = _rowwise(
        f"mix{l}_mid_bwd1", _mix_bwd1_tile(A, B, kw, tile),
        [('row', z), ('prev', z), ('row', dycat), ('row', dxo)] + consts, [(2 * A, F32), (B, F32)], accs, tile)
    dz, dbin = _rowwise(f"mix{l}_mid_bwd2", _mix_bwd2_tile(A, B, kw, tile, T // tile),
                        [('row', z), ('row', dza), ('row', dconv), ('next', dconv), consts[4]],
                        [(Z, BF16)], [(1, Z)], tile)
    tmd = _pick(D, TM_BIG)

    def dwin_fn(refs, outs):
        outs[0][...] = _dot(refs[0][...], refs[1][...], 'tn').astype(BF16)

    G['ab_w_in'] = _mm1(f"mix{l}_dwin", (N_CHIPS, D // tmd),
                        [(h, _bs((T, tmd), lambda j, i: (0, i))), (dz, _bs((T, zs), lambda j, i: (0, j)))],
                        [(_sds(w_in.shape, BF16), _bs((None, tmd, zs), lambda j, i: (j, i, 0)))], dwin_fn)[0]
    tm, tn = _pick(T, TM_MID), _pick(D, TN_SMALL)

    def dh_fn(refs, outs):
        acc = None
        for j in range(N_CHIPS):
            t = _dot(refs[0][:, j * zs:(j + 1) * zs], refs[1][j], 'nt')
            acc = t if acc is None else acc + t
        outs[0][...] = acc

    dh = _mm1(f"mix{l}_dh", (T // tm, D // tn),
              [(dz, _bs((tm, Z), lambda i, j: (i, 0))), (w_in, _bs((N_CHIPS, tn, zs), lambda i, j: (0, j, 0)))],
              [(_sds((T, D), F32), _bs((tm, tn), lambda i, j: (i, j)))], dh_fn)[0]
    dx, dxb2, dgm = _norm_bwd(f"mix{l}_norm_bwd", x, gm, dh, dxo, tile)
    small = {'ab_b_in': dbin.reshape(Z), 'gmlp_w_s': dws, 'gmlp_b_s': dbs.reshape(GMLP_BLOCK, A_HEADS, hd).sum(-1).T,
             'gmlp_ln_g': dglg.reshape(A), 'gmlp_ln_b': dglb.reshape(A), 'conv_w': dcw[:kw], 'conv_b': dcb.reshape(B),
             'conv_ln_g': dclg.reshape(B), 'conv_ln_b': dclb.reshape(B), 'ab_b_out': dbo.reshape(D)}
    return dx, dxb2, dgm, small


def _pool_counts(t, cg):
    return jnp.concatenate([jnp.broadcast_to(jnp.minimum(t + 1, w).astype(F32), (t.shape[0], cg))
                            for w in POOL_WINDOWS], axis=1)


def _window_sums(cat, cg, back):
    n = cat.shape[0]
    outs = []
    for gi, w in enumerate(POOL_WINDOWS):
        s = cat[:, gi * cg:(gi + 1) * cg]
        step = 1
        while step < w:
            s = s + pltpu.roll(s, step if back else n - step, 0)
            step *= 2
        outs.append(s)
    return jnp.concatenate(outs, axis=1)


def _pool_fwd_tile(D, tile):
    cg = D // len(POOL_WINDOWS)

    def fn(i, x, xp, g):
        h = _rms(x, g)
        hp = jnp.where(i > 0, _rms(xp, g), 0.0)
        sums = _window_sums(jnp.concatenate([hp, h], axis=0), cg, True)[HALO:]
        return [sums / _pool_counts(_row_ids(i, tile, tile), cg) - h], []
    return fn


def _pool_bwd_tile(D, tile, n_tiles):
    cg = D // len(POOL_WINDOWS)

    def fn(i, dd, ddn, x, dxo, g):
        e = dd / _pool_counts(_row_ids(i, tile, tile), cg)
        en = jnp.where(i < n_tiles - 1, ddn / _pool_counts(_row_ids(i + 1, tile, HALO), cg), 0.0)
        dh = _window_sums(jnp.concatenate([e, en], axis=0), cg, False)[:tile] - dd
        _, vjp = jax.vjp(_rms, x, g)
        dx, dg = vjp(dh)
        return [dxo + dx, dxo + dx], [dg]
    return fn


def _odd_fwd(l, o, x, gm, p, W, tile):
    T, D = x.shape
    wc = W['pool_w']
    cg = wc.shape[2]
    cs = cg // N_CHIPS
    ng = len(POOL_WINDOWS)
    tm = _pick(T, TM)
    d = _rowwise(f"mix{l}_pool", _pool_fwd_tile(D, tile), [('row', x), ('prev', x), ('const', gm.reshape(1, D))],
                 [(D, BF16)], [], tile)[0]
    gspec = _bs((tm, cg), lambda i, j, k: (i, j))
    vspec = _bs((1, cg), lambda i, j, k: (0, j))

    def epi(accs, ex):
        pre = accs[0] + ex[0]
        return [ex[2] + pre * ex[1], pre]

    xo, pre = _mm(f"mix{l}_poolmm", (T // tm, ng, N_CHIPS),
                  [(d, _bs((tm, cs), lambda i, j, k: (i, j * N_CHIPS + k))),
                   (wc, _bs((None, cs, cg), lambda i, j, k: (k, j, 0))),
                   (p['pool_b_full'][o].reshape(1, D), vspec), (p['pool_scale_full'][o].reshape(1, D), vspec),
                   (x, gspec)],
                  [(0, 1, 'nn', 0)], [(_sds((T, D), F32), gspec)] * 2, [(tm, cg)], epi, extras=(2, 3, 4))
    return xo, (x, d, pre)


def _odd_bwd(l, o, dxo, dxb, saved, gm, p, W, G, tile):
    x, d, pre = saved
    T, D = x.shape
    wc = W['pool_w']
    cg = wc.shape[2]
    cs = cg // N_CHIPS
    ng = len(POOL_WINDOWS)
    tm, tkt = _pick(T, TM), _pick(T, TK)

    def fn(i, dxv, prev, sc):
        return [dxv * sc], [jnp.sum(dxv * prev, axis=0, keepdims=True), jnp.sum(dxv * sc, axis=0, keepdims=True)]

    do, dscale, dbc = _rowwise(f"mix{l}_pool_bwd1", fn,
                               [('row', dxo), ('row', pre), ('const', p['pool_scale_full'][o].reshape(1, D))],
                               [(D, BF16)], [(1, D), (1, D)], tile)
    nb = ng * N_CHIPS
    dd = _mm(f"mix{l}_pool_dd", (T // tm, nb, 1),
             [(do, _bs((tm, cg), lambda i, j, k: (i, j // N_CHIPS))),
              (wc, _bs((None, cs, cg), lambda i, j, k: (j % N_CHIPS, j // N_CHIPS, 0)))],
             [(0, 1, 'nt', 0)], [(_sds((T, D), F32), _bs((tm, cs), lambda i, j, k: (i, j)))], [(tm, cs)],
             lambda a, _: a)[0]
    G['pool_w'] = _mm(f"mix{l}_pool_dw", (nb, 1, T // tkt),
                      [(d, _bs((tkt, cs), lambda i, j, k: (k, i))), (do, _bs((tkt, cg), lambda i, j, k: (k, i // N_CHIPS)))],
                      [(0, 1, 'tn', 0)],
                      [(_sds(wc.shape, BF16), _bs((None, cs, cg), lambda i, j, k: (i % N_CHIPS, i // N_CHIPS, 0)))],
                      [(cs, cg)], lambda a, _: a)[0]
    dx, dxb2, dgm = _rowwise(f"mix{l}_pool_bwd2", _pool_bwd_tile(D, tile, T // tile),
                             [('row', dd), ('next', dd), ('row', x), ('row', dxo), ('const', gm.reshape(1, D))],
                             [(D, F32), (D, BF16)], [(1, D)], tile)
    small = {'pool_b': dbc.reshape(ng, cg), 'pool_scale': dscale.reshape(D)}
    return dx, dxb2, dgm, small


def _final(x, g, tgt, tile):
    T, D = x.shape

    def fn(i, xv, tv, gv):
        y, vjp = jax.vjp(_rms, xv, gv)
        err = y - tv
        dx, dg = vjp(err / D)
        loss = 0.5 * jnp.sum(jnp.mean(err * err, axis=-1, keepdims=True), axis=0, keepdims=True)
        return [dx, dx], [dg, jnp.broadcast_to(loss, (1, LANES))]

    dx, dxb, dg, loss = _rowwise("final", fn, [('row', x), ('row', tgt), ('const', g.reshape(1, D))],
                                 [(D, F32), (D, BF16)], [(1, D), (1, LANES)], tile)
    return dx, dxb, dg.reshape(D), loss[0, 0]


def _as3d(name, w):
    return w.reshape(w.shape[0], -1, w.shape[-1]) if name == 'pool_w' else w


def kernel(x, mem, norm_ffn1, ffn1_gate, ffn1_up, ffn1_down, norm_mix, ab_w_in, ab_b_in, gmlp_w_s, gmlp_b_s, gmlp_ln_g, gmlp_ln_b, conv_w, conv_b, conv_ln_g, conv_ln_b, ab_w_out, ab_b_out, pool_w, pool_b, pool_scale, norm_xq, norm_xkv, xattn_wq, xattn_wk, xattn_wv, xattn_wo, norm_ffn2, ffn2_gate, ffn2_up, ffn2_down, norm_final, loss_target, m_norm_ffn1, m_ffn1_gate, m_ffn1_up, m_ffn1_down, m_norm_mix, m_ab_w_in, m_ab_b_in, m_gmlp_w_s, m_gmlp_b_s, m_gmlp_ln_g, m_gmlp_ln_b, m_conv_w, m_conv_b, m_conv_ln_g, m_conv_ln_b, m_ab_w_out, m_ab_b_out, m_pool_w, m_pool_b, m_pool_scale, m_norm_xq, m_norm_xkv, m_xattn_wq, m_xattn_wk, m_xattn_wv, m_xattn_wo, m_norm_ffn2, m_ffn2_gate, m_ffn2_up, m_ffn2_down, m_norm_final, v_norm_ffn1, v_ffn1_gate, v_ffn1_up, v_ffn1_down, v_norm_mix, v_ab_w_in, v_ab_b_in, v_gmlp_w_s, v_gmlp_b_s, v_gmlp_ln_g, v_gmlp_ln_b, v_conv_w, v_conv_b, v_conv_ln_g, v_conv_ln_b, v_ab_w_out, v_ab_b_out, v_pool_w, v_pool_b, v_pool_scale, v_norm_xq, v_norm_xkv, v_xattn_wq, v_xattn_wk, v_xattn_wv, v_xattn_wo, v_norm_ffn2, v_ffn2_gate, v_ffn2_up, v_ffn2_down, v_norm_final):
    w = dict(zip(WEIGHTS, [norm_ffn1, ffn1_gate, ffn1_up, ffn1_down, norm_mix, ab_w_in, ab_b_in, gmlp_w_s, gmlp_b_s, gmlp_ln_g, gmlp_ln_b, conv_w, conv_b, conv_ln_g, conv_ln_b, ab_w_out, ab_b_out, pool_w, pool_b, pool_scale, norm_xq, norm_xkv, xattn_wq, xattn_wk, xattn_wv, xattn_wo, norm_ffn2, ffn2_gate, ffn2_up, ffn2_down, norm_final]))
    m = dict(zip(WEIGHTS, [m_norm_ffn1, m_ffn1_gate, m_ffn1_up, m_ffn1_down, m_norm_mix, m_ab_w_in, m_ab_b_in, m_gmlp_w_s, m_gmlp_b_s, m_gmlp_ln_g, m_gmlp_ln_b, m_conv_w, m_conv_b, m_conv_ln_g, m_conv_ln_b, m_ab_w_out, m_ab_b_out, m_pool_w, m_pool_b, m_pool_scale, m_norm_xq, m_norm_xkv, m_xattn_wq, m_xattn_wk, m_xattn_wv, m_xattn_wo, m_norm_ffn2, m_ffn2_gate, m_ffn2_up, m_ffn2_down, m_norm_final]))
    v = dict(zip(WEIGHTS, [v_norm_ffn1, v_ffn1_gate, v_ffn1_up, v_ffn1_down, v_norm_mix, v_ab_w_in, v_ab_b_in, v_gmlp_w_s, v_gmlp_b_s, v_gmlp_ln_g, v_gmlp_ln_b, v_conv_w, v_conv_b, v_conv_ln_g, v_conv_ln_b, v_ab_w_out, v_ab_b_out, v_pool_w, v_pool_b, v_pool_scale, v_norm_xq, v_norm_xkv, v_xattn_wq, v_xattn_wk, v_xattn_wv, v_xattn_wo, v_norm_ffn2, v_ffn2_gate, v_ffn2_up, v_ffn2_down, v_norm_final]))

    xs, mems, tgt = x[0], mem[0], loss_target[0]
    T, D = xs.shape
    L = norm_ffn1.shape[0]
    tile = _pick(T, ROW_TILE)
    cx, cy, cc = _mesh_pos()
    chip = 2 * cx + cy
    w3 = {n: _as3d(n, w[n]) for n in BIG}
    names = [_layer_names(l) for l in range(L)]

    sh_shapes = [w[n].shape for n in SMALL_SHARDED]
    slots = _gather_all("gather_small_shards", _pack([w[n] for n in SMALL_SHARDED]), jnp.zeros((8, LANES), F32))
    per_chip = [_unpack(slots[2 * j], sh_shapes) for j in range(N_CHIPS)]
    full = {n: jnp.concatenate([per_chip[j][k] for j in range(N_CHIPS)], axis=-1) for k, n in enumerate(SMALL_SHARDED)}
    p = dict(w)
    p['conv_w_full'] = full['conv_w'].reshape(full['conv_w'].shape[0], full['conv_w'].shape[1], -1)
    p['pool_b_full'] = full['pool_b']
    p['pool_scale_full'] = full['pool_scale']

    near, far = {}, {}

    def start_near(l, after, carry=()):
        slabs = [_cast_slab(f"cast_{n}_{l}", w3[n], _stack_index(n, l), chip) for n in names[l]]
        sends, arrivals, slabs, carry, tok = _split_start(f"gather_near_start_{l}", slabs, after, 2 * len(slabs),
                                                          _gather_near_copies, carry)
        near[l] = (sends, arrivals, slabs)
        return carry, tok

    def start_far(l, after, carry=()):
        sends, arrivals, slabs = near.pop(l)
        slabs = _split_wait(f"gather_near_wait_{l}", slabs, sends, arrivals, after, _gather_near_copies)
        sends, arrivals, slabs, carry, tok = _split_start(f"gather_far_start_{l}", slabs, jnp.zeros((8, LANES), F32),
                                                          2 * len(slabs), _gather_far_copies, carry)
        far[l] = (sends, arrivals, slabs)
        return carry, tok

    def finish_gather(l, after):
        sends, arrivals, slabs = far.pop(l)
        slabs = _split_wait(f"gather_far_wait_{l}", slabs, sends, arrivals, after, _gather_far_copies)
        return dict(zip(names[l], _forward_halves(f"gather_fwd_{l}", slabs)))

    _, tok = start_near(0, slots)
    _, tok = start_far(0, tok)
    if L > 1:
        _, tok = start_near(1, tok)

    saved, Wl = [], []
    xc = xs + tok[0, 0]
    W = finish_gather(0, xc)
    for l in range(L):
        Wl.append(W)
        s = {}
        xc, s['ffn1'] = _ffn_fwd(f"ffn1_{l}", xc, w['norm_ffn1'][l], W['ffn1_gate'], W['ffn1_up'], W['ffn1_down'], tile)
        if l % 2 == 0:
            xc, s['mix'] = _even_fwd(l, l // 2, xc, w['norm_mix'][l], p, W, tile)
        else:
            xc, s['mix'] = _odd_fwd(l, l // 2, xc, w['norm_mix'][l], p, W, tile)
        if l + 1 < L:
            (xc,), tok = start_far(l + 1, xc, (xc,))
            if l + 2 < L:
                (xc,), tok = start_near(l + 2, tok, (xc,))
        xc, s['xa'] = _attn_fwd(l, xc, mems, w['norm_xq'][l], w['norm_xkv'][l], W, tile)
        xc, s['ffn2'] = _ffn_fwd(f"ffn2_{l}", xc, w['norm_ffn2'][l], W['ffn2_gate'], W['ffn2_up'], W['ffn2_down'], tile)
        saved.append(s)
        if l + 1 < L:
            W = finish_gather(l + 1, xc)

    dx, dxb, g_final, loss_local = _final(xc, w['norm_final'], tgt, tile)
    loss = lax.psum(loss_local, ("x", "y", "c"))
    gfull = {n: lax.empty(w3[n].shape, F32) for n in BIG}
    gs = {n: [None] * w[n].shape[0] for n in SMALL if n != 'norm_final'}

    def finish_exchange(pending, after):
        tag, l, ns, sends, arrivals, thru = pending
        thru = _split_wait(f"rs_wait_{tag}", thru, sends, arrivals, after, _exchange_copies)
        parts, lands = thru[:len(ns)], thru[len(ns):]
        for n, part, land in zip(ns, parts, lands):
            gfull[n] = _sum_into(f"rs_sum_{n}_{l}", part, land, gfull[n], _stack_index(n, l), chip, cc)
        joined = _join_halves(f"rs_join_{tag}", [gfull[n] for n in ns], [_stack_index(n, l) for n in ns])
        gfull.update(zip(ns, joined))

    def start_exchange(tag, l, ns, G, pending, dx, dxb):
        if len(pending) >= EXCHANGES_IN_FLIGHT:
            finish_exchange(pending.pop(0), dx)
        grads_g = [G[n] for n in ns]
        got = _swap_halves(f"rs_swap_{tag}", grads_g)
        parts = [_add_halves(f"rs_add_{n}_{l}", g, r, cc) for n, g, r in zip(ns, grads_g, got)]
        lands = [lax.empty((3,) + part.shape[1:], BF16) for part in parts]
        sends, arrivals, thru, (dx, dxb), tok = _split_start(
            f"rs_start_{tag}", parts + lands, jnp.zeros((8, LANES), F32), 3 * len(parts), _exchange_copies,
            carry=(dx, dxb))
        pending.append((tag, l, ns, sends, arrivals, thru))
        return dx, dxb, tok

    pending = []
    for l in reversed(range(L)):
        first = [n for n in names[l] if n.startswith(('ffn2', 'xattn'))]
        second = [n for n in names[l] if n not in first]
        s, W, G = saved[l], Wl[l], {}
        dx, dxb, dg, G['ffn2_gate'], G['ffn2_up'], G['ffn2_down'] = _ffn_bwd(
            f"ffn2_{l}", dx, dxb, s['ffn2'], w['norm_ffn2'][l], W['ffn2_gate'], W['ffn2_up'], W['ffn2_down'], tile)
        gs['norm_ffn2'][l] = dg.reshape(D)
        dx, dxb, dgq, dgkv = _attn_bwd(l, dx, dxb, s['xa'], mems, w['norm_xq'][l], w['norm_xkv'][l], W, G, tile)
        gs['norm_xq'][l], gs['norm_xkv'][l] = dgq.reshape(D), dgkv.reshape(D)
        dx, dxb, _ = start_exchange(f"a{l}", l, first, G, pending, dx, dxb)
        if l % 2 == 0:
            dx, dxb, dgm, small = _even_bwd(l, l // 2, dx, dxb, s['mix'], w['norm_mix'][l], p, W, G, tile)
        else:
            dx, dxb, dgm, small = _odd_bwd(l, l // 2, dx, dxb, s['mix'], w['norm_mix'][l], p, W, G, tile)
        for n, val in small.items():
            gs[n][l // 2] = val
        gs['norm_mix'][l] = dgm.reshape(D)
        dx, dxb, dg, G['ffn1_gate'], G['ffn1_up'], G['ffn1_down'] = _ffn_bwd(
            f"ffn1_{l}", dx, dxb, s['ffn1'], w['norm_ffn1'][l], W['ffn1_gate'], W['ffn1_up'], W['ffn1_down'], tile)
        gs['norm_ffn1'][l] = dg.reshape(D)
        dx, dxb, tok = start_exchange(f"b{l}", l, second, G, pending, dx, dxb)
    grad_x = dx[None]

    def flat2(n, t):
        t3 = _as3d(n, t)
        return t3.reshape(-1, t3.shape[-1])

    early, dep = {}, []
    for n in BIG:
        R = w3[n].shape[1]
        lo = 0 if n in ODD_ONLY else R
        early[n] = _adam_rows(f"adam_early_{n}", flat2(n, w[n]), flat2(n, gfull[n]), flat2(n, m[n]), flat2(n, v[n]),
                              lo, w3[n].shape[0] * R, R, after=tok)
        dep.append(early[n][1][-1, 0])
    dep = jnp.stack(dep)

    small_full = {n: jnp.stack(gs[n]) for n in gs}
    small_full['norm_final'] = g_final
    full_shapes = [small_full[n].shape for n in SMALL]
    summed = _sum_slots("sum_small", _gather_all("gather_small_grads", _pack([small_full[n] for n in SMALL]), dep))
    g_small = dict(zip(SMALL, _unpack(summed, full_shapes)))
    for n in SMALL_SHARDED:
        width = w[n].shape[-1]
        g_small[n] = lax.dynamic_slice_in_dim(g_small[n], chip * width, width, axis=g_small[n].ndim - 1).reshape(w[n].shape)

    for group in pending:
        finish_exchange(group, summed)
    grads, delta, new_m, new_v = {}, {}, {}, {}
    for n in BIG:
        outs = early[n]
        if n not in ODD_ONLY:
            R = w3[n].shape[1]
            outs = _adam_rows(f"adam_late_{n}", flat2(n, w[n]), flat2(n, gfull[n]), flat2(n, m[n]), flat2(n, v[n]),
                              0, R, R, prev=early[n])
        grads[n], delta[n], new_m[n], new_v[n] = (t.reshape(w[n].shape) for t in outs)
    small_shapes = [w[n].shape for n in SMALL]
    d2, m2, v2 = _adam("adam_small", _pack([w[n] for n in SMALL]), _pack([g_small[n] for n in SMALL]),
                       _pack([m[n] for n in SMALL]), _pack([v[n] for n in SMALL]))
    for n, dn, mn_, vn_ in zip(SMALL, _unpack(d2, small_shapes), _unpack(m2, small_shapes), _unpack(v2, small_shapes)):
        grads[n], delta[n], new_m[n], new_v[n] = g_small[n].reshape(w[n].shape), dn, mn_, vn_

    return (loss, grad_x, *[grads[n] for n in WEIGHTS], *[delta[n] for n in WEIGHTS],
            *[new_m[n] for n in WEIGHTS], *[new_v[n] for n in WEIGHTS])
```

```python
import jax
import jax.numpy as jnp
from jax import lax
from jax.experimental import pallas as pl
from jax.experimental.pallas import tpu as pltpu

F32, BF16 = jnp.float32, jnp.bfloat16
EPS = 1e-6
N_MEM_HEADS = 4
A_HEADS = 8
GMLP_BLOCK = 128
CHUNK = 64
POOL_WINDOWS = (2, 4, 8, 16)
N_CHIPS = 4
N_DEV = 8
HALO = 32
LANES = 128
TM, TN, TK = 512, 1024, 512
TM_BIG, TM_MID = 1024, 512
TN_BIG, TN_MID, TN_SMALL = 1024, 512, 256
EPI_ROWS = 256
EXCHANGES_IN_FLIGHT = 2
ROW_TILE = 256
PACK_ROWS = 512
VMEM_LIMIT = 48 * 1024 * 1024
ADAM_LR, ADAM_B1, ADAM_B2, ADAM_EPS, ADAM_WD, ADAM_STEP = 0.001, 0.9, 0.999, 1e-08, 0.01, 10
MESH = pl.DeviceIdType.MESH
HBM = pl.BlockSpec(memory_space=pltpu.HBM)
SEM = pl.BlockSpec(memory_space=pltpu.SEMAPHORE)
ANY = pl.BlockSpec(memory_space=pl.ANY)
EFFECT = pltpu.SideEffectType.DATAFLOW_SIDE_EFFECTING

WEIGHTS = ['norm_ffn1', 'ffn1_gate', 'ffn1_up', 'ffn1_down', 'norm_mix', 'ab_w_in', 'ab_b_in', 'gmlp_w_s',
           'gmlp_b_s', 'gmlp_ln_g', 'gmlp_ln_b', 'conv_w', 'conv_b', 'conv_ln_g', 'conv_ln_b', 'ab_w_out',
           'ab_b_out', 'pool_w', 'pool_b', 'pool_scale', 'norm_xq', 'norm_xkv', 'xattn_wq', 'xattn_wk',
           'xattn_wv', 'xattn_wo', 'norm_ffn2', 'ffn2_gate', 'ffn2_up', 'ffn2_down', 'norm_final']
BIG = ['ffn1_gate', 'ffn1_up', 'ffn1_down', 'ab_w_in', 'ab_w_out', 'pool_w', 'xattn_wq', 'xattn_wk', 'xattn_wv',
       'xattn_wo', 'ffn2_gate', 'ffn2_up', 'ffn2_down']
EVEN_ONLY, ODD_ONLY = ['ab_w_in', 'ab_w_out'], ['pool_w']
SMALL = [n for n in WEIGHTS if n not in BIG]
SMALL_SHARDED = ['conv_w', 'pool_b', 'pool_scale']

NN = (((1,), (0,)), ((), ()))
NT = (((1,), (1,)), ((), ()))
TN_ = (((0,), (0,)), ((), ()))
_DIMS = {'nn': NN, 'nt': NT, 'tn': TN_}


def _pick(n, pref, unit=LANES):
    if n <= pref:
        return n
    t = (pref // unit) * unit
    while t >= unit:
        if n % t == 0:
            return t
        t -= unit
    return n


def _sds(shape, dtype):
    return jax.ShapeDtypeStruct(tuple(shape), dtype)


def _layer_names(l):
    mix = EVEN_ONLY if l % 2 == 0 else ODD_ONLY
    return ['ffn1_gate', 'ffn1_up', 'ffn1_down'] + mix + ['xattn_wq', 'xattn_wk', 'xattn_wv', 'xattn_wo',
                                                          'ffn2_gate', 'ffn2_up', 'ffn2_down']


def _stack_index(name, l):
    return l // 2 if name in EVEN_ONLY + ODD_ONLY else l


def _mm(name, grid, ins, pairs, outs, acc_shapes, epilogue, extras=()):
    n_in, n_out = len(ins), len(outs)
    nk = grid[2]

    def body(*refs):
        in_refs, out_refs, acc_refs = refs[:n_in], refs[n_in:n_in + n_out], refs[n_in + n_out:]
        k = pl.program_id(2)

        @pl.when(k == 0)
        def _():
            for acc in acc_refs:
                acc[...] = jnp.zeros_like(acc)

        for ai, bi, mode, ci in pairs:
            a = in_refs[ai][...].astype(BF16)
            b = in_refs[bi][...].astype(BF16)
            acc_refs[ci][...] += lax.dot_general(a, b, _DIMS[mode], preferred_element_type=F32)

        @pl.when(k == nk - 1)
        def _():
            res = epilogue([acc[...] for acc in acc_refs], [in_refs[e][...] for e in extras])
            for o, r in zip(out_refs, res):
                o[...] = r.astype(o.dtype)

    return pl.pallas_call(
        body, name=name, grid=grid,
        in_specs=[s for _, s in ins], out_specs=[s for _, s in outs], out_shape=[s for s, _ in outs],
        scratch_shapes=[pltpu.VMEM(s, F32) for s in acc_shapes],
        compiler_params=pltpu.CompilerParams(dimension_semantics=("parallel", "parallel", "arbitrary"),
                                             vmem_limit_bytes=VMEM_LIMIT),
    )(*[a for a, _ in ins])


def _bs(shape, fn):
    return pl.BlockSpec(shape, fn)


def _mm1(name, grid, ins, outs, compute):
    n_in = len(ins)

    def body(*refs):
        compute(refs[:n_in], refs[n_in:])

    return pl.pallas_call(
        body, name=name, grid=grid,
        in_specs=[s for _, s in ins], out_specs=[s for _, s in outs], out_shape=[s for s, _ in outs],
        compiler_params=pltpu.CompilerParams(dimension_semantics=("parallel", "parallel"),
                                             vmem_limit_bytes=VMEM_LIMIT),
    )(*[a for a, _ in ins])


def _dot(a, b, mode):
    return lax.dot_general(a.astype(BF16), b.astype(BF16), _DIMS[mode], preferred_element_type=F32)


def _row_chunks(rows):
    step = min(rows, EPI_ROWS)
    return [slice(r, r + step) for r in range(0, rows, step)]


def _rowwise(name, fn, ins, row_outs, acc_outs, tile):
    T = next(a.shape[0] for k, a in ins if k == 'row')
    n = T // tile
    per = tile // HALO if tile % HALO == 0 else 1
    last = T // HALO - 1
    in_specs = []
    for kind, a in ins:
        if kind == 'row':
            in_specs.append(pl.BlockSpec((tile, a.shape[1]), lambda i: (i, 0)))
        elif kind == 'prev':
            in_specs.append(pl.BlockSpec((HALO, a.shape[1]), lambda i: (jnp.maximum(i * per - 1, 0), 0)))
        elif kind == 'next':
            in_specs.append(pl.BlockSpec((HALO, a.shape[1]), lambda i: (jnp.minimum((i + 1) * per, last), 0)))
        else:
            in_specs.append(pl.BlockSpec(a.shape, lambda i, nd=a.ndim: (0,) * nd))
    n_in, n_row = len(ins), len(row_outs)
    out_shape = [_sds((T, c), dt) for c, dt in row_outs] + [_sds(s, F32) for s in acc_outs]
    out_specs = [pl.BlockSpec((tile, c), lambda i: (i, 0)) for c, _ in row_outs]
    out_specs += [pl.BlockSpec(s, lambda i, nd=len(s): (0,) * nd) for s in acc_outs]
    kinds = [k for k, _ in ins]

    def body(*refs):
        i = pl.program_id(0)
        vals = [r if k == 'cref' else r[...] for k, r in zip(kinds, refs[:n_in])]
        ro, ao = fn(i, *vals)
        for r, v in zip(refs[n_in:n_in + n_row], ro):
            r[...] = v.astype(r.dtype)
        for r, v in zip(refs[n_in + n_row:], ao):
            @pl.when(i == 0)
            def _(r=r, v=v):
                r[...] = v

            @pl.when(i > 0)
            def _(r=r, v=v):
                r[...] += v

    return pl.pallas_call(
        body, name=name, grid=(n,), in_specs=in_specs, out_specs=out_specs, out_shape=out_shape,
        compiler_params=pltpu.CompilerParams(dimension_semantics=("arbitrary",), vmem_limit_bytes=VMEM_LIMIT),
    )(*[a for _, a in ins])


def _rms(x, g):
    return x * lax.rsqrt(jnp.mean(x * x, axis=-1, keepdims=True) + EPS) * g


def _ln(x, g, b):
    mu = jnp.mean(x, axis=-1, keepdims=True)
    xc = x - mu
    var = jnp.mean(xc * xc, axis=-1, keepdims=True)
    return xc * lax.rsqrt(var + EPS) * g + b


def _gelu(x):
    return 0.5 * x * (1.0 + jnp.tanh(0.7978845608028654 * (x + 0.044715 * (x * x * x))))


def _silu(x):
    return x * jax.nn.sigmoid(x)


def _glu(a, g):
    return a * jax.nn.sigmoid(g)


def _row_ids(i, tile, rows):
    return i * tile + lax.broadcasted_iota(jnp.int32, (rows, 1), 0)


def _mesh_pos():
    return lax.axis_index("x"), lax.axis_index("y"), lax.axis_index("c")


def _other_chips(x, y):
    return [(1 - x, y), (x, 1 - y), (1 - x, 1 - y)]


def _remote(src, dst, send_sems, recv_sems, s, to):
    return pltpu.make_async_remote_copy(src_ref=src, dst_ref=dst, send_sem=send_sems.at[s], recv_sem=recv_sems.at[s],
                                        device_id=to, device_id_type=MESH)


def _gather_near_copies(refs, send_sems, recv_sems):
    x, y, c = _mesh_pos()
    me = 2 * x + y
    out = []
    for t, ref in enumerate(refs):
        rh = ref.shape[1] // 2
        half = pl.ds(c * rh, rh)
        for k, (cx, cy) in enumerate(_other_chips(x, y)[:2]):
            mine, theirs = ref.at[me, half], ref.at[2 * cx + cy, half]
            out.append((_remote(mine, mine, send_sems, recv_sems, 2 * t + k, (cx, cy, c)),
                        _remote(theirs, theirs, send_sems, recv_sems, 2 * t + k, (cx, cy, c))))
    return out


def _gather_far_copies(refs, send_sems, recv_sems):
    x, y, c = _mesh_pos()
    xn, yn, diag = 2 * (1 - x) + y, 2 * x + (1 - y), 2 * (1 - x) + (1 - y)
    out = []
    for t, ref in enumerate(refs):
        rq = ref.shape[1] // 4
        q0, q1 = pl.ds(2 * c * rq, rq), pl.ds((2 * c + 1) * rq, rq)
        out.append((_remote(ref.at[yn, q1], ref.at[yn, q1], send_sems, recv_sems, 2 * t, (1 - x, y, c)),
                    _remote(ref.at[diag, q1], ref.at[diag, q1], send_sems, recv_sems, 2 * t, (1 - x, y, c))))
        out.append((_remote(ref.at[xn, q0], ref.at[xn, q0], send_sems, recv_sems, 2 * t + 1, (x, 1 - y, c)),
                    _remote(ref.at[diag, q0], ref.at[diag, q0], send_sems, recv_sems, 2 * t + 1, (x, 1 - y, c))))
    return out


def _exchange_copies(refs, send_sems, recv_sems):
    x, y, c = _mesh_pos()
    n = len(refs) // 2
    out = []
    for t in range(n):
        part, land = refs[t], refs[n + t]
        for k, (cx, cy) in enumerate(_other_chips(x, y)):
            out.append((_remote(part.at[2 * cx + cy], land.at[k], send_sems, recv_sems, 3 * t + k, (cx, cy, c)),
                        _remote(land.at[k], land.at[k], send_sems, recv_sems, 3 * t + k, (cx, cy, c))))
    return out


def _split_start(name, thru, after, n_sems, copies, carry=()):
    n, nc = len(thru), len(carry)
    both = list(thru) + list(carry)

    def body(*refs):
        outs = refs[n + nc + 1:]
        send_sems, recv_sems, thru_refs, token = outs[0], outs[1], outs[2:2 + n], outs[2 + n + nc]
        for send, _ in copies(thru_refs, send_sems, recv_sems):
            send.start()
        token[...] = jnp.zeros_like(token)

    res = pl.pallas_call(
        body, name=name,
        out_shape=(pltpu.SemaphoreType.DMA((n_sems,)), pltpu.SemaphoreType.DMA((n_sems,)),
                   *[pltpu.HBM(b.shape, b.dtype) for b in both], _sds((8, LANES), F32)),
        in_specs=[HBM] * (n + nc) + [ANY],
        out_specs=(SEM, SEM, *[HBM] * (n + nc), pl.BlockSpec(memory_space=pltpu.VMEM)),
        input_output_aliases={i: 2 + i for i in range(n + nc)},
        compiler_params=pltpu.CompilerParams(has_side_effects=EFFECT),
    )(*[pltpu.with_memory_space_constraint(b, pltpu.HBM) for b in both], after)
    return res[0], res[1], list(res[2:2 + n]), list(res[2 + n:2 + n + nc]), res[2 + n + nc]


def _split_wait(name, thru, send_sems, recv_sems, after, copies):
    n = len(thru)

    def body(*refs):
        sends, recvs, outs = refs[n], refs[n + 1], refs[n + 3:]
        for send, arrival in copies(outs, sends, recvs):
            send.wait_send()
            arrival.wait_recv()

    res = pl.pallas_call(
        body, name=name, out_shape=tuple(pltpu.HBM(b.shape, b.dtype) for b in thru),
        in_specs=[HBM] * n + [SEM, SEM, ANY], out_specs=tuple([HBM] * n),
        input_output_aliases={i: i for i in range(n)},
        compiler_params=pltpu.CompilerParams(has_side_effects=EFFECT),
    )(*thru, send_sems, recv_sems, after)
    return list(res)


def _forward_halves(name, bufs):
    n = len(bufs)

    def body(*refs):
        outs, send_sems, recv_sems = refs[n:2 * n], refs[2 * n], refs[2 * n + 1]
        x, y, c = _mesh_pos()
        sibling = (x, y, 1 - c)
        sends, arrivals = [], []
        for t, ref in enumerate(outs):
            rh = ref.shape[1] // 2
            mine, other = pl.ds(c * rh, rh), pl.ds((1 - c) * rh, rh)
            for k, (cx, cy) in enumerate(_other_chips(x, y)):
                landed, coming = ref.at[2 * cx + cy, mine], ref.at[2 * cx + cy, other]
                cp = _remote(landed, landed, send_sems, recv_sems, 3 * t + k, sibling)
                cp.start()
                sends.append(cp)
                arrivals.append(_remote(coming, coming, send_sems, recv_sems, 3 * t + k, sibling))
        for a in arrivals:
            a.wait_recv()
        for cp in sends:
            cp.wait_send()

    res = pl.pallas_call(
        body, name=name, out_shape=tuple(_sds(b.shape, b.dtype) for b in bufs),
        in_specs=[HBM] * n, out_specs=tuple([HBM] * n), input_output_aliases={i: i for i in range(n)},
        scratch_shapes=[pltpu.SemaphoreType.DMA((3 * n,)), pltpu.SemaphoreType.DMA((3 * n,))],
    )(*bufs)
    return list(res)


def _swap_halves(name, gs):
    n = len(gs)

    def body(*refs):
        ins, outs, send_sems, recv_sems = refs[:n], refs[n:2 * n], refs[2 * n], refs[2 * n + 1]
        x, y, c = _mesh_pos()
        cps = []
        for t, (g_ref, o_ref) in enumerate(zip(ins, outs)):
            rh = g_ref.shape[1] // 2
            cp = _remote(g_ref.at[:, pl.ds((1 - c) * rh, rh)], o_ref, send_sems, recv_sems, t, (x, y, 1 - c))
            cp.start()
            cps.append(cp)
        for cp in cps:
            cp.wait_recv()
        for cp in cps:
            cp.wait_send()

    res = pl.pallas_call(
        body, name=name, out_shape=tuple(_sds((g.shape[0], g.shape[1] // 2, g.shape[2]), g.dtype) for g in gs),
        in_specs=[HBM] * n, out_specs=tuple([HBM] * n),
        scratch_shapes=[pltpu.SemaphoreType.DMA((n,)), pltpu.SemaphoreType.DMA((n,))],
    )(*gs)
    return list(res)


def _swap_copies(refs, send_sems, recv_sems):
    x, y, c = _mesh_pos()
    n = len(refs) // 2
    out = []
    for t in range(n):
        g_ref, land = refs[t], refs[n + t]
        rh = g_ref.shape[1] // 2
        out.append((_remote(g_ref.at[:, pl.ds((1 - c) * rh, rh)], land, send_sems, recv_sems, t, (x, y, 1 - c)),
                    _remote(land, land, send_sems, recv_sems, t, (x, y, 1 - c))))
    return out


def _join_copies(idx):
    def copies(refs, send_sems, recv_sems):
        x, y, c = _mesh_pos()
        out = []
        for t, ref in enumerate(refs):
            rh = ref.shape[1] // 2
            mine, theirs = ref.at[idx[t], pl.ds(c * rh, rh)], ref.at[idx[t], pl.ds((1 - c) * rh, rh)]
            out.append((_remote(mine, mine, send_sems, recv_sems, t, (x, y, 1 - c)),
                        _remote(theirs, theirs, send_sems, recv_sems, t, (x, y, 1 - c))))
        return out
    return copies


def _gather_all(name, buf, after):
    def body(b_ref, after_ref, out_ref, send_sems, recv_sems, local_sem):
        x, y, c = _mesh_pos()
        me = 4 * x + 2 * y + c
        local = pltpu.make_async_copy(b_ref, out_ref.at[me], local_sem)
        local.start()
        peers = []
        for k in range(1, N_DEV):
            peers.append((1 - x if k & 4 else x, 1 - y if k & 2 else y, 1 - c if k & 1 else c))
        sends = []
        for k, peer in enumerate(peers):
            cp = _remote(b_ref, out_ref.at[me], send_sems, recv_sems, k, peer)
            cp.start()
            sends.append(cp)
        for k, (px, py, pc) in enumerate(peers):
            slot = out_ref.at[4 * px + 2 * py + pc]
            _remote(slot, slot, send_sems, recv_sems, k, (px, py, pc)).wait_recv()
        for cp in sends:
            cp.wait_send()
        local.wait()

    return pl.pallas_call(
        body, name=name, out_shape=_sds((N_DEV,) + buf.shape, buf.dtype), in_specs=[HBM, ANY], out_specs=HBM,
        scratch_shapes=[pltpu.SemaphoreType.DMA((N_DEV - 1,)), pltpu.SemaphoreType.DMA((N_DEV - 1,)),
                        pltpu.SemaphoreType.DMA],
    )(buf, after)


def _scalars(*vals):
    return jnp.stack([jnp.asarray(v, jnp.int32) for v in vals])


def _cast_slab(name, w3, li, chip):
    _, R, C = w3.shape
    tr = _pick(R, ROW_TILE, 16)

    def body(s_ref, w_ref, o_ref):
        o_ref[...] = w_ref[...].astype(o_ref.dtype)

    grid_spec = pltpu.PrefetchScalarGridSpec(
        num_scalar_prefetch=1, grid=(R // tr,),
        in_specs=[pl.BlockSpec((None, tr, C), lambda r, s: (li, r, 0))],
        out_specs=pl.BlockSpec((None, tr, C), lambda r, s: (s[0], r, 0)))
    return pl.pallas_call(
        body, name=name, grid_spec=grid_spec, out_shape=_sds((N_CHIPS, R, C), BF16),
        compiler_params=pltpu.CompilerParams(dimension_semantics=("arbitrary",), vmem_limit_bytes=VMEM_LIMIT),
    )(_scalars(chip), w3)


def _add_halves(name, g, recv, c):
    _, R, C = g.shape
    rh = R // 2
    tr = _pick(rh, 512, 16)
    nr = rh // tr

    def body(s_ref, g_ref, a_ref, o_ref):
        o_ref[...] = (g_ref[...].astype(F32) + a_ref[...].astype(F32)).astype(o_ref.dtype)

    blk = (None, tr, C)
    grid_spec = pltpu.PrefetchScalarGridSpec(
        num_scalar_prefetch=1, grid=(N_CHIPS, nr),
        in_specs=[pl.BlockSpec(blk, lambda j, r, s: (j, s[0] * nr + r, 0)),
                  pl.BlockSpec(blk, lambda j, r, s: (j, r, 0))],
        out_specs=pl.BlockSpec(blk, lambda j, r, s: (j, r, 0)))
    return pl.pallas_call(
        body, name=name, grid_spec=grid_spec, out_shape=_sds((N_CHIPS, rh, C), g.dtype),
        compiler_params=pltpu.CompilerParams(dimension_semantics=("arbitrary",) * 2, vmem_limit_bytes=VMEM_LIMIT),
    )(_scalars(c), g, recv)


def _sum_into(name, p, recv, gfull, li, chip, c):
    _, rh, C = p.shape
    tr = _pick(rh, 512, 16)
    nr = rh // tr

    def body(s_ref, p_ref, r_ref, g_ref, o_ref):
        acc = p_ref[...].astype(F32)
        for k in range(3):
            acc = acc + r_ref[k].astype(F32)
        o_ref[...] = acc

    grid_spec = pltpu.PrefetchScalarGridSpec(
        num_scalar_prefetch=1, grid=(nr,),
        in_specs=[pl.BlockSpec((None, tr, C), lambda r, s: (s[0], r, 0)),
                  pl.BlockSpec((3, tr, C), lambda r, s: (0, r, 0)), HBM],
        out_specs=pl.BlockSpec((None, tr, C), lambda r, s: (li, s[1] * nr + r, 0)))
    return pl.pallas_call(
        body, name=name, grid_spec=grid_spec, out_shape=_sds(gfull.shape, F32), input_output_aliases={3: 0},
        compiler_params=pltpu.CompilerParams(dimension_semantics=("arbitrary",), vmem_limit_bytes=VMEM_LIMIT),
    )(_scalars(chip, c), p, recv, gfull)


def _sum_slots(name, buf):
    _, n, _ = buf.shape

    def body(b_ref, o_ref):
        acc = b_ref[0]
        for k in range(1, N_DEV):
            acc = acc + b_ref[k]
        o_ref[...] = acc

    return pl.pallas_call(
        body, name=name, grid=(n // PACK_ROWS,), out_shape=_sds((n, LANES), F32),
        in_specs=[pl.BlockSpec((N_DEV, PACK_ROWS, LANES), lambda i: (0, i, 0))],
        out_specs=pl.BlockSpec((PACK_ROWS, LANES), lambda i: (i, 0)),
    )(buf)


def _adam_tile(i, w, g, m, v):
    m = ADAM_B1 * m + (1.0 - ADAM_B1) * g
    v = ADAM_B2 * v + (1.0 - ADAM_B2) * (g * g)
    m_hat = m / (1.0 - ADAM_B1 ** ADAM_STEP)
    v_hat = v / (1.0 - ADAM_B2 ** ADAM_STEP)
    delta = -ADAM_LR * (m_hat / (jnp.sqrt(v_hat) + ADAM_EPS) + ADAM_WD * w)
    return [delta, m, v], []


def _adam(name, w, g, m, v):
    rows, C = w.shape
    tile = _pick(rows, ROW_TILE, 8)
    return _rowwise(name, _adam_tile, [('row', w), ('row', g), ('row', m), ('row', v)], [(C, F32)] * 3, [], tile)


def _adam_rows(name, w, g, m, v, lo, hi, unit, prev=None, after=None):
    rows, C = w.shape
    tile = _pick(unit, ROW_TILE, 8)
    first = lo // tile
    spec = pl.BlockSpec((tile, C), lambda i: (i + first, 0))
    ins, in_specs = [w, g, m, v], [spec] * 4
    if prev is not None:
        ins, in_specs = ins + list(prev), in_specs + [ANY] * 4
    if after is not None:
        ins, in_specs = ins + [after], in_specs + [ANY]

    def body(*refs):
        n_in = len(ins)
        outs, _ = _adam_tile(0, *[r[...] for r in refs[:4]])
        refs[n_in][...] = refs[1][...]
        for o, val in zip(refs[n_in + 1:n_in + 4], outs):
            o[...] = val

    return pl.pallas_call(
        body, name=name, grid=((hi - lo) // tile,), in_specs=in_specs, out_specs=[spec] * 4,
        out_shape=[_sds((rows, C), F32)] * 4,
        input_output_aliases={4 + k: k for k in range(4)} if prev is not None else {},
        compiler_params=pltpu.CompilerParams(dimension_semantics=("arbitrary",), vmem_limit_bytes=VMEM_LIMIT),
    )(*ins)


def _pack(arrs):
    flat = jnp.concatenate([a.reshape(-1).astype(F32) for a in arrs])
    unit = PACK_ROWS * LANES
    n = -(-flat.shape[0] // unit) * unit
    return jnp.pad(flat, (0, n - flat.shape[0])).reshape(-1, LANES)


def _unpack(buf, shapes):
    flat = buf.reshape(-1)
    out, off = [], 0
    for s in shapes:
        n = 1
        for d in s:
            n *= d
        out.append(flat[off:off + n].reshape(s))
        off += n
    return out


def _norm_fwd(name, x, g, tile):
    D = x.shape[1]
    return _rowwise(name, lambda i, xv, gv: ([_rms(xv, gv)], []), [('row', x), ('const', g.reshape(1, D))],
                    [(D, BF16)], [], tile)[0]


def _norm_bwd(name, x, g, dh, dxo, tile):
    D = x.shape[1]
    if dxo is None:
        def fn(i, xv, dhv, gv):
            _, vjp = jax.vjp(_rms, xv, gv)
            return [], [vjp(dhv)[1]]
        return _rowwise(name, fn, [('row', x), ('row', dh), ('const', g.reshape(1, D))], [], [(1, D)], tile)[0]

    def fn(i, xv, dhv, dxv, gv):
        _, vjp = jax.vjp(_rms, xv, gv)
        dx, dg = vjp(dhv)
        return [dxv + dx, dxv + dx], [dg]
    return _rowwise(name, fn, [('row', x), ('row', dh), ('row', dxo), ('const', g.reshape(1, D))],
                    [(D, F32), (D, BF16)], [(1, D)], tile)


def _ffn_fwd(tag, x, g, wg, wu, wd, tile):
    T, D = x.shape
    fs = wg.shape[2]
    F = N_CHIPS * fs
    tm, tn = _pick(T, TM_BIG), _pick(D, TN_SMALL)
    h = _norm_fwd(f"{tag}_norm", x, g, tile)
    hspec = _bs((tm, D), lambda j, i: (i, 0))
    wspec = _bs((None, D, fs), lambda j, i: (j, 0, 0))
    ospec = _bs((tm, fs), lambda j, i: (i, j))

    def gate(ins, outs):
        outs[0][...] = _dot(ins[0][...], ins[1][...], 'nn').astype(BF16)

    a = _mm1(f"{tag}_gate", (N_CHIPS, T // tm), [(h, hspec), (wg, wspec)], [(_sds((T, F), BF16), ospec)], gate)[0]

    def up(ins, outs):
        bv = _dot(ins[0][...], ins[1][...], 'nn')
        for rows in _row_chunks(tm):
            bb = bv[rows]
            outs[0][rows, :] = bb.astype(BF16)
            outs[1][rows, :] = (_silu(ins[2][rows, :].astype(F32)) * bb).astype(BF16)

    b, s = _mm1(f"{tag}_up", (N_CHIPS, T // tm), [(h, hspec), (wu, wspec), (a, ospec)],
                [(_sds((T, F), BF16), ospec)] * 2, up)

    def down(ins, outs):
        outs[0][...] = ins[2][...] + 0.5 * _dot(ins[0][...], ins[1][...].reshape(F, tn), 'nn')

    xspec = _bs((tm, tn), lambda i, j: (i, j))
    xo = _mm1(f"{tag}_down", (T // tm, D // tn),
              [(s, _bs((tm, F), lambda i, j: (i, 0))), (wd, _bs((N_CHIPS, fs, tn), lambda i, j: (0, 0, j))), (x, xspec)],
              [(_sds((T, D), F32), xspec)], down)[0]
    return xo, (x, h, a, b, s)


def _ffn_bwd(tag, dxo, dxb, saved, g, wg, wu, wd, tile, on_grads=None):
    x, h, a, b, s = saved
    T, D = x.shape
    fs = wg.shape[2]
    F = N_CHIPS * fs
    tm = _pick(T, TM_MID)
    tspec = _bs((tm, fs), lambda j, i: (i, j))

    def ds_fn(ins, outs):
        d = _dot(ins[0][...], ins[1][...], 'nt')
        for rows in _row_chunks(tm):
            ds = 0.5 * d[rows]
            av, bv = ins[2][rows, :].astype(F32), ins[3][rows, :].astype(F32)
            sig = jax.nn.sigmoid(av)
            outs[0][rows, :] = (ds * bv * (sig * (1.0 + av * (1.0 - sig)))).astype(BF16)
            outs[1][rows, :] = (ds * (av * sig)).astype(BF16)

    da, db = _mm1(f"{tag}_ds", (N_CHIPS, T // tm),
                  [(dxb, _bs((tm, D), lambda j, i: (i, 0))), (wd, _bs((None, fs, D), lambda j, i: (j, 0, 0))),
                   (a, tspec), (b, tspec)], [(_sds((T, F), BF16), tspec)] * 2, ds_fn)

    tn = _pick(D, TN_BIG)

    def dwd_fn(ins, outs):
        outs[0][...] = (0.5 * _dot(ins[0][...], ins[1][...], 'tn')).astype(BF16)

    gd = _mm1(f"{tag}_dwd", (N_CHIPS, D // tn),
              [(s, _bs((T, fs), lambda i, j: (0, i))), (dxb, _bs((T, tn), lambda i, j: (0, j)))],
              [(_sds(wd.shape, BF16), _bs((None, fs, tn), lambda i, j: (i, 0, j)))], dwd_fn)[0]

    tmd = _pick(D, TM_BIG)

    def dw_fn(ins, outs):
        outs[0][...] = _dot(ins[0][...], ins[1][...], 'tn').astype(BF16)

    def dw(name, dy):
        return _mm1(name, (N_CHIPS, D // tmd),
                    [(h, _bs((T, tmd), lambda j, i: (0, i))), (dy, _bs((T, fs), lambda j, i: (0, j)))],
                    [(_sds(wg.shape, BF16), _bs((None, tmd, fs), lambda j, i: (j, i, 0)))], dw_fn)[0]

    gg, gu = dw(f"{tag}_dwg", da), dw(f"{tag}_dwu", db)
    if on_grads is not None:
        (gg, gu, gd), (da, db) = on_grads([gg, gu, gd], (da, db))

    tmb, tnb = _pick(T, TM_BIG), _pick(D, TN_BIG)
    aspec = _bs((tmb, fs), lambda i, j, k: (i, k))
    wtspec = _bs((None, tnb, fs), lambda i, j, k: (k, j, 0))
    dh = _mm(f"{tag}_dh", (T // tmb, D // tnb, N_CHIPS), [(da, aspec), (wg, wtspec), (db, aspec), (wu, wtspec)],
             [(0, 1, 'nt', 0), (2, 3, 'nt', 0)], [(_sds((T, D), F32), _bs((tmb, tnb), lambda i, j, k: (i, j)))],
             [(tmb, tnb)], lambda accs, _: accs)[0]
    dx, dxb2, dg = _norm_bwd(f"{tag}_norm_bwd", x, g, dh, dxo, tile)
    return dx, dxb2, dg, gg, gu, gd


def _proj_rows(name, a, w, out_dtype, extras=(), epilogue=None):
    M, K = a.shape
    ks, N = w.shape[1], w.shape[2]
    tm, tn = _pick(M, TM_BIG), _pick(N, TN_MID)
    ins = [(a, _bs((tm, K), lambda i, j: (i, 0))), (w, _bs((N_CHIPS, ks, tn), lambda i, j: (0, 0, j)))]
    for e in extras:
        if e.shape[0] == 1:
            ins.append((e, _bs((1, tn), lambda i, j: (0, j))))
        else:
            ins.append((e, _bs((tm, tn), lambda i, j: (i, j))))

    def fn(refs, outs):
        acc = _dot(refs[0][...], refs[1][...].reshape(K, tn), 'nn')
        if epilogue is not None:
            acc = epilogue(acc, [r[...] for r in refs[2:]])
        outs[0][...] = acc.astype(out_dtype)

    return _mm1(name, (M // tm, N // tn), ins, [(_sds((M, N), out_dtype), _bs((tm, tn), lambda i, j: (i, j)))], fn)[0]


def _proj_rows_t(name, pairs, out_dtype):
    dy0, w0 = pairs[0]
    M, N = dy0.shape
    ks = w0.shape[1]
    tm = _pick(M, TM_BIG)
    ins = []
    for dy, w in pairs:
        ins.append((dy, _bs((tm, N), lambda i, j: (i, 0))))
        ins.append((w, _bs((None, ks, N), lambda i, j: (j, 0, 0))))

    def fn(refs, outs):
        acc = _dot(refs[0][...], refs[1][...], 'nt')
        for p in range(1, len(pairs)):
            acc = acc + _dot(refs[2 * p][...], refs[2 * p + 1][...], 'nt')
        outs[0][...] = acc.astype(out_dtype)

    return _mm1(name, (M // tm, N_CHIPS), ins,
                [(_sds((M, N_CHIPS * ks), out_dtype), _bs((tm, ks), lambda i, j: (i, j)))], fn)[0]


def _grad_rows(name, a, dys):
    T, K = a.shape
    N = dys[0].shape[1]
    ks = K // N_CHIPS
    tn = _pick(N, TN_BIG)
    ins = [(a, _bs((T, ks), lambda i, j: (0, i)))] + [(dy, _bs((T, tn), lambda i, j: (0, j))) for dy in dys]

    def fn(refs, outs):
        av = refs[0][...]
        for p in range(len(dys)):
            outs[p][...] = _dot(av, refs[1 + p][...], 'tn').astype(BF16)

    gspec = _bs((None, ks, tn), lambda i, j: (i, 0, j))
    return _mm1(name, (N_CHIPS, N // tn), ins, [(_sds((N_CHIPS, ks, N), BF16), gspec)] * len(dys), fn)


def _softmax_rows(s):
    s = s - jnp.max(s, axis=-1, keepdims=True)
    p = jnp.exp(s)
    return p / jnp.sum(p, axis=-1, keepdims=True)


def _attn_fwd_tile(hd, scale):
    def fn(i, q, k, v):
        outs = []
        for h in range(N_MEM_HEADS):
            sl = slice(h * hd, (h + 1) * hd)
            p = _softmax_rows(lax.dot_general(q[:, sl], k[:, sl], NT, preferred_element_type=F32) * scale)
            outs.append(lax.dot_general(p.astype(BF16), v[:, sl], NN, preferred_element_type=F32))
        return [jnp.concatenate(outs, axis=1)], []
    return fn


def _attn_bwd_tile(hd, scale):
    def fn(i, q, do, k, v):
        dqs, dks, dvs = [], [], []
        for h in range(N_MEM_HEADS):
            sl = slice(h * hd, (h + 1) * hd)
            qh, kh, vh, doh = q[:, sl], k[:, sl], v[:, sl], do[:, sl]
            p = _softmax_rows(lax.dot_general(qh, kh, NT, preferred_element_type=F32) * scale)
            dvs.append(lax.dot_general(p.astype(BF16), doh, TN_, preferred_element_type=F32))
            dp = lax.dot_general(doh, vh, NT, preferred_element_type=F32)
            ds = (p * (dp - jnp.sum(dp * p, axis=-1, keepdims=True)) * scale).astype(BF16)
            dqs.append(lax.dot_general(ds, kh, NN, preferred_element_type=F32))
            dks.append(lax.dot_general(ds, qh, TN_, preferred_element_type=F32))
        return [jnp.concatenate(dqs, axis=1)], [jnp.concatenate(dks, axis=1), jnp.concatenate(dvs, axis=1)]
    return fn


def _attn_fwd(l, x, mem, gq, gkv, W, tile):
    T, D = x.shape
    M = mem.shape[0]
    hd = D // N_MEM_HEADS
    hq = _norm_fwd(f"xa{l}_normq", x, gq, tile)
    mn = _norm_fwd(f"xa{l}_normkv", mem, gkv, _pick(M, tile, 16))
    q = _proj_rows(f"xa{l}_q", hq, W['xattn_wq'], BF16)
    k = _proj_rows(f"xa{l}_k", mn, W['xattn_wk'], BF16)
    v = _proj_rows(f"xa{l}_v", mn, W['xattn_wv'], BF16)
    o = _rowwise(f"xa{l}_attn", _attn_fwd_tile(hd, hd ** -0.5), [('row', q), ('const', k), ('const', v)],
                 [(D, BF16)], [], tile)[0]
    xo = _proj_rows(f"xa{l}_o", o, W['xattn_wo'], F32, extras=(x,), epilogue=lambda acc, ex: ex[0] + acc)
    return xo, (x, hq, mn, q, k, v, o)


def _attn_bwd(l, dxo, dxb, saved, mem, gq, gkv, W, G, tile):
    x, hq, mn, q, k, v, o = saved
    T, D = x.shape
    M = mem.shape[0]
    hd = D // N_MEM_HEADS
    do = _proj_rows_t(f"xa{l}_do", [(dxb, W['xattn_wo'])], BF16)
    G['xattn_wo'] = _grad_rows(f"xa{l}_dwo", o, [dxb])[0]
    dq, dk, dv = _rowwise(f"xa{l}_attn_bwd", _attn_bwd_tile(hd, hd ** -0.5),
                          [('row', q), ('row', do), ('const', k), ('const', v)], [(D, BF16)], [(M, D), (M, D)], tile)
    dhq = _proj_rows_t(f"xa{l}_dhq", [(dq, W['xattn_wq'])], F32)
    G['xattn_wq'] = _grad_rows(f"xa{l}_dwq", hq, [dq])[0]
    dmn = _proj_rows_t(f"xa{l}_dmn", [(dk, W['xattn_wk']), (dv, W['xattn_wv'])], F32)
    G['xattn_wk'], G['xattn_wv'] = _grad_rows(f"xa{l}_dwkv", mn, [dk, dv])
    dx, dxb2, dgq = _norm_bwd(f"xa{l}_normq_bwd", x, gq, dhq, dxo, tile)
    dgkv = _norm_bwd(f"xa{l}_normkv_bwd", mem, gkv, dmn, None, _pick(M, tile, 16))
    return dx, dxb2, dgq, dgkv


def _chunk_mask():
    p = lax.broadcasted_iota(jnp.int32, (GMLP_BLOCK, GMLP_BLOCK), 0)
    q = lax.broadcasted_iota(jnp.int32, (GMLP_BLOCK, GMLP_BLOCK), 1)
    return (q // CHUNK) <= (p // CHUNK)


def _spatial_fwd(vn, ws_ref, bsf, mask, hd):
    vb = vn.astype(BF16)
    wsm = [jnp.where(mask, ws_ref[h], 0.0).astype(BF16) for h in range(A_HEADS)]
    rows = []
    for n in range(vn.shape[0] // GMLP_BLOCK):
        blk = vb[n * GMLP_BLOCK:(n + 1) * GMLP_BLOCK]
        cols = [lax.dot_general(wsm[h], blk[:, h * hd:(h + 1) * hd], NN, preferred_element_type=F32)
                for h in range(A_HEADS)]
        rows.append(jnp.concatenate(cols, axis=1) + bsf)
    return jnp.concatenate(rows, axis=0)


def _spatial_bwd(dsp, vn, ws_ref, mask, hd):
    vb, db16 = vn.astype(BF16), dsp.astype(BF16)
    wsm = [jnp.where(mask, ws_ref[h], 0.0).astype(BF16) for h in range(A_HEADS)]
    dws = [jnp.zeros((GMLP_BLOCK, GMLP_BLOCK), F32) for _ in range(A_HEADS)]
    dbs = jnp.zeros((GMLP_BLOCK, vn.shape[1]), F32)
    rows = []
    for n in range(vn.shape[0] // GMLP_BLOCK):
        sl = slice(n * GMLP_BLOCK, (n + 1) * GMLP_BLOCK)
        cols = []
        for h in range(A_HEADS):
            hs = slice(h * hd, (h + 1) * hd)
            cols.append(lax.dot_general(wsm[h], db16[sl, hs], TN_, preferred_element_type=F32))
            dws[h] = dws[h] + lax.dot_general(db16[sl, hs], vb[sl, hs], NT, preferred_element_type=F32)
        rows.append(jnp.concatenate(cols, axis=1))
        dbs = dbs + dsp[sl]
    dws = [jnp.where(mask, d, 0.0) for d in dws]
    return jnp.concatenate(rows, axis=0), dws, dbs


def _conv_taps(cat, cw_ref, kw, tile):
    acc = jnp.zeros((tile, cat.shape[1]), F32)
    for k in range(kw):
        sh = kw - 1 - k
        r = cat if sh == 0 else pltpu.roll(cat, sh, 0)
        acc = acc + r[HALO:] * cw_ref[k:k + 1, :]
    return acc


def _mix_fwd_tile(A, B, kw, tile):
    hd = A // A_HEADS

    def fn(i, z, zp, ws_ref, bsf, glg, glb, cw_ref, cb, clg, clb):
        mask = _chunk_mask()
        u = _gelu(z[:, :A])
        vn = _ln(_gelu(z[:, A:2 * A]), glg, glb)
        ya = u * _spatial_fwd(vn, ws_ref, bsf, mask, hd)
        hb = _glu(z[:, 2 * A:2 * A + B], z[:, 2 * A + B:])
        hp = jnp.where(i > 0, _glu(zp[:, 2 * A:2 * A + B], zp[:, 2 * A + B:]), 0.0)
        conv = _conv_taps(jnp.concatenate([hp, hb], axis=0), cw_ref, kw, tile) + cb
        yb = _silu(_ln(conv, clg, clb))
        return [jnp.concatenate([ya, yb], axis=1)], []
    return fn


def _mix_bwd1_tile(A, B, kw, tile):
    hd = A // A_HEADS

    def fn(i, z, zp, dy, dxo, ws_ref, bsf, glg, glb, cw_ref, cb, clg, clb):
        mask = _chunk_mask()
        dya, dyb = dy[:, :A], dy[:, A:]
        zu, zv = z[:, :A], z[:, A:2 * A]
        u, vjp_u = jax.vjp(_gelu, zu)
        vn, vjp_v = jax.vjp(lambda t, g, b: _ln(_gelu(t), g, b), zv, glg, glb)
        sp = _spatial_fwd(vn, ws_ref, bsf, mask, hd)
        dzu = vjp_u(dya * sp)[0]
        dvn, dws, dbs = _spatial_bwd(dya * u, vn, ws_ref, mask, hd)
        dzv, dglg, dglb = vjp_v(dvn)
        hb = _glu(z[:, 2 * A:2 * A + B], z[:, 2 * A + B:])
        hp = jnp.where(i > 0, _glu(zp[:, 2 * A:2 * A + B], zp[:, 2 * A + B:]), 0.0)
        cat = jnp.concatenate([hp, hb], axis=0)
        conv = _conv_taps(cat, cw_ref, kw, tile) + cb
        _, vjp_c = jax.vjp(lambda t, g, b: _silu(_ln(t, g, b)), conv, clg, clb)
        dconv, dclg, dclb = vjp_c(dyb)
        tap = lax.broadcasted_iota(jnp.int32, (HALO, 1), 0)
        dcw = jnp.zeros((HALO, B), F32)
        for k in range(kw):
            sh = kw - 1 - k
            r = cat if sh == 0 else pltpu.roll(cat, sh, 0)
            dcw = dcw + jnp.where(tap == k, jnp.sum(dconv * r[HALO:], axis=0, keepdims=True), 0.0)
        dcb = jnp.sum(dconv, axis=0, keepdims=True)
        dbo = jnp.sum(dxo, axis=0, keepdims=True)
        dws = jnp.concatenate([d[None] for d in dws], axis=0)
        return [jnp.concatenate([dzu, dzv], axis=1), dconv], [dws, dbs, dglg, dglb, dcw, dcb, dclg, dclb, dbo]
    return fn


def _mix_bwd2_tile(A, B, kw, tile, n_tiles):
    def fn(i, z, dza, dc, dcn, cw_ref):
        dcn = jnp.where(i < n_tiles - 1, dcn, 0.0)
        cat = jnp.concatenate([dc, dcn], axis=0)
        n = tile + HALO
        dhb = jnp.zeros((tile, B), F32)
        for k in range(kw):
            sh = kw - 1 - k
            r = cat if sh == 0 else pltpu.roll(cat, n - sh, 0)
            dhb = dhb + r[:tile] * cw_ref[k:k + 1, :]
        _, vjp_g = jax.vjp(_glu, z[:, 2 * A:2 * A + B], z[:, 2 * A + B:])
        da, dg = vjp_g(dhb)
        dz = jnp.concatenate([dza, da, dg], axis=1)
        return [dz], [jnp.sum(dz, axis=0, keepdims=True)]
    return fn


def _even_consts(p, e, A, B, kw):
    hd = A // A_HEADS
    bsf = jnp.repeat(p['gmlp_b_s'][e].T, hd, axis=1)
    cw = jnp.pad(p['conv_w_full'][e], ((0, HALO - kw), (0, 0)))
    return [('cref', p['gmlp_w_s'][e]), ('const', bsf), ('const', p['gmlp_ln_g'][e].reshape(1, A)),
            ('const', p['gmlp_ln_b'][e].reshape(1, A)), ('cref', cw), ('const', p['conv_b'][e].reshape(1, B)),
            ('const', p['conv_ln_g'][e].reshape(1, B)), ('const', p['conv_ln_b'][e].reshape(1, B))]


def _even_fwd(l, e, x, gm, p, W, tile):
    T, D = x.shape
    w_in, w_out = W['ab_w_in'], W['ab_w_out']
    zs = w_in.shape[2]
    Z = N_CHIPS * zs
    A = p['gmlp_ln_g'].shape[1]
    B = p['conv_b'].shape[1]
    kw = p['conv_w_full'].shape[1]
    tm = _pick(T, TM_BIG)
    h = _norm_fwd(f"mix{l}_norm", x, gm, tile)

    def in_fn(refs, outs):
        outs[0][...] = _dot(refs[0][...], refs[1][...], 'nn') + refs[2][...]

    z = _mm1(f"mix{l}_in", (N_CHIPS, T // tm),
             [(h, _bs((tm, D), lambda j, i: (i, 0))), (w_in, _bs((None, D, zs), lambda j, i: (j, 0, 0))),
              (p['ab_b_in'][e].reshape(1, Z), _bs((1, zs), lambda j, i: (0, j)))],
             [(_sds((T, Z), F32), _bs((tm, zs), lambda j, i: (i, j)))], in_fn)[0]
    consts = _even_consts(p, e, A, B, kw)
    ycat = _rowwise(f"mix{l}_mid", _mix_fwd_tile(A, B, kw, tile), [('row', z), ('prev', z)] + consts,
                    [(A + B, BF16)], [], tile)[0]
    xo = _proj_rows(f"mix{l}_out", ycat, w_out, F32, extras=(x, p['ab_b_out'][e].reshape(1, D)),
                    epilogue=lambda acc, ex: ex[0] + acc + ex[1])
    return xo, (x, h, z, ycat)


def _even_bwd(l, e, dxo, dxb, saved, gm, p, W, G, tile):
    x, h, z, ycat = saved
    T, D = x.shape
    w_in, w_out = W['ab_w_in'], W['ab_w_out']
    zs = w_in.shape[2]
    Z = N_CHIPS * zs
    A = p['gmlp_ln_g'].shape[1]
    B = p['conv_b'].shape[1]
    kw = p['conv_w_full'].shape[1]
    hd = A // A_HEADS
    dycat = _proj_rows_t(f"mix{l}_dycat", [(dxb, w_out)], F32)
    G['ab_w_out'] = _grad_rows(f"mix{l}_dwout", ycat, [dxb])[0]
    consts = _even_consts(p, e, A, B, kw)
    accs = [(A_HEADS, GMLP_BLOCK, GMLP_BLOCK), (GMLP_BLOCK, A), (1, A), (1, A), (HALO, B), (1, B), (1, B), (1, B),
            (1, D)]
    dza, dconv, dws, dbs, dglg, dglb, dcw, dcb, dclg, dclb, dbo = _rowwise(
        f"mix{l}_mid_bwd1", _mix_bwd1_tile(A, B, kw, tile),
        [('row', z), ('prev', z), ('row', dycat), ('row', dxo)] + consts, [(2 * A, F32), (B, F32)], accs, tile)
    dz, dbin = _rowwise(f"mix{l}_mid_bwd2", _mix_bwd2_tile(A, B, kw, tile, T // tile),
                        [('row', z), ('row', dza), ('row', dconv), ('next', dconv), consts[4]],
                        [(Z, BF16)], [(1, Z)], tile)
    tmd = _pick(D, TM_BIG)

    def dwin_fn(refs, outs):
        outs[0][...] = _dot(refs[0][...], refs[1][...], 'tn').astype(BF16)

    G['ab_w_in'] = _mm1(f"mix{l}_dwin", (N_CHIPS, D // tmd),
                        [(h, _bs((T, tmd), lambda j, i: (0, i))), (dz, _bs((T, zs), lambda j, i: (0, j)))],
                        [(_sds(w_in.shape, BF16), _bs((None, tmd, zs), lambda j, i: (j, i, 0)))], dwin_fn)[0]
    tm, tn = _pick(T, TM_MID), _pick(D, TN_SMALL)

    def dh_fn(refs, outs):
        acc = None
        for j in range(N_CHIPS):
            t = _dot(refs[0][:, j * zs:(j + 1) * zs], refs[1][j], 'nt')
            acc = t if acc is None else acc + t
        outs[0][...] = acc

    dh = _mm1(f"mix{l}_dh", (T // tm, D // tn),
              [(dz, _bs((tm, Z), lambda i, j: (i, 0))), (w_in, _bs((N_CHIPS, tn, zs), lambda i, j: (0, j, 0)))],
              [(_sds((T, D), F32), _bs((tm, tn), lambda i, j: (i, j)))], dh_fn)[0]
    dx, dxb2, dgm = _norm_bwd(f"mix{l}_norm_bwd", x, gm, dh, dxo, tile)
    small = {'ab_b_in': dbin.reshape(Z), 'gmlp_w_s': dws, 'gmlp_b_s': dbs.reshape(GMLP_BLOCK, A_HEADS, hd).sum(-1).T,
             'gmlp_ln_g': dglg.reshape(A), 'gmlp_ln_b': dglb.reshape(A), 'conv_w': dcw[:kw], 'conv_b': dcb.reshape(B),
             'conv_ln_g': dclg.reshape(B), 'conv_ln_b': dclb.reshape(B), 'ab_b_out': dbo.reshape(D)}
    return dx, dxb2, dgm, small


def _pool_counts(t, cg):
    return jnp.concatenate([jnp.broadcast_to(jnp.minimum(t + 1, w).astype(F32), (t.shape[0], cg))
                            for w in POOL_WINDOWS], axis=1)


def _window_sums(cat, cg, back):
    n = cat.shape[0]
    outs = []
    for gi, w in enumerate(POOL_WINDOWS):
        s = cat[:, gi * cg:(gi + 1) * cg]
        step = 1
        while step < w:
            s = s + pltpu.roll(s, step if back else n - step, 0)
            step *= 2
        outs.append(s)
    return jnp.concatenate(outs, axis=1)


def _pool_fwd_tile(D, tile):
    cg = D // len(POOL_WINDOWS)

    def fn(i, x, xp, g):
        h = _rms(x, g)
        hp = jnp.where(i > 0, _rms(xp, g), 0.0)
        sums = _window_sums(jnp.concatenate([hp, h], axis=0), cg, True)[HALO:]
        return [sums / _pool_counts(_row_ids(i, tile, tile), cg) - h], []
    return fn


def _pool_bwd_tile(D, tile, n_tiles):
    cg = D // len(POOL_WINDOWS)

    def fn(i, dd, ddn, x, dxo, g):
        e = dd / _pool_counts(_row_ids(i, tile, tile), cg)
        en = jnp.where(i < n_tiles - 1, ddn / _pool_counts(_row_ids(i + 1, tile, HALO), cg), 0.0)
        dh = _window_sums(jnp.concatenate([e, en], axis=0), cg, False)[:tile] - dd
        _, vjp = jax.vjp(_rms, x, g)
        dx, dg = vjp(dh)
        return [dxo + dx, dxo + dx], [dg]
    return fn


def _odd_fwd(l, o, x, gm, p, W, tile):
    T, D = x.shape
    wc = W['pool_w']
    cg = wc.shape[2]
    cs = cg // N_CHIPS
    ng = len(POOL_WINDOWS)
    tm = _pick(T, TM)
    d = _rowwise(f"mix{l}_pool", _pool_fwd_tile(D, tile), [('row', x), ('prev', x), ('const', gm.reshape(1, D))],
                 [(D, BF16)], [], tile)[0]
    gspec = _bs((tm, cg), lambda i, j, k: (i, j))
    vspec = _bs((1, cg), lambda i, j, k: (0, j))

    def epi(accs, ex):
        pre = accs[0] + ex[0]
        return [ex[2] + pre * ex[1], pre]

    xo, pre = _mm(f"mix{l}_poolmm", (T // tm, ng, N_CHIPS),
                  [(d, _bs((tm, cs), lambda i, j, k: (i, j * N_CHIPS + k))),
                   (wc, _bs((None, cs, cg), lambda i, j, k: (k, j, 0))),
                   (p['pool_b_full'][o].reshape(1, D), vspec), (p['pool_scale_full'][o].reshape(1, D), vspec),
                   (x, gspec)],
                  [(0, 1, 'nn', 0)], [(_sds((T, D), F32), gspec)] * 2, [(tm, cg)], epi, extras=(2, 3, 4))
    return xo, (x, d, pre)


def _odd_bwd(l, o, dxo, dxb, saved, gm, p, W, G, tile):
    x, d, pre = saved
    T, D = x.shape
    wc = W['pool_w']
    cg = wc.shape[2]
    cs = cg // N_CHIPS
    ng = len(POOL_WINDOWS)
    tm, tkt = _pick(T, TM), _pick(T, TK)

    def fn(i, dxv, prev, sc):
        return [dxv * sc], [jnp.sum(dxv * prev, axis=0, keepdims=True), jnp.sum(dxv * sc, axis=0, keepdims=True)]

    do, dscale, dbc = _rowwise(f"mix{l}_pool_bwd1", fn,
                               [('row', dxo), ('row', pre), ('const', p['pool_scale_full'][o].reshape(1, D))],
                               [(D, BF16)], [(1, D), (1, D)], tile)
    nb = ng * N_CHIPS
    dd = _mm(f"mix{l}_pool_dd", (T // tm, nb, 1),
             [(do, _bs((tm, cg), lambda i, j, k: (i, j // N_CHIPS))),
              (wc, _bs((None, cs, cg), lambda i, j, k: (j % N_CHIPS, j // N_CHIPS, 0)))],
             [(0, 1, 'nt', 0)], [(_sds((T, D), F32), _bs((tm, cs), lambda i, j, k: (i, j)))], [(tm, cs)],
             lambda a, _: a)[0]
    G['pool_w'] = _mm(f"mix{l}_pool_dw", (nb, 1, T // tkt),
                      [(d, _bs((tkt, cs), lambda i, j, k: (k, i))), (do, _bs((tkt, cg), lambda i, j, k: (k, i // N_CHIPS)))],
                      [(0, 1, 'tn', 0)],
                      [(_sds(wc.shape, BF16), _bs((None, cs, cg), lambda i, j, k: (i % N_CHIPS, i // N_CHIPS, 0)))],
                      [(cs, cg)], lambda a, _: a)[0]
    dx, dxb2, dgm = _rowwise(f"mix{l}_pool_bwd2", _pool_bwd_tile(D, tile, T // tile),
                             [('row', dd), ('next', dd), ('row', x), ('row', dxo), ('const', gm.reshape(1, D))],
                             [(D, F32), (D, BF16)], [(1, D)], tile)
    small = {'pool_b': dbc.reshape(ng, cg), 'pool_scale': dscale.reshape(D)}
    return dx, dxb2, dgm, small


def _final(x, g, tgt, tile):
    T, D = x.shape

    def fn(i, xv, tv, gv):
        y, vjp = jax.vjp(_rms, xv, gv)
        err = y - tv
        dx, dg = vjp(err / D)
        loss = 0.5 * jnp.sum(jnp.mean(err * err, axis=-1, keepdims=True), axis=0, keepdims=True)
        return [dx, dx], [dg, jnp.broadcast_to(loss, (1, LANES))]

    dx, dxb, dg, loss = _rowwise("final", fn, [('row', x), ('row', tgt), ('const', g.reshape(1, D))],
                                 [(D, F32), (D, BF16)], [(1, D), (1, LANES)], tile)
    return dx, dxb, dg.reshape(D), loss[0, 0]


def _as3d(name, w):
    return w.reshape(w.shape[0], -1, w.shape[-1]) if name == 'pool_w' else w


def kernel(x, mem, norm_ffn1, ffn1_gate, ffn1_up, ffn1_down, norm_mix, ab_w_in, ab_b_in, gmlp_w_s, gmlp_b_s, gmlp_ln_g, gmlp_ln_b, conv_w, conv_b, conv_ln_g, conv_ln_b, ab_w_out, ab_b_out, pool_w, pool_b, pool_scale, norm_xq, norm_xkv, xattn_wq, xattn_wk, xattn_wv, xattn_wo, norm_ffn2, ffn2_gate, ffn2_up, ffn2_down, norm_final, loss_target, m_norm_ffn1, m_ffn1_gate, m_ffn1_up, m_ffn1_down, m_norm_mix, m_ab_w_in, m_ab_b_in, m_gmlp_w_s, m_gmlp_b_s, m_gmlp_ln_g, m_gmlp_ln_b, m_conv_w, m_conv_b, m_conv_ln_g, m_conv_ln_b, m_ab_w_out, m_ab_b_out, m_pool_w, m_pool_b, m_pool_scale, m_norm_xq, m_norm_xkv, m_xattn_wq, m_xattn_wk, m_xattn_wv, m_xattn_wo, m_norm_ffn2, m_ffn2_gate, m_ffn2_up, m_ffn2_down, m_norm_final, v_norm_ffn1, v_ffn1_gate, v_ffn1_up, v_ffn1_down, v_norm_mix, v_ab_w_in, v_ab_b_in, v_gmlp_w_s, v_gmlp_b_s, v_gmlp_ln_g, v_gmlp_ln_b, v_conv_w, v_conv_b, v_conv_ln_g, v_conv_ln_b, v_ab_w_out, v_ab_b_out, v_pool_w, v_pool_b, v_pool_scale, v_norm_xq, v_norm_xkv, v_xattn_wq, v_xattn_wk, v_xattn_wv, v_xattn_wo, v_norm_ffn2, v_ffn2_gate, v_ffn2_up, v_ffn2_down, v_norm_final):
    w = dict(zip(WEIGHTS, [norm_ffn1, ffn1_gate, ffn1_up, ffn1_down, norm_mix, ab_w_in, ab_b_in, gmlp_w_s, gmlp_b_s, gmlp_ln_g, gmlp_ln_b, conv_w, conv_b, conv_ln_g, conv_ln_b, ab_w_out, ab_b_out, pool_w, pool_b, pool_scale, norm_xq, norm_xkv, xattn_wq, xattn_wk, xattn_wv, xattn_wo, norm_ffn2, ffn2_gate, ffn2_up, ffn2_down, norm_final]))
    m = dict(zip(WEIGHTS, [m_norm_ffn1, m_ffn1_gate, m_ffn1_up, m_ffn1_down, m_norm_mix, m_ab_w_in, m_ab_b_in, m_gmlp_w_s, m_gmlp_b_s, m_gmlp_ln_g, m_gmlp_ln_b, m_conv_w, m_conv_b, m_conv_ln_g, m_conv_ln_b, m_ab_w_out, m_ab_b_out, m_pool_w, m_pool_b, m_pool_scale, m_norm_xq, m_norm_xkv, m_xattn_wq, m_xattn_wk, m_xattn_wv, m_xattn_wo, m_norm_ffn2, m_ffn2_gate, m_ffn2_up, m_ffn2_down, m_norm_final]))
    v = dict(zip(WEIGHTS, [v_norm_ffn1, v_ffn1_gate, v_ffn1_up, v_ffn1_down, v_norm_mix, v_ab_w_in, v_ab_b_in, v_gmlp_w_s, v_gmlp_b_s, v_gmlp_ln_g, v_gmlp_ln_b, v_conv_w, v_conv_b, v_conv_ln_g, v_conv_ln_b, v_ab_w_out, v_ab_b_out, v_pool_w, v_pool_b, v_pool_scale, v_norm_xq, v_norm_xkv, v_xattn_wq, v_xattn_wk, v_xattn_wv, v_xattn_wo, v_norm_ffn2, v_ffn2_gate, v_ffn2_up, v_ffn2_down, v_norm_final]))

    xs, mems, tgt = x[0], mem[0], loss_target[0]
    T, D = xs.shape
    L = norm_ffn1.shape[0]
    tile = _pick(T, ROW_TILE)
    cx, cy, cc = _mesh_pos()
    chip = 2 * cx + cy
    w3 = {n: _as3d(n, w[n]) for n in BIG}
    names = [_layer_names(l) for l in range(L)]

    sh_shapes = [w[n].shape for n in SMALL_SHARDED]
    slots = _gather_all("gather_small_shards", _pack([w[n] for n in SMALL_SHARDED]), jnp.zeros((8, LANES), F32))
    per_chip = [_unpack(slots[2 * j], sh_shapes) for j in range(N_CHIPS)]
    full = {n: jnp.concatenate([per_chip[j][k] for j in range(N_CHIPS)], axis=-1) for k, n in enumerate(SMALL_SHARDED)}
    p = dict(w)
    p['conv_w_full'] = full['conv_w'].reshape(full['conv_w'].shape[0], full['conv_w'].shape[1], -1)
    p['pool_b_full'] = full['pool_b']
    p['pool_scale_full'] = full['pool_scale']

    near, far = {}, {}

    def start_near(l, after, carry=()):
        slabs = [_cast_slab(f"cast_{n}_{l}", w3[n], _stack_index(n, l), chip) for n in names[l]]
        sends, arrivals, slabs, carry, tok = _split_start(f"gather_near_start_{l}", slabs, after, 2 * len(slabs),
                                                          _gather_near_copies, carry)
        near[l] = (sends, arrivals, slabs)
        return carry, tok

    def start_far(l, after, carry=()):
        sends, arrivals, slabs = near.pop(l)
        slabs = _split_wait(f"gather_near_wait_{l}", slabs, sends, arrivals, after, _gather_near_copies)
        sends, arrivals, slabs, carry, tok = _split_start(f"gather_far_start_{l}", slabs, jnp.zeros((8, LANES), F32),
                                                          2 * len(slabs), _gather_far_copies, carry)
        far[l] = (sends, arrivals, slabs)
        return carry, tok

    def finish_gather(l, after):
        sends, arrivals, slabs = far.pop(l)
        slabs = _split_wait(f"gather_far_wait_{l}", slabs, sends, arrivals, after, _gather_far_copies)
        return dict(zip(names[l], _forward_halves(f"gather_fwd_{l}", slabs)))

    _, tok = start_near(0, slots)
    _, tok = start_far(0, tok)
    if L > 1:
        _, tok = start_near(1, tok)

    saved, Wl = [], []
    xc = xs + tok[0, 0]
    W = finish_gather(0, xc)
    for l in range(L):
        Wl.append(W)
        s = {}
        xc, s['ffn1'] = _ffn_fwd(f"ffn1_{l}", xc, w['norm_ffn1'][l], W['ffn1_gate'], W['ffn1_up'], W['ffn1_down'], tile)
        if l % 2 == 0:
            xc, s['mix'] = _even_fwd(l, l // 2, xc, w['norm_mix'][l], p, W, tile)
        else:
            xc, s['mix'] = _odd_fwd(l, l // 2, xc, w['norm_mix'][l], p, W, tile)
        if l + 1 < L:
            (xc,), tok = start_far(l + 1, xc, (xc,))
            if l + 2 < L:
                (xc,), tok = start_near(l + 2, tok, (xc,))
        xc, s['xa'] = _attn_fwd(l, xc, mems, w['norm_xq'][l], w['norm_xkv'][l], W, tile)
        xc, s['ffn2'] = _ffn_fwd(f"ffn2_{l}", xc, w['norm_ffn2'][l], W['ffn2_gate'], W['ffn2_up'], W['ffn2_down'], tile)
        saved.append(s)
        if l + 1 < L:
            W = finish_gather(l + 1, xc)

    dx, dxb, g_final, loss_local = _final(xc, w['norm_final'], tgt, tile)
    loss = lax.psum(loss_local, ("x", "y", "c"))
    gfull = {n: lax.empty(w3[n].shape, F32) for n in BIG}
    gs = {n: [None] * w[n].shape[0] for n in SMALL if n != 'norm_final'}

    def finish_exchange(pending, after):
        tag, l, ns, sends, arrivals, thru = pending
        thru = _split_wait(f"rs_wait_{tag}", thru, sends, arrivals, after, _exchange_copies)
        parts, lands = thru[:len(ns)], thru[len(ns):]
        for n, part, land in zip(ns, parts, lands):
            gfull[n] = _sum_into(f"rs_sum_{n}_{l}", part, land, gfull[n], _stack_index(n, l), chip, cc)
        idx = [_stack_index(n, l) for n in ns]
        sends, arrivals, thru, _, _ = _split_start(f"rs_join_start_{tag}", [gfull[n] for n in ns],
                                                   jnp.zeros((8, LANES), F32), len(ns), _join_copies(idx))
        gfull.update(zip(ns, thru))
        joins.append((tag, ns, idx, sends, arrivals))

    def wait_joins(after):
        while joins:
            tag, ns, idx, sends, arrivals = joins.pop(0)
            thru = _split_wait(f"rs_join_wait_{tag}", [gfull[n] for n in ns], sends, arrivals, after, _join_copies(idx))
            gfull.update(zip(ns, thru))

    def swap_hook(key):
        def hook(grads, carry):
            lands = [lax.empty((g.shape[0], g.shape[1] // 2, g.shape[2]), BF16) for g in grads]
            sends, arrivals, thru, carry, _ = _split_start(f"rs_swap_start_{key}", list(grads) + lands,
                                                           jnp.zeros((8, LANES), F32), len(grads), _swap_copies, carry)
            swaps[key] = (sends, arrivals, thru[len(grads):])
            return thru[:len(grads)], carry
        return hook

    def start_exchange(tag, l, ns, G, pending, dx, dxb):
        if len(pending) >= EXCHANGES_IN_FLIGHT:
            finish_exchange(pending.pop(0), dx)
        ffn = [n for n in ns if n.startswith('ffn')]
        rest = [n for n in ns if n not in ffn]
        key = f"{ffn[0][:4]}_{l}"
        sends, arrivals, lands = swaps.pop(key)
        thru = _split_wait(f"rs_swap_wait_{key}", [G[n] for n in ffn] + lands, sends, arrivals, dx, _swap_copies)
        got = dict(zip(ffn, thru[len(ffn):]))
        G.update(zip(ffn, thru[:len(ffn)]))
        got.update(zip(rest, _swap_halves(f"rs_swap_{tag}", [G[n] for n in rest])))
        grads_g = [G[n] for n in ns]
        parts = [_add_halves(f"rs_add_{n}_{l}", G[n], got[n], cc) for n in ns]
        lands = [lax.empty((3,) + part.shape[1:], BF16) for part in parts]
        sends, arrivals, thru, (dx, dxb), tok = _split_start(
            f"rs_start_{tag}", parts + lands, jnp.zeros((8, LANES), F32), 3 * len(parts), _exchange_copies,
            carry=(dx, dxb))
        pending.append((tag, l, ns, sends, arrivals, thru))
        return dx, dxb, tok

    pending, joins, swaps = [], [], {}
    for l in reversed(range(L)):
        first = [n for n in names[l] if n.startswith(('ffn2', 'xattn'))]
        second = [n for n in names[l] if n not in first]
        s, W, G = saved[l], Wl[l], {}
        dx, dxb, dg, G['ffn2_gate'], G['ffn2_up'], G['ffn2_down'] = _ffn_bwd(
            f"ffn2_{l}", dx, dxb, s['ffn2'], w['norm_ffn2'][l], W['ffn2_gate'], W['ffn2_up'], W['ffn2_down'], tile,
            swap_hook(f"ffn2_{l}"))
        gs['norm_ffn2'][l] = dg.reshape(D)
        dx, dxb, dgq, dgkv = _attn_bwd(l, dx, dxb, s['xa'], mems, w['norm_xq'][l], w['norm_xkv'][l], W, G, tile)
        gs['norm_xq'][l], gs['norm_xkv'][l] = dgq.reshape(D), dgkv.reshape(D)
        dx, dxb, _ = start_exchange(f"a{l}", l, first, G, pending, dx, dxb)
        if l % 2 == 0:
            dx, dxb, dgm, small = _even_bwd(l, l // 2, dx, dxb, s['mix'], w['norm_mix'][l], p, W, G, tile)
        else:
            dx, dxb, dgm, small = _odd_bwd(l, l // 2, dx, dxb, s['mix'], w['norm_mix'][l], p, W, G, tile)
        for n, val in small.items():
            gs[n][l // 2] = val
        gs['norm_mix'][l] = dgm.reshape(D)
        dx, dxb, dg, G['ffn1_gate'], G['ffn1_up'], G['ffn1_down'] = _ffn_bwd(
            f"ffn1_{l}", dx, dxb, s['ffn1'], w['norm_ffn1'][l], W['ffn1_gate'], W['ffn1_up'], W['ffn1_down'], tile,
            swap_hook(f"ffn1_{l}"))
        gs['norm_ffn1'][l] = dg.reshape(D)
        dx, dxb, tok = start_exchange(f"b{l}", l, second, G, pending, dx, dxb)
    grad_x = dx[None]
    wait_joins(tok)

    def flat2(n, t):
        t3 = _as3d(n, t)
        return t3.reshape(-1, t3.shape[-1])

    early, dep = {}, []
    for n in BIG:
        R = w3[n].shape[1]
        lo = 0 if n in ODD_ONLY else R
        early[n] = _adam_rows(f"adam_early_{n}", flat2(n, w[n]), flat2(n, gfull[n]), flat2(n, m[n]), flat2(n, v[n]),
                              lo, w3[n].shape[0] * R, R, after=tok)
        dep.append(early[n][1][-1, 0])
    dep = jnp.stack(dep)

    small_full = {n: jnp.stack(gs[n]) for n in gs}
    small_full['norm_final'] = g_final
    full_shapes = [small_full[n].shape for n in SMALL]
    summed = _sum_slots("sum_small", _gather_all("gather_small_grads", _pack([small_full[n] for n in SMALL]), dep))
    g_small = dict(zip(SMALL, _unpack(summed, full_shapes)))
    for n in SMALL_SHARDED:
        width = w[n].shape[-1]
        g_small[n] = lax.dynamic_slice_in_dim(g_small[n], chip * width, width, axis=g_small[n].ndim - 1).reshape(w[n].shape)

    for group in pending:
        finish_exchange(group, summed)
    wait_joins(summed)
    grads, delta, new_m, new_v = {}, {}, {}, {}
    for n in BIG:
        outs = early[n]
        if n not in ODD_ONLY:
            R = w3[n].shape[1]
            outs = _adam_rows(f"adam_late_{n}", flat2(n, w[n]), flat2(n, gfull[n]), flat2(n, m[n]), flat2(n, v[n]),
                              0, R, R, prev=early[n])
        grads[n], delta[n], new_m[n], new_v[n] = (t.reshape(w[n].shape) for t in outs)
    small_shapes = [w[n].shape for n in SMALL]
    d2, m2, v2 = _adam("adam_small", _pack([w[n] for n in SMALL]), _pack([g_small[n] for n in SMALL]),
                       _pack([m[n] for n in SMALL]), _pack([v[n] for n in SMALL]))
    for n, dn, mn_, vn_ in zip(SMALL, _unpack(d2, small_shapes), _unpack(m2, small_shapes), _unpack(v2, small_shapes)):
        grads[n], delta[n], new_m[n], new_v[n] = g_small[n].reshape(w[n].shape), dn, mn_, vn_

    return (loss, grad_x, *[grads[n] for n in WEIGHTS], *[delta[n] for n in WEIGHTS],
            *[new_m[n] for n in WEIGHTS], *[new_v[n] for n in WEIGHTS])
```

```python
import jax
import jax.numpy as jnp
from jax import lax
from jax.experimental import pallas as pl
from jax.experimental.pallas import tpu as pltpu

F32, BF16 = jnp.float32, jnp.bfloat16
EPS = 1e-6
N_MEM_HEADS = 4
A_HEADS = 8
GMLP_BLOCK = 128
CHUNK = 64
POOL_WINDOWS = (2, 4, 8, 16)
N_CHIPS = 4
N_DEV = 8
HALO = 32
LANES = 128
TM, TN, TK = 512, 1024, 512
TM_BIG, TM_MID = 1024, 512
TN_BIG, TN_MID, TN_SMALL = 1024, 512, 256
EPI_ROWS = 256
EXCHANGES_IN_FLIGHT = 2
ROW_TILE = 256
PACK_ROWS = 512
VMEM_LIMIT = 48 * 1024 * 1024
ADAM_LR, ADAM_B1, ADAM_B2, ADAM_EPS, ADAM_WD, ADAM_STEP = 0.001, 0.9, 0.999, 1e-08, 0.01, 10
MESH = pl.DeviceIdType.MESH
HBM = pl.BlockSpec(memory_space=pltpu.HBM)
SEM = pl.BlockSpec(memory_space=pltpu.SEMAPHORE)
ANY = pl.BlockSpec(memory_space=pl.ANY)
EFFECT = pltpu.SideEffectType.DATAFLOW_SIDE_EFFECTING

WEIGHTS = ['norm_ffn1', 'ffn1_gate', 'ffn1_up', 'ffn1_down', 'norm_mix', 'ab_w_in', 'ab_b_in', 'gmlp_w_s',
           'gmlp_b_s', 'gmlp_ln_g', 'gmlp_ln_b', 'conv_w', 'conv_b', 'conv_ln_g', 'conv_ln_b', 'ab_w_out',
           'ab_b_out', 'pool_w', 'pool_b', 'pool_scale', 'norm_xq', 'norm_xkv', 'xattn_wq', 'xattn_wk',
           'xattn_wv', 'xattn_wo', 'norm_ffn2', 'ffn2_gate', 'ffn2_up', 'ffn2_down', 'norm_final']
BIG = ['ffn1_gate', 'ffn1_up', 'ffn1_down', 'ab_w_in', 'ab_w_out', 'pool_w', 'xattn_wq', 'xattn_wk', 'xattn_wv',
       'xattn_wo', 'ffn2_gate', 'ffn2_up', 'ffn2_down']
EVEN_ONLY, ODD_ONLY = ['ab_w_in', 'ab_w_out'], ['pool_w']
SMALL = [n for n in WEIGHTS if n not in BIG]
SMALL_SHARDED = ['conv_w', 'pool_b', 'pool_scale']

NN = (((1,), (0,)), ((), ()))
NT = (((1,), (1,)), ((), ()))
TN_ = (((0,), (0,)), ((), ()))
_DIMS = {'nn': NN, 'nt': NT, 'tn': TN_}


def _pick(n, pref, unit=LANES):
    if n <= pref:
        return n
    t = (pref // unit) * unit
    while t >= unit:
        if n % t == 0:
            return t
        t -= unit
    return n


def _sds(shape, dtype):
    return jax.ShapeDtypeStruct(tuple(shape), dtype)


def _layer_names(l):
    mix = EVEN_ONLY if l % 2 == 0 else ODD_ONLY
    return ['ffn1_gate', 'ffn1_up', 'ffn1_down'] + mix + ['xattn_wq', 'xattn_wk', 'xattn_wv', 'xattn_wo',
                                                          'ffn2_gate', 'ffn2_up', 'ffn2_down']


def _stack_index(name, l):
    return l // 2 if name in EVEN_ONLY + ODD_ONLY else l


def _mm(name, grid, ins, pairs, outs, acc_shapes, epilogue, extras=()):
    n_in, n_out = len(ins), len(outs)
    nk = grid[2]

    def body(*refs):
        in_refs, out_refs, acc_refs = refs[:n_in], refs[n_in:n_in + n_out], refs[n_in + n_out:]
        k = pl.program_id(2)

        @pl.when(k == 0)
        def _():
            for acc in acc_refs:
                acc[...] = jnp.zeros_like(acc)

        for ai, bi, mode, ci in pairs:
            a = in_refs[ai][...].astype(BF16)
            b = in_refs[bi][...].astype(BF16)
            acc_refs[ci][...] += lax.dot_general(a, b, _DIMS[mode], preferred_element_type=F32)

        @pl.when(k == nk - 1)
        def _():
            res = epilogue([acc[...] for acc in acc_refs], [in_refs[e][...] for e in extras])
            for o, r in zip(out_refs, res):
                o[...] = r.astype(o.dtype)

    return pl.pallas_call(
        body, name=name, grid=grid,
        in_specs=[s for _, s in ins], out_specs=[s for _, s in outs], out_shape=[s for s, _ in outs],
        scratch_shapes=[pltpu.VMEM(s, F32) for s in acc_shapes],
        compiler_params=pltpu.CompilerParams(dimension_semantics=("parallel", "parallel", "arbitrary"),
                                             vmem_limit_bytes=VMEM_LIMIT),
    )(*[a for a, _ in ins])


def _bs(shape, fn):
    return pl.BlockSpec(shape, fn)


def _mm1(name, grid, ins, outs, compute):
    n_in = len(ins)

    def body(*refs):
        compute(refs[:n_in], refs[n_in:])

    return pl.pallas_call(
        body, name=name, grid=grid,
        in_specs=[s for _, s in ins], out_specs=[s for _, s in outs], out_shape=[s for s, _ in outs],
        compiler_params=pltpu.CompilerParams(dimension_semantics=("parallel", "parallel"),
                                             vmem_limit_bytes=VMEM_LIMIT),
    )(*[a for a, _ in ins])


def _dot(a, b, mode):
    return lax.dot_general(a.astype(BF16), b.astype(BF16), _DIMS[mode], preferred_element_type=F32)


def _row_chunks(rows):
    step = min(rows, EPI_ROWS)
    return [slice(r, r + step) for r in range(0, rows, step)]


def _rowwise(name, fn, ins, row_outs, acc_outs, tile):
    T = next(a.shape[0] for k, a in ins if k == 'row')
    n = T // tile
    per = tile // HALO if tile % HALO == 0 else 1
    last = T // HALO - 1
    in_specs = []
    for kind, a in ins:
        if kind == 'row':
            in_specs.append(pl.BlockSpec((tile, a.shape[1]), lambda i: (i, 0)))
        elif kind == 'prev':
            in_specs.append(pl.BlockSpec((HALO, a.shape[1]), lambda i: (jnp.maximum(i * per - 1, 0), 0)))
        elif kind == 'next':
            in_specs.append(pl.BlockSpec((HALO, a.shape[1]), lambda i: (jnp.minimum((i + 1) * per, last), 0)))
        else:
            in_specs.append(pl.BlockSpec(a.shape, lambda i, nd=a.ndim: (0,) * nd))
    n_in, n_row = len(ins), len(row_outs)
    out_shape = [_sds((T, c), dt) for c, dt in row_outs] + [_sds(s, F32) for s in acc_outs]
    out_specs = [pl.BlockSpec((tile, c), lambda i: (i, 0)) for c, _ in row_outs]
    out_specs += [pl.BlockSpec(s, lambda i, nd=len(s): (0,) * nd) for s in acc_outs]
    kinds = [k for k, _ in ins]

    def body(*refs):
        i = pl.program_id(0)
        vals = [r if k == 'cref' else r[...] for k, r in zip(kinds, refs[:n_in])]
        ro, ao = fn(i, *vals)
        for r, v in zip(refs[n_in:n_in + n_row], ro):
            r[...] = v.astype(r.dtype)
        for r, v in zip(refs[n_in + n_row:], ao):
            @pl.when(i == 0)
            def _(r=r, v=v):
                r[...] = v

            @pl.when(i > 0)
            def _(r=r, v=v):
                r[...] += v

    return pl.pallas_call(
        body, name=name, grid=(n,), in_specs=in_specs, out_specs=out_specs, out_shape=out_shape,
        compiler_params=pltpu.CompilerParams(dimension_semantics=("arbitrary",), vmem_limit_bytes=VMEM_LIMIT),
    )(*[a for _, a in ins])


def _rms(x, g):
    return x * lax.rsqrt(jnp.mean(x * x, axis=-1, keepdims=True) + EPS) * g


def _ln(x, g, b):
    mu = jnp.mean(x, axis=-1, keepdims=True)
    xc = x - mu
    var = jnp.mean(xc * xc, axis=-1, keepdims=True)
    return xc * lax.rsqrt(var + EPS) * g + b


def _gelu(x):
    return 0.5 * x * (1.0 + jnp.tanh(0.7978845608028654 * (x + 0.044715 * (x * x * x))))


def _silu(x):
    return x * jax.nn.sigmoid(x)


def _glu(a, g):
    return a * jax.nn.sigmoid(g)


def _row_ids(i, tile, rows):
    return i * tile + lax.broadcasted_iota(jnp.int32, (rows, 1), 0)


def _mesh_pos():
    return lax.axis_index("x"), lax.axis_index("y"), lax.axis_index("c")


def _other_chips(x, y):
    return [(1 - x, y), (x, 1 - y), (1 - x, 1 - y)]


def _remote(src, dst, send_sems, recv_sems, s, to):
    return pltpu.make_async_remote_copy(src_ref=src, dst_ref=dst, send_sem=send_sems.at[s], recv_sem=recv_sems.at[s],
                                        device_id=to, device_id_type=MESH)


def _gather_near_copies(refs, send_sems, recv_sems):
    x, y, c = _mesh_pos()
    me = 2 * x + y
    out = []
    for t, ref in enumerate(refs):
        rh = ref.shape[1] // 2
        half = pl.ds(c * rh, rh)
        for k, (cx, cy) in enumerate(_other_chips(x, y)[:2]):
            mine, theirs = ref.at[me, half], ref.at[2 * cx + cy, half]
            out.append((_remote(mine, mine, send_sems, recv_sems, 2 * t + k, (cx, cy, c)),
                        _remote(theirs, theirs, send_sems, recv_sems, 2 * t + k, (cx, cy, c))))
    return out


def _gather_far_copies(refs, send_sems, recv_sems):
    x, y, c = _mesh_pos()
    xn, yn, diag = 2 * (1 - x) + y, 2 * x + (1 - y), 2 * (1 - x) + (1 - y)
    out = []
    for t, ref in enumerate(refs):
        rq = ref.shape[1] // 4
        q0, q1 = pl.ds(2 * c * rq, rq), pl.ds((2 * c + 1) * rq, rq)
        out.append((_remote(ref.at[yn, q1], ref.at[yn, q1], send_sems, recv_sems, 2 * t, (1 - x, y, c)),
                    _remote(ref.at[diag, q1], ref.at[diag, q1], send_sems, recv_sems, 2 * t, (1 - x, y, c))))
        out.append((_remote(ref.at[xn, q0], ref.at[xn, q0], send_sems, recv_sems, 2 * t + 1, (x, 1 - y, c)),
                    _remote(ref.at[diag, q0], ref.at[diag, q0], send_sems, recv_sems, 2 * t + 1, (x, 1 - y, c))))
    return out


def _exchange_copies(refs, send_sems, recv_sems):
    x, y, c = _mesh_pos()
    n = len(refs) // 2
    out = []
    for t in range(n):
        part, land = refs[t], refs[n + t]
        for k, (cx, cy) in enumerate(_other_chips(x, y)):
            out.append((_remote(part.at[2 * cx + cy], land.at[k], send_sems, recv_sems, 3 * t + k, (cx, cy, c)),
                        _remote(land.at[k], land.at[k], send_sems, recv_sems, 3 * t + k, (cx, cy, c))))
    return out


def _split_start(name, thru, after, n_sems, copies, carry=()):
    n, nc = len(thru), len(carry)
    both = list(thru) + list(carry)

    def body(*refs):
        outs = refs[n + nc + 1:]
        send_sems, recv_sems, thru_refs, token = outs[0], outs[1], outs[2:2 + n], outs[2 + n + nc]
        for send, _ in copies(thru_refs, send_sems, recv_sems):
            send.start()
        token[...] = jnp.zeros_like(token)

    res = pl.pallas_call(
        body, name=name,
        out_shape=(pltpu.SemaphoreType.DMA((n_sems,)), pltpu.SemaphoreType.DMA((n_sems,)),
                   *[pltpu.HBM(b.shape, b.dtype) for b in both], _sds((8, LANES), F32)),
        in_specs=[HBM] * (n + nc) + [ANY],
        out_specs=(SEM, SEM, *[HBM] * (n + nc), pl.BlockSpec(memory_space=pltpu.VMEM)),
        input_output_aliases={i: 2 + i for i in range(n + nc)},
        compiler_params=pltpu.CompilerParams(has_side_effects=EFFECT),
    )(*[pltpu.with_memory_space_constraint(b, pltpu.HBM) for b in both], after)
    return res[0], res[1], list(res[2:2 + n]), list(res[2 + n:2 + n + nc]), res[2 + n + nc]


def _split_wait(name, thru, send_sems, recv_sems, after, copies):
    n = len(thru)

    def body(*refs):
        sends, recvs, outs = refs[n], refs[n + 1], refs[n + 3:]
        for send, arrival in copies(outs, sends, recvs):
            send.wait_send()
            arrival.wait_recv()

    res = pl.pallas_call(
        body, name=name, out_shape=tuple(pltpu.HBM(b.shape, b.dtype) for b in thru),
        in_specs=[HBM] * n + [SEM, SEM, ANY], out_specs=tuple([HBM] * n),
        input_output_aliases={i: i for i in range(n)},
        compiler_params=pltpu.CompilerParams(has_side_effects=EFFECT),
    )(*thru, send_sems, recv_sems, after)
    return list(res)


def _forward_halves(name, bufs):
    n = len(bufs)

    def body(*refs):
        outs, send_sems, recv_sems = refs[n:2 * n], refs[2 * n], refs[2 * n + 1]
        x, y, c = _mesh_pos()
        sibling = (x, y, 1 - c)
        sends, arrivals = [], []
        for t, ref in enumerate(outs):
            rh = ref.shape[1] // 2
            mine, other = pl.ds(c * rh, rh), pl.ds((1 - c) * rh, rh)
            for k, (cx, cy) in enumerate(_other_chips(x, y)):
                landed, coming = ref.at[2 * cx + cy, mine], ref.at[2 * cx + cy, other]
                cp = _remote(landed, landed, send_sems, recv_sems, 3 * t + k, sibling)
                cp.start()
                sends.append(cp)
                arrivals.append(_remote(coming, coming, send_sems, recv_sems, 3 * t + k, sibling))
        for a in arrivals:
            a.wait_recv()
        for cp in sends:
            cp.wait_send()

    res = pl.pallas_call(
        body, name=name, out_shape=tuple(_sds(b.shape, b.dtype) for b in bufs),
        in_specs=[HBM] * n, out_specs=tuple([HBM] * n), input_output_aliases={i: i for i in range(n)},
        scratch_shapes=[pltpu.SemaphoreType.DMA((3 * n,)), pltpu.SemaphoreType.DMA((3 * n,))],
    )(*bufs)
    return list(res)


def _swap_halves(name, gs):
    n = len(gs)

    def body(*refs):
        ins, outs, send_sems, recv_sems = refs[:n], refs[n:2 * n], refs[2 * n], refs[2 * n + 1]
        x, y, c = _mesh_pos()
        cps = []
        for t, (g_ref, o_ref) in enumerate(zip(ins, outs)):
            rh = g_ref.shape[1] // 2
            cp = _remote(g_ref.at[:, pl.ds((1 - c) * rh, rh)], o_ref, send_sems, recv_sems, t, (x, y, 1 - c))
            cp.start()
            cps.append(cp)
        for cp in cps:
            cp.wait_recv()
        for cp in cps:
            cp.wait_send()

    res = pl.pallas_call(
        body, name=name, out_shape=tuple(_sds((g.shape[0], g.shape[1] // 2, g.shape[2]), g.dtype) for g in gs),
        in_specs=[HBM] * n, out_specs=tuple([HBM] * n),
        scratch_shapes=[pltpu.SemaphoreType.DMA((n,)), pltpu.SemaphoreType.DMA((n,))],
    )(*gs)
    return list(res)


def _swap_copies(refs, send_sems, recv_sems):
    x, y, c = _mesh_pos()
    n = len(refs) // 2
    out = []
    for t in range(n):
        g_ref, land = refs[t], refs[n + t]
        rh = g_ref.shape[1] // 2
        out.append((_remote(g_ref.at[:, pl.ds((1 - c) * rh, rh)], land, send_sems, recv_sems, t, (x, y, 1 - c)),
                    _remote(land, land, send_sems, recv_sems, t, (x, y, 1 - c))))
    return out


def _join_copies(idx):
    def copies(refs, send_sems, recv_sems):
        x, y, c = _mesh_pos()
        out = []
        for t, ref in enumerate(refs):
            rh = ref.shape[1] // 2
            mine, theirs = ref.at[idx[t], pl.ds(c * rh, rh)], ref.at[idx[t], pl.ds((1 - c) * rh, rh)]
            out.append((_remote(mine, mine, send_sems, recv_sems, t, (x, y, 1 - c)),
                        _remote(theirs, theirs, send_sems, recv_sems, t, (x, y, 1 - c))))
        return out
    return copies


def _gather_all(name, buf, after):
    def body(b_ref, after_ref, out_ref, send_sems, recv_sems, local_sem):
        x, y, c = _mesh_pos()
        me = 4 * x + 2 * y + c
        local = pltpu.make_async_copy(b_ref, out_ref.at[me], local_sem)
        local.start()
        peers = []
        for k in range(1, N_DEV):
            peers.append((1 - x if k & 4 else x, 1 - y if k & 2 else y, 1 - c if k & 1 else c))
        sends = []
        for k, peer in enumerate(peers):
            cp = _remote(b_ref, out_ref.at[me], send_sems, recv_sems, k, peer)
            cp.start()
            sends.append(cp)
        for k, (px, py, pc) in enumerate(peers):
            slot = out_ref.at[4 * px + 2 * py + pc]
            _remote(slot, slot, send_sems, recv_sems, k, (px, py, pc)).wait_recv()
        for cp in sends:
            cp.wait_send()
        local.wait()

    return pl.pallas_call(
        body, name=name, out_shape=_sds((N_DEV,) + buf.shape, buf.dtype), in_specs=[HBM, ANY], out_specs=HBM,
        scratch_shapes=[pltpu.SemaphoreType.DMA((N_DEV - 1,)), pltpu.SemaphoreType.DMA((N_DEV - 1,)),
                        pltpu.SemaphoreType.DMA],
    )(buf, after)


def _scalars(*vals):
    return jnp.stack([jnp.asarray(v, jnp.int32) for v in vals])


def _cast_slab(name, w3, li, chip, after):
    _, R, C = w3.shape
    tr = _pick(R, ROW_TILE, 16)

    def body(s_ref, w_ref, after_ref, o_ref):
        o_ref[...] = w_ref[...].astype(o_ref.dtype)

    grid_spec = pltpu.PrefetchScalarGridSpec(
        num_scalar_prefetch=1, grid=(R // tr,),
        in_specs=[pl.BlockSpec((None, tr, C), lambda r, s: (li, r, 0)), ANY],
        out_specs=pl.BlockSpec((None, tr, C), lambda r, s: (s[0], r, 0)))
    return pl.pallas_call(
        body, name=name, grid_spec=grid_spec, out_shape=_sds((N_CHIPS, R, C), BF16),
        compiler_params=pltpu.CompilerParams(dimension_semantics=("arbitrary",), vmem_limit_bytes=VMEM_LIMIT),
    )(_scalars(chip), w3, after)


def _add_halves(name, g, recv, c):
    _, R, C = g.shape
    rh = R // 2
    tr = _pick(rh, 512, 16)
    nr = rh // tr

    def body(s_ref, g_ref, a_ref, o_ref):
        o_ref[...] = (g_ref[...].astype(F32) + a_ref[...].astype(F32)).astype(o_ref.dtype)

    blk = (None, tr, C)
    grid_spec = pltpu.PrefetchScalarGridSpec(
        num_scalar_prefetch=1, grid=(N_CHIPS, nr),
        in_specs=[pl.BlockSpec(blk, lambda j, r, s: (j, s[0] * nr + r, 0)),
                  pl.BlockSpec(blk, lambda j, r, s: (j, r, 0))],
        out_specs=pl.BlockSpec(blk, lambda j, r, s: (j, r, 0)))
    return pl.pallas_call(
        body, name=name, grid_spec=grid_spec, out_shape=_sds((N_CHIPS, rh, C), g.dtype),
        compiler_params=pltpu.CompilerParams(dimension_semantics=("arbitrary",) * 2, vmem_limit_bytes=VMEM_LIMIT),
    )(_scalars(c), g, recv)


def _sum_into(name, p, recv, gfull, li, chip, c):
    _, rh, C = p.shape
    tr = _pick(rh, 512, 16)
    nr = rh // tr

    def body(s_ref, p_ref, r_ref, g_ref, o_ref):
        acc = p_ref[...].astype(F32)
        for k in range(3):
            acc = acc + r_ref[k].astype(F32)
        o_ref[...] = acc

    grid_spec = pltpu.PrefetchScalarGridSpec(
        num_scalar_prefetch=1, grid=(nr,),
        in_specs=[pl.BlockSpec((None, tr, C), lambda r, s: (s[0], r, 0)),
                  pl.BlockSpec((3, tr, C), lambda r, s: (0, r, 0)), HBM],
        out_specs=pl.BlockSpec((None, tr, C), lambda r, s: (li, s[1] * nr + r, 0)))
    return pl.pallas_call(
        body, name=name, grid_spec=grid_spec, out_shape=_sds(gfull.shape, F32), input_output_aliases={3: 0},
        compiler_params=pltpu.CompilerParams(dimension_semantics=("arbitrary",), vmem_limit_bytes=VMEM_LIMIT),
    )(_scalars(chip, c), p, recv, gfull)


def _all_copies(refs, send_sems, recv_sems):
    b_ref, out_ref = refs
    x, y, c = _mesh_pos()
    me = 4 * x + 2 * y + c
    out = []
    for k in range(1, N_DEV):
        px, py, pc = 1 - x if k & 4 else x, 1 - y if k & 2 else y, 1 - c if k & 1 else c
        slot = out_ref.at[4 * px + 2 * py + pc]
        out.append((_remote(b_ref, out_ref.at[me], send_sems, recv_sems, k - 1, (px, py, pc)),
                    _remote(slot, slot, send_sems, recv_sems, k - 1, (px, py, pc))))
    return out


def _sum_slots(name, slots, own, me):
    _, n, _ = slots.shape

    def body(s_ref, b_ref, own_ref, o_ref):
        acc = None
        for k in range(N_DEV):
            term = jnp.where(s_ref[0] == k, own_ref[...], b_ref[k])
            acc = term if acc is None else acc + term
        o_ref[...] = acc

    grid_spec = pltpu.PrefetchScalarGridSpec(
        num_scalar_prefetch=1, grid=(n // PACK_ROWS,),
        in_specs=[pl.BlockSpec((N_DEV, PACK_ROWS, LANES), lambda i, s: (0, i, 0)),
                  pl.BlockSpec((PACK_ROWS, LANES), lambda i, s: (i, 0))],
        out_specs=pl.BlockSpec((PACK_ROWS, LANES), lambda i, s: (i, 0)))
    return pl.pallas_call(body, name=name, grid_spec=grid_spec, out_shape=_sds((n, LANES), F32))(_scalars(me), slots, own)


def _adam_tile(i, w, g, m, v):
    m = ADAM_B1 * m + (1.0 - ADAM_B1) * g
    v = ADAM_B2 * v + (1.0 - ADAM_B2) * (g * g)
    m_hat = m / (1.0 - ADAM_B1 ** ADAM_STEP)
    v_hat = v / (1.0 - ADAM_B2 ** ADAM_STEP)
    delta = -ADAM_LR * (m_hat / (jnp.sqrt(v_hat) + ADAM_EPS) + ADAM_WD * w)
    return [delta, m, v], []


def _adam(name, w, g, m, v):
    rows, C = w.shape
    tile = _pick(rows, ROW_TILE, 8)
    return _rowwise(name, _adam_tile, [('row', w), ('row', g), ('row', m), ('row', v)], [(C, F32)] * 3, [], tile)


def _adam_rows(name, w, g, m, v, lo, hi, unit, prev=None, after=None):
    rows, C = w.shape
    tile = _pick(unit, ROW_TILE, 8)
    first = lo // tile
    spec = pl.BlockSpec((tile, C), lambda i: (i + first, 0))
    ins, in_specs = [w, g, m, v], [spec] * 4
    if prev is not None:
        ins, in_specs = ins + list(prev), in_specs + [ANY] * 4
    if after is not None:
        ins, in_specs = ins + [after], in_specs + [ANY]

    def body(*refs):
        n_in = len(ins)
        outs, _ = _adam_tile(0, *[r[...] for r in refs[:4]])
        refs[n_in][...] = refs[1][...]
        for o, val in zip(refs[n_in + 1:n_in + 4], outs):
            o[...] = val

    return pl.pallas_call(
        body, name=name, grid=((hi - lo) // tile,), in_specs=in_specs, out_specs=[spec] * 4,
        out_shape=[_sds((rows, C), F32)] * 4,
        input_output_aliases={4 + k: k for k in range(4)} if prev is not None else {},
        compiler_params=pltpu.CompilerParams(dimension_semantics=("arbitrary",), vmem_limit_bytes=VMEM_LIMIT),
    )(*ins)


def _pack(arrs):
    flat = jnp.concatenate([a.reshape(-1).astype(F32) for a in arrs])
    unit = PACK_ROWS * LANES
    n = -(-flat.shape[0] // unit) * unit
    return jnp.pad(flat, (0, n - flat.shape[0])).reshape(-1, LANES)


def _unpack(buf, shapes):
    flat = buf.reshape(-1)
    out, off = [], 0
    for s in shapes:
        n = 1
        for d in s:
            n *= d
        out.append(flat[off:off + n].reshape(s))
        off += n
    return out


def _norm_fwd(name, x, g, tile):
    D = x.shape[1]
    return _rowwise(name, lambda i, xv, gv: ([_rms(xv, gv)], []), [('row', x), ('const', g.reshape(1, D))],
                    [(D, BF16)], [], tile)[0]


def _norm_bwd(name, x, g, dh, dxo, tile):
    D = x.shape[1]
    if dxo is None:
        def fn(i, xv, dhv, gv):
            _, vjp = jax.vjp(_rms, xv, gv)
            return [], [vjp(dhv)[1]]
        return _rowwise(name, fn, [('row', x), ('row', dh), ('const', g.reshape(1, D))], [], [(1, D)], tile)[0]

    def fn(i, xv, dhv, dxv, gv):
        _, vjp = jax.vjp(_rms, xv, gv)
        dx, dg = vjp(dhv)
        return [dxv + dx, dxv + dx], [dg]
    return _rowwise(name, fn, [('row', x), ('row', dh), ('row', dxo), ('const', g.reshape(1, D))],
                    [(D, F32), (D, BF16)], [(1, D)], tile)


def _ffn_fwd(tag, x, g, wg, wu, wd, tile):
    T, D = x.shape
    fs = wg.shape[2]
    F = N_CHIPS * fs
    tm, tn = _pick(T, TM_BIG), _pick(D, TN_SMALL)
    h = _norm_fwd(f"{tag}_norm", x, g, tile)
    hspec = _bs((tm, D), lambda j, i: (i, 0))
    wspec = _bs((None, D, fs), lambda j, i: (j, 0, 0))
    ospec = _bs((tm, fs), lambda j, i: (i, j))

    def gate(ins, outs):
        outs[0][...] = _dot(ins[0][...], ins[1][...], 'nn').astype(BF16)

    a = _mm1(f"{tag}_gate", (N_CHIPS, T // tm), [(h, hspec), (wg, wspec)], [(_sds((T, F), BF16), ospec)], gate)[0]

    def up(ins, outs):
        bv = _dot(ins[0][...], ins[1][...], 'nn')
        for rows in _row_chunks(tm):
            bb = bv[rows]
            outs[0][rows, :] = bb.astype(BF16)
            outs[1][rows, :] = (_silu(ins[2][rows, :].astype(F32)) * bb).astype(BF16)

    b, s = _mm1(f"{tag}_up", (N_CHIPS, T // tm), [(h, hspec), (wu, wspec), (a, ospec)],
                [(_sds((T, F), BF16), ospec)] * 2, up)

    def down(ins, outs):
        outs[0][...] = ins[2][...] + 0.5 * _dot(ins[0][...], ins[1][...].reshape(F, tn), 'nn')

    xspec = _bs((tm, tn), lambda i, j: (i, j))
    xo = _mm1(f"{tag}_down", (T // tm, D // tn),
              [(s, _bs((tm, F), lambda i, j: (i, 0))), (wd, _bs((N_CHIPS, fs, tn), lambda i, j: (0, 0, j))), (x, xspec)],
              [(_sds((T, D), F32), xspec)], down)[0]
    return xo, (x, h, a, b, s)


def _ffn_bwd(tag, dxo, dxb, saved, g, wg, wu, wd, tile, on_grads=None):
    x, h, a, b, s = saved
    T, D = x.shape
    fs = wg.shape[2]
    F = N_CHIPS * fs
    tm = _pick(T, TM_MID)
    tspec = _bs((tm, fs), lambda j, i: (i, j))

    def ds_fn(ins, outs):
        d = _dot(ins[0][...], ins[1][...], 'nt')
        for rows in _row_chunks(tm):
            ds = 0.5 * d[rows]
            av, bv = ins[2][rows, :].astype(F32), ins[3][rows, :].astype(F32)
            sig = jax.nn.sigmoid(av)
            outs[0][rows, :] = (ds * bv * (sig * (1.0 + av * (1.0 - sig)))).astype(BF16)
            outs[1][rows, :] = (ds * (av * sig)).astype(BF16)

    da, db = _mm1(f"{tag}_ds", (N_CHIPS, T // tm),
                  [(dxb, _bs((tm, D), lambda j, i: (i, 0))), (wd, _bs((None, fs, D), lambda j, i: (j, 0, 0))),
                   (a, tspec), (b, tspec)], [(_sds((T, F), BF16), tspec)] * 2, ds_fn)

    tn = _pick(D, TN_BIG)

    def dwd_fn(ins, outs):
        outs[0][...] = (0.5 * _dot(ins[0][...], ins[1][...], 'tn')).astype(BF16)

    gd = _mm1(f"{tag}_dwd", (N_CHIPS, D // tn),
              [(s, _bs((T, fs), lambda i, j: (0, i))), (dxb, _bs((T, tn), lambda i, j: (0, j)))],
              [(_sds(wd.shape, BF16), _bs((None, fs, tn), lambda i, j: (i, 0, j)))], dwd_fn)[0]

    tmd = _pick(D, TM_BIG)

    def dw_fn(ins, outs):
        outs[0][...] = _dot(ins[0][...], ins[1][...], 'tn').astype(BF16)

    def dw(name, dy):
        return _mm1(name, (N_CHIPS, D // tmd),
                    [(h, _bs((T, tmd), lambda j, i: (0, i))), (dy, _bs((T, fs), lambda j, i: (0, j)))],
                    [(_sds(wg.shape, BF16), _bs((None, tmd, fs), lambda j, i: (j, i, 0)))], dw_fn)[0]

    gg, gu = dw(f"{tag}_dwg", da), dw(f"{tag}_dwu", db)
    if on_grads is not None:
        (gg, gu, gd), (da, db) = on_grads([gg, gu, gd], (da, db))

    tmb, tnb = _pick(T, TM_BIG), _pick(D, TN_BIG)
    aspec = _bs((tmb, fs), lambda i, j, k: (i, k))
    wtspec = _bs((None, tnb, fs), lambda i, j, k: (k, j, 0))
    dh = _mm(f"{tag}_dh", (T // tmb, D // tnb, N_CHIPS), [(da, aspec), (wg, wtspec), (db, aspec), (wu, wtspec)],
             [(0, 1, 'nt', 0), (2, 3, 'nt', 0)], [(_sds((T, D), F32), _bs((tmb, tnb), lambda i, j, k: (i, j)))],
             [(tmb, tnb)], lambda accs, _: accs)[0]
    dx, dxb2, dg = _norm_bwd(f"{tag}_norm_bwd", x, g, dh, dxo, tile)
    return dx, dxb2, dg, gg, gu, gd


def _proj_rows(name, a, w, out_dtype, extras=(), epilogue=None):
    M, K = a.shape
    ks, N = w.shape[1], w.shape[2]
    tm, tn = _pick(M, TM_BIG), _pick(N, TN_MID)
    ins = [(a, _bs((tm, K), lambda i, j: (i, 0))), (w, _bs((N_CHIPS, ks, tn), lambda i, j: (0, 0, j)))]
    for e in extras:
        if e.shape[0] == 1:
            ins.append((e, _bs((1, tn), lambda i, j: (0, j))))
        else:
            ins.append((e, _bs((tm, tn), lambda i, j: (i, j))))

    def fn(refs, outs):
        acc = _dot(refs[0][...], refs[1][...].reshape(K, tn), 'nn')
        if epilogue is not None:
            acc = epilogue(acc, [r[...] for r in refs[2:]])
        outs[0][...] = acc.astype(out_dtype)

    return _mm1(name, (M // tm, N // tn), ins, [(_sds((M, N), out_dtype), _bs((tm, tn), lambda i, j: (i, j)))], fn)[0]


def _proj_rows_t(name, pairs, out_dtype):
    dy0, w0 = pairs[0]
    M, N = dy0.shape
    ks = w0.shape[1]
    tm = _pick(M, TM_BIG)
    ins = []
    for dy, w in pairs:
        ins.append((dy, _bs((tm, N), lambda i, j: (i, 0))))
        ins.append((w, _bs((None, ks, N), lambda i, j: (j, 0, 0))))

    def fn(refs, outs):
        acc = _dot(refs[0][...], refs[1][...], 'nt')
        for p in range(1, len(pairs)):
            acc = acc + _dot(refs[2 * p][...], refs[2 * p + 1][...], 'nt')
        outs[0][...] = acc.astype(out_dtype)

    return _mm1(name, (M // tm, N_CHIPS), ins,
                [(_sds((M, N_CHIPS * ks), out_dtype), _bs((tm, ks), lambda i, j: (i, j)))], fn)[0]


def _grad_rows(name, a, dys):
    T, K = a.shape
    N = dys[0].shape[1]
    ks = K // N_CHIPS
    tn = _pick(N, TN_BIG)
    ins = [(a, _bs((T, ks), lambda i, j: (0, i)))] + [(dy, _bs((T, tn), lambda i, j: (0, j))) for dy in dys]

    def fn(refs, outs):
        av = refs[0][...]
        for p in range(len(dys)):
            outs[p][...] = _dot(av, refs[1 + p][...], 'tn').astype(BF16)

    gspec = _bs((None, ks, tn), lambda i, j: (i, 0, j))
    return _mm1(name, (N_CHIPS, N // tn), ins, [(_sds((N_CHIPS, ks, N), BF16), gspec)] * len(dys), fn)


def _softmax_rows(s):
    s = s - jnp.max(s, axis=-1, keepdims=True)
    p = jnp.exp(s)
    return p / jnp.sum(p, axis=-1, keepdims=True)


def _attn_fwd_tile(hd, scale):
    def fn(i, q, k, v):
        outs = []
        for h in range(N_MEM_HEADS):
            sl = slice(h * hd, (h + 1) * hd)
            p = _softmax_rows(lax.dot_general(q[:, sl], k[:, sl], NT, preferred_element_type=F32) * scale)
            outs.append(lax.dot_general(p.astype(BF16), v[:, sl], NN, preferred_element_type=F32))
        return [jnp.concatenate(outs, axis=1)], []
    return fn


def _attn_bwd_tile(hd, scale):
    def fn(i, q, do, k, v):
        dqs, dks, dvs = [], [], []
        for h in range(N_MEM_HEADS):
            sl = slice(h * hd, (h + 1) * hd)
            qh, kh, vh, doh = q[:, sl], k[:, sl], v[:, sl], do[:, sl]
            p = _softmax_rows(lax.dot_general(qh, kh, NT, preferred_element_type=F32) * scale)
            dvs.append(lax.dot_general(p.astype(BF16), doh, TN_, preferred_element_type=F32))
            dp = lax.dot_general(doh, vh, NT, preferred_element_type=F32)
            ds = (p * (dp - jnp.sum(dp * p, axis=-1, keepdims=True)) * scale).astype(BF16)
            dqs.append(lax.dot_general(ds, kh, NN, preferred_element_type=F32))
            dks.append(lax.dot_general(ds, qh, TN_, preferred_element_type=F32))
        return [jnp.concatenate(dqs, axis=1)], [jnp.concatenate(dks, axis=1), jnp.concatenate(dvs, axis=1)]
    return fn


def _attn_fwd(l, x, mem, gq, gkv, W, tile):
    T, D = x.shape
    M = mem.shape[0]
    hd = D // N_MEM_HEADS
    hq = _norm_fwd(f"xa{l}_normq", x, gq, tile)
    mn = _norm_fwd(f"xa{l}_normkv", mem, gkv, _pick(M, tile, 16))
    q = _proj_rows(f"xa{l}_q", hq, W['xattn_wq'], BF16)
    k = _proj_rows(f"xa{l}_k", mn, W['xattn_wk'], BF16)
    v = _proj_rows(f"xa{l}_v", mn, W['xattn_wv'], BF16)
    o = _rowwise(f"xa{l}_attn", _attn_fwd_tile(hd, hd ** -0.5), [('row', q), ('const', k), ('const', v)],
                 [(D, BF16)], [], tile)[0]
    xo = _proj_rows(f"xa{l}_o", o, W['xattn_wo'], F32, extras=(x,), epilogue=lambda acc, ex: ex[0] + acc)
    return xo, (x, hq, mn, q, k, v, o)


def _attn_bwd(l, dxo, dxb, saved, mem, gq, gkv, W, G, tile):
    x, hq, mn, q, k, v, o = saved
    T, D = x.shape
    M = mem.shape[0]
    hd = D // N_MEM_HEADS
    do = _proj_rows_t(f"xa{l}_do", [(dxb, W['xattn_wo'])], BF16)
    G['xattn_wo'] = _grad_rows(f"xa{l}_dwo", o, [dxb])[0]
    dq, dk, dv = _rowwise(f"xa{l}_attn_bwd", _attn_bwd_tile(hd, hd ** -0.5),
                          [('row', q), ('row', do), ('const', k), ('const', v)], [(D, BF16)], [(M, D), (M, D)], tile)
    dhq = _proj_rows_t(f"xa{l}_dhq", [(dq, W['xattn_wq'])], F32)
    G['xattn_wq'] = _grad_rows(f"xa{l}_dwq", hq, [dq])[0]
    dmn = _proj_rows_t(f"xa{l}_dmn", [(dk, W['xattn_wk']), (dv, W['xattn_wv'])], F32)
    G['xattn_wk'], G['xattn_wv'] = _grad_rows(f"xa{l}_dwkv", mn, [dk, dv])
    dx, dxb2, dgq = _norm_bwd(f"xa{l}_normq_bwd", x, gq, dhq, dxo, tile)
    dgkv = _norm_bwd(f"xa{l}_normkv_bwd", mem, gkv, dmn, None, _pick(M, tile, 16))
    return dx, dxb2, dgq, dgkv


def _chunk_mask():
    p = lax.broadcasted_iota(jnp.int32, (GMLP_BLOCK, GMLP_BLOCK), 0)
    q = lax.broadcasted_iota(jnp.int32, (GMLP_BLOCK, GMLP_BLOCK), 1)
    return (q // CHUNK) <= (p // CHUNK)


def _spatial_fwd(vn, ws_ref, bsf, mask, hd):
    vb = vn.astype(BF16)
    wsm = [jnp.where(mask, ws_ref[h], 0.0).astype(BF16) for h in range(A_HEADS)]
    rows = []
    for n in range(vn.shape[0] // GMLP_BLOCK):
        blk = vb[n * GMLP_BLOCK:(n + 1) * GMLP_BLOCK]
        cols = [lax.dot_general(wsm[h], blk[:, h * hd:(h + 1) * hd], NN, preferred_element_type=F32)
                for h in range(A_HEADS)]
        rows.append(jnp.concatenate(cols, axis=1) + bsf)
    return jnp.concatenate(rows, axis=0)


def _spatial_bwd(dsp, vn, ws_ref, mask, hd):
    vb, db16 = vn.astype(BF16), dsp.astype(BF16)
    wsm = [jnp.where(mask, ws_ref[h], 0.0).astype(BF16) for h in range(A_HEADS)]
    dws = [jnp.zeros((GMLP_BLOCK, GMLP_BLOCK), F32) for _ in range(A_HEADS)]
    dbs = jnp.zeros((GMLP_BLOCK, vn.shape[1]), F32)
    rows = []
    for n in range(vn.shape[0] // GMLP_BLOCK):
        sl = slice(n * GMLP_BLOCK, (n + 1) * GMLP_BLOCK)
        cols = []
        for h in range(A_HEADS):
            hs = slice(h * hd, (h + 1) * hd)
            cols.append(lax.dot_general(wsm[h], db16[sl, hs], TN_, preferred_element_type=F32))
            dws[h] = dws[h] + lax.dot_general(db16[sl, hs], vb[sl, hs], NT, preferred_element_type=F32)
        rows.append(jnp.concatenate(cols, axis=1))
        dbs = dbs + dsp[sl]
    dws = [jnp.where(mask, d, 0.0) for d in dws]
    return jnp.concatenate(rows, axis=0), dws, dbs


def _conv_taps(cat, cw_ref, kw, tile):
    acc = jnp.zeros((tile, cat.shape[1]), F32)
    for k in range(kw):
        sh = kw - 1 - k
        r = cat if sh == 0 else pltpu.roll(cat, sh, 0)
        acc = acc + r[HALO:] * cw_ref[k:k + 1, :]
    return acc


def _mix_fwd_tile(A, B, kw, tile):
    hd = A // A_HEADS

    def fn(i, z, zp, ws_ref, bsf, glg, glb, cw_ref, cb, clg, clb):
        mask = _chunk_mask()
        u = _gelu(z[:, :A])
        vn = _ln(_gelu(z[:, A:2 * A]), glg, glb)
        ya = u * _spatial_fwd(vn, ws_ref, bsf, mask, hd)
        hb = _glu(z[:, 2 * A:2 * A + B], z[:, 2 * A + B:])
        hp = jnp.where(i > 0, _glu(zp[:, 2 * A:2 * A + B], zp[:, 2 * A + B:]), 0.0)
        conv = _conv_taps(jnp.concatenate([hp, hb], axis=0), cw_ref, kw, tile) + cb
        yb = _silu(_ln(conv, clg, clb))
        return [jnp.concatenate([ya, yb], axis=1)], []
    return fn


def _mix_bwd1_tile(A, B, kw, tile):
    hd = A // A_HEADS

    def fn(i, z, zp, dy, dxo, ws_ref, bsf, glg, glb, cw_ref, cb, clg, clb):
        mask = _chunk_mask()
        dya, dyb = dy[:, :A], dy[:, A:]
        zu, zv = z[:, :A], z[:, A:2 * A]
        u, vjp_u = jax.vjp(_gelu, zu)
        vn, vjp_v = jax.vjp(lambda t, g, b: _ln(_gelu(t), g, b), zv, glg, glb)
        sp = _spatial_fwd(vn, ws_ref, bsf, mask, hd)
        dzu = vjp_u(dya * sp)[0]
        dvn, dws, dbs = _spatial_bwd(dya * u, vn, ws_ref, mask, hd)
        dzv, dglg, dglb = vjp_v(dvn)
        hb = _glu(z[:, 2 * A:2 * A + B], z[:, 2 * A + B:])
        hp = jnp.where(i > 0, _glu(zp[:, 2 * A:2 * A + B], zp[:, 2 * A + B:]), 0.0)
        cat = jnp.concatenate([hp, hb], axis=0)
        conv = _conv_taps(cat, cw_ref, kw, tile) + cb
        _, vjp_c = jax.vjp(lambda t, g, b: _silu(_ln(t, g, b)), conv, clg, clb)
        dconv, dclg, dclb = vjp_c(dyb)
        tap = lax.broadcasted_iota(jnp.int32, (HALO, 1), 0)
        dcw = jnp.zeros((HALO, B), F32)
        for k in range(kw):
            sh = kw - 1 - k
            r = cat if sh == 0 else pltpu.roll(cat, sh, 0)
            dcw = dcw + jnp.where(tap == k, jnp.sum(dconv * r[HALO:], axis=0, keepdims=True), 0.0)
        dcb = jnp.sum(dconv, axis=0, keepdims=True)
        dbo = jnp.sum(dxo, axis=0, keepdims=True)
        dws = jnp.concatenate([d[None] for d in dws], axis=0)
        return [jnp.concatenate([dzu, dzv], axis=1), dconv], [dws, dbs, dglg, dglb, dcw, dcb, dclg, dclb, dbo]
    return fn


def _mix_bwd2_tile(A, B, kw, tile, n_tiles):
    def fn(i, z, dza, dc, dcn, cw_ref):
        dcn = jnp.where(i < n_tiles - 1, dcn, 0.0)
        cat = jnp.concatenate([dc, dcn], axis=0)
        n = tile + HALO
        dhb = jnp.zeros((tile, B), F32)
        for k in range(kw):
            sh = kw - 1 - k
            r = cat if sh == 0 else pltpu.roll(cat, n - sh, 0)
            dhb = dhb + r[:tile] * cw_ref[k:k + 1, :]
        _, vjp_g = jax.vjp(_glu, z[:, 2 * A:2 * A + B], z[:, 2 * A + B:])
        da, dg = vjp_g(dhb)
        dz = jnp.concatenate([dza, da, dg], axis=1)
        return [dz], [jnp.sum(dz, axis=0, keepdims=True)]
    return fn


def _even_consts(p, e, A, B, kw):
    hd = A // A_HEADS
    bsf = jnp.repeat(p['gmlp_b_s'][e].T, hd, axis=1)
    cw = jnp.pad(p['conv_w_full'][e], ((0, HALO - kw), (0, 0)))
    return [('cref', p['gmlp_w_s'][e]), ('const', bsf), ('const', p['gmlp_ln_g'][e].reshape(1, A)),
            ('const', p['gmlp_ln_b'][e].reshape(1, A)), ('cref', cw), ('const', p['conv_b'][e].reshape(1, B)),
            ('const', p['conv_ln_g'][e].reshape(1, B)), ('const', p['conv_ln_b'][e].reshape(1, B))]


def _even_fwd(l, e, x, gm, p, W, tile):
    T, D = x.shape
    w_in, w_out = W['ab_w_in'], W['ab_w_out']
    zs = w_in.shape[2]
    Z = N_CHIPS * zs
    A = p['gmlp_ln_g'].shape[1]
    B = p['conv_b'].shape[1]
    kw = p['conv_w_full'].shape[1]
    tm = _pick(T, TM_BIG)
    h = _norm_fwd(f"mix{l}_norm", x, gm, tile)

    def in_fn(refs, outs):
        outs[0][...] = _dot(refs[0][...], refs[1][...], 'nn') + refs[2][...]

    z = _mm1(f"mix{l}_in", (N_CHIPS, T // tm),
             [(h, _bs((tm, D), lambda j, i: (i, 0))), (w_in, _bs((None, D, zs), lambda j, i: (j, 0, 0))),
              (p['ab_b_in'][e].reshape(1, Z), _bs((1, zs), lambda j, i: (0, j)))],
             [(_sds((T, Z), F32), _bs((tm, zs), lambda j, i: (i, j)))], in_fn)[0]
    consts = _even_consts(p, e, A, B, kw)
    ycat = _rowwise(f"mix{l}_mid", _mix_fwd_tile(A, B, kw, tile), [('row', z), ('prev', z)] + consts,
                    [(A + B, BF16)], [], tile)[0]
    xo = _proj_rows(f"mix{l}_out", ycat, w_out, F32, extras=(x, p['ab_b_out'][e].reshape(1, D)),
                    epilogue=lambda acc, ex: ex[0] + acc + ex[1])
    return xo, (x, h, z, ycat)


def _even_bwd(l, e, dxo, dxb, saved, gm, p, W, G, tile):
    x, h, z, ycat = saved
    T, D = x.shape
    w_in, w_out = W['ab_w_in'], W['ab_w_out']
    zs = w_in.shape[2]
    Z = N_CHIPS * zs
    A = p['gmlp_ln_g'].shape[1]
    B = p['conv_b'].shape[1]
    kw = p['conv_w_full'].shape[1]
    hd = A // A_HEADS
    dycat = _proj_rows_t(f"mix{l}_dycat", [(dxb, w_out)], F32)
    G['ab_w_out'] = _grad_rows(f"mix{l}_dwout", ycat, [dxb])[0]
    consts = _even_consts(p, e, A, B, kw)
    accs = [(A_HEADS, GMLP_BLOCK, GMLP_BLOCK), (GMLP_BLOCK, A), (1, A), (1, A), (HALO, B), (1, B), (1, B), (1, B),
            (1, D)]
    dza, dconv, dws, dbs, dglg, dglb, dcw, dcb, dclg, dclb, dbo = _rowwise(
        f"mix{l}_mid_bwd1", _mix_bwd1_tile(A, B, kw, tile),
        [('row', z), ('prev', z), ('row', dycat), ('row', dxo)] + consts, [(2 * A, F32), (B, F32)], accs, tile)
    dz, dbin = _rowwise(f"mix{l}_mid_bwd2", _mix_bwd2_tile(A, B, kw, tile, T // tile),
                        [('row', z), ('row', dza), ('row', dconv), ('next', dconv), consts[4]],
                        [(Z, BF16)], [(1, Z)], tile)
    tmd = _pick(D, TM_BIG)

    def dwin_fn(refs, outs):
        outs[0][...] = _dot(refs[0][...], refs[1][...], 'tn').astype(BF16)

    G['ab_w_in'] = _mm1(f"mix{l}_dwin", (N_CHIPS, D // tmd),
                        [(h, _bs((T, tmd), lambda j, i: (0, i))), (dz, _bs((T, zs), lambda j, i: (0, j)))],
                        [(_sds(w_in.shape, BF16), _bs((None, tmd, zs), lambda j, i: (j, i, 0)))], dwin_fn)[0]
    tm, tn = _pick(T, TM_MID), _pick(D, TN_SMALL)

    def dh_fn(refs, outs):
        acc = None
        for j in range(N_CHIPS):
            t = _dot(refs[0][:, j * zs:(j + 1) * zs], refs[1][j], 'nt')
            acc = t if acc is None else acc + t
        outs[0][...] = acc

    dh = _mm1(f"mix{l}_dh", (T // tm, D // tn),
              [(dz, _bs((tm, Z), lambda i, j: (i, 0))), (w_in, _bs((N_CHIPS, tn, zs), lambda i, j: (0, j, 0)))],
              [(_sds((T, D), F32), _bs((tm, tn), lambda i, j: (i, j)))], dh_fn)[0]
    dx, dxb2, dgm = _norm_bwd(f"mix{l}_norm_bwd", x, gm, dh, dxo, tile)
    small = {'ab_b_in': dbin.reshape(Z), 'gmlp_w_s': dws, 'gmlp_b_s': dbs.reshape(GMLP_BLOCK, A_HEADS, hd).sum(-1).T,
             'gmlp_ln_g': dglg.reshape(A), 'gmlp_ln_b': dglb.reshape(A), 'conv_w': dcw[:kw], 'conv_b': dcb.reshape(B),
             'conv_ln_g': dclg.reshape(B), 'conv_ln_b': dclb.reshape(B), 'ab_b_out': dbo.reshape(D)}
    return dx, dxb2, dgm, small


def _pool_counts(t, cg):
    return jnp.concatenate([jnp.broadcast_to(jnp.minimum(t + 1, w).astype(F32), (t.shape[0], cg))
                            for w in POOL_WINDOWS], axis=1)


def _window_sums(cat, cg, back):
    n = cat.shape[0]
    outs = []
    for gi, w in enumerate(POOL_WINDOWS):
        s = cat[:, gi * cg:(gi + 1) * cg]
        step = 1
        while step < w:
            s = s + pltpu.roll(s, step if back else n - step, 0)
            step *= 2
        outs.append(s)
    return jnp.concatenate(outs, axis=1)


def _pool_fwd_tile(D, tile):
    cg = D // len(POOL_WINDOWS)

    def fn(i, x, xp, g):
        h = _rms(x, g)
        hp = jnp.where(i > 0, _rms(xp, g), 0.0)
        sums = _window_sums(jnp.concatenate([hp, h], axis=0), cg, True)[HALO:]
        return [sums / _pool_counts(_row_ids(i, tile, tile), cg) - h], []
    return fn


def _pool_bwd_tile(D, tile, n_tiles):
    cg = D // len(POOL_WINDOWS)

    def fn(i, dd, ddn, x, dxo, g):
        e = dd / _pool_counts(_row_ids(i, tile, tile), cg)
        en = jnp.where(i < n_tiles - 1, ddn / _pool_counts(_row_ids(i + 1, tile, HALO), cg), 0.0)
        dh = _window_sums(jnp.concatenate([e, en], axis=0), cg, False)[:tile] - dd
        _, vjp = jax.vjp(_rms, x, g)
        dx, dg = vjp(dh)
        return [dxo + dx, dxo + dx], [dg]
    return fn


def _odd_fwd(l, o, x, gm, p, W, tile):
    T, D = x.shape
    wc = W['pool_w']
    cg = wc.shape[2]
    cs = cg // N_CHIPS
    ng = len(POOL_WINDOWS)
    tm = _pick(T, TM)
    d = _rowwise(f"mix{l}_pool", _pool_fwd_tile(D, tile), [('row', x), ('prev', x), ('const', gm.reshape(1, D))],
                 [(D, BF16)], [], tile)[0]
    gspec = _bs((tm, cg), lambda i, j, k: (i, j))
    vspec = _bs((1, cg), lambda i, j, k: (0, j))

    def epi(accs, ex):
        pre = accs[0] + ex[0]
        return [ex[2] + pre * ex[1], pre]

    xo, pre = _mm(f"mix{l}_poolmm", (T // tm, ng, N_CHIPS),
                  [(d, _bs((tm, cs), lambda i, j, k: (i, j * N_CHIPS + k))),
                   (wc, _bs((None, cs, cg), lambda i, j, k: (k, j, 0))),
                   (p['pool_b_full'][o].reshape(1, D), vspec), (p['pool_scale_full'][o].reshape(1, D), vspec),
                   (x, gspec)],
                  [(0, 1, 'nn', 0)], [(_sds((T, D), F32), gspec)] * 2, [(tm, cg)], epi, extras=(2, 3, 4))
    return xo, (x, d, pre)


def _odd_bwd(l, o, dxo, dxb, saved, gm, p, W, G, tile):
    x, d, pre = saved
    T, D = x.shape
    wc = W['pool_w']
    cg = wc.shape[2]
    cs = cg // N_CHIPS
    ng = len(POOL_WINDOWS)
    tm, tkt = _pick(T, TM), _pick(T, TK)

    def fn(i, dxv, prev, sc):
        return [dxv * sc], [jnp.sum(dxv * prev, axis=0, keepdims=True), jnp.sum(dxv * sc, axis=0, keepdims=True)]

    do, dscale, dbc = _rowwise(f"mix{l}_pool_bwd1", fn,
                               [('row', dxo), ('row', pre), ('const', p['pool_scale_full'][o].reshape(1, D))],
                               [(D, BF16)], [(1, D), (1, D)], tile)
    nb = ng * N_CHIPS
    dd = _mm(f"mix{l}_pool_dd", (T // tm, nb, 1),
             [(do, _bs((tm, cg), lambda i, j, k: (i, j // N_CHIPS))),
              (wc, _bs((None, cs, cg), lambda i, j, k: (j % N_CHIPS, j // N_CHIPS, 0)))],
             [(0, 1, 'nt', 0)], [(_sds((T, D), F32), _bs((tm, cs), lambda i, j, k: (i, j)))], [(tm, cs)],
             lambda a, _: a)[0]
    G['pool_w'] = _mm(f"mix{l}_pool_dw", (nb, 1, T // tkt),
                      [(d, _bs((tkt, cs), lambda i, j, k: (k, i))), (do, _bs((tkt, cg), lambda i, j, k: (k, i // N_CHIPS)))],
                      [(0, 1, 'tn', 0)],
                      [(_sds(wc.shape, BF16), _bs((None, cs, cg), lambda i, j, k: (i % N_CHIPS, i // N_CHIPS, 0)))],
                      [(cs, cg)], lambda a, _: a)[0]
    dx, dxb2, dgm = _rowwise(f"mix{l}_pool_bwd2", _pool_bwd_tile(D, tile, T // tile),
                             [('row', dd), ('next', dd), ('row', x), ('row', dxo), ('const', gm.reshape(1, D))],
                             [(D, F32), (D, BF16)], [(1, D)], tile)
    small = {'pool_b': dbc.reshape(ng, cg), 'pool_scale': dscale.reshape(D)}
    return dx, dxb2, dgm, small


def _final(x, g, tgt, tile):
    T, D = x.shape

    def fn(i, xv, tv, gv):
        y, vjp = jax.vjp(_rms, xv, gv)
        err = y - tv
        dx, dg = vjp(err / D)
        loss = 0.5 * jnp.sum(jnp.mean(err * err, axis=-1, keepdims=True), axis=0, keepdims=True)
        return [dx, dx], [dg, jnp.broadcast_to(loss, (1, LANES))]

    dx, dxb, dg, loss = _rowwise("final", fn, [('row', x), ('row', tgt), ('const', g.reshape(1, D))],
                                 [(D, F32), (D, BF16)], [(1, D), (1, LANES)], tile)
    return dx, dxb, dg.reshape(D), loss[0, 0]


def _as3d(name, w):
    return w.reshape(w.shape[0], -1, w.shape[-1]) if name == 'pool_w' else w


def kernel(x, mem, norm_ffn1, ffn1_gate, ffn1_up, ffn1_down, norm_mix, ab_w_in, ab_b_in, gmlp_w_s, gmlp_b_s, gmlp_ln_g, gmlp_ln_b, conv_w, conv_b, conv_ln_g, conv_ln_b, ab_w_out, ab_b_out, pool_w, pool_b, pool_scale, norm_xq, norm_xkv, xattn_wq, xattn_wk, xattn_wv, xattn_wo, norm_ffn2, ffn2_gate, ffn2_up, ffn2_down, norm_final, loss_target, m_norm_ffn1, m_ffn1_gate, m_ffn1_up, m_ffn1_down, m_norm_mix, m_ab_w_in, m_ab_b_in, m_gmlp_w_s, m_gmlp_b_s, m_gmlp_ln_g, m_gmlp_ln_b, m_conv_w, m_conv_b, m_conv_ln_g, m_conv_ln_b, m_ab_w_out, m_ab_b_out, m_pool_w, m_pool_b, m_pool_scale, m_norm_xq, m_norm_xkv, m_xattn_wq, m_xattn_wk, m_xattn_wv, m_xattn_wo, m_norm_ffn2, m_ffn2_gate, m_ffn2_up, m_ffn2_down, m_norm_final, v_norm_ffn1, v_ffn1_gate, v_ffn1_up, v_ffn1_down, v_norm_mix, v_ab_w_in, v_ab_b_in, v_gmlp_w_s, v_gmlp_b_s, v_gmlp_ln_g, v_gmlp_ln_b, v_conv_w, v_conv_b, v_conv_ln_g, v_conv_ln_b, v_ab_w_out, v_ab_b_out, v_pool_w, v_pool_b, v_pool_scale, v_norm_xq, v_norm_xkv, v_xattn_wq, v_xattn_wk, v_xattn_wv, v_xattn_wo, v_norm_ffn2, v_ffn2_gate, v_ffn2_up, v_ffn2_down, v_norm_final):
    w = dict(zip(WEIGHTS, [norm_ffn1, ffn1_gate, ffn1_up, ffn1_down, norm_mix, ab_w_in, ab_b_in, gmlp_w_s, gmlp_b_s, gmlp_ln_g, gmlp_ln_b, conv_w, conv_b, conv_ln_g, conv_ln_b, ab_w_out, ab_b_out, pool_w, pool_b, pool_scale, norm_xq, norm_xkv, xattn_wq, xattn_wk, xattn_wv, xattn_wo, norm_ffn2, ffn2_gate, ffn2_up, ffn2_down, norm_final]))
    m = dict(zip(WEIGHTS, [m_norm_ffn1, m_ffn1_gate, m_ffn1_up, m_ffn1_down, m_norm_mix, m_ab_w_in, m_ab_b_in, m_gmlp_w_s, m_gmlp_b_s, m_gmlp_ln_g, m_gmlp_ln_b, m_conv_w, m_conv_b, m_conv_ln_g, m_conv_ln_b, m_ab_w_out, m_ab_b_out, m_pool_w, m_pool_b, m_pool_scale, m_norm_xq, m_norm_xkv, m_xattn_wq, m_xattn_wk, m_xattn_wv, m_xattn_wo, m_norm_ffn2, m_ffn2_gate, m_ffn2_up, m_ffn2_down, m_norm_final]))
    v = dict(zip(WEIGHTS, [v_norm_ffn1, v_ffn1_gate, v_ffn1_up, v_ffn1_down, v_norm_mix, v_ab_w_in, v_ab_b_in, v_gmlp_w_s, v_gmlp_b_s, v_gmlp_ln_g, v_gmlp_ln_b, v_conv_w, v_conv_b, v_conv_ln_g, v_conv_ln_b, v_ab_w_out, v_ab_b_out, v_pool_w, v_pool_b, v_pool_scale, v_norm_xq, v_norm_xkv, v_xattn_wq, v_xattn_wk, v_xattn_wv, v_xattn_wo, v_norm_ffn2, v_ffn2_gate, v_ffn2_up, v_ffn2_down, v_norm_final]))

    xs, mems, tgt = x[0], mem[0], loss_target[0]
    T, D = xs.shape
    L = norm_ffn1.shape[0]
    tile = _pick(T, ROW_TILE)
    cx, cy, cc = _mesh_pos()
    chip = 2 * cx + cy
    w3 = {n: _as3d(n, w[n]) for n in BIG}
    names = [_layer_names(l) for l in range(L)]

    sh_shapes = [w[n].shape for n in SMALL_SHARDED]
    slots = _gather_all("gather_small_shards", _pack([w[n] for n in SMALL_SHARDED]), jnp.zeros((8, LANES), F32))
    per_chip = [_unpack(slots[2 * j], sh_shapes) for j in range(N_CHIPS)]
    full = {n: jnp.concatenate([per_chip[j][k] for j in range(N_CHIPS)], axis=-1) for k, n in enumerate(SMALL_SHARDED)}
    p = dict(w)
    p['conv_w_full'] = full['conv_w'].reshape(full['conv_w'].shape[0], full['conv_w'].shape[1], -1)
    p['pool_b_full'] = full['pool_b']
    p['pool_scale_full'] = full['pool_scale']

    first_ffn = [n for n in names[0] if n.startswith('ffn1')]
    units = {'0a': (0, first_ffn), '0b': (0, [n for n in names[0] if n not in first_ffn])}
    units.update({str(l): (l, names[l]) for l in range(1, L)})
    cast, near, far = {}, {}, {}

    def cast_unit(u, after):
        l, ns = units[u]
        cast[u] = [_cast_slab(f"cast_{n}_{l}", w3[n], _stack_index(n, l), chip, after) for n in ns]

    def start_near(u, after, carry=()):
        slabs = cast.pop(u)
        sends, arrivals, slabs, carry, tok = _split_start(f"gather_near_start_{u}", slabs, after, 2 * len(slabs),
                                                          _gather_near_copies, carry)
        near[u] = (sends, arrivals, slabs)
        return carry, tok

    def start_far(u, after, carry=()):
        sends, arrivals, slabs = near.pop(u)
        slabs = _split_wait(f"gather_near_wait_{u}", slabs, sends, arrivals, after, _gather_near_copies)
        sends, arrivals, slabs, carry, tok = _split_start(f"gather_far_start_{u}", slabs, jnp.zeros((8, LANES), F32),
                                                          2 * len(slabs), _gather_far_copies, carry)
        far[u] = (sends, arrivals, slabs)
        return carry, tok

    def finish_gather(u, after):
        sends, arrivals, slabs = far.pop(u)
        slabs = _split_wait(f"gather_far_wait_{u}", slabs, sends, arrivals, after, _gather_far_copies)
        return dict(zip(units[u][1], _forward_halves(f"gather_fwd_{u}", slabs)))

    cast_unit('0a', slots)
    _, tok = start_near('0a', slots)
    for u in units:
        if u != '0a':
            cast_unit(u, tok)
    casts_done = jnp.stack([s[chip, 0, 0] for u in cast for s in cast[u]]).astype(F32)
    _, tok = start_far('0a', casts_done)
    _, tok = start_near('0b', tok)

    saved, Wl = [], []
    xc = xs + tok[0, 0]
    W = finish_gather('0a', xc)
    for l in range(L):
        Wl.append(W)
        s = {}
        xc, s['ffn1'] = _ffn_fwd(f"ffn1_{l}", xc, w['norm_ffn1'][l], W['ffn1_gate'], W['ffn1_up'], W['ffn1_down'], tile)
        if l == 0:
            (xc,), tok = start_far('0b', xc, (xc,))
            if L > 1:
                (xc,), tok = start_near('1', tok, (xc,))
            W.update(finish_gather('0b', xc))
        if l % 2 == 0:
            xc, s['mix'] = _even_fwd(l, l // 2, xc, w['norm_mix'][l], p, W, tile)
        else:
            xc, s['mix'] = _odd_fwd(l, l // 2, xc, w['norm_mix'][l], p, W, tile)
        if l + 1 < L:
            (xc,), tok = start_far(str(l + 1), xc, (xc,))
            if l + 2 < L:
                (xc,), tok = start_near(str(l + 2), tok, (xc,))
        xc, s['xa'] = _attn_fwd(l, xc, mems, w['norm_xq'][l], w['norm_xkv'][l], W, tile)
        xc, s['ffn2'] = _ffn_fwd(f"ffn2_{l}", xc, w['norm_ffn2'][l], W['ffn2_gate'], W['ffn2_up'], W['ffn2_down'], tile)
        saved.append(s)
        if l + 1 < L:
            W = finish_gather(str(l + 1), xc)

    dx, dxb, g_final, loss_local = _final(xc, w['norm_final'], tgt, tile)
    loss = lax.psum(loss_local, ("x", "y", "c"))
    gfull = {n: lax.empty(w3[n].shape, F32) for n in BIG}
    gs = {n: [None] * w[n].shape[0] for n in SMALL if n != 'norm_final'}

    def finish_exchange(pending, after):
        tag, l, ns, sends, arrivals, thru = pending
        thru = _split_wait(f"rs_wait_{tag}", thru, sends, arrivals, after, _exchange_copies)
        parts, lands = thru[:len(ns)], thru[len(ns):]
        for n, part, land in zip(ns, parts, lands):
            gfull[n] = _sum_into(f"rs_sum_{n}_{l}", part, land, gfull[n], _stack_index(n, l), chip, cc)
        idx = [_stack_index(n, l) for n in ns]
        sends, arrivals, thru, _, _ = _split_start(f"rs_join_start_{tag}", [gfull[n] for n in ns],
                                                   jnp.zeros((8, LANES), F32), len(ns), _join_copies(idx))
        gfull.update(zip(ns, thru))
        joins.append((tag, ns, idx, sends, arrivals))

    def wait_joins(after):
        while joins:
            tag, ns, idx, sends, arrivals = joins.pop(0)
            thru = _split_wait(f"rs_join_wait_{tag}", [gfull[n] for n in ns], sends, arrivals, after, _join_copies(idx))
            gfull.update(zip(ns, thru))

    def swap_hook(key):
        def hook(grads, carry):
            lands = [lax.empty((g.shape[0], g.shape[1] // 2, g.shape[2]), BF16) for g in grads]
            sends, arrivals, thru, carry, _ = _split_start(f"rs_swap_start_{key}", list(grads) + lands,
                                                           jnp.zeros((8, LANES), F32), len(grads), _swap_copies, carry)
            swaps[key] = (sends, arrivals, thru[len(grads):])
            return thru[:len(grads)], carry
        return hook

    def start_exchange(tag, l, ns, G, pending, dx, dxb):
        if len(pending) >= EXCHANGES_IN_FLIGHT:
            finish_exchange(pending.pop(0), dx)
        ffn = [n for n in ns if n.startswith('ffn')]
        rest = [n for n in ns if n not in ffn]
        key = f"{ffn[0][:4]}_{l}"
        sends, arrivals, lands = swaps.pop(key)
        thru = _split_wait(f"rs_swap_wait_{key}", [G[n] for n in ffn] + lands, sends, arrivals, dx, _swap_copies)
        got = dict(zip(ffn, thru[len(ffn):]))
        G.update(zip(ffn, thru[:len(ffn)]))
        got.update(zip(rest, _swap_halves(f"rs_swap_{tag}", [G[n] for n in rest])))
        grads_g = [G[n] for n in ns]
        parts = [_add_halves(f"rs_add_{n}_{l}", G[n], got[n], cc) for n in ns]
        lands = [lax.empty((3,) + part.shape[1:], BF16) for part in parts]
        sends, arrivals, thru, (dx, dxb), tok = _split_start(
            f"rs_start_{tag}", parts + lands, jnp.zeros((8, LANES), F32), 3 * len(parts), _exchange_copies,
            carry=(dx, dxb))
        pending.append((tag, l, ns, sends, arrivals, thru))
        return dx, dxb, tok

    pending, joins, swaps = [], [], {}
    for l in reversed(range(L)):
        first = [n for n in names[l] if n.startswith(('ffn2', 'xattn'))]
        second = [n for n in names[l] if n not in first]
        s, W, G = saved[l], Wl[l], {}
        dx, dxb, dg, G['ffn2_gate'], G['ffn2_up'], G['ffn2_down'] = _ffn_bwd(
            f"ffn2_{l}", dx, dxb, s['ffn2'], w['norm_ffn2'][l], W['ffn2_gate'], W['ffn2_up'], W['ffn2_down'], tile,
            swap_hook(f"ffn2_{l}"))
        gs['norm_ffn2'][l] = dg.reshape(D)
        dx, dxb, dgq, dgkv = _attn_bwd(l, dx, dxb, s['xa'], mems, w['norm_xq'][l], w['norm_xkv'][l], W, G, tile)
        gs['norm_xq'][l], gs['norm_xkv'][l] = dgq.reshape(D), dgkv.reshape(D)
        dx, dxb, _ = start_exchange(f"a{l}", l, first, G, pending, dx, dxb)
        if l % 2 == 0:
            dx, dxb, dgm, small = _even_bwd(l, l // 2, dx, dxb, s['mix'], w['norm_mix'][l], p, W, G, tile)
        else:
            dx, dxb, dgm, small = _odd_bwd(l, l // 2, dx, dxb, s['mix'], w['norm_mix'][l], p, W, G, tile)
        for n, val in small.items():
            gs[n][l // 2] = val
        gs['norm_mix'][l] = dgm.reshape(D)
        dx, dxb, dg, G['ffn1_gate'], G['ffn1_up'], G['ffn1_down'] = _ffn_bwd(
            f"ffn1_{l}", dx, dxb, s['ffn1'], w['norm_ffn1'][l], W['ffn1_gate'], W['ffn1_up'], W['ffn1_down'], tile,
            swap_hook(f"ffn1_{l}"))
        gs['norm_ffn1'][l] = dg.reshape(D)
        dx, dxb, tok = start_exchange(f"b{l}", l, second, G, pending, dx, dxb)
    grad_x = dx[None]
    wait_joins(tok)

    small_full = {n: jnp.stack(gs[n]) for n in gs}
    small_full['norm_final'] = g_final
    full_shapes = [small_full[n].shape for n in SMALL]
    packed = _pack([small_full[n] for n in SMALL])
    sg_sends, sg_arrivals, sg_thru, _, _ = _split_start(
        "small_grads_start", [packed, lax.empty((N_DEV,) + packed.shape, F32)], tok, N_DEV - 1, _all_copies)

    def flat2(n, t):
        t3 = _as3d(n, t)
        return t3.reshape(-1, t3.shape[-1])

    early, dep = {}, []
    for n in BIG:
        R = w3[n].shape[1]
        lo = 0 if n in ODD_ONLY else R
        early[n] = _adam_rows(f"adam_early_{n}", flat2(n, w[n]), flat2(n, gfull[n]), flat2(n, m[n]), flat2(n, v[n]),
                              lo, w3[n].shape[0] * R, R, after=tok)
        dep.append(early[n][1][-1, 0])
    dep = jnp.stack(dep)

    packed, slots8 = _split_wait("small_grads_wait", sg_thru, sg_sends, sg_arrivals, dep, _all_copies)
    summed = _sum_slots("sum_small", slots8, packed, 4 * cx + 2 * cy + cc)
    g_small = dict(zip(SMALL, _unpack(summed, full_shapes)))
    for n in SMALL_SHARDED:
        width = w[n].shape[-1]
        g_small[n] = lax.dynamic_slice_in_dim(g_small[n], chip * width, width, axis=g_small[n].ndim - 1).reshape(w[n].shape)

    for group in pending:
        finish_exchange(group, summed)
    wait_joins(summed)
    grads, delta, new_m, new_v = {}, {}, {}, {}
    for n in BIG:
        outs = early[n]
        if n not in ODD_ONLY:
            R = w3[n].shape[1]
            outs = _adam_rows(f"adam_late_{n}", flat2(n, w[n]), flat2(n, gfull[n]), flat2(n, m[n]), flat2(n, v[n]),
                              0, R, R, prev=early[n])
        grads[n], delta[n], new_m[n], new_v[n] = (t.reshape(w[n].shape) for t in outs)
    small_shapes = [w[n].shape for n in SMALL]
    d2, m2, v2 = _adam("adam_small", _pack([w[n] for n in SMALL]), _pack([g_small[n] for n in SMALL]),
                       _pack([m[n] for n in SMALL]), _pack([v[n] for n in SMALL]))
    for n, dn, mn_, vn_ in zip(SMALL, _unpack(d2, small_shapes), _unpack(m2, small_shapes), _unpack(v2, small_shapes)):
        grads[n], delta[n], new_m[n], new_v[n] = g_small[n].reshape(w[n].shape), dn, mn_, vn_

    return (loss, grad_x, *[grads[n] for n in WEIGHTS], *[delta[n] for n in WEIGHTS],
            *[new_m[n] for n in WEIGHTS], *[new_v[n] for n in WEIGHTS])
```

```python
import jax
import jax.numpy as jnp
from jax import lax
from jax.experimental import pallas as pl
from jax.experimental.pallas import tpu as pltpu

F32, BF16 = jnp.float32, jnp.bfloat16
EPS = 1e-6
N_MEM_HEADS = 4
A_HEADS = 8
GMLP_BLOCK = 128
CHUNK = 64
POOL_WINDOWS = (2, 4, 8, 16)
N_CHIPS = 4
N_DEV = 8
HALO = 32
LANES = 128
TM, TN, TK = 512, 1024, 512
TM_BIG, TM_MID = 1024, 512
TN_BIG, TN_MID, TN_SMALL = 1024, 512, 256
EPI_ROWS = 256
EXCHANGES_IN_FLIGHT = 2
ROW_TILE = 256
PACK_ROWS = 512
VMEM_LIMIT = 48 * 1024 * 1024
VMEM_LIMIT_BIG = 56 * 1024 * 1024
ADAM_LR, ADAM_B1, ADAM_B2, ADAM_EPS, ADAM_WD, ADAM_STEP = 0.001, 0.9, 0.999, 1e-08, 0.01, 10
MESH = pl.DeviceIdType.MESH
HBM = pl.BlockSpec(memory_space=pltpu.HBM)
SEM = pl.BlockSpec(memory_space=pltpu.SEMAPHORE)
ANY = pl.BlockSpec(memory_space=pl.ANY)
EFFECT = pltpu.SideEffectType.DATAFLOW_SIDE_EFFECTING

WEIGHTS = ['norm_ffn1', 'ffn1_gate', 'ffn1_up', 'ffn1_down', 'norm_mix', 'ab_w_in', 'ab_b_in', 'gmlp_w_s',
           'gmlp_b_s', 'gmlp_ln_g', 'gmlp_ln_b', 'conv_w', 'conv_b', 'conv_ln_g', 'conv_ln_b', 'ab_w_out',
           'ab_b_out', 'pool_w', 'pool_b', 'pool_scale', 'norm_xq', 'norm_xkv', 'xattn_wq', 'xattn_wk',
           'xattn_wv', 'xattn_wo', 'norm_ffn2', 'ffn2_gate', 'ffn2_up', 'ffn2_down', 'norm_final']
BIG = ['ffn1_gate', 'ffn1_up', 'ffn1_down', 'ab_w_in', 'ab_w_out', 'pool_w', 'xattn_wq', 'xattn_wk', 'xattn_wv',
       'xattn_wo', 'ffn2_gate', 'ffn2_up', 'ffn2_down']
EVEN_ONLY, ODD_ONLY = ['ab_w_in', 'ab_w_out'], ['pool_w']
SMALL = [n for n in WEIGHTS if n not in BIG]
SMALL_SHARDED = ['conv_w', 'pool_b', 'pool_scale']

NN = (((1,), (0,)), ((), ()))
NT = (((1,), (1,)), ((), ()))
TN_ = (((0,), (0,)), ((), ()))
_DIMS = {'nn': NN, 'nt': NT, 'tn': TN_}


def _pick(n, pref, unit=LANES):
    if n <= pref:
        return n
    t = (pref // unit) * unit
    while t >= unit:
        if n % t == 0:
            return t
        t -= unit
    return n


def _sds(shape, dtype):
    return jax.ShapeDtypeStruct(tuple(shape), dtype)


def _layer_names(l):
    mix = EVEN_ONLY if l % 2 == 0 else ODD_ONLY
    return ['ffn1_gate', 'ffn1_up', 'ffn1_down'] + mix + ['xattn_wq', 'xattn_wk', 'xattn_wv', 'xattn_wo',
                                                          'ffn2_gate', 'ffn2_up', 'ffn2_down']


def _stack_index(name, l):
    return l // 2 if name in EVEN_ONLY + ODD_ONLY else l


def _mm(name, grid, ins, pairs, outs, acc_shapes, epilogue, extras=()):
    n_in, n_out = len(ins), len(outs)
    nk = grid[2]

    def body(*refs):
        in_refs, out_refs, acc_refs = refs[:n_in], refs[n_in:n_in + n_out], refs[n_in + n_out:]
        k = pl.program_id(2)

        @pl.when(k == 0)
        def _():
            for acc in acc_refs:
                acc[...] = jnp.zeros_like(acc)

        for ai, bi, mode, ci in pairs:
            a = in_refs[ai][...].astype(BF16)
            b = in_refs[bi][...].astype(BF16)
            acc_refs[ci][...] += lax.dot_general(a, b, _DIMS[mode], preferred_element_type=F32)

        @pl.when(k == nk - 1)
        def _():
            res = epilogue([acc[...] for acc in acc_refs], [in_refs[e][...] for e in extras])
            for o, r in zip(out_refs, res):
                o[...] = r.astype(o.dtype)

    return pl.pallas_call(
        body, name=name, grid=grid,
        in_specs=[s for _, s in ins], out_specs=[s for _, s in outs], out_shape=[s for s, _ in outs],
        scratch_shapes=[pltpu.VMEM(s, F32) for s in acc_shapes],
        compiler_params=pltpu.CompilerParams(dimension_semantics=("parallel", "parallel", "arbitrary"),
                                             vmem_limit_bytes=VMEM_LIMIT),
    )(*[a for a, _ in ins])


def _bs(shape, fn):
    return pl.BlockSpec(shape, fn)


def _mm1(name, grid, ins, outs, compute, vmem=None):
    n_in = len(ins)

    def body(*refs):
        compute(refs[:n_in], refs[n_in:])

    return pl.pallas_call(
        body, name=name, grid=grid,
        in_specs=[s for _, s in ins], out_specs=[s for _, s in outs], out_shape=[s for s, _ in outs],
        compiler_params=pltpu.CompilerParams(dimension_semantics=("parallel", "parallel"),
                                             vmem_limit_bytes=vmem or VMEM_LIMIT),
    )(*[a for a, _ in ins])


def _dot(a, b, mode):
    return lax.dot_general(a.astype(BF16), b.astype(BF16), _DIMS[mode], preferred_element_type=F32)


def _row_chunks(rows):
    step = min(rows, EPI_ROWS)
    return [slice(r, r + step) for r in range(0, rows, step)]


def _rowwise(name, fn, ins, row_outs, acc_outs, tile):
    T = next(a.shape[0] for k, a in ins if k == 'row')
    n = T // tile
    per = tile // HALO if tile % HALO == 0 else 1
    last = T // HALO - 1
    in_specs = []
    for kind, a in ins:
        if kind == 'row':
            in_specs.append(pl.BlockSpec((tile, a.shape[1]), lambda i: (i, 0)))
        elif kind == 'prev':
            in_specs.append(pl.BlockSpec((HALO, a.shape[1]), lambda i: (jnp.maximum(i * per - 1, 0), 0)))
        elif kind == 'next':
            in_specs.append(pl.BlockSpec((HALO, a.shape[1]), lambda i: (jnp.minimum((i + 1) * per, last), 0)))
        else:
            in_specs.append(pl.BlockSpec(a.shape, lambda i, nd=a.ndim: (0,) * nd))
    n_in, n_row = len(ins), len(row_outs)
    out_shape = [_sds((T, c), dt) for c, dt in row_outs] + [_sds(s, F32) for s in acc_outs]
    out_specs = [pl.BlockSpec((tile, c), lambda i: (i, 0)) for c, _ in row_outs]
    out_specs += [pl.BlockSpec(s, lambda i, nd=len(s): (0,) * nd) for s in acc_outs]
    kinds = [k for k, _ in ins]

    def body(*refs):
        i = pl.program_id(0)
        vals = [r if k == 'cref' else r[...] for k, r in zip(kinds, refs[:n_in])]
        ro, ao = fn(i, *vals)
        for r, v in zip(refs[n_in:n_in + n_row], ro):
            r[...] = v.astype(r.dtype)
        for r, v in zip(refs[n_in + n_row:], ao):
            @pl.when(i == 0)
            def _(r=r, v=v):
                r[...] = v

            @pl.when(i > 0)
            def _(r=r, v=v):
                r[...] += v

    return pl.pallas_call(
        body, name=name, grid=(n,), in_specs=in_specs, out_specs=out_specs, out_shape=out_shape,
        compiler_params=pltpu.CompilerParams(dimension_semantics=("arbitrary",), vmem_limit_bytes=VMEM_LIMIT),
    )(*[a for _, a in ins])


def _rms(x, g):
    return x * lax.rsqrt(jnp.mean(x * x, axis=-1, keepdims=True) + EPS) * g


def _ln(x, g, b):
    mu = jnp.mean(x, axis=-1, keepdims=True)
    xc = x - mu
    var = jnp.mean(xc * xc, axis=-1, keepdims=True)
    return xc * lax.rsqrt(var + EPS) * g + b


def _gelu(x):
    return 0.5 * x * (1.0 + jnp.tanh(0.7978845608028654 * (x + 0.044715 * (x * x * x))))


def _silu(x):
    return x * jax.nn.sigmoid(x)


def _glu(a, g):
    return a * jax.nn.sigmoid(g)


def _row_ids(i, tile, rows):
    return i * tile + lax.broadcasted_iota(jnp.int32, (rows, 1), 0)


def _mesh_pos():
    return lax.axis_index("x"), lax.axis_index("y"), lax.axis_index("c")


def _other_chips(x, y):
    return [(1 - x, y), (x, 1 - y), (1 - x, 1 - y)]


def _remote(src, dst, send_sems, recv_sems, s, to):
    return pltpu.make_async_remote_copy(src_ref=src, dst_ref=dst, send_sem=send_sems.at[s], recv_sem=recv_sems.at[s],
                                        device_id=to, device_id_type=MESH)


def _gather_near_copies(refs, send_sems, recv_sems):
    x, y, c = _mesh_pos()
    me = 2 * x + y
    out = []
    for t, ref in enumerate(refs):
        rh = ref.shape[1] // 2
        half = pl.ds(c * rh, rh)
        for k, (cx, cy) in enumerate(_other_chips(x, y)[:2]):
            mine, theirs = ref.at[me, half], ref.at[2 * cx + cy, half]
            out.append((_remote(mine, mine, send_sems, recv_sems, 2 * t + k, (cx, cy, c)),
                        _remote(theirs, theirs, send_sems, recv_sems, 2 * t + k, (cx, cy, c))))
    return out


def _gather_far_copies(refs, send_sems, recv_sems):
    x, y, c = _mesh_pos()
    xn, yn, diag = 2 * (1 - x) + y, 2 * x + (1 - y), 2 * (1 - x) + (1 - y)
    out = []
    for t, ref in enumerate(refs):
        rq = ref.shape[1] // 4
        q0, q1 = pl.ds(2 * c * rq, rq), pl.ds((2 * c + 1) * rq, rq)
        out.append((_remote(ref.at[yn, q1], ref.at[yn, q1], send_sems, recv_sems, 2 * t, (1 - x, y, c)),
                    _remote(ref.at[diag, q1], ref.at[diag, q1], send_sems, recv_sems, 2 * t, (1 - x, y, c))))
        out.append((_remote(ref.at[xn, q0], ref.at[xn, q0], send_sems, recv_sems, 2 * t + 1, (x, 1 - y, c)),
                    _remote(ref.at[diag, q0], ref.at[diag, q0], send_sems, recv_sems, 2 * t + 1, (x, 1 - y, c))))
    return out


def _exchange_copies(refs, send_sems, recv_sems):
    x, y, c = _mesh_pos()
    n = len(refs) // 2
    out = []
    for t in range(n):
        part, land = refs[t], refs[n + t]
        for k, (cx, cy) in enumerate(_other_chips(x, y)):
            out.append((_remote(part.at[2 * cx + cy], land.at[k], send_sems, recv_sems, 3 * t + k, (cx, cy, c)),
                        _remote(land.at[k], land.at[k], send_sems, recv_sems, 3 * t + k, (cx, cy, c))))
    return out


def _split_start(name, thru, after, n_sems, copies, carry=()):
    n, nc = len(thru), len(carry)
    both = list(thru) + list(carry)

    def body(*refs):
        outs = refs[n + nc + 1:]
        send_sems, recv_sems, thru_refs, token = outs[0], outs[1], outs[2:2 + n], outs[2 + n + nc]
        for send, _ in copies(thru_refs, send_sems, recv_sems):
            send.start()
        token[...] = jnp.zeros_like(token)

    res = pl.pallas_call(
        body, name=name,
        out_shape=(pltpu.SemaphoreType.DMA((n_sems,)), pltpu.SemaphoreType.DMA((n_sems,)),
                   *[pltpu.HBM(b.shape, b.dtype) for b in both], _sds((8, LANES), F32)),
        in_specs=[HBM] * (n + nc) + [ANY],
        out_specs=(SEM, SEM, *[HBM] * (n + nc), pl.BlockSpec(memory_space=pltpu.VMEM)),
        input_output_aliases={i: 2 + i for i in range(n + nc)},
        compiler_params=pltpu.CompilerParams(has_side_effects=EFFECT),
    )(*[pltpu.with_memory_space_constraint(b, pltpu.HBM) for b in both], after)
    return res[0], res[1], list(res[2:2 + n]), list(res[2 + n:2 + n + nc]), res[2 + n + nc]


def _split_wait(name, thru, send_sems, recv_sems, after, copies):
    n = len(thru)

    def body(*refs):
        sends, recvs, outs = refs[n], refs[n + 1], refs[n + 3:]
        for send, arrival in copies(outs, sends, recvs):
            send.wait_send()
            arrival.wait_recv()

    res = pl.pallas_call(
        body, name=name, out_shape=tuple(pltpu.HBM(b.shape, b.dtype) for b in thru),
        in_specs=[HBM] * n + [SEM, SEM, ANY], out_specs=tuple([HBM] * n),
        input_output_aliases={i: i for i in range(n)},
        compiler_params=pltpu.CompilerParams(has_side_effects=EFFECT),
    )(*thru, send_sems, recv_sems, after)
    return list(res)


def _near_slabs(x, y):
    return [2 * (1 - x) + y, 2 * x + (1 - y)]


def _far_slabs(x, y):
    return [2 * (1 - x) + (1 - y)]


def _forward_copies(slabs_of):
    def copies(refs, send_sems, recv_sems):
        x, y, c = _mesh_pos()
        slabs = slabs_of(x, y)
        out = []
        for t, ref in enumerate(refs):
            rh = ref.shape[1] // 2
            mine, other = pl.ds(c * rh, rh), pl.ds((1 - c) * rh, rh)
            for k, j in enumerate(slabs):
                s = len(slabs) * t + k
                out.append((_remote(ref.at[j, mine], ref.at[j, mine], send_sems, recv_sems, s, (x, y, 1 - c)),
                            _remote(ref.at[j, other], ref.at[j, other], send_sems, recv_sems, s, (x, y, 1 - c))))
        return out
    return copies


def _forward_halves(name, bufs, slabs_of, per):
    n = len(bufs)

    def body(*refs):
        outs, send_sems, recv_sems = refs[n:2 * n], refs[2 * n], refs[2 * n + 1]
        pairs = _forward_copies(slabs_of)(outs, send_sems, recv_sems)
        for send, _ in pairs:
            send.start()
        for _, arrival in pairs:
            arrival.wait_recv()
        for send, _ in pairs:
            send.wait_send()

    res = pl.pallas_call(
        body, name=name, out_shape=tuple(_sds(b.shape, b.dtype) for b in bufs),
        in_specs=[HBM] * n, out_specs=tuple([HBM] * n), input_output_aliases={i: i for i in range(n)},
        scratch_shapes=[pltpu.SemaphoreType.DMA((per * n,)), pltpu.SemaphoreType.DMA((per * n,))],
    )(*bufs)
    return list(res)


def _swap_halves(name, gs):
    n = len(gs)

    def body(*refs):
        ins, outs, send_sems, recv_sems = refs[:n], refs[n:2 * n], refs[2 * n], refs[2 * n + 1]
        x, y, c = _mesh_pos()
        cps = []
        for t, (g_ref, o_ref) in enumerate(zip(ins, outs)):
            rh = g_ref.shape[1] // 2
            cp = _remote(g_ref.at[:, pl.ds((1 - c) * rh, rh)], o_ref, send_sems, recv_sems, t, (x, y, 1 - c))
            cp.start()
            cps.append(cp)
        for cp in cps:
            cp.wait_recv()
        for cp in cps:
            cp.wait_send()

    res = pl.pallas_call(
        body, name=name, out_shape=tuple(_sds((g.shape[0], g.shape[1] // 2, g.shape[2]), g.dtype) for g in gs),
        in_specs=[HBM] * n, out_specs=tuple([HBM] * n),
        scratch_shapes=[pltpu.SemaphoreType.DMA((n,)), pltpu.SemaphoreType.DMA((n,))],
    )(*gs)
    return list(res)


def _swap_copies(refs, send_sems, recv_sems):
    x, y, c = _mesh_pos()
    n = len(refs) // 2
    out = []
    for t in range(n):
        g_ref, land = refs[t], refs[n + t]
        rh = g_ref.shape[1] // 2
        out.append((_remote(g_ref.at[:, pl.ds((1 - c) * rh, rh)], land, send_sems, recv_sems, t, (x, y, 1 - c)),
                    _remote(land, land, send_sems, recv_sems, t, (x, y, 1 - c))))
    return out


def _join_copies(idx):
    def copies(refs, send_sems, recv_sems):
        x, y, c = _mesh_pos()
        out = []
        for t, ref in enumerate(refs):
            rh = ref.shape[1] // 2
            mine, theirs = ref.at[idx[t], pl.ds(c * rh, rh)], ref.at[idx[t], pl.ds((1 - c) * rh, rh)]
            out.append((_remote(mine, mine, send_sems, recv_sems, t, (x, y, 1 - c)),
                        _remote(theirs, theirs, send_sems, recv_sems, t, (x, y, 1 - c))))
        return out
    return copies


def _gather_all(name, buf, after):
    def body(b_ref, after_ref, out_ref, send_sems, recv_sems, local_sem):
        x, y, c = _mesh_pos()
        me = 4 * x + 2 * y + c
        local = pltpu.make_async_copy(b_ref, out_ref.at[me], local_sem)
        local.start()
        peers = []
        for k in range(1, N_DEV):
            peers.append((1 - x if k & 4 else x, 1 - y if k & 2 else y, 1 - c if k & 1 else c))
        sends = []
        for k, peer in enumerate(peers):
            cp = _remote(b_ref, out_ref.at[me], send_sems, recv_sems, k, peer)
            cp.start()
            sends.append(cp)
        for k, (px, py, pc) in enumerate(peers):
            slot = out_ref.at[4 * px + 2 * py + pc]
            _remote(slot, slot, send_sems, recv_sems, k, (px, py, pc)).wait_recv()
        for cp in sends:
            cp.wait_send()
        local.wait()

    return pl.pallas_call(
        body, name=name, out_shape=_sds((N_DEV,) + buf.shape, buf.dtype), in_specs=[HBM, ANY], out_specs=HBM,
        scratch_shapes=[pltpu.SemaphoreType.DMA((N_DEV - 1,)), pltpu.SemaphoreType.DMA((N_DEV - 1,)),
                        pltpu.SemaphoreType.DMA],
    )(buf, after)


def _scalars(*vals):
    return jnp.stack([jnp.asarray(v, jnp.int32) for v in vals])


def _cast_slab(name, w3, li, chip, after):
    _, R, C = w3.shape
    tr = _pick(R, ROW_TILE, 16)

    def body(s_ref, w_ref, after_ref, o_ref):
        o_ref[...] = w_ref[...].astype(o_ref.dtype)

    grid_spec = pltpu.PrefetchScalarGridSpec(
        num_scalar_prefetch=1, grid=(R // tr,),
        in_specs=[pl.BlockSpec((None, tr, C), lambda r, s: (li, r, 0)), ANY],
        out_specs=pl.BlockSpec((None, tr, C), lambda r, s: (s[0], r, 0)))
    return pl.pallas_call(
        body, name=name, grid_spec=grid_spec, out_shape=_sds((N_CHIPS, R, C), BF16),
        compiler_params=pltpu.CompilerParams(dimension_semantics=("arbitrary",), vmem_limit_bytes=VMEM_LIMIT),
    )(_scalars(chip), w3, after)


def _add_halves(name, g, recv, c):
    _, R, C = g.shape
    rh = R // 2
    tr = _pick(rh, 512, 16)
    nr = rh // tr

    def body(s_ref, g_ref, a_ref, o_ref):
        o_ref[...] = (g_ref[...].astype(F32) + a_ref[...].astype(F32)).astype(o_ref.dtype)

    blk = (None, tr, C)
    grid_spec = pltpu.PrefetchScalarGridSpec(
        num_scalar_prefetch=1, grid=(N_CHIPS, nr),
        in_specs=[pl.BlockSpec(blk, lambda j, r, s: (j, s[0] * nr + r, 0)),
                  pl.BlockSpec(blk, lambda j, r, s: (j, r, 0))],
        out_specs=pl.BlockSpec(blk, lambda j, r, s: (j, r, 0)))
    return pl.pallas_call(
        body, name=name, grid_spec=grid_spec, out_shape=_sds((N_CHIPS, rh, C), g.dtype),
        compiler_params=pltpu.CompilerParams(dimension_semantics=("arbitrary",) * 2, vmem_limit_bytes=VMEM_LIMIT),
    )(_scalars(c), g, recv)


def _sum_into(name, p, recv, gfull, li, chip, c):
    _, rh, C = p.shape
    tr = _pick(rh, 512, 16)
    nr = rh // tr

    def body(s_ref, p_ref, r_ref, g_ref, o_ref):
        acc = p_ref[...].astype(F32)
        for k in range(3):
            acc = acc + r_ref[k].astype(F32)
        o_ref[...] = acc

    grid_spec = pltpu.PrefetchScalarGridSpec(
        num_scalar_prefetch=1, grid=(nr,),
        in_specs=[pl.BlockSpec((None, tr, C), lambda r, s: (s[0], r, 0)),
                  pl.BlockSpec((3, tr, C), lambda r, s: (0, r, 0)), HBM],
        out_specs=pl.BlockSpec((None, tr, C), lambda r, s: (li, s[1] * nr + r, 0)))
    return pl.pallas_call(
        body, name=name, grid_spec=grid_spec, out_shape=_sds(gfull.shape, F32), input_output_aliases={3: 0},
        compiler_params=pltpu.CompilerParams(dimension_semantics=("arbitrary",), vmem_limit_bytes=VMEM_LIMIT),
    )(_scalars(chip, c), p, recv, gfull)


def _all_copies(refs, send_sems, recv_sems):
    b_ref, out_ref = refs
    x, y, c = _mesh_pos()
    me = 4 * x + 2 * y + c
    out = []
    for k in range(1, N_DEV):
        px, py, pc = 1 - x if k & 4 else x, 1 - y if k & 2 else y, 1 - c if k & 1 else c
        slot = out_ref.at[4 * px + 2 * py + pc]
        out.append((_remote(b_ref, out_ref.at[me], send_sems, recv_sems, k - 1, (px, py, pc)),
                    _remote(slot, slot, send_sems, recv_sems, k - 1, (px, py, pc))))
    return out


def _sum_slots(name, slots, own, me):
    _, n, _ = slots.shape

    def body(s_ref, b_ref, own_ref, o_ref):
        acc = None
        for k in range(N_DEV):
            term = jnp.where(s_ref[0] == k, own_ref[...], b_ref[k])
            acc = term if acc is None else acc + term
        o_ref[...] = acc

    grid_spec = pltpu.PrefetchScalarGridSpec(
        num_scalar_prefetch=1, grid=(n // PACK_ROWS,),
        in_specs=[pl.BlockSpec((N_DEV, PACK_ROWS, LANES), lambda i, s: (0, i, 0)),
                  pl.BlockSpec((PACK_ROWS, LANES), lambda i, s: (i, 0))],
        out_specs=pl.BlockSpec((PACK_ROWS, LANES), lambda i, s: (i, 0)))
    return pl.pallas_call(body, name=name, grid_spec=grid_spec, out_shape=_sds((n, LANES), F32))(_scalars(me), slots, own)


def _adam_tile(i, w, g, m, v):
    m = ADAM_B1 * m + (1.0 - ADAM_B1) * g
    v = ADAM_B2 * v + (1.0 - ADAM_B2) * (g * g)
    m_hat = m / (1.0 - ADAM_B1 ** ADAM_STEP)
    v_hat = v / (1.0 - ADAM_B2 ** ADAM_STEP)
    delta = -ADAM_LR * (m_hat / (jnp.sqrt(v_hat) + ADAM_EPS) + ADAM_WD * w)
    return [delta, m, v], []


def _adam(name, w, g, m, v):
    rows, C = w.shape
    tile = _pick(rows, ROW_TILE, 8)
    return _rowwise(name, _adam_tile, [('row', w), ('row', g), ('row', m), ('row', v)], [(C, F32)] * 3, [], tile)


def _adam_rows(name, w, g, m, v, lo, hi, unit, prev=None, after=None):
    rows, C = w.shape
    tile = _pick(unit, ROW_TILE, 8)
    first = lo // tile
    spec = pl.BlockSpec((tile, C), lambda i: (i + first, 0))
    ins, in_specs = [w, g, m, v], [spec] * 4
    if prev is not None:
        ins, in_specs = ins + list(prev), in_specs + [ANY] * 4
    if after is not None:
        ins, in_specs = ins + [after], in_specs + [ANY]

    def body(*refs):
        n_in = len(ins)
        outs, _ = _adam_tile(0, *[r[...] for r in refs[:4]])
        refs[n_in][...] = refs[1][...]
        for o, val in zip(refs[n_in + 1:n_in + 4], outs):
            o[...] = val

    return pl.pallas_call(
        body, name=name, grid=((hi - lo) // tile,), in_specs=in_specs, out_specs=[spec] * 4,
        out_shape=[_sds((rows, C), F32)] * 4,
        input_output_aliases={4 + k: k for k in range(4)} if prev is not None else {},
        compiler_params=pltpu.CompilerParams(dimension_semantics=("arbitrary",), vmem_limit_bytes=VMEM_LIMIT),
    )(*ins)


def _pack(arrs):
    flat = jnp.concatenate([a.reshape(-1).astype(F32) for a in arrs])
    unit = PACK_ROWS * LANES
    n = -(-flat.shape[0] // unit) * unit
    return jnp.pad(flat, (0, n - flat.shape[0])).reshape(-1, LANES)


def _unpack(buf, shapes):
    flat = buf.reshape(-1)
    out, off = [], 0
    for s in shapes:
        n = 1
        for d in s:
            n *= d
        out.append(flat[off:off + n].reshape(s))
        off += n
    return out


def _norm_fwd(name, x, g, tile):
    D = x.shape[1]
    return _rowwise(name, lambda i, xv, gv: ([_rms(xv, gv)], []), [('row', x), ('const', g.reshape(1, D))],
                    [(D, BF16)], [], tile)[0]


def _norm_bwd(name, x, g, dh, dxo, tile):
    D = x.shape[1]
    if dxo is None:
        def fn(i, xv, dhv, gv):
            _, vjp = jax.vjp(_rms, xv, gv)
            return [], [vjp(dhv)[1]]
        return _rowwise(name, fn, [('row', x), ('row', dh), ('const', g.reshape(1, D))], [], [(1, D)], tile)[0]

    def fn(i, xv, dhv, dxv, gv):
        _, vjp = jax.vjp(_rms, xv, gv)
        dx, dg = vjp(dhv)
        return [dxv + dx, dxv + dx], [dg]
    return _rowwise(name, fn, [('row', x), ('row', dh), ('row', dxo), ('const', g.reshape(1, D))],
                    [(D, F32), (D, BF16)], [(1, D)], tile)


def _ffn_fwd(tag, x, g, wg, wu, wd, tile):
    T, D = x.shape
    fs = wg.shape[2]
    F = N_CHIPS * fs
    tm, tn = _pick(T, TM_BIG), _pick(D, TN_SMALL)
    h = _norm_fwd(f"{tag}_norm", x, g, tile)
    hspec = _bs((tm, D), lambda j, i: (i, 0))
    wspec = _bs((None, D, fs), lambda j, i: (j, 0, 0))
    ospec = _bs((tm, fs), lambda j, i: (i, j))

    def gate(ins, outs):
        outs[0][...] = _dot(ins[0][...], ins[1][...], 'nn').astype(BF16)

    a = _mm1(f"{tag}_gate", (N_CHIPS, T // tm), [(h, hspec), (wg, wspec)], [(_sds((T, F), BF16), ospec)], gate)[0]

    def up(ins, outs):
        bv = _dot(ins[0][...], ins[1][...], 'nn')
        for rows in _row_chunks(tm):
            bb = bv[rows]
            outs[0][rows, :] = bb.astype(BF16)
            outs[1][rows, :] = (_silu(ins[2][rows, :].astype(F32)) * bb).astype(BF16)

    b, s = _mm1(f"{tag}_up", (N_CHIPS, T // tm), [(h, hspec), (wu, wspec), (a, ospec)],
                [(_sds((T, F), BF16), ospec)] * 2, up)

    def down(ins, outs):
        outs[0][...] = ins[2][...] + 0.5 * _dot(ins[0][...], ins[1][...].reshape(F, tn), 'nn')

    xspec = _bs((tm, tn), lambda i, j: (i, j))
    xo = _mm1(f"{tag}_down", (T // tm, D // tn),
              [(s, _bs((tm, F), lambda i, j: (i, 0))), (wd, _bs((N_CHIPS, fs, tn), lambda i, j: (0, 0, j))), (x, xspec)],
              [(_sds((T, D), F32), xspec)], down)[0]
    return xo, (x, h, a, b, s)


def _ffn_bwd(tag, dxo, dxb, saved, g, wg, wu, wd, tile, on_grads=None):
    x, h, a, b, s = saved
    T, D = x.shape
    fs = wg.shape[2]
    F = N_CHIPS * fs
    tm = _pick(T, TM_BIG)
    tspec = _bs((tm, fs), lambda j, i: (i, j))

    def ds_fn(ins, outs):
        d = _dot(ins[0][...], ins[1][...], 'nt')
        for rows in _row_chunks(tm):
            ds = 0.5 * d[rows]
            av, bv = ins[2][rows, :].astype(F32), ins[3][rows, :].astype(F32)
            sig = jax.nn.sigmoid(av)
            outs[0][rows, :] = (ds * bv * (sig * (1.0 + av * (1.0 - sig)))).astype(BF16)
            outs[1][rows, :] = (ds * (av * sig)).astype(BF16)

    da, db = _mm1(f"{tag}_ds", (N_CHIPS, T // tm),
                  [(dxb, _bs((tm, D), lambda j, i: (i, 0))), (wd, _bs((None, fs, D), lambda j, i: (j, 0, 0))),
                   (a, tspec), (b, tspec)], [(_sds((T, F), BF16), tspec)] * 2, ds_fn, vmem=VMEM_LIMIT_BIG)

    tn = _pick(D, TN_BIG)

    def dwd_fn(ins, outs):
        outs[0][...] = (0.5 * _dot(ins[0][...], ins[1][...], 'tn')).astype(BF16)

    gd = _mm1(f"{tag}_dwd", (N_CHIPS, D // tn),
              [(s, _bs((T, fs), lambda i, j: (0, i))), (dxb, _bs((T, tn), lambda i, j: (0, j)))],
              [(_sds(wd.shape, BF16), _bs((None, fs, tn), lambda i, j: (i, 0, j)))], dwd_fn)[0]

    tmd = _pick(D, TM_BIG)

    def dw_fn(ins, outs):
        outs[0][...] = _dot(ins[0][...], ins[1][...], 'tn').astype(BF16)

    def dw(name, dy):
        return _mm1(name, (N_CHIPS, D // tmd),
                    [(h, _bs((T, tmd), lambda j, i: (0, i))), (dy, _bs((T, fs), lambda j, i: (0, j)))],
                    [(_sds(wg.shape, BF16), _bs((None, tmd, fs), lambda j, i: (j, i, 0)))], dw_fn)[0]

    gg, gu = dw(f"{tag}_dwg", da), dw(f"{tag}_dwu", db)
    if on_grads is not None:
        (gg, gu, gd), (da, db) = on_grads([gg, gu, gd], (da, db))

    tmb, tnb = _pick(T, TM_BIG), _pick(D, TN_BIG)
    aspec = _bs((tmb, fs), lambda i, j, k: (i, k))
    wtspec = _bs((None, tnb, fs), lambda i, j, k: (k, j, 0))
    dh = _mm(f"{tag}_dh", (T // tmb, D // tnb, N_CHIPS), [(da, aspec), (wg, wtspec), (db, aspec), (wu, wtspec)],
             [(0, 1, 'nt', 0), (2, 3, 'nt', 0)], [(_sds((T, D), F32), _bs((tmb, tnb), lambda i, j, k: (i, j)))],
             [(tmb, tnb)], lambda accs, _: accs)[0]
    dx, dxb2, dg = _norm_bwd(f"{tag}_norm_bwd", x, g, dh, dxo, tile)
    return dx, dxb2, dg, gg, gu, gd


def _proj_rows(name, a, w, out_dtype, extras=(), epilogue=None):
    M, K = a.shape
    ks, N = w.shape[1], w.shape[2]
    tm, tn = _pick(M, TM_BIG), _pick(N, TN_MID)
    ins = [(a, _bs((tm, K), lambda i, j: (i, 0))), (w, _bs((N_CHIPS, ks, tn), lambda i, j: (0, 0, j)))]
    for e in extras:
        if e.shape[0] == 1:
            ins.append((e, _bs((1, tn), lambda i, j: (0, j))))
        else:
            ins.append((e, _bs((tm, tn), lambda i, j: (i, j))))

    def fn(refs, outs):
        acc = _dot(refs[0][...], refs[1][...].reshape(K, tn), 'nn')
        if epilogue is not None:
            acc = epilogue(acc, [r[...] for r in refs[2:]])
        outs[0][...] = acc.astype(out_dtype)

    return _mm1(name, (M // tm, N // tn), ins, [(_sds((M, N), out_dtype), _bs((tm, tn), lambda i, j: (i, j)))], fn)[0]


def _proj_rows_t(name, pairs, out_dtype):
    dy0, w0 = pairs[0]
    M, N = dy0.shape
    ks = w0.shape[1]
    tm = _pick(M, TM_BIG)
    ins = []
    for dy, w in pairs:
        ins.append((dy, _bs((tm, N), lambda i, j: (i, 0))))
        ins.append((w, _bs((None, ks, N), lambda i, j: (j, 0, 0))))

    def fn(refs, outs):
        acc = _dot(refs[0][...], refs[1][...], 'nt')
        for p in range(1, len(pairs)):
            acc = acc + _dot(refs[2 * p][...], refs[2 * p + 1][...], 'nt')
        outs[0][...] = acc.astype(out_dtype)

    return _mm1(name, (M // tm, N_CHIPS), ins,
                [(_sds((M, N_CHIPS * ks), out_dtype), _bs((tm, ks), lambda i, j: (i, j)))], fn)[0]


def _grad_rows(name, a, dys):
    T, K = a.shape
    N = dys[0].shape[1]
    ks = K // N_CHIPS
    tn = _pick(N, TN_BIG)
    ins = [(a, _bs((T, ks), lambda i, j: (0, i)))] + [(dy, _bs((T, tn), lambda i, j: (0, j))) for dy in dys]

    def fn(refs, outs):
        av = refs[0][...]
        for p in range(len(dys)):
            outs[p][...] = _dot(av, refs[1 + p][...], 'tn').astype(BF16)

    gspec = _bs((None, ks, tn), lambda i, j: (i, 0, j))
    return _mm1(name, (N_CHIPS, N // tn), ins, [(_sds((N_CHIPS, ks, N), BF16), gspec)] * len(dys), fn)


def _softmax_rows(s):
    s = s - jnp.max(s, axis=-1, keepdims=True)
    p = jnp.exp(s)
    return p / jnp.sum(p, axis=-1, keepdims=True)


def _attn_fwd_tile(hd, scale):
    def fn(i, q, k, v):
        outs = []
        for h in range(N_MEM_HEADS):
            sl = slice(h * hd, (h + 1) * hd)
            p = _softmax_rows(lax.dot_general(q[:, sl], k[:, sl], NT, preferred_element_type=F32) * scale)
            outs.append(lax.dot_general(p.astype(BF16), v[:, sl], NN, preferred_element_type=F32))
        return [jnp.concatenate(outs, axis=1)], []
    return fn


def _attn_bwd_tile(hd, scale):
    def fn(i, q, do, k, v):
        dqs, dks, dvs = [], [], []
        for h in range(N_MEM_HEADS):
            sl = slice(h * hd, (h + 1) * hd)
            qh, kh, vh, doh = q[:, sl], k[:, sl], v[:, sl], do[:, sl]
            p = _softmax_rows(lax.dot_general(qh, kh, NT, preferred_element_type=F32) * scale)
            dvs.append(lax.dot_general(p.astype(BF16), doh, TN_, preferred_element_type=F32))
            dp = lax.dot_general(doh, vh, NT, preferred_element_type=F32)
            ds = (p * (dp - jnp.sum(dp * p, axis=-1, keepdims=True)) * scale).astype(BF16)
            dqs.append(lax.dot_general(ds, kh, NN, preferred_element_type=F32))
            dks.append(lax.dot_general(ds, qh, TN_, preferred_element_type=F32))
        return [jnp.concatenate(dqs, axis=1)], [jnp.concatenate(dks, axis=1), jnp.concatenate(dvs, axis=1)]
    return fn


def _attn_fwd(l, x, mem, gq, gkv, W, tile):
    T, D = x.shape
    M = mem.shape[0]
    hd = D // N_MEM_HEADS
    hq = _norm_fwd(f"xa{l}_normq", x, gq, tile)
    mn = _norm_fwd(f"xa{l}_normkv", mem, gkv, _pick(M, tile, 16))
    q = _proj_rows(f"xa{l}_q", hq, W['xattn_wq'], BF16)
    k = _proj_rows(f"xa{l}_k", mn, W['xattn_wk'], BF16)
    v = _proj_rows(f"xa{l}_v", mn, W['xattn_wv'], BF16)
    o = _rowwise(f"xa{l}_attn", _attn_fwd_tile(hd, hd ** -0.5), [('row', q), ('const', k), ('const', v)],
                 [(D, BF16)], [], tile)[0]
    xo = _proj_rows(f"xa{l}_o", o, W['xattn_wo'], F32, extras=(x,), epilogue=lambda acc, ex: ex[0] + acc)
    return xo, (x, hq, mn, q, k, v, o)


def _attn_bwd(l, dxo, dxb, saved, mem, gq, gkv, W, G, tile):
    x, hq, mn, q, k, v, o = saved
    T, D = x.shape
    M = mem.shape[0]
    hd = D // N_MEM_HEADS
    do = _proj_rows_t(f"xa{l}_do", [(dxb, W['xattn_wo'])], BF16)
    G['xattn_wo'] = _grad_rows(f"xa{l}_dwo", o, [dxb])[0]
    dq, dk, dv = _rowwise(f"xa{l}_attn_bwd", _attn_bwd_tile(hd, hd ** -0.5),
                          [('row', q), ('row', do), ('const', k), ('const', v)], [(D, BF16)], [(M, D), (M, D)], tile)
    dhq = _proj_rows_t(f"xa{l}_dhq", [(dq, W['xattn_wq'])], F32)
    G['xattn_wq'] = _grad_rows(f"xa{l}_dwq", hq, [dq])[0]
    dmn = _proj_rows_t(f"xa{l}_dmn", [(dk, W['xattn_wk']), (dv, W['xattn_wv'])], F32)
    G['xattn_wk'], G['xattn_wv'] = _grad_rows(f"xa{l}_dwkv", mn, [dk, dv])
    dx, dxb2, dgq = _norm_bwd(f"xa{l}_normq_bwd", x, gq, dhq, dxo, tile)
    dgkv = _norm_bwd(f"xa{l}_normkv_bwd", mem, gkv, dmn, None, _pick(M, tile, 16))
    return dx, dxb2, dgq, dgkv


def _chunk_mask():
    p = lax.broadcasted_iota(jnp.int32, (GMLP_BLOCK, GMLP_BLOCK), 0)
    q = lax.broadcasted_iota(jnp.int32, (GMLP_BLOCK, GMLP_BLOCK), 1)
    return (q // CHUNK) <= (p // CHUNK)


def _spatial_fwd(vn, ws_ref, bsf, mask, hd):
    vb = vn.astype(BF16)
    wsm = [jnp.where(mask, ws_ref[h], 0.0).astype(BF16) for h in range(A_HEADS)]
    rows = []
    for n in range(vn.shape[0] // GMLP_BLOCK):
        blk = vb[n * GMLP_BLOCK:(n + 1) * GMLP_BLOCK]
        cols = [lax.dot_general(wsm[h], blk[:, h * hd:(h + 1) * hd], NN, preferred_element_type=F32)
                for h in range(A_HEADS)]
        rows.append(jnp.concatenate(cols, axis=1) + bsf)
    return jnp.concatenate(rows, axis=0)


def _spatial_bwd(dsp, vn, ws_ref, mask, hd):
    vb, db16 = vn.astype(BF16), dsp.astype(BF16)
    wsm = [jnp.where(mask, ws_ref[h], 0.0).astype(BF16) for h in range(A_HEADS)]
    dws = [jnp.zeros((GMLP_BLOCK, GMLP_BLOCK), F32) for _ in range(A_HEADS)]
    dbs = jnp.zeros((GMLP_BLOCK, vn.shape[1]), F32)
    rows = []
    for n in range(vn.shape[0] // GMLP_BLOCK):
        sl = slice(n * GMLP_BLOCK, (n + 1) * GMLP_BLOCK)
        cols = []
        for h in range(A_HEADS):
            hs = slice(h * hd, (h + 1) * hd)
            cols.append(lax.dot_general(wsm[h], db16[sl, hs], TN_, preferred_element_type=F32))
            dws[h] = dws[h] + lax.dot_general(db16[sl, hs], vb[sl, hs], NT, preferred_element_type=F32)
        rows.append(jnp.concatenate(cols, axis=1))
        dbs = dbs + dsp[sl]
    dws = [jnp.where(mask, d, 0.0) for d in dws]
    return jnp.concatenate(rows, axis=0), dws, dbs


def _conv_taps(cat, cw_ref, kw, tile):
    acc = jnp.zeros((tile, cat.shape[1]), F32)
    for k in range(kw):
        sh = kw - 1 - k
        r = cat if sh == 0 else pltpu.roll(cat, sh, 0)
        acc = acc + r[HALO:] * cw_ref[k:k + 1, :]
    return acc


def _mix_fwd_tile(A, B, kw, tile):
    hd = A // A_HEADS

    def fn(i, z, zp, ws_ref, bsf, glg, glb, cw_ref, cb, clg, clb):
        mask = _chunk_mask()
        u = _gelu(z[:, :A])
        vn = _ln(_gelu(z[:, A:2 * A]), glg, glb)
        ya = u * _spatial_fwd(vn, ws_ref, bsf, mask, hd)
        hb = _glu(z[:, 2 * A:2 * A + B], z[:, 2 * A + B:])
        hp = jnp.where(i > 0, _glu(zp[:, 2 * A:2 * A + B], zp[:, 2 * A + B:]), 0.0)
        conv = _conv_taps(jnp.concatenate([hp, hb], axis=0), cw_ref, kw, tile) + cb
        yb = _silu(_ln(conv, clg, clb))
        return [jnp.concatenate([ya, yb], axis=1)], []
    return fn


def _mix_bwd1_tile(A, B, kw, tile):
    hd = A // A_HEADS

    def fn(i, z, zp, dy, dxo, ws_ref, bsf, glg, glb, cw_ref, cb, clg, clb):
        mask = _chunk_mask()
        dya, dyb = dy[:, :A], dy[:, A:]
        zu, zv = z[:, :A], z[:, A:2 * A]
        u, vjp_u = jax.vjp(_gelu, zu)
        vn, vjp_v = jax.vjp(lambda t, g, b: _ln(_gelu(t), g, b), zv, glg, glb)
        sp = _spatial_fwd(vn, ws_ref, bsf, mask, hd)
        dzu = vjp_u(dya * sp)[0]
        dvn, dws, dbs = _spatial_bwd(dya * u, vn, ws_ref, mask, hd)
        dzv, dglg, dglb = vjp_v(dvn)
        hb = _glu(z[:, 2 * A:2 * A + B], z[:, 2 * A + B:])
        hp = jnp.where(i > 0, _glu(zp[:, 2 * A:2 * A + B], zp[:, 2 * A + B:]), 0.0)
        cat = jnp.concatenate([hp, hb], axis=0)
        conv = _conv_taps(cat, cw_ref, kw, tile) + cb
        _, vjp_c = jax.vjp(lambda t, g, b: _silu(_ln(t, g, b)), conv, clg, clb)
        dconv, dclg, dclb = vjp_c(dyb)
        tap = lax.broadcasted_iota(jnp.int32, (HALO, 1), 0)
        dcw = jnp.zeros((HALO, B), F32)
        for k in range(kw):
            sh = kw - 1 - k
            r = cat if sh == 0 else pltpu.roll(cat, sh, 0)
            dcw = dcw + jnp.where(tap == k, jnp.sum(dconv * r[HALO:], axis=0, keepdims=True), 0.0)
        dcb = jnp.sum(dconv, axis=0, keepdims=True)
        dbo = jnp.sum(dxo, axis=0, keepdims=True)
        dws = jnp.concatenate([d[None] for d in dws], axis=0)
        return [jnp.concatenate([dzu, dzv], axis=1), dconv], [dws, dbs, dglg, dglb, dcw, dcb, dclg, dclb, dbo]
    return fn


def _mix_bwd2_tile(A, B, kw, tile, n_tiles):
    def fn(i, z, dza, dc, dcn, cw_ref):
        dcn = jnp.where(i < n_tiles - 1, dcn, 0.0)
        cat = jnp.concatenate([dc, dcn], axis=0)
        n = tile + HALO
        dhb = jnp.zeros((tile, B), F32)
        for k in range(kw):
            sh = kw - 1 - k
            r = cat if sh == 0 else pltpu.roll(cat, n - sh, 0)
            dhb = dhb + r[:tile] * cw_ref[k:k + 1, :]
        _, vjp_g = jax.vjp(_glu, z[:, 2 * A:2 * A + B], z[:, 2 * A + B:])
        da, dg = vjp_g(dhb)
        dz = jnp.concatenate([dza, da, dg], axis=1)
        return [dz], [jnp.sum(dz, axis=0, keepdims=True)]
    return fn


def _even_consts(p, e, A, B, kw):
    hd = A // A_HEADS
    bsf = jnp.repeat(p['gmlp_b_s'][e].T, hd, axis=1)
    cw = jnp.pad(p['conv_w_full'][e], ((0, HALO - kw), (0, 0)))
    return [('cref', p['gmlp_w_s'][e]), ('const', bsf), ('const', p['gmlp_ln_g'][e].reshape(1, A)),
            ('const', p['gmlp_ln_b'][e].reshape(1, A)), ('cref', cw), ('const', p['conv_b'][e].reshape(1, B)),
            ('const', p['conv_ln_g'][e].reshape(1, B)), ('const', p['conv_ln_b'][e].reshape(1, B))]


def _even_fwd(l, e, x, gm, p, W, tile):
    T, D = x.shape
    w_in, w_out = W['ab_w_in'], W['ab_w_out']
    zs = w_in.shape[2]
    Z = N_CHIPS * zs
    A = p['gmlp_ln_g'].shape[1]
    B = p['conv_b'].shape[1]
    kw = p['conv_w_full'].shape[1]
    tm = _pick(T, TM_BIG)
    h = _norm_fwd(f"mix{l}_norm", x, gm, tile)

    def in_fn(refs, outs):
        outs[0][...] = _dot(refs[0][...], refs[1][...], 'nn') + refs[2][...]

    z = _mm1(f"mix{l}_in", (N_CHIPS, T // tm),
             [(h, _bs((tm, D), lambda j, i: (i, 0))), (w_in, _bs((None, D, zs), lambda j, i: (j, 0, 0))),
              (p['ab_b_in'][e].reshape(1, Z), _bs((1, zs), lambda j, i: (0, j)))],
             [(_sds((T, Z), F32), _bs((tm, zs), lambda j, i: (i, j)))], in_fn)[0]
    consts = _even_consts(p, e, A, B, kw)
    ycat = _rowwise(f"mix{l}_mid", _mix_fwd_tile(A, B, kw, tile), [('row', z), ('prev', z)] + consts,
                    [(A + B, BF16)], [], tile)[0]
    xo = _proj_rows(f"mix{l}_out", ycat, w_out, F32, extras=(x, p['ab_b_out'][e].reshape(1, D)),
                    epilogue=lambda acc, ex: ex[0] + acc + ex[1])
    return xo, (x, h, z, ycat)


def _even_bwd(l, e, dxo, dxb, saved, gm, p, W, G, tile):
    x, h, z, ycat = saved
    T, D = x.shape
    w_in, w_out = W['ab_w_in'], W['ab_w_out']
    zs = w_in.shape[2]
    Z = N_CHIPS * zs
    A = p['gmlp_ln_g'].shape[1]
    B = p['conv_b'].shape[1]
    kw = p['conv_w_full'].shape[1]
    hd = A // A_HEADS
    dycat = _proj_rows_t(f"mix{l}_dycat", [(dxb, w_out)], F32)
    G['ab_w_out'] = _grad_rows(f"mix{l}_dwout", ycat, [dxb])[0]
    consts = _even_consts(p, e, A, B, kw)
    accs = [(A_HEADS, GMLP_BLOCK, GMLP_BLOCK), (GMLP_BLOCK, A), (1, A), (1, A), (HALO, B), (1, B), (1, B), (1, B),
            (1, D)]
    dza, dconv, dws, dbs, dglg, dglb, dcw, dcb, dclg, dclb, dbo = _rowwise(
        f"mix{l}_mid_bwd1", _mix_bwd1_tile(A, B, kw, tile),
        [('row', z), ('prev', z), ('row', dycat), ('row', dxo)] + consts, [(2 * A, F32), (B, F32)], accs, tile)
    dz, dbin = _rowwise(f"mix{l}_mid_bwd2", _mix_bwd2_tile(A, B, kw, tile, T // tile),
                        [('row', z), ('row', dza), ('row', dconv), ('next', dconv), consts[4]],
                        [(Z, BF16)], [(1, Z)], tile)
    tmd = _pick(D, TM_BIG)

    def dwin_fn(refs, outs):
        outs[0][...] = _dot(refs[0][...], refs[1][...], 'tn').astype(BF16)

    G['ab_w_in'] = _mm1(f"mix{l}_dwin", (N_CHIPS, D // tmd),
                        [(h, _bs((T, tmd), lambda j, i: (0, i))), (dz, _bs((T, zs), lambda j, i: (0, j)))],
                        [(_sds(w_in.shape, BF16), _bs((None, tmd, zs), lambda j, i: (j, i, 0)))], dwin_fn)[0]
    tm, tn = _pick(T, TM_MID), _pick(D, TN_SMALL)

    def dh_fn(refs, outs):
        acc = None
        for j in range(N_CHIPS):
            t = _dot(refs[0][:, j * zs:(j + 1) * zs], refs[1][j], 'nt')
            acc = t if acc is None else acc + t
        outs[0][...] = acc

    dh = _mm1(f"mix{l}_dh", (T // tm, D // tn),
              [(dz, _bs((tm, Z), lambda i, j: (i, 0))), (w_in, _bs((N_CHIPS, tn, zs), lambda i, j: (0, j, 0)))],
              [(_sds((T, D), F32), _bs((tm, tn), lambda i, j: (i, j)))], dh_fn)[0]
    dx, dxb2, dgm = _norm_bwd(f"mix{l}_norm_bwd", x, gm, dh, dxo, tile)
    small = {'ab_b_in': dbin.reshape(Z), 'gmlp_w_s': dws, 'gmlp_b_s': dbs.reshape(GMLP_BLOCK, A_HEADS, hd).sum(-1).T,
             'gmlp_ln_g': dglg.reshape(A), 'gmlp_ln_b': dglb.reshape(A), 'conv_w': dcw[:kw], 'conv_b': dcb.reshape(B),
             'conv_ln_g': dclg.reshape(B), 'conv_ln_b': dclb.reshape(B), 'ab_b_out': dbo.reshape(D)}
    return dx, dxb2, dgm, small


def _pool_counts(t, cg):
    return jnp.concatenate([jnp.broadcast_to(jnp.minimum(t + 1, w).astype(F32), (t.shape[0], cg))
                            for w in POOL_WINDOWS], axis=1)


def _window_sums(cat, cg, back):
    n = cat.shape[0]
    outs = []
    for gi, w in enumerate(POOL_WINDOWS):
        s = cat[:, gi * cg:(gi + 1) * cg]
        step = 1
        while step < w:
            s = s + pltpu.roll(s, step if back else n - step, 0)
            step *= 2
        outs.append(s)
    return jnp.concatenate(outs, axis=1)


def _pool_fwd_tile(D, tile):
    cg = D // len(POOL_WINDOWS)

    def fn(i, x, xp, g):
        h = _rms(x, g)
        hp = jnp.where(i > 0, _rms(xp, g), 0.0)
        sums = _window_sums(jnp.concatenate([hp, h], axis=0), cg, True)[HALO:]
        return [sums / _pool_counts(_row_ids(i, tile, tile), cg) - h], []
    return fn


def _pool_bwd_tile(D, tile, n_tiles):
    cg = D // len(POOL_WINDOWS)

    def fn(i, dd, ddn, x, dxo, g):
        e = dd / _pool_counts(_row_ids(i, tile, tile), cg)
        en = jnp.where(i < n_tiles - 1, ddn / _pool_counts(_row_ids(i + 1, tile, HALO), cg), 0.0)
        dh = _window_sums(jnp.concatenate([e, en], axis=0), cg, False)[:tile] - dd
        _, vjp = jax.vjp(_rms, x, g)
        dx, dg = vjp(dh)
        return [dxo + dx, dxo + dx], [dg]
    return fn


def _odd_fwd(l, o, x, gm, p, W, tile):
    T, D = x.shape
    wc = W['pool_w']
    cg = wc.shape[2]
    cs = cg // N_CHIPS
    ng = len(POOL_WINDOWS)
    tm = _pick(T, TM)
    d = _rowwise(f"mix{l}_pool", _pool_fwd_tile(D, tile), [('row', x), ('prev', x), ('const', gm.reshape(1, D))],
                 [(D, BF16)], [], tile)[0]
    gspec = _bs((tm, cg), lambda i, j, k: (i, j))
    vspec = _bs((1, cg), lambda i, j, k: (0, j))

    def epi(accs, ex):
        pre = accs[0] + ex[0]
        return [ex[2] + pre * ex[1], pre]

    xo, pre = _mm(f"mix{l}_poolmm", (T // tm, ng, N_CHIPS),
                  [(d, _bs((tm, cs), lambda i, j, k: (i, j * N_CHIPS + k))),
                   (wc, _bs((None, cs, cg), lambda i, j, k: (k, j, 0))),
                   (p['pool_b_full'][o].reshape(1, D), vspec), (p['pool_scale_full'][o].reshape(1, D), vspec),
                   (x, gspec)],
                  [(0, 1, 'nn', 0)], [(_sds((T, D), F32), gspec)] * 2, [(tm, cg)], epi, extras=(2, 3, 4))
    return xo, (x, d, pre)


def _odd_bwd(l, o, dxo, dxb, saved, gm, p, W, G, tile):
    x, d, pre = saved
    T, D = x.shape
    wc = W['pool_w']
    cg = wc.shape[2]
    cs = cg // N_CHIPS
    ng = len(POOL_WINDOWS)
    tm, tkt = _pick(T, TM), _pick(T, TK)

    def fn(i, dxv, prev, sc):
        return [dxv * sc], [jnp.sum(dxv * prev, axis=0, keepdims=True), jnp.sum(dxv * sc, axis=0, keepdims=True)]

    do, dscale, dbc = _rowwise(f"mix{l}_pool_bwd1", fn,
                               [('row', dxo), ('row', pre), ('const', p['pool_scale_full'][o].reshape(1, D))],
                               [(D, BF16)], [(1, D), (1, D)], tile)
    nb = ng * N_CHIPS
    dd = _mm(f"mix{l}_pool_dd", (T // tm, nb, 1),
             [(do, _bs((tm, cg), lambda i, j, k: (i, j // N_CHIPS))),
              (wc, _bs((None, cs, cg), lambda i, j, k: (j % N_CHIPS, j // N_CHIPS, 0)))],
             [(0, 1, 'nt', 0)], [(_sds((T, D), F32), _bs((tm, cs), lambda i, j, k: (i, j)))], [(tm, cs)],
             lambda a, _: a)[0]
    G['pool_w'] = _mm(f"mix{l}_pool_dw", (nb, 1, T // tkt),
                      [(d, _bs((tkt, cs), lambda i, j, k: (k, i))), (do, _bs((tkt, cg), lambda i, j, k: (k, i // N_CHIPS)))],
                      [(0, 1, 'tn', 0)],
                      [(_sds(wc.shape, BF16), _bs((None, cs, cg), lambda i, j, k: (i % N_CHIPS, i // N_CHIPS, 0)))],
                      [(cs, cg)], lambda a, _: a)[0]
    dx, dxb2, dgm = _rowwise(f"mix{l}_pool_bwd2", _pool_bwd_tile(D, tile, T // tile),
                             [('row', dd), ('next', dd), ('row', x), ('row', dxo), ('const', gm.reshape(1, D))],
                             [(D, F32), (D, BF16)], [(1, D)], tile)
    small = {'pool_b': dbc.reshape(ng, cg), 'pool_scale': dscale.reshape(D)}
    return dx, dxb2, dgm, small


def _final(x, g, tgt, tile):
    T, D = x.shape

    def fn(i, xv, tv, gv):
        y, vjp = jax.vjp(_rms, xv, gv)
        err = y - tv
        dx, dg = vjp(err / D)
        loss = 0.5 * jnp.sum(jnp.mean(err * err, axis=-1, keepdims=True), axis=0, keepdims=True)
        return [dx, dx], [dg, jnp.broadcast_to(loss, (1, LANES))]

    dx, dxb, dg, loss = _rowwise("final", fn, [('row', x), ('row', tgt), ('const', g.reshape(1, D))],
                                 [(D, F32), (D, BF16)], [(1, D), (1, LANES)], tile)
    return dx, dxb, dg.reshape(D), loss[0, 0]


def _as3d(name, w):
    return w.reshape(w.shape[0], -1, w.shape[-1]) if name == 'pool_w' else w


def kernel(x, mem, norm_ffn1, ffn1_gate, ffn1_up, ffn1_down, norm_mix, ab_w_in, ab_b_in, gmlp_w_s, gmlp_b_s, gmlp_ln_g, gmlp_ln_b, conv_w, conv_b, conv_ln_g, conv_ln_b, ab_w_out, ab_b_out, pool_w, pool_b, pool_scale, norm_xq, norm_xkv, xattn_wq, xattn_wk, xattn_wv, xattn_wo, norm_ffn2, ffn2_gate, ffn2_up, ffn2_down, norm_final, loss_target, m_norm_ffn1, m_ffn1_gate, m_ffn1_up, m_ffn1_down, m_norm_mix, m_ab_w_in, m_ab_b_in, m_gmlp_w_s, m_gmlp_b_s, m_gmlp_ln_g, m_gmlp_ln_b, m_conv_w, m_conv_b, m_conv_ln_g, m_conv_ln_b, m_ab_w_out, m_ab_b_out, m_pool_w, m_pool_b, m_pool_scale, m_norm_xq, m_norm_xkv, m_xattn_wq, m_xattn_wk, m_xattn_wv, m_xattn_wo, m_norm_ffn2, m_ffn2_gate, m_ffn2_up, m_ffn2_down, m_norm_final, v_norm_ffn1, v_ffn1_gate, v_ffn1_up, v_ffn1_down, v_norm_mix, v_ab_w_in, v_ab_b_in, v_gmlp_w_s, v_gmlp_b_s, v_gmlp_ln_g, v_gmlp_ln_b, v_conv_w, v_conv_b, v_conv_ln_g, v_conv_ln_b, v_ab_w_out, v_ab_b_out, v_pool_w, v_pool_b, v_pool_scale, v_norm_xq, v_norm_xkv, v_xattn_wq, v_xattn_wk, v_xattn_wv, v_xattn_wo, v_norm_ffn2, v_ffn2_gate, v_ffn2_up, v_ffn2_down, v_norm_final):
    w = dict(zip(WEIGHTS, [norm_ffn1, ffn1_gate, ffn1_up, ffn1_down, norm_mix, ab_w_in, ab_b_in, gmlp_w_s, gmlp_b_s, gmlp_ln_g, gmlp_ln_b, conv_w, conv_b, conv_ln_g, conv_ln_b, ab_w_out, ab_b_out, pool_w, pool_b, pool_scale, norm_xq, norm_xkv, xattn_wq, xattn_wk, xattn_wv, xattn_wo, norm_ffn2, ffn2_gate, ffn2_up, ffn2_down, norm_final]))
    m = dict(zip(WEIGHTS, [m_norm_ffn1, m_ffn1_gate, m_ffn1_up, m_ffn1_down, m_norm_mix, m_ab_w_in, m_ab_b_in, m_gmlp_w_s, m_gmlp_b_s, m_gmlp_ln_g, m_gmlp_ln_b, m_conv_w, m_conv_b, m_conv_ln_g, m_conv_ln_b, m_ab_w_out, m_ab_b_out, m_pool_w, m_pool_b, m_pool_scale, m_norm_xq, m_norm_xkv, m_xattn_wq, m_xattn_wk, m_xattn_wv, m_xattn_wo, m_norm_ffn2, m_ffn2_gate, m_ffn2_up, m_ffn2_down, m_norm_final]))
    v = dict(zip(WEIGHTS, [v_norm_ffn1, v_ffn1_gate, v_ffn1_up, v_ffn1_down, v_norm_mix, v_ab_w_in, v_ab_b_in, v_gmlp_w_s, v_gmlp_b_s, v_gmlp_ln_g, v_gmlp_ln_b, v_conv_w, v_conv_b, v_conv_ln_g, v_conv_ln_b, v_ab_w_out, v_ab_b_out, v_pool_w, v_pool_b, v_pool_scale, v_norm_xq, v_norm_xkv, v_xattn_wq, v_xattn_wk, v_xattn_wv, v_xattn_wo, v_norm_ffn2, v_ffn2_gate, v_ffn2_up, v_ffn2_down, v_norm_final]))

    xs, mems, tgt = x[0], mem[0], loss_target[0]
    T, D = xs.shape
    L = norm_ffn1.shape[0]
    tile = _pick(T, ROW_TILE)
    cx, cy, cc = _mesh_pos()
    chip = 2 * cx + cy
    w3 = {n: _as3d(n, w[n]) for n in BIG}
    names = [_layer_names(l) for l in range(L)]

    sh_shapes = [w[n].shape for n in SMALL_SHARDED]
    slots = _gather_all("gather_small_shards", _pack([w[n] for n in SMALL_SHARDED]), jnp.zeros((8, LANES), F32))
    per_chip = [_unpack(slots[2 * j], sh_shapes) for j in range(N_CHIPS)]
    full = {n: jnp.concatenate([per_chip[j][k] for j in range(N_CHIPS)], axis=-1) for k, n in enumerate(SMALL_SHARDED)}
    p = dict(w)
    p['conv_w_full'] = full['conv_w'].reshape(full['conv_w'].shape[0], full['conv_w'].shape[1], -1)
    p['pool_b_full'] = full['pool_b']
    p['pool_scale_full'] = full['pool_scale']

    first_ffn = [n for n in names[0] if n.startswith('ffn1')]
    units = {'0a': (0, first_ffn), '0b': (0, [n for n in names[0] if n not in first_ffn])}
    units.update({str(l): (l, names[l]) for l in range(1, L)})
    cast, near, far = {}, {}, {}

    def cast_unit(u, after):
        l, ns = units[u]
        cast[u] = [_cast_slab(f"cast_{n}_{l}", w3[n], _stack_index(n, l), chip, after) for n in ns]

    def start_near(u, after, carry=()):
        slabs = cast.pop(u)
        sends, arrivals, slabs, carry, tok = _split_start(f"gather_near_start_{u}", slabs, after, 2 * len(slabs),
                                                          _gather_near_copies, carry)
        near[u] = (sends, arrivals, slabs)
        return carry, tok

    def start_far(u, after, carry=()):
        sends, arrivals, slabs = near.pop(u)
        slabs = _split_wait(f"gather_near_wait_{u}", slabs, sends, arrivals, after, _gather_near_copies)
        sends, arrivals, slabs, carry, tok = _split_start(f"gather_far_start_{u}", slabs, jnp.zeros((8, LANES), F32),
                                                          2 * len(slabs), _gather_far_copies, carry)
        d2d = _split_start(f"gather_fwd_near_start_{u}", slabs, jnp.zeros((8, LANES), F32), 2 * len(slabs),
                           _forward_copies(_near_slabs), carry)
        far[u] = (sends, arrivals, d2d[0], d2d[1], d2d[2])
        return d2d[3], tok

    def finish_gather(u, after):
        sends, arrivals, d2d_sends, d2d_arrivals, slabs = far.pop(u)
        slabs = _split_wait(f"gather_far_wait_{u}", slabs, sends, arrivals, after, _gather_far_copies)
        slabs = _split_wait(f"gather_fwd_near_wait_{u}", slabs, d2d_sends, d2d_arrivals, after,
                            _forward_copies(_near_slabs))
        return dict(zip(units[u][1], _forward_halves(f"gather_fwd_{u}", slabs, _far_slabs, 1)))

    cast_unit('0a', slots)
    _, tok = start_near('0a', slots)
    for u in units:
        if u != '0a':
            cast_unit(u, tok)
    casts_done = jnp.stack([s[chip, 0, 0] for u in cast for s in cast[u]]).astype(F32)
    _, tok = start_far('0a', casts_done)
    _, tok = start_near('0b', tok)

    saved, Wl = [], []
    xc = xs + tok[0, 0]
    W = finish_gather('0a', xc)
    for l in range(L):
        Wl.append(W)
        s = {}
        xc, s['ffn1'] = _ffn_fwd(f"ffn1_{l}", xc, w['norm_ffn1'][l], W['ffn1_gate'], W['ffn1_up'], W['ffn1_down'], tile)
        if l == 0:
            (xc,), tok = start_far('0b', xc, (xc,))
            if L > 1:
                (xc,), tok = start_near('1', tok, (xc,))
            W.update(finish_gather('0b', xc))
        if l % 2 == 0:
            xc, s['mix'] = _even_fwd(l, l // 2, xc, w['norm_mix'][l], p, W, tile)
        else:
            xc, s['mix'] = _odd_fwd(l, l // 2, xc, w['norm_mix'][l], p, W, tile)
        if l + 1 < L:
            (xc,), tok = start_far(str(l + 1), xc, (xc,))
            if l + 2 < L:
                (xc,), tok = start_near(str(l + 2), tok, (xc,))
        xc, s['xa'] = _attn_fwd(l, xc, mems, w['norm_xq'][l], w['norm_xkv'][l], W, tile)
        xc, s['ffn2'] = _ffn_fwd(f"ffn2_{l}", xc, w['norm_ffn2'][l], W['ffn2_gate'], W['ffn2_up'], W['ffn2_down'], tile)
        saved.append(s)
        if l + 1 < L:
            W = finish_gather(str(l + 1), xc)

    dx, dxb, g_final, loss_local = _final(xc, w['norm_final'], tgt, tile)
    loss = lax.psum(loss_local, ("x", "y", "c"))
    gfull = {n: lax.empty(w3[n].shape, F32) for n in BIG}
    gs = {n: [None] * w[n].shape[0] for n in SMALL if n != 'norm_final'}

    def finish_exchange(pending, after):
        tag, l, ns, sends, arrivals, thru = pending
        thru = _split_wait(f"rs_wait_{tag}", thru, sends, arrivals, after, _exchange_copies)
        parts, lands = thru[:len(ns)], thru[len(ns):]
        for n, part, land in zip(ns, parts, lands):
            gfull[n] = _sum_into(f"rs_sum_{n}_{l}", part, land, gfull[n], _stack_index(n, l), chip, cc)
        idx = [_stack_index(n, l) for n in ns]
        sends, arrivals, thru, _, _ = _split_start(f"rs_join_start_{tag}", [gfull[n] for n in ns],
                                                   jnp.zeros((8, LANES), F32), len(ns), _join_copies(idx))
        gfull.update(zip(ns, thru))
        joins.append((tag, ns, idx, sends, arrivals))

    def wait_joins(after):
        while joins:
            tag, ns, idx, sends, arrivals = joins.pop(0)
            thru = _split_wait(f"rs_join_wait_{tag}", [gfull[n] for n in ns], sends, arrivals, after, _join_copies(idx))
            gfull.update(zip(ns, thru))

    def swap_hook(key):
        def hook(grads, carry):
            lands = [lax.empty((g.shape[0], g.shape[1] // 2, g.shape[2]), BF16) for g in grads]
            sends, arrivals, thru, carry, _ = _split_start(f"rs_swap_start_{key}", list(grads) + lands,
                                                           jnp.zeros((8, LANES), F32), len(grads), _swap_copies, carry)
            swaps[key] = (sends, arrivals, thru[len(grads):])
            return thru[:len(grads)], carry
        return hook

    def start_exchange(tag, l, ns, G, pending, dx, dxb, keep=EXCHANGES_IN_FLIGHT):
        while len(pending) >= max(keep, 1):
            finish_exchange(pending.pop(0), dx)
        ffn = [n for n in ns if n.startswith('ffn')]
        rest = [n for n in ns if n not in ffn]
        key = f"{ffn[0][:4]}_{l}"
        sends, arrivals, lands = swaps.pop(key)
        thru = _split_wait(f"rs_swap_wait_{key}", [G[n] for n in ffn] + lands, sends, arrivals, dx, _swap_copies)
        got = dict(zip(ffn, thru[len(ffn):]))
        G.update(zip(ffn, thru[:len(ffn)]))
        got.update(zip(rest, _swap_halves(f"rs_swap_{tag}", [G[n] for n in rest])))
        grads_g = [G[n] for n in ns]
        parts = [_add_halves(f"rs_add_{n}_{l}", G[n], got[n], cc) for n in ns]
        lands = [lax.empty((3,) + part.shape[1:], BF16) for part in parts]
        sends, arrivals, thru, (dx, dxb), tok = _split_start(
            f"rs_start_{tag}", parts + lands, jnp.zeros((8, LANES), F32), 3 * len(parts), _exchange_copies,
            carry=(dx, dxb))
        pending.append((tag, l, ns, sends, arrivals, thru))
        return dx, dxb, tok

    pending, joins, swaps = [], [], {}
    for l in reversed(range(L)):
        first = [n for n in names[l] if n.startswith(('ffn2', 'xattn'))]
        second = [n for n in names[l] if n not in first]
        s, W, G = saved[l], Wl[l], {}
        dx, dxb, dg, G['ffn2_gate'], G['ffn2_up'], G['ffn2_down'] = _ffn_bwd(
            f"ffn2_{l}", dx, dxb, s['ffn2'], w['norm_ffn2'][l], W['ffn2_gate'], W['ffn2_up'], W['ffn2_down'], tile,
            swap_hook(f"ffn2_{l}"))
        gs['norm_ffn2'][l] = dg.reshape(D)
        dx, dxb, dgq, dgkv = _attn_bwd(l, dx, dxb, s['xa'], mems, w['norm_xq'][l], w['norm_xkv'][l], W, G, tile)
        gs['norm_xq'][l], gs['norm_xkv'][l] = dgq.reshape(D), dgkv.reshape(D)
        dx, dxb, _ = start_exchange(f"a{l}", l, first, G, pending, dx, dxb,
                                    keep=EXCHANGES_IN_FLIGHT if l > 0 else 1)
        if l % 2 == 0:
            dx, dxb, dgm, small = _even_bwd(l, l // 2, dx, dxb, s['mix'], w['norm_mix'][l], p, W, G, tile)
        else:
            dx, dxb, dgm, small = _odd_bwd(l, l // 2, dx, dxb, s['mix'], w['norm_mix'][l], p, W, G, tile)
        for n, val in small.items():
            gs[n][l // 2] = val
        gs['norm_mix'][l] = dgm.reshape(D)
        dx, dxb, dg, G['ffn1_gate'], G['ffn1_up'], G['ffn1_down'] = _ffn_bwd(
            f"ffn1_{l}", dx, dxb, s['ffn1'], w['norm_ffn1'][l], W['ffn1_gate'], W['ffn1_up'], W['ffn1_down'], tile,
            swap_hook(f"ffn1_{l}"))
        gs['norm_ffn1'][l] = dg.reshape(D)
        dx, dxb, tok = start_exchange(f"b{l}", l, second, G, pending, dx, dxb)
    grad_x = dx[None]
    wait_joins(tok)

    small_full = {n: jnp.stack(gs[n]) for n in gs}
    small_full['norm_final'] = g_final
    full_shapes = [small_full[n].shape for n in SMALL]
    packed = _pack([small_full[n] for n in SMALL])
    sg_sends, sg_arrivals, sg_thru, _, _ = _split_start(
        "small_grads_start", [packed, lax.empty((N_DEV,) + packed.shape, F32)], tok, N_DEV - 1, _all_copies)

    def flat2(n, t):
        t3 = _as3d(n, t)
        return t3.reshape(-1, t3.shape[-1])

    early, dep = {}, []
    for n in BIG:
        R = w3[n].shape[1]
        lo = 0 if n in ODD_ONLY else R
        early[n] = _adam_rows(f"adam_early_{n}", flat2(n, w[n]), flat2(n, gfull[n]), flat2(n, m[n]), flat2(n, v[n]),
                              lo, w3[n].shape[0] * R, R, after=tok)
        dep.append(early[n][1][-1, 0])
    dep = jnp.stack(dep)

    packed, slots8 = _split_wait("small_grads_wait", sg_thru, sg_sends, sg_arrivals, dep, _all_copies)
    summed = _sum_slots("sum_small", slots8, packed, 4 * cx + 2 * cy + cc)
    g_small = dict(zip(SMALL, _unpack(summed, full_shapes)))
    for n in SMALL_SHARDED:
        width = w[n].shape[-1]
        g_small[n] = lax.dynamic_slice_in_dim(g_small[n], chip * width, width, axis=g_small[n].ndim - 1).reshape(w[n].shape)

    for group in pending:
        finish_exchange(group, summed)
    wait_joins(summed)
    grads, delta, new_m, new_v = {}, {}, {}, {}
    for n in BIG:
        outs = early[n]
        if n not in ODD_ONLY:
            R = w3[n].shape[1]
            outs = _adam_rows(f"adam_late_{n}", flat2(n, w[n]), flat2(n, gfull[n]), flat2(n, m[n]), flat2(n, v[n]),
                              0, R, R, prev=early[n])
        grads[n], delta[n], new_m[n], new_v[n] = (t.reshape(w[n].shape) for t in outs)
    small_shapes = [w[n].shape for n in SMALL]
    d2, m2, v2 = _adam("adam_small", _pack([w[n] for n in SMALL]), _pack([g_small[n] for n in SMALL]),
                       _pack([m[n] for n in SMALL]), _pack([v[n] for n in SMALL]))
    for n, dn, mn_, vn_ in zip(SMALL, _unpack(d2, small_shapes), _unpack(m2, small_shapes), _unpack(v2, small_shapes)):
        grads[n], delta[n], new_m[n], new_v[n] = g_small[n].reshape(w[n].shape), dn, mn_, vn_

    return (loss, grad_x, *[grads[n] for n in WEIGHTS], *[delta[n] for n in WEIGHTS],
            *[new_m[n] for n in WEIGHTS], *[new_v[n] for n in WEIGHTS])
```

```python
import jax
import jax.numpy as jnp
from jax import lax
from jax.experimental import pallas as pl
from jax.experimental.pallas import tpu as pltpu

F32, BF16 = jnp.float32, jnp.bfloat16
EPS = 1e-6
N_MEM_HEADS = 4
A_HEADS = 8
GMLP_BLOCK = 128
CHUNK = 64
POOL_WINDOWS = (2, 4, 8, 16)
N_CHIPS = 4
N_DEV = 8
HALO = 32
LANES = 128
TM, TN, TK = 512, 1024, 512
TM_BIG, TM_MID = 1024, 512
TN_BIG, TN_MID, TN_SMALL = 1024, 512, 256
EPI_ROWS = 256
EXCHANGES_IN_FLIGHT = 2
ROW_TILE = 256
PACK_ROWS = 512
VMEM_LIMIT = 48 * 1024 * 1024
VMEM_LIMIT_BIG = 56 * 1024 * 1024
ADAM_LR, ADAM_B1, ADAM_B2, ADAM_EPS, ADAM_WD, ADAM_STEP = 0.001, 0.9, 0.999, 1e-08, 0.01, 10
MESH = pl.DeviceIdType.MESH
HBM = pl.BlockSpec(memory_space=pltpu.HBM)
SEM = pl.BlockSpec(memory_space=pltpu.SEMAPHORE)
ANY = pl.BlockSpec(memory_space=pl.ANY)
EFFECT = pltpu.SideEffectType.DATAFLOW_SIDE_EFFECTING

WEIGHTS = ['norm_ffn1', 'ffn1_gate', 'ffn1_up', 'ffn1_down', 'norm_mix', 'ab_w_in', 'ab_b_in', 'gmlp_w_s',
           'gmlp_b_s', 'gmlp_ln_g', 'gmlp_ln_b', 'conv_w', 'conv_b', 'conv_ln_g', 'conv_ln_b', 'ab_w_out',
           'ab_b_out', 'pool_w', 'pool_b', 'pool_scale', 'norm_xq', 'norm_xkv', 'xattn_wq', 'xattn_wk',
           'xattn_wv', 'xattn_wo', 'norm_ffn2', 'ffn2_gate', 'ffn2_up', 'ffn2_down', 'norm_final']
BIG = ['ffn1_gate', 'ffn1_up', 'ffn1_down', 'ab_w_in', 'ab_w_out', 'pool_w', 'xattn_wq', 'xattn_wk', 'xattn_wv',
       'xattn_wo', 'ffn2_gate', 'ffn2_up', 'ffn2_down']
EVEN_ONLY, ODD_ONLY = ['ab_w_in', 'ab_w_out'], ['pool_w']
SMALL = [n for n in WEIGHTS if n not in BIG]
SMALL_SHARDED = ['conv_w', 'pool_b', 'pool_scale']

NN = (((1,), (0,)), ((), ()))
NT = (((1,), (1,)), ((), ()))
TN_ = (((0,), (0,)), ((), ()))
_DIMS = {'nn': NN, 'nt': NT, 'tn': TN_}


def _pick(n, pref, unit=LANES):
    if n <= pref:
        return n
    t = (pref // unit) * unit
    while t >= unit:
        if n % t == 0:
            return t
        t -= unit
    return n


def _sds(shape, dtype):
    return jax.ShapeDtypeStruct(tuple(shape), dtype)


def _layer_names(l):
    mix = EVEN_ONLY if l % 2 == 0 else ODD_ONLY
    return ['ffn1_gate', 'ffn1_up', 'ffn1_down'] + mix + ['xattn_wq', 'xattn_wk', 'xattn_wv', 'xattn_wo',
                                                          'ffn2_gate', 'ffn2_up', 'ffn2_down']


def _stack_index(name, l):
    return l // 2 if name in EVEN_ONLY + ODD_ONLY else l


def _mm(name, grid, ins, pairs, outs, acc_shapes, epilogue, extras=()):
    n_in, n_out = len(ins), len(outs)
    nk = grid[2]

    def body(*refs):
        in_refs, out_refs, acc_refs = refs[:n_in], refs[n_in:n_in + n_out], refs[n_in + n_out:]
        k = pl.program_id(2)

        @pl.when(k == 0)
        def _():
            for acc in acc_refs:
                acc[...] = jnp.zeros_like(acc)

        for ai, bi, mode, ci in pairs:
            a = in_refs[ai][...].astype(BF16)
            b = in_refs[bi][...].astype(BF16)
            acc_refs[ci][...] += lax.dot_general(a, b, _DIMS[mode], preferred_element_type=F32)

        @pl.when(k == nk - 1)
        def _():
            res = epilogue([acc[...] for acc in acc_refs], [in_refs[e][...] for e in extras])
            for o, r in zip(out_refs, res):
                o[...] = r.astype(o.dtype)

    return pl.pallas_call(
        body, name=name, grid=grid,
        in_specs=[s for _, s in ins], out_specs=[s for _, s in outs], out_shape=[s for s, _ in outs],
        scratch_shapes=[pltpu.VMEM(s, F32) for s in acc_shapes],
        compiler_params=pltpu.CompilerParams(dimension_semantics=("parallel", "parallel", "arbitrary"),
                                             vmem_limit_bytes=VMEM_LIMIT),
    )(*[a for a, _ in ins])


def _bs(shape, fn):
    return pl.BlockSpec(shape, fn)


def _mm1(name, grid, ins, outs, compute, vmem=None):
    n_in = len(ins)

    def body(*refs):
        compute(refs[:n_in], refs[n_in:])

    return pl.pallas_call(
        body, name=name, grid=grid,
        in_specs=[s for _, s in ins], out_specs=[s for _, s in outs], out_shape=[s for s, _ in outs],
        compiler_params=pltpu.CompilerParams(dimension_semantics=("parallel", "parallel"),
                                             vmem_limit_bytes=vmem or VMEM_LIMIT),
    )(*[a for a, _ in ins])


def _dot(a, b, mode):
    return lax.dot_general(a.astype(BF16), b.astype(BF16), _DIMS[mode], preferred_element_type=F32)


def _row_chunks(rows):
    step = min(rows, EPI_ROWS)
    return [slice(r, r + step) for r in range(0, rows, step)]


def _rowwise(name, fn, ins, row_outs, acc_outs, tile):
    T = next(a.shape[0] for k, a in ins if k == 'row')
    n = T // tile
    per = tile // HALO if tile % HALO == 0 else 1
    last = T // HALO - 1
    in_specs = []
    for kind, a in ins:
        if kind == 'row':
            in_specs.append(pl.BlockSpec((tile, a.shape[1]), lambda i: (i, 0)))
        elif kind == 'prev':
            in_specs.append(pl.BlockSpec((HALO, a.shape[1]), lambda i: (jnp.maximum(i * per - 1, 0), 0)))
        elif kind == 'next':
            in_specs.append(pl.BlockSpec((HALO, a.shape[1]), lambda i: (jnp.minimum((i + 1) * per, last), 0)))
        else:
            in_specs.append(pl.BlockSpec(a.shape, lambda i, nd=a.ndim: (0,) * nd))
    n_in, n_row = len(ins), len(row_outs)
    out_shape = [_sds((T, c), dt) for c, dt in row_outs] + [_sds(s, F32) for s in acc_outs]
    out_specs = [pl.BlockSpec((tile, c), lambda i: (i, 0)) for c, _ in row_outs]
    out_specs += [pl.BlockSpec(s, lambda i, nd=len(s): (0,) * nd) for s in acc_outs]
    kinds = [k for k, _ in ins]

    def body(*refs):
        i = pl.program_id(0)
        vals = [r if k == 'cref' else r[...] for k, r in zip(kinds, refs[:n_in])]
        ro, ao = fn(i, *vals)
        for r, v in zip(refs[n_in:n_in + n_row], ro):
            r[...] = v.astype(r.dtype)
        for r, v in zip(refs[n_in + n_row:], ao):
            @pl.when(i == 0)
            def _(r=r, v=v):
                r[...] = v

            @pl.when(i > 0)
            def _(r=r, v=v):
                r[...] += v

    return pl.pallas_call(
        body, name=name, grid=(n,), in_specs=in_specs, out_specs=out_specs, out_shape=out_shape,
        compiler_params=pltpu.CompilerParams(dimension_semantics=("arbitrary",), vmem_limit_bytes=VMEM_LIMIT),
    )(*[a for _, a in ins])


def _rms(x, g):
    return x * lax.rsqrt(jnp.mean(x * x, axis=-1, keepdims=True) + EPS) * g


def _ln(x, g, b):
    mu = jnp.mean(x, axis=-1, keepdims=True)
    xc = x - mu
    var = jnp.mean(xc * xc, axis=-1, keepdims=True)
    return xc * lax.rsqrt(var + EPS) * g + b


def _gelu(x):
    return 0.5 * x * (1.0 + jnp.tanh(0.7978845608028654 * (x + 0.044715 * (x * x * x))))


def _silu(x):
    return x * jax.nn.sigmoid(x)


def _glu(a, g):
    return a * jax.nn.sigmoid(g)


def _row_ids(i, tile, rows):
    return i * tile + lax.broadcasted_iota(jnp.int32, (rows, 1), 0)


def _mesh_pos():
    return lax.axis_index("x"), lax.axis_index("y"), lax.axis_index("c")


def _other_chips(x, y):
    return [(1 - x, y), (x, 1 - y), (1 - x, 1 - y)]


def _remote(src, dst, send_sems, recv_sems, s, to):
    return pltpu.make_async_remote_copy(src_ref=src, dst_ref=dst, send_sem=send_sems.at[s], recv_sem=recv_sems.at[s],
                                        device_id=to, device_id_type=MESH)


def _gather_near_copies(refs, send_sems, recv_sems):
    x, y, c = _mesh_pos()
    me = 2 * x + y
    out = []
    for t, ref in enumerate(refs):
        rh = ref.shape[1] // 2
        half = pl.ds(c * rh, rh)
        for k, (cx, cy) in enumerate(_other_chips(x, y)[:2]):
            mine, theirs = ref.at[me, half], ref.at[2 * cx + cy, half]
            out.append((_remote(mine, mine, send_sems, recv_sems, 2 * t + k, (cx, cy, c)),
                        _remote(theirs, theirs, send_sems, recv_sems, 2 * t + k, (cx, cy, c))))
    return out


def _gather_far_copies(refs, send_sems, recv_sems):
    x, y, c = _mesh_pos()
    xn, yn, diag = 2 * (1 - x) + y, 2 * x + (1 - y), 2 * (1 - x) + (1 - y)
    out = []
    for t, ref in enumerate(refs):
        rq = ref.shape[1] // 4
        q0, q1 = pl.ds(2 * c * rq, rq), pl.ds((2 * c + 1) * rq, rq)
        out.append((_remote(ref.at[yn, q1], ref.at[yn, q1], send_sems, recv_sems, 2 * t, (1 - x, y, c)),
                    _remote(ref.at[diag, q1], ref.at[diag, q1], send_sems, recv_sems, 2 * t, (1 - x, y, c))))
        out.append((_remote(ref.at[xn, q0], ref.at[xn, q0], send_sems, recv_sems, 2 * t + 1, (x, 1 - y, c)),
                    _remote(ref.at[diag, q0], ref.at[diag, q0], send_sems, recv_sems, 2 * t + 1, (x, 1 - y, c))))
    return out


def _exchange_copies(refs, send_sems, recv_sems):
    x, y, c = _mesh_pos()
    n = len(refs) // 2
    out = []
    for t in range(n):
        part, land = refs[t], refs[n + t]
        for k, (cx, cy) in enumerate(_other_chips(x, y)):
            out.append((_remote(part.at[2 * cx + cy], land.at[k], send_sems, recv_sems, 3 * t + k, (cx, cy, c)),
                        _remote(land.at[k], land.at[k], send_sems, recv_sems, 3 * t + k, (cx, cy, c))))
    return out


def _split_start(name, thru, after, n_sems, copies, carry=()):
    n, nc = len(thru), len(carry)
    both = list(thru) + list(carry)

    def body(*refs):
        outs = refs[n + nc + 1:]
        send_sems, recv_sems, thru_refs, token = outs[0], outs[1], outs[2:2 + n], outs[2 + n + nc]
        for send, _ in copies(thru_refs, send_sems, recv_sems):
            send.start()
        token[...] = jnp.zeros_like(token)

    res = pl.pallas_call(
        body, name=name,
        out_shape=(pltpu.SemaphoreType.DMA((n_sems,)), pltpu.SemaphoreType.DMA((n_sems,)),
                   *[pltpu.HBM(b.shape, b.dtype) for b in both], _sds((8, LANES), F32)),
        in_specs=[HBM] * (n + nc) + [ANY],
        out_specs=(SEM, SEM, *[HBM] * (n + nc), pl.BlockSpec(memory_space=pltpu.VMEM)),
        input_output_aliases={i: 2 + i for i in range(n + nc)},
        compiler_params=pltpu.CompilerParams(has_side_effects=EFFECT),
    )(*[pltpu.with_memory_space_constraint(b, pltpu.HBM) for b in both], after)
    return res[0], res[1], list(res[2:2 + n]), list(res[2 + n:2 + n + nc]), res[2 + n + nc]


def _split_wait(name, thru, send_sems, recv_sems, after, copies):
    n = len(thru)

    def body(*refs):
        sends, recvs, outs = refs[n], refs[n + 1], refs[n + 3:]
        for send, arrival in copies(outs, sends, recvs):
            send.wait_send()
            arrival.wait_recv()

    res = pl.pallas_call(
        body, name=name, out_shape=tuple(pltpu.HBM(b.shape, b.dtype) for b in thru),
        in_specs=[HBM] * n + [SEM, SEM, ANY], out_specs=tuple([HBM] * n),
        input_output_aliases={i: i for i in range(n)},
        compiler_params=pltpu.CompilerParams(has_side_effects=EFFECT),
    )(*thru, send_sems, recv_sems, after)
    return list(res)


def _near_slabs(x, y):
    return [2 * (1 - x) + y, 2 * x + (1 - y)]


def _far_slabs(x, y):
    return [2 * (1 - x) + (1 - y)]


def _forward_copies(slabs_of):
    def copies(refs, send_sems, recv_sems):
        x, y, c = _mesh_pos()
        slabs = slabs_of(x, y)
        out = []
        for t, ref in enumerate(refs):
            rh = ref.shape[1] // 2
            mine, other = pl.ds(c * rh, rh), pl.ds((1 - c) * rh, rh)
            for k, j in enumerate(slabs):
                s = len(slabs) * t + k
                out.append((_remote(ref.at[j, mine], ref.at[j, mine], send_sems, recv_sems, s, (x, y, 1 - c)),
                            _remote(ref.at[j, other], ref.at[j, other], send_sems, recv_sems, s, (x, y, 1 - c))))
        return out
    return copies


def _forward_halves(name, bufs, slabs_of, per):
    n = len(bufs)

    def body(*refs):
        outs, send_sems, recv_sems = refs[n:2 * n], refs[2 * n], refs[2 * n + 1]
        pairs = _forward_copies(slabs_of)(outs, send_sems, recv_sems)
        for send, _ in pairs:
            send.start()
        for _, arrival in pairs:
            arrival.wait_recv()
        for send, _ in pairs:
            send.wait_send()

    res = pl.pallas_call(
        body, name=name, out_shape=tuple(_sds(b.shape, b.dtype) for b in bufs),
        in_specs=[HBM] * n, out_specs=tuple([HBM] * n), input_output_aliases={i: i for i in range(n)},
        scratch_shapes=[pltpu.SemaphoreType.DMA((per * n,)), pltpu.SemaphoreType.DMA((per * n,))],
    )(*bufs)
    return list(res)


def _swap_halves(name, gs):
    n = len(gs)

    def body(*refs):
        ins, outs, send_sems, recv_sems = refs[:n], refs[n:2 * n], refs[2 * n], refs[2 * n + 1]
        x, y, c = _mesh_pos()
        cps = []
        for t, (g_ref, o_ref) in enumerate(zip(ins, outs)):
            rh = g_ref.shape[1] // 2
            cp = _remote(g_ref.at[:, pl.ds((1 - c) * rh, rh)], o_ref, send_sems, recv_sems, t, (x, y, 1 - c))
            cp.start()
            cps.append(cp)
        for cp in cps:
            cp.wait_recv()
        for cp in cps:
            cp.wait_send()

    res = pl.pallas_call(
        body, name=name, out_shape=tuple(_sds((g.shape[0], g.shape[1] // 2, g.shape[2]), g.dtype) for g in gs),
        in_specs=[HBM] * n, out_specs=tuple([HBM] * n),
        scratch_shapes=[pltpu.SemaphoreType.DMA((n,)), pltpu.SemaphoreType.DMA((n,))],
    )(*gs)
    return list(res)


def _swap_copies(refs, send_sems, recv_sems):
    x, y, c = _mesh_pos()
    n = len(refs) // 2
    out = []
    for t in range(n):
        g_ref, land = refs[t], refs[n + t]
        rh = g_ref.shape[1] // 2
        out.append((_remote(g_ref.at[:, pl.ds((1 - c) * rh, rh)], land, send_sems, recv_sems, t, (x, y, 1 - c)),
                    _remote(land, land, send_sems, recv_sems, t, (x, y, 1 - c))))
    return out


def _join_copies(idx):
    def copies(refs, send_sems, recv_sems):
        x, y, c = _mesh_pos()
        out = []
        for t, ref in enumerate(refs):
            rh = ref.shape[1] // 2
            mine, theirs = ref.at[idx[t], pl.ds(c * rh, rh)], ref.at[idx[t], pl.ds((1 - c) * rh, rh)]
            out.append((_remote(mine, mine, send_sems, recv_sems, t, (x, y, 1 - c)),
                        _remote(theirs, theirs, send_sems, recv_sems, t, (x, y, 1 - c))))
        return out
    return copies


def _gather_all(name, buf, after):
    def body(b_ref, after_ref, out_ref, send_sems, recv_sems, local_sem):
        x, y, c = _mesh_pos()
        me = 4 * x + 2 * y + c
        local = pltpu.make_async_copy(b_ref, out_ref.at[me], local_sem)
        local.start()
        peers = []
        for k in range(1, N_DEV):
            peers.append((1 - x if k & 4 else x, 1 - y if k & 2 else y, 1 - c if k & 1 else c))
        sends = []
        for k, peer in enumerate(peers):
            cp = _remote(b_ref, out_ref.at[me], send_sems, recv_sems, k, peer)
            cp.start()
            sends.append(cp)
        for k, (px, py, pc) in enumerate(peers):
            slot = out_ref.at[4 * px + 2 * py + pc]
            _remote(slot, slot, send_sems, recv_sems, k, (px, py, pc)).wait_recv()
        for cp in sends:
            cp.wait_send()
        local.wait()

    return pl.pallas_call(
        body, name=name, out_shape=_sds((N_DEV,) + buf.shape, buf.dtype), in_specs=[HBM, ANY], out_specs=HBM,
        scratch_shapes=[pltpu.SemaphoreType.DMA((N_DEV - 1,)), pltpu.SemaphoreType.DMA((N_DEV - 1,)),
                        pltpu.SemaphoreType.DMA],
    )(buf, after)


def _scalars(*vals):
    return jnp.stack([jnp.asarray(v, jnp.int32) for v in vals])


def _cast_slab(name, w3, li, chip, after):
    _, R, C = w3.shape
    tr = _pick(R, ROW_TILE, 16)

    def body(s_ref, w_ref, after_ref, o_ref):
        o_ref[...] = w_ref[...].astype(o_ref.dtype)

    grid_spec = pltpu.PrefetchScalarGridSpec(
        num_scalar_prefetch=1, grid=(R // tr,),
        in_specs=[pl.BlockSpec((None, tr, C), lambda r, s: (li, r, 0)), ANY],
        out_specs=pl.BlockSpec((None, tr, C), lambda r, s: (s[0], r, 0)))
    return pl.pallas_call(
        body, name=name, grid_spec=grid_spec, out_shape=_sds((N_CHIPS, R, C), BF16),
        compiler_params=pltpu.CompilerParams(dimension_semantics=("arbitrary",), vmem_limit_bytes=VMEM_LIMIT),
    )(_scalars(chip), w3, after)


def _add_halves(name, g, recv, c):
    _, R, C = g.shape
    rh = R // 2
    tr = _pick(rh, 512, 16)
    nr = rh // tr

    def body(s_ref, g_ref, a_ref, o_ref):
        o_ref[...] = (g_ref[...].astype(F32) + a_ref[...].astype(F32)).astype(o_ref.dtype)

    blk = (None, tr, C)
    grid_spec = pltpu.PrefetchScalarGridSpec(
        num_scalar_prefetch=1, grid=(N_CHIPS, nr),
        in_specs=[pl.BlockSpec(blk, lambda j, r, s: (j, s[0] * nr + r, 0)),
                  pl.BlockSpec(blk, lambda j, r, s: (j, r, 0))],
        out_specs=pl.BlockSpec(blk, lambda j, r, s: (j, r, 0)))
    return pl.pallas_call(
        body, name=name, grid_spec=grid_spec, out_shape=_sds((N_CHIPS, rh, C), g.dtype),
        compiler_params=pltpu.CompilerParams(dimension_semantics=("arbitrary",) * 2, vmem_limit_bytes=VMEM_LIMIT),
    )(_scalars(c), g, recv)


def _sum_into(name, p, recv, gfull, li, chip, c):
    _, rh, C = p.shape
    tr = _pick(rh, 512, 16)
    nr = rh // tr

    def body(s_ref, p_ref, r_ref, g_ref, o_ref):
        acc = p_ref[...].astype(F32)
        for k in range(3):
            acc = acc + r_ref[k].astype(F32)
        o_ref[...] = acc

    grid_spec = pltpu.PrefetchScalarGridSpec(
        num_scalar_prefetch=1, grid=(nr,),
        in_specs=[pl.BlockSpec((None, tr, C), lambda r, s: (s[0], r, 0)),
                  pl.BlockSpec((3, tr, C), lambda r, s: (0, r, 0)), HBM],
        out_specs=pl.BlockSpec((None, tr, C), lambda r, s: (li, s[1] * nr + r, 0)))
    return pl.pallas_call(
        body, name=name, grid_spec=grid_spec, out_shape=_sds(gfull.shape, F32), input_output_aliases={3: 0},
        compiler_params=pltpu.CompilerParams(dimension_semantics=("arbitrary",), vmem_limit_bytes=VMEM_LIMIT),
    )(_scalars(chip, c), p, recv, gfull)


def _all_copies(refs, send_sems, recv_sems):
    b_ref, out_ref = refs
    x, y, c = _mesh_pos()
    me = 4 * x + 2 * y + c
    out = []
    for k in range(1, N_DEV):
        px, py, pc = 1 - x if k & 4 else x, 1 - y if k & 2 else y, 1 - c if k & 1 else c
        slot = out_ref.at[4 * px + 2 * py + pc]
        out.append((_remote(b_ref, out_ref.at[me], send_sems, recv_sems, k - 1, (px, py, pc)),
                    _remote(slot, slot, send_sems, recv_sems, k - 1, (px, py, pc))))
    return out


def _sum_slots(name, slots, own, me):
    _, n, _ = slots.shape

    def body(s_ref, b_ref, own_ref, o_ref):
        acc = None
        for k in range(N_DEV):
            term = jnp.where(s_ref[0] == k, own_ref[...], b_ref[k])
            acc = term if acc is None else acc + term
        o_ref[...] = acc

    grid_spec = pltpu.PrefetchScalarGridSpec(
        num_scalar_prefetch=1, grid=(n // PACK_ROWS,),
        in_specs=[pl.BlockSpec((N_DEV, PACK_ROWS, LANES), lambda i, s: (0, i, 0)),
                  pl.BlockSpec((PACK_ROWS, LANES), lambda i, s: (i, 0))],
        out_specs=pl.BlockSpec((PACK_ROWS, LANES), lambda i, s: (i, 0)))
    return pl.pallas_call(body, name=name, grid_spec=grid_spec, out_shape=_sds((n, LANES), F32))(_scalars(me), slots, own)


def _adam_tile(i, w, g, m, v):
    m = ADAM_B1 * m + (1.0 - ADAM_B1) * g
    v = ADAM_B2 * v + (1.0 - ADAM_B2) * (g * g)
    m_hat = m / (1.0 - ADAM_B1 ** ADAM_STEP)
    v_hat = v / (1.0 - ADAM_B2 ** ADAM_STEP)
    delta = -ADAM_LR * (m_hat / (jnp.sqrt(v_hat) + ADAM_EPS) + ADAM_WD * w)
    return [delta, m, v], []


def _adam(name, w, g, m, v):
    rows, C = w.shape
    tile = _pick(rows, ROW_TILE, 8)
    return _rowwise(name, _adam_tile, [('row', w), ('row', g), ('row', m), ('row', v)], [(C, F32)] * 3, [], tile)


def _adam_rows(name, w, g, m, v, lo, hi, unit, prev=None, after=None):
    rows, C = w.shape
    tile = _pick(unit, ROW_TILE, 8)
    first = lo // tile
    spec = pl.BlockSpec((tile, C), lambda i: (i + first, 0))
    ins, in_specs = [w, g, m, v], [spec] * 4
    if prev is not None:
        ins, in_specs = ins + list(prev), in_specs + [ANY] * 4
    if after is not None:
        ins, in_specs = ins + [after], in_specs + [ANY]

    def body(*refs):
        n_in = len(ins)
        outs, _ = _adam_tile(0, *[r[...] for r in refs[:4]])
        refs[n_in][...] = refs[1][...]
        for o, val in zip(refs[n_in + 1:n_in + 4], outs):
            o[...] = val

    return pl.pallas_call(
        body, name=name, grid=((hi - lo) // tile,), in_specs=in_specs, out_specs=[spec] * 4,
        out_shape=[_sds((rows, C), F32)] * 4,
        input_output_aliases={4 + k: k for k in range(4)} if prev is not None else {},
        compiler_params=pltpu.CompilerParams(dimension_semantics=("arbitrary",), vmem_limit_bytes=VMEM_LIMIT),
    )(*ins)


def _pack(arrs):
    flat = jnp.concatenate([a.reshape(-1).astype(F32) for a in arrs])
    unit = PACK_ROWS * LANES
    n = -(-flat.shape[0] // unit) * unit
    return jnp.pad(flat, (0, n - flat.shape[0])).reshape(-1, LANES)


def _unpack(buf, shapes):
    flat = buf.reshape(-1)
    out, off = [], 0
    for s in shapes:
        n = 1
        for d in s:
            n *= d
        out.append(flat[off:off + n].reshape(s))
        off += n
    return out


def _norm_fwd(name, x, g, tile):
    D = x.shape[1]
    return _rowwise(name, lambda i, xv, gv: ([_rms(xv, gv)], []), [('row', x), ('const', g.reshape(1, D))],
                    [(D, BF16)], [], tile)[0]


def _norm_bwd(name, x, g, dh, dxo, tile):
    D = x.shape[1]
    if dxo is None:
        def fn(i, xv, dhv, gv):
            _, vjp = jax.vjp(_rms, xv, gv)
            return [], [vjp(dhv)[1]]
        return _rowwise(name, fn, [('row', x), ('row', dh), ('const', g.reshape(1, D))], [], [(1, D)], tile)[0]

    def fn(i, xv, dhv, dxv, gv):
        _, vjp = jax.vjp(_rms, xv, gv)
        dx, dg = vjp(dhv)
        return [dxv + dx, dxv + dx], [dg]
    return _rowwise(name, fn, [('row', x), ('row', dh), ('row', dxo), ('const', g.reshape(1, D))],
                    [(D, F32), (D, BF16)], [(1, D)], tile)


def _ffn_fwd(tag, x, g, wg, wu, wd, tile):
    T, D = x.shape
    fs = wg.shape[2]
    F = N_CHIPS * fs
    tm, tn = _pick(T, TM_BIG), _pick(D, TN_SMALL)
    h = _norm_fwd(f"{tag}_norm", x, g, tile)
    hspec = _bs((tm, D), lambda j, i: (i, 0))
    wspec = _bs((None, D, fs), lambda j, i: (j, 0, 0))
    ospec = _bs((tm, fs), lambda j, i: (i, j))

    def gate(ins, outs):
        outs[0][...] = _dot(ins[0][...], ins[1][...], 'nn').astype(BF16)

    a = _mm1(f"{tag}_gate", (N_CHIPS, T // tm), [(h, hspec), (wg, wspec)], [(_sds((T, F), BF16), ospec)], gate)[0]

    def up(ins, outs):
        bv = _dot(ins[0][...], ins[1][...], 'nn')
        for rows in _row_chunks(tm):
            bb = bv[rows]
            outs[0][rows, :] = bb.astype(BF16)
            outs[1][rows, :] = (_silu(ins[2][rows, :].astype(F32)) * bb).astype(BF16)

    b, s = _mm1(f"{tag}_up", (N_CHIPS, T // tm), [(h, hspec), (wu, wspec), (a, ospec)],
                [(_sds((T, F), BF16), ospec)] * 2, up)

    def down(ins, outs):
        outs[0][...] = ins[2][...] + 0.5 * _dot(ins[0][...], ins[1][...].reshape(F, tn), 'nn')

    xspec = _bs((tm, tn), lambda i, j: (i, j))
    xo = _mm1(f"{tag}_down", (T // tm, D // tn),
              [(s, _bs((tm, F), lambda i, j: (i, 0))), (wd, _bs((N_CHIPS, fs, tn), lambda i, j: (0, 0, j))), (x, xspec)],
              [(_sds((T, D), F32), xspec)], down)[0]
    return xo, (x, h, a, b, s)


def _ffn_bwd(tag, dxo, dxb, saved, g, wg, wu, wd, tile, on_grads=None):
    x, h, a, b, s = saved
    T, D = x.shape
    fs = wg.shape[2]
    F = N_CHIPS * fs
    tm = _pick(T, TM_BIG)
    tspec = _bs((tm, fs), lambda j, i: (i, j))

    def ds_fn(ins, outs):
        d = _dot(ins[0][...], ins[1][...], 'nt')
        for rows in _row_chunks(tm):
            ds = 0.5 * d[rows]
            av, bv = ins[2][rows, :].astype(F32), ins[3][rows, :].astype(F32)
            sig = jax.nn.sigmoid(av)
            outs[0][rows, :] = (ds * bv * (sig * (1.0 + av * (1.0 - sig)))).astype(BF16)
            outs[1][rows, :] = (ds * (av * sig)).astype(BF16)

    da, db = _mm1(f"{tag}_ds", (N_CHIPS, T // tm),
                  [(dxb, _bs((tm, D), lambda j, i: (i, 0))), (wd, _bs((None, fs, D), lambda j, i: (j, 0, 0))),
                   (a, tspec), (b, tspec)], [(_sds((T, F), BF16), tspec)] * 2, ds_fn, vmem=VMEM_LIMIT_BIG)

    tn = _pick(D, TN_BIG)

    def dwd_fn(ins, outs):
        outs[0][...] = (0.5 * _dot(ins[0][...], ins[1][...], 'tn')).astype(BF16)

    gd = _mm1(f"{tag}_dwd", (N_CHIPS, D // tn),
              [(s, _bs((T, fs), lambda i, j: (0, i))), (dxb, _bs((T, tn), lambda i, j: (0, j)))],
              [(_sds(wd.shape, BF16), _bs((None, fs, tn), lambda i, j: (i, 0, j)))], dwd_fn)[0]

    tmd = _pick(D, TM_BIG)

    def dw_fn(ins, outs):
        outs[0][...] = _dot(ins[0][...], ins[1][...], 'tn').astype(BF16)

    def dw(name, dy):
        return _mm1(name, (N_CHIPS, D // tmd),
                    [(h, _bs((T, tmd), lambda j, i: (0, i))), (dy, _bs((T, fs), lambda j, i: (0, j)))],
                    [(_sds(wg.shape, BF16), _bs((None, tmd, fs), lambda j, i: (j, i, 0)))], dw_fn)[0]

    gg, gu = dw(f"{tag}_dwg", da), dw(f"{tag}_dwu", db)
    if on_grads is not None:
        (gg, gu, gd), (da, db) = on_grads([gg, gu, gd], (da, db))

    tmb, tnb = _pick(T, TM_BIG), _pick(D, TN_BIG)
    aspec = _bs((tmb, fs), lambda i, j, k: (i, k))
    wtspec = _bs((None, tnb, fs), lambda i, j, k: (k, j, 0))
    dh = _mm(f"{tag}_dh", (T // tmb, D // tnb, N_CHIPS), [(da, aspec), (wg, wtspec), (db, aspec), (wu, wtspec)],
             [(0, 1, 'nt', 0), (2, 3, 'nt', 0)], [(_sds((T, D), F32), _bs((tmb, tnb), lambda i, j, k: (i, j)))],
             [(tmb, tnb)], lambda accs, _: accs)[0]
    dx, dxb2, dg = _norm_bwd(f"{tag}_norm_bwd", x, g, dh, dxo, tile)
    return dx, dxb2, dg, gg, gu, gd


def _proj_rows(name, a, w, out_dtype, extras=(), epilogue=None):
    M, K = a.shape
    ks, N = w.shape[1], w.shape[2]
    tm, tn = _pick(M, TM_BIG), _pick(N, TN_MID)
    ins = [(a, _bs((tm, K), lambda i, j: (i, 0))), (w, _bs((N_CHIPS, ks, tn), lambda i, j: (0, 0, j)))]
    for e in extras:
        if e.shape[0] == 1:
            ins.append((e, _bs((1, tn), lambda i, j: (0, j))))
        else:
            ins.append((e, _bs((tm, tn), lambda i, j: (i, j))))

    def fn(refs, outs):
        acc = _dot(refs[0][...], refs[1][...].reshape(K, tn), 'nn')
        if epilogue is not None:
            acc = epilogue(acc, [r[...] for r in refs[2:]])
        outs[0][...] = acc.astype(out_dtype)

    return _mm1(name, (M // tm, N // tn), ins, [(_sds((M, N), out_dtype), _bs((tm, tn), lambda i, j: (i, j)))], fn)[0]


def _proj_rows_t(name, pairs, out_dtype):
    dy0, w0 = pairs[0]
    M, N = dy0.shape
    ks = w0.shape[1]
    tm = _pick(M, TM_BIG)
    ins = []
    for dy, w in pairs:
        ins.append((dy, _bs((tm, N), lambda i, j: (i, 0))))
        ins.append((w, _bs((None, ks, N), lambda i, j: (j, 0, 0))))

    def fn(refs, outs):
        acc = _dot(refs[0][...], refs[1][...], 'nt')
        for p in range(1, len(pairs)):
            acc = acc + _dot(refs[2 * p][...], refs[2 * p + 1][...], 'nt')
        outs[0][...] = acc.astype(out_dtype)

    return _mm1(name, (M // tm, N_CHIPS), ins,
                [(_sds((M, N_CHIPS * ks), out_dtype), _bs((tm, ks), lambda i, j: (i, j)))], fn)[0]


def _grad_rows(name, a, dys):
    T, K = a.shape
    N = dys[0].shape[1]
    ks = K // N_CHIPS
    tn = _pick(N, TN_BIG)
    ins = [(a, _bs((T, ks), lambda i, j: (0, i)))] + [(dy, _bs((T, tn), lambda i, j: (0, j))) for dy in dys]

    def fn(refs, outs):
        av = refs[0][...]
        for p in range(len(dys)):
            outs[p][...] = _dot(av, refs[1 + p][...], 'tn').astype(BF16)

    gspec = _bs((None, ks, tn), lambda i, j: (i, 0, j))
    return _mm1(name, (N_CHIPS, N // tn), ins, [(_sds((N_CHIPS, ks, N), BF16), gspec)] * len(dys), fn)


def _softmax_rows(s):
    s = s - jnp.max(s, axis=-1, keepdims=True)
    p = jnp.exp(s)
    return p / jnp.sum(p, axis=-1, keepdims=True)


def _attn_fwd_tile(hd, scale):
    def fn(i, q, k, v):
        outs = []
        for h in range(N_MEM_HEADS):
            sl = slice(h * hd, (h + 1) * hd)
            p = _softmax_rows(lax.dot_general(q[:, sl], k[:, sl], NT, preferred_element_type=F32) * scale)
            outs.append(lax.dot_general(p.astype(BF16), v[:, sl], NN, preferred_element_type=F32))
        return [jnp.concatenate(outs, axis=1)], []
    return fn


def _attn_bwd_tile(hd, scale):
    def fn(i, q, do, k, v):
        dqs, dks, dvs = [], [], []
        for h in range(N_MEM_HEADS):
            sl = slice(h * hd, (h + 1) * hd)
            qh, kh, vh, doh = q[:, sl], k[:, sl], v[:, sl], do[:, sl]
            p = _softmax_rows(lax.dot_general(qh, kh, NT, preferred_element_type=F32) * scale)
            dvs.append(lax.dot_general(p.astype(BF16), doh, TN_, preferred_element_type=F32))
            dp = lax.dot_general(doh, vh, NT, preferred_element_type=F32)
            ds = (p * (dp - jnp.sum(dp * p, axis=-1, keepdims=True)) * scale).astype(BF16)
            dqs.append(lax.dot_general(ds, kh, NN, preferred_element_type=F32))
            dks.append(lax.dot_general(ds, qh, TN_, preferred_element_type=F32))
        return [jnp.concatenate(dqs, axis=1)], [jnp.concatenate(dks, axis=1), jnp.concatenate(dvs, axis=1)]
    return fn


def _attn_fwd(l, x, mem, gq, gkv, W, tile):
    T, D = x.shape
    M = mem.shape[0]
    hd = D // N_MEM_HEADS
    hq = _norm_fwd(f"xa{l}_normq", x, gq, tile)
    mn = _norm_fwd(f"xa{l}_normkv", mem, gkv, _pick(M, tile, 16))
    q = _proj_rows(f"xa{l}_q", hq, W['xattn_wq'], BF16)
    k = _proj_rows(f"xa{l}_k", mn, W['xattn_wk'], BF16)
    v = _proj_rows(f"xa{l}_v", mn, W['xattn_wv'], BF16)
    o = _rowwise(f"xa{l}_attn", _attn_fwd_tile(hd, hd ** -0.5), [('row', q), ('const', k), ('const', v)],
                 [(D, BF16)], [], tile)[0]
    xo = _proj_rows(f"xa{l}_o", o, W['xattn_wo'], F32, extras=(x,), epilogue=lambda acc, ex: ex[0] + acc)
    return xo, (x, hq, mn, q, k, v, o)


def _attn_bwd(l, dxo, dxb, saved, mem, gq, gkv, W, G, tile):
    x, hq, mn, q, k, v, o = saved
    T, D = x.shape
    M = mem.shape[0]
    hd = D // N_MEM_HEADS
    do = _proj_rows_t(f"xa{l}_do", [(dxb, W['xattn_wo'])], BF16)
    G['xattn_wo'] = _grad_rows(f"xa{l}_dwo", o, [dxb])[0]
    dq, dk, dv = _rowwise(f"xa{l}_attn_bwd", _attn_bwd_tile(hd, hd ** -0.5),
                          [('row', q), ('row', do), ('const', k), ('const', v)], [(D, BF16)], [(M, D), (M, D)], tile)
    dhq = _proj_rows_t(f"xa{l}_dhq", [(dq, W['xattn_wq'])], F32)
    G['xattn_wq'] = _grad_rows(f"xa{l}_dwq", hq, [dq])[0]
    dmn = _proj_rows_t(f"xa{l}_dmn", [(dk, W['xattn_wk']), (dv, W['xattn_wv'])], F32)
    G['xattn_wk'], G['xattn_wv'] = _grad_rows(f"xa{l}_dwkv", mn, [dk, dv])
    dx, dxb2, dgq = _norm_bwd(f"xa{l}_normq_bwd", x, gq, dhq, dxo, tile)
    dgkv = _norm_bwd(f"xa{l}_normkv_bwd", mem, gkv, dmn, None, _pick(M, tile, 16))
    return dx, dxb2, dgq, dgkv


def _chunk_mask():
    p = lax.broadcasted_iota(jnp.int32, (GMLP_BLOCK, GMLP_BLOCK), 0)
    q = lax.broadcasted_iota(jnp.int32, (GMLP_BLOCK, GMLP_BLOCK), 1)
    return (q // CHUNK) <= (p // CHUNK)


def _spatial_fwd(vn, ws_ref, bsf, mask, hd):
    vb = vn.astype(BF16)
    wsm = [jnp.where(mask, ws_ref[h], 0.0).astype(BF16) for h in range(A_HEADS)]
    rows = []
    for n in range(vn.shape[0] // GMLP_BLOCK):
        blk = vb[n * GMLP_BLOCK:(n + 1) * GMLP_BLOCK]
        cols = [lax.dot_general(wsm[h], blk[:, h * hd:(h + 1) * hd], NN, preferred_element_type=F32)
                for h in range(A_HEADS)]
        rows.append(jnp.concatenate(cols, axis=1) + bsf)
    return jnp.concatenate(rows, axis=0)


def _spatial_bwd(dsp, vn, ws_ref, mask, hd):
    vb, db16 = vn.astype(BF16), dsp.astype(BF16)
    wsm = [jnp.where(mask, ws_ref[h], 0.0).astype(BF16) for h in range(A_HEADS)]
    dws = [jnp.zeros((GMLP_BLOCK, GMLP_BLOCK), F32) for _ in range(A_HEADS)]
    dbs = jnp.zeros((GMLP_BLOCK, vn.shape[1]), F32)
    rows = []
    for n in range(vn.shape[0] // GMLP_BLOCK):
        sl = slice(n * GMLP_BLOCK, (n + 1) * GMLP_BLOCK)
        cols = []
        for h in range(A_HEADS):
            hs = slice(h * hd, (h + 1) * hd)
            cols.append(lax.dot_general(wsm[h], db16[sl, hs], TN_, preferred_element_type=F32))
            dws[h] = dws[h] + lax.dot_general(db16[sl, hs], vb[sl, hs], NT, preferred_element_type=F32)
        rows.append(jnp.concatenate(cols, axis=1))
        dbs = dbs + dsp[sl]
    dws = [jnp.where(mask, d, 0.0) for d in dws]
    return jnp.concatenate(rows, axis=0), dws, dbs


def _conv_taps(cat, cw_ref, kw, tile):
    acc = jnp.zeros((tile, cat.shape[1]), F32)
    for k in range(kw):
        sh = kw - 1 - k
        r = cat if sh == 0 else pltpu.roll(cat, sh, 0)
        acc = acc + r[HALO:] * cw_ref[k:k + 1, :]
    return acc


def _mix_fwd_tile(A, B, kw, tile):
    hd = A // A_HEADS

    def fn(i, z, zp, ws_ref, bsf, glg, glb, cw_ref, cb, clg, clb):
        mask = _chunk_mask()
        u = _gelu(z[:, :A])
        vn = _ln(_gelu(z[:, A:2 * A]), glg, glb)
        ya = u * _spatial_fwd(vn, ws_ref, bsf, mask, hd)
        hb = _glu(z[:, 2 * A:2 * A + B], z[:, 2 * A + B:])
        hp = jnp.where(i > 0, _glu(zp[:, 2 * A:2 * A + B], zp[:, 2 * A + B:]), 0.0)
        conv = _conv_taps(jnp.concatenate([hp, hb], axis=0), cw_ref, kw, tile) + cb
        yb = _silu(_ln(conv, clg, clb))
        return [jnp.concatenate([ya, yb], axis=1)], []
    return fn


def _mix_bwd1_tile(A, B, kw, tile):
    hd = A // A_HEADS

    def fn(i, z, zp, dy, dxo, ws_ref, bsf, glg, glb, cw_ref, cb, clg, clb):
        mask = _chunk_mask()
        dya, dyb = dy[:, :A], dy[:, A:]
        zu, zv = z[:, :A], z[:, A:2 * A]
        u, vjp_u = jax.vjp(_gelu, zu)
        vn, vjp_v = jax.vjp(lambda t, g, b: _ln(_gelu(t), g, b), zv, glg, glb)
        sp = _spatial_fwd(vn, ws_ref, bsf, mask, hd)
        dzu = vjp_u(dya * sp)[0]
        dvn, dws, dbs = _spatial_bwd(dya * u, vn, ws_ref, mask, hd)
        dzv, dglg, dglb = vjp_v(dvn)
        hb = _glu(z[:, 2 * A:2 * A + B], z[:, 2 * A + B:])
        hp = jnp.where(i > 0, _glu(zp[:, 2 * A:2 * A + B], zp[:, 2 * A + B:]), 0.0)
        cat = jnp.concatenate([hp, hb], axis=0)
        conv = _conv_taps(cat, cw_ref, kw, tile) + cb
        _, vjp_c = jax.vjp(lambda t, g, b: _silu(_ln(t, g, b)), conv, clg, clb)
        dconv, dclg, dclb = vjp_c(dyb)
        tap = lax.broadcasted_iota(jnp.int32, (HALO, 1), 0)
        dcw = jnp.zeros((HALO, B), F32)
        for k in range(kw):
            sh = kw - 1 - k
            r = cat if sh == 0 else pltpu.roll(cat, sh, 0)
            dcw = dcw + jnp.where(tap == k, jnp.sum(dconv * r[HALO:], axis=0, keepdims=True), 0.0)
        dcb = jnp.sum(dconv, axis=0, keepdims=True)
        dbo = jnp.sum(dxo, axis=0, keepdims=True)
        dws = jnp.concatenate([d[None] for d in dws], axis=0)
        return [jnp.concatenate([dzu, dzv], axis=1), dconv], [dws, dbs, dglg, dglb, dcw, dcb, dclg, dclb, dbo]
    return fn


def _mix_bwd2_tile(A, B, kw, tile, n_tiles):
    def fn(i, z, dza, dc, dcn, cw_ref):
        dcn = jnp.where(i < n_tiles - 1, dcn, 0.0)
        cat = jnp.concatenate([dc, dcn], axis=0)
        n = tile + HALO
        dhb = jnp.zeros((tile, B), F32)
        for k in range(kw):
            sh = kw - 1 - k
            r = cat if sh == 0 else pltpu.roll(cat, n - sh, 0)
            dhb = dhb + r[:tile] * cw_ref[k:k + 1, :]
        _, vjp_g = jax.vjp(_glu, z[:, 2 * A:2 * A + B], z[:, 2 * A + B:])
        da, dg = vjp_g(dhb)
        dz = jnp.concatenate([dza, da, dg], axis=1)
        return [dz], [jnp.sum(dz, axis=0, keepdims=True)]
    return fn


def _even_consts(p, e, A, B, kw):
    hd = A // A_HEADS
    bsf = jnp.repeat(p['gmlp_b_s'][e].T, hd, axis=1)
    cw = jnp.pad(p['conv_w_full'][e], ((0, HALO - kw), (0, 0)))
    return [('cref', p['gmlp_w_s'][e]), ('const', bsf), ('const', p['gmlp_ln_g'][e].reshape(1, A)),
            ('const', p['gmlp_ln_b'][e].reshape(1, A)), ('cref', cw), ('const', p['conv_b'][e].reshape(1, B)),
            ('const', p['conv_ln_g'][e].reshape(1, B)), ('const', p['conv_ln_b'][e].reshape(1, B))]


def _even_fwd(l, e, x, gm, p, W, tile):
    T, D = x.shape
    w_in, w_out = W['ab_w_in'], W['ab_w_out']
    zs = w_in.shape[2]
    Z = N_CHIPS * zs
    A = p['gmlp_ln_g'].shape[1]
    B = p['conv_b'].shape[1]
    kw = p['conv_w_full'].shape[1]
    tm = _pick(T, TM_BIG)
    h = _norm_fwd(f"mix{l}_norm", x, gm, tile)

    def in_fn(refs, outs):
        outs[0][...] = _dot(refs[0][...], refs[1][...], 'nn') + refs[2][...]

    z = _mm1(f"mix{l}_in", (N_CHIPS, T // tm),
             [(h, _bs((tm, D), lambda j, i: (i, 0))), (w_in, _bs((None, D, zs), lambda j, i: (j, 0, 0))),
              (p['ab_b_in'][e].reshape(1, Z), _bs((1, zs), lambda j, i: (0, j)))],
             [(_sds((T, Z), F32), _bs((tm, zs), lambda j, i: (i, j)))], in_fn)[0]
    consts = _even_consts(p, e, A, B, kw)
    ycat = _rowwise(f"mix{l}_mid", _mix_fwd_tile(A, B, kw, tile), [('row', z), ('prev', z)] + consts,
                    [(A + B, BF16)], [], tile)[0]
    xo = _proj_rows(f"mix{l}_out", ycat, w_out, F32, extras=(x, p['ab_b_out'][e].reshape(1, D)),
                    epilogue=lambda acc, ex: ex[0] + acc + ex[1])
    return xo, (x, h, z, ycat)


def _even_bwd(l, e, dxo, dxb, saved, gm, p, W, G, tile):
    x, h, z, ycat = saved
    T, D = x.shape
    w_in, w_out = W['ab_w_in'], W['ab_w_out']
    zs = w_in.shape[2]
    Z = N_CHIPS * zs
    A = p['gmlp_ln_g'].shape[1]
    B = p['conv_b'].shape[1]
    kw = p['conv_w_full'].shape[1]
    hd = A // A_HEADS
    dycat = _proj_rows_t(f"mix{l}_dycat", [(dxb, w_out)], F32)
    G['ab_w_out'] = _grad_rows(f"mix{l}_dwout", ycat, [dxb])[0]
    consts = _even_consts(p, e, A, B, kw)
    accs = [(A_HEADS, GMLP_BLOCK, GMLP_BLOCK), (GMLP_BLOCK, A), (1, A), (1, A), (HALO, B), (1, B), (1, B), (1, B),
            (1, D)]
    dza, dconv, dws, dbs, dglg, dglb, dcw, dcb, dclg, dclb, dbo = _rowwise(
        f"mix{l}_mid_bwd1", _mix_bwd1_tile(A, B, kw, tile),
        [('row', z), ('prev', z), ('row', dycat), ('row', dxo)] + consts, [(2 * A, F32), (B, F32)], accs, tile)
    dz, dbin = _rowwise(f"mix{l}_mid_bwd2", _mix_bwd2_tile(A, B, kw, tile, T // tile),
                        [('row', z), ('row', dza), ('row', dconv), ('next', dconv), consts[4]],
                        [(Z, BF16)], [(1, Z)], tile)
    tmd = _pick(D, TM_BIG)

    def dwin_fn(refs, outs):
        outs[0][...] = _dot(refs[0][...], refs[1][...], 'tn').astype(BF16)

    G['ab_w_in'] = _mm1(f"mix{l}_dwin", (N_CHIPS, D // tmd),
                        [(h, _bs((T, tmd), lambda j, i: (0, i))), (dz, _bs((T, zs), lambda j, i: (0, j)))],
                        [(_sds(w_in.shape, BF16), _bs((None, tmd, zs), lambda j, i: (j, i, 0)))], dwin_fn)[0]
    tm, tn = _pick(T, TM_MID), _pick(D, TN_SMALL)

    def dh_fn(refs, outs):
        acc = None
        for j in range(N_CHIPS):
            t = _dot(refs[0][:, j * zs:(j + 1) * zs], refs[1][j], 'nt')
            acc = t if acc is None else acc + t
        outs[0][...] = acc

    dh = _mm1(f"mix{l}_dh", (T // tm, D // tn),
              [(dz, _bs((tm, Z), lambda i, j: (i, 0))), (w_in, _bs((N_CHIPS, tn, zs), lambda i, j: (0, j, 0)))],
              [(_sds((T, D), F32), _bs((tm, tn), lambda i, j: (i, j)))], dh_fn)[0]
    dx, dxb2, dgm = _norm_bwd(f"mix{l}_norm_bwd", x, gm, dh, dxo, tile)
    small = {'ab_b_in': dbin.reshape(Z), 'gmlp_w_s': dws, 'gmlp_b_s': dbs.reshape(GMLP_BLOCK, A_HEADS, hd).sum(-1).T,
             'gmlp_ln_g': dglg.reshape(A), 'gmlp_ln_b': dglb.reshape(A), 'conv_w': dcw[:kw], 'conv_b': dcb.reshape(B),
             'conv_ln_g': dclg.reshape(B), 'conv_ln_b': dclb.reshape(B), 'ab_b_out': dbo.reshape(D)}
    return dx, dxb2, dgm, small


def _pool_counts(t, cg):
    return jnp.concatenate([jnp.broadcast_to(jnp.minimum(t + 1, w).astype(F32), (t.shape[0], cg))
                            for w in POOL_WINDOWS], axis=1)


def _window_sums(cat, cg, back):
    n = cat.shape[0]
    outs = []
    for gi, w in enumerate(POOL_WINDOWS):
        s = cat[:, gi * cg:(gi + 1) * cg]
        step = 1
        while step < w:
            s = s + pltpu.roll(s, step if back else n - step, 0)
            step *= 2
        outs.append(s)
    return jnp.concatenate(outs, axis=1)


def _pool_fwd_tile(D, tile):
    cg = D // len(POOL_WINDOWS)

    def fn(i, x, xp, g):
        h = _rms(x, g)
        hp = jnp.where(i > 0, _rms(xp, g), 0.0)
        sums = _window_sums(jnp.concatenate([hp, h], axis=0), cg, True)[HALO:]
        return [sums / _pool_counts(_row_ids(i, tile, tile), cg) - h], []
    return fn


def _pool_bwd_tile(D, tile, n_tiles):
    cg = D // len(POOL_WINDOWS)

    def fn(i, dd, ddn, x, dxo, g):
        e = dd / _pool_counts(_row_ids(i, tile, tile), cg)
        en = jnp.where(i < n_tiles - 1, ddn / _pool_counts(_row_ids(i + 1, tile, HALO), cg), 0.0)
        dh = _window_sums(jnp.concatenate([e, en], axis=0), cg, False)[:tile] - dd
        _, vjp = jax.vjp(_rms, x, g)
        dx, dg = vjp(dh)
        return [dxo + dx, dxo + dx], [dg]
    return fn


def _odd_fwd(l, o, x, gm, p, W, tile):
    T, D = x.shape
    wc = W['pool_w']
    cg = wc.shape[2]
    cs = cg // N_CHIPS
    ng = len(POOL_WINDOWS)
    tm = _pick(T, TM)
    d = _rowwise(f"mix{l}_pool", _pool_fwd_tile(D, tile), [('row', x), ('prev', x), ('const', gm.reshape(1, D))],
                 [(D, BF16)], [], tile)[0]
    gspec = _bs((tm, cg), lambda i, j, k: (i, j))
    vspec = _bs((1, cg), lambda i, j, k: (0, j))

    def epi(accs, ex):
        pre = accs[0] + ex[0]
        return [ex[2] + pre * ex[1], pre]

    xo, pre = _mm(f"mix{l}_poolmm", (T // tm, ng, N_CHIPS),
                  [(d, _bs((tm, cs), lambda i, j, k: (i, j * N_CHIPS + k))),
                   (wc, _bs((None, cs, cg), lambda i, j, k: (k, j, 0))),
                   (p['pool_b_full'][o].reshape(1, D), vspec), (p['pool_scale_full'][o].reshape(1, D), vspec),
                   (x, gspec)],
                  [(0, 1, 'nn', 0)], [(_sds((T, D), F32), gspec)] * 2, [(tm, cg)], epi, extras=(2, 3, 4))
    return xo, (x, d, pre)


def _odd_bwd(l, o, dxo, dxb, saved, gm, p, W, G, tile):
    x, d, pre = saved
    T, D = x.shape
    wc = W['pool_w']
    cg = wc.shape[2]
    cs = cg // N_CHIPS
    ng = len(POOL_WINDOWS)
    tm, tkt = _pick(T, TM), _pick(T, TK)

    def fn(i, dxv, prev, sc):
        return [dxv * sc], [jnp.sum(dxv * prev, axis=0, keepdims=True), jnp.sum(dxv * sc, axis=0, keepdims=True)]

    do, dscale, dbc = _rowwise(f"mix{l}_pool_bwd1", fn,
                               [('row', dxo), ('row', pre), ('const', p['pool_scale_full'][o].reshape(1, D))],
                               [(D, BF16)], [(1, D), (1, D)], tile)
    nb = ng * N_CHIPS
    dd = _mm(f"mix{l}_pool_dd", (T // tm, nb, 1),
             [(do, _bs((tm, cg), lambda i, j, k: (i, j // N_CHIPS))),
              (wc, _bs((None, cs, cg), lambda i, j, k: (j % N_CHIPS, j // N_CHIPS, 0)))],
             [(0, 1, 'nt', 0)], [(_sds((T, D), F32), _bs((tm, cs), lambda i, j, k: (i, j)))], [(tm, cs)],
             lambda a, _: a)[0]
    G['pool_w'] = _mm(f"mix{l}_pool_dw", (nb, 1, T // tkt),
                      [(d, _bs((tkt, cs), lambda i, j, k: (k, i))), (do, _bs((tkt, cg), lambda i, j, k: (k, i // N_CHIPS)))],
                      [(0, 1, 'tn', 0)],
                      [(_sds(wc.shape, BF16), _bs((None, cs, cg), lambda i, j, k: (i % N_CHIPS, i // N_CHIPS, 0)))],
                      [(cs, cg)], lambda a, _: a)[0]
    dx, dxb2, dgm = _rowwise(f"mix{l}_pool_bwd2", _pool_bwd_tile(D, tile, T // tile),
                             [('row', dd), ('next', dd), ('row', x), ('row', dxo), ('const', gm.reshape(1, D))],
                             [(D, F32), (D, BF16)], [(1, D)], tile)
    small = {'pool_b': dbc.reshape(ng, cg), 'pool_scale': dscale.reshape(D)}
    return dx, dxb2, dgm, small


def _final(x, g, tgt, tile):
    T, D = x.shape

    def fn(i, xv, tv, gv):
        y, vjp = jax.vjp(_rms, xv, gv)
        err = y - tv
        dx, dg = vjp(err / D)
        loss = 0.5 * jnp.sum(jnp.mean(err * err, axis=-1, keepdims=True), axis=0, keepdims=True)
        return [dx, dx], [dg, jnp.broadcast_to(loss, (1, LANES))]

    dx, dxb, dg, loss = _rowwise("final", fn, [('row', x), ('row', tgt), ('const', g.reshape(1, D))],
                                 [(D, F32), (D, BF16)], [(1, D), (1, LANES)], tile)
    return dx, dxb, dg.reshape(D), loss[0, 0]


def _as3d(name, w):
    return w.reshape(w.shape[0], -1, w.shape[-1]) if name == 'pool_w' else w


def kernel(x, mem, norm_ffn1, ffn1_gate, ffn1_up, ffn1_down, norm_mix, ab_w_in, ab_b_in, gmlp_w_s, gmlp_b_s, gmlp_ln_g, gmlp_ln_b, conv_w, conv_b, conv_ln_g, conv_ln_b, ab_w_out, ab_b_out, pool_w, pool_b, pool_scale, norm_xq, norm_xkv, xattn_wq, xattn_wk, xattn_wv, xattn_wo, norm_ffn2, ffn2_gate, ffn2_up, ffn2_down, norm_final, loss_target, m_norm_ffn1, m_ffn1_gate, m_ffn1_up, m_ffn1_down, m_norm_mix, m_ab_w_in, m_ab_b_in, m_gmlp_w_s, m_gmlp_b_s, m_gmlp_ln_g, m_gmlp_ln_b, m_conv_w, m_conv_b, m_conv_ln_g, m_conv_ln_b, m_ab_w_out, m_ab_b_out, m_pool_w, m_pool_b, m_pool_scale, m_norm_xq, m_norm_xkv, m_xattn_wq, m_xattn_wk, m_xattn_wv, m_xattn_wo, m_norm_ffn2, m_ffn2_gate, m_ffn2_up, m_ffn2_down, m_norm_final, v_norm_ffn1, v_ffn1_gate, v_ffn1_up, v_ffn1_down, v_norm_mix, v_ab_w_in, v_ab_b_in, v_gmlp_w_s, v_gmlp_b_s, v_gmlp_ln_g, v_gmlp_ln_b, v_conv_w, v_conv_b, v_conv_ln_g, v_conv_ln_b, v_ab_w_out, v_ab_b_out, v_pool_w, v_pool_b, v_pool_scale, v_norm_xq, v_norm_xkv, v_xattn_wq, v_xattn_wk, v_xattn_wv, v_xattn_wo, v_norm_ffn2, v_ffn2_gate, v_ffn2_up, v_ffn2_down, v_norm_final):
    w = dict(zip(WEIGHTS, [norm_ffn1, ffn1_gate, ffn1_up, ffn1_down, norm_mix, ab_w_in, ab_b_in, gmlp_w_s, gmlp_b_s, gmlp_ln_g, gmlp_ln_b, conv_w, conv_b, conv_ln_g, conv_ln_b, ab_w_out, ab_b_out, pool_w, pool_b, pool_scale, norm_xq, norm_xkv, xattn_wq, xattn_wk, xattn_wv, xattn_wo, norm_ffn2, ffn2_gate, ffn2_up, ffn2_down, norm_final]))
    m = dict(zip(WEIGHTS, [m_norm_ffn1, m_ffn1_gate, m_ffn1_up, m_ffn1_down, m_norm_mix, m_ab_w_in, m_ab_b_in, m_gmlp_w_s, m_gmlp_b_s, m_gmlp_ln_g, m_gmlp_ln_b, m_conv_w, m_conv_b, m_conv_ln_g, m_conv_ln_b, m_ab_w_out, m_ab_b_out, m_pool_w, m_pool_b, m_pool_scale, m_norm_xq, m_norm_xkv, m_xattn_wq, m_xattn_wk, m_xattn_wv, m_xattn_wo, m_norm_ffn2, m_ffn2_gate, m_ffn2_up, m_ffn2_down, m_norm_final]))
    v = dict(zip(WEIGHTS, [v_norm_ffn1, v_ffn1_gate, v_ffn1_up, v_ffn1_down, v_norm_mix, v_ab_w_in, v_ab_b_in, v_gmlp_w_s, v_gmlp_b_s, v_gmlp_ln_g, v_gmlp_ln_b, v_conv_w, v_conv_b, v_conv_ln_g, v_conv_ln_b, v_ab_w_out, v_ab_b_out, v_pool_w, v_pool_b, v_pool_scale, v_norm_xq, v_norm_xkv, v_xattn_wq, v_xattn_wk, v_xattn_wv, v_xattn_wo, v_norm_ffn2, v_ffn2_gate, v_ffn2_up, v_ffn2_down, v_norm_final]))

    xs, mems, tgt = x[0], mem[0], loss_target[0]
    T, D = xs.shape
    L = norm_ffn1.shape[0]
    tile = _pick(T, ROW_TILE)
    cx, cy, cc = _mesh_pos()
    chip = 2 * cx + cy
    w3 = {n: _as3d(n, w[n]) for n in BIG}
    names = [_layer_names(l) for l in range(L)]

    sh_shapes = [w[n].shape for n in SMALL_SHARDED]
    slots = _gather_all("gather_small_shards", _pack([w[n] for n in SMALL_SHARDED]), jnp.zeros((8, LANES), F32))
    per_chip = [_unpack(slots[2 * j], sh_shapes) for j in range(N_CHIPS)]
    full = {n: jnp.concatenate([per_chip[j][k] for j in range(N_CHIPS)], axis=-1) for k, n in enumerate(SMALL_SHARDED)}
    p = dict(w)
    p['conv_w_full'] = full['conv_w'].reshape(full['conv_w'].shape[0], full['conv_w'].shape[1], -1)
    p['pool_b_full'] = full['pool_b']
    p['pool_scale_full'] = full['pool_scale']

    first_ffn = [n for n in names[0] if n.startswith('ffn1')]
    units = {'0a': (0, first_ffn), '0b': (0, [n for n in names[0] if n not in first_ffn])}
    units.update({str(l): (l, names[l]) for l in range(1, L)})
    cast, near, far = {}, {}, {}

    def cast_unit(u, after):
        l, ns = units[u]
        cast[u] = [_cast_slab(f"cast_{n}_{l}", w3[n], _stack_index(n, l), chip, after) for n in ns]

    def start_near(u, after, carry=()):
        slabs = cast.pop(u)
        sends, arrivals, slabs, carry, tok = _split_start(f"gather_near_start_{u}", slabs, after, 2 * len(slabs),
                                                          _gather_near_copies, carry)
        near[u] = (sends, arrivals, slabs)
        return carry, tok

    def start_far(u, after, carry=()):
        sends, arrivals, slabs = near.pop(u)
        slabs = _split_wait(f"gather_near_wait_{u}", slabs, sends, arrivals, after, _gather_near_copies)
        sends, arrivals, slabs, carry, tok = _split_start(f"gather_far_start_{u}", slabs, jnp.zeros((8, LANES), F32),
                                                          2 * len(slabs), _gather_far_copies, carry)
        d2d = _split_start(f"gather_fwd_near_start_{u}", slabs, jnp.zeros((8, LANES), F32), 2 * len(slabs),
                           _forward_copies(_near_slabs), carry)
        far[u] = (sends, arrivals, d2d[0], d2d[1], d2d[2])
        return d2d[3], tok

    def finish_gather(u, after):
        sends, arrivals, d2d_sends, d2d_arrivals, slabs = far.pop(u)
        slabs = _split_wait(f"gather_far_wait_{u}", slabs, sends, arrivals, after, _gather_far_copies)
        slabs = _split_wait(f"gather_fwd_near_wait_{u}", slabs, d2d_sends, d2d_arrivals, after,
                            _forward_copies(_near_slabs))
        return dict(zip(units[u][1], _forward_halves(f"gather_fwd_{u}", slabs, _far_slabs, 1)))

    cast_unit('0a', slots)
    _, tok = start_near('0a', slots)
    for u in units:
        if u != '0a':
            cast_unit(u, tok)
    casts_done = jnp.stack([s[chip, 0, 0] for u in cast for s in cast[u]]).astype(F32)
    _, tok = start_far('0a', casts_done)
    _, tok = start_near('0b', tok)

    saved, Wl = [], []
    xc = xs + tok[0, 0]
    W = finish_gather('0a', xc)
    for l in range(L):
        Wl.append(W)
        s = {}
        xc, s['ffn1'] = _ffn_fwd(f"ffn1_{l}", xc, w['norm_ffn1'][l], W['ffn1_gate'], W['ffn1_up'], W['ffn1_down'], tile)
        if l == 0:
            (xc,), tok = start_far('0b', xc, (xc,))
            if L > 1:
                (xc,), tok = start_near('1', tok, (xc,))
            W.update(finish_gather('0b', xc))
        if l % 2 == 0:
            xc, s['mix'] = _even_fwd(l, l // 2, xc, w['norm_mix'][l], p, W, tile)
        else:
            xc, s['mix'] = _odd_fwd(l, l // 2, xc, w['norm_mix'][l], p, W, tile)
        if l + 1 < L:
            (xc,), tok = start_far(str(l + 1), xc, (xc,))
            if l + 2 < L:
                (xc,), tok = start_near(str(l + 2), tok, (xc,))
        xc, s['xa'] = _attn_fwd(l, xc, mems, w['norm_xq'][l], w['norm_xkv'][l], W, tile)
        xc, s['ffn2'] = _ffn_fwd(f"ffn2_{l}", xc, w['norm_ffn2'][l], W['ffn2_gate'], W['ffn2_up'], W['ffn2_down'], tile)
        saved.append(s)
        if l + 1 < L:
            W = finish_gather(str(l + 1), xc)

    dx, dxb, g_final, loss_local = _final(xc, w['norm_final'], tgt, tile)
    loss = lax.psum(loss_local, ("x", "y", "c"))
    gfull = {n: lax.empty(w3[n].shape, F32) for n in BIG}
    gs = {n: [None] * w[n].shape[0] for n in SMALL if n != 'norm_final'}

    def finish_exchange(pending, after):
        tag, l, ns, sends, arrivals, thru = pending
        thru = _split_wait(f"rs_wait_{tag}", thru, sends, arrivals, after, _exchange_copies)
        parts, lands = thru[:len(ns)], thru[len(ns):]
        for n, part, land in zip(ns, parts, lands):
            gfull[n] = _sum_into(f"rs_sum_{n}_{l}", part, land, gfull[n], _stack_index(n, l), chip, cc)
        idx = [_stack_index(n, l) for n in ns]
        sends, arrivals, thru, _, _ = _split_start(f"rs_join_start_{tag}", [gfull[n] for n in ns],
                                                   jnp.zeros((8, LANES), F32), len(ns), _join_copies(idx))
        gfull.update(zip(ns, thru))
        joins.append((tag, ns, idx, sends, arrivals))

    def wait_joins(after, limit=None):
        for _ in range(len(joins) if limit is None else limit):
            tag, ns, idx, sends, arrivals = joins.pop(0)
            thru = _split_wait(f"rs_join_wait_{tag}", [gfull[n] for n in ns], sends, arrivals, after, _join_copies(idx))
            gfull.update(zip(ns, thru))

    def swap_hook(key):
        def hook(grads, carry):
            lands = [lax.empty((g.shape[0], g.shape[1] // 2, g.shape[2]), BF16) for g in grads]
            sends, arrivals, thru, carry, _ = _split_start(f"rs_swap_start_{key}", list(grads) + lands,
                                                           jnp.zeros((8, LANES), F32), len(grads), _swap_copies, carry)
            swaps[key] = (sends, arrivals, thru[len(grads):])
            return thru[:len(grads)], carry
        return hook

    def start_exchange(tag, l, ns, G, pending, dx, dxb, keep=EXCHANGES_IN_FLIGHT):
        while len(pending) >= max(keep, 1):
            finish_exchange(pending.pop(0), dx)
        ffn = [n for n in ns if n.startswith('ffn')]
        rest = [n for n in ns if n not in ffn]
        key = f"{ffn[0][:4]}_{l}"
        sends, arrivals, lands = swaps.pop(key)
        thru = _split_wait(f"rs_swap_wait_{key}", [G[n] for n in ffn] + lands, sends, arrivals, dx, _swap_copies)
        got = dict(zip(ffn, thru[len(ffn):]))
        G.update(zip(ffn, thru[:len(ffn)]))
        got.update(zip(rest, _swap_halves(f"rs_swap_{tag}", [G[n] for n in rest])))
        grads_g = [G[n] for n in ns]
        parts = [_add_halves(f"rs_add_{n}_{l}", G[n], got[n], cc) for n in ns]
        lands = [lax.empty((3,) + part.shape[1:], BF16) for part in parts]
        sends, arrivals, thru, (dx, dxb), tok = _split_start(
            f"rs_start_{tag}", parts + lands, jnp.zeros((8, LANES), F32), 3 * len(parts), _exchange_copies,
            carry=(dx, dxb))
        pending.append((tag, l, ns, sends, arrivals, thru))
        return dx, dxb, tok

    pending, joins, swaps = [], [], {}
    for l in reversed(range(L)):
        first = [n for n in names[l] if n.startswith(('ffn2', 'xattn'))]
        second = [n for n in names[l] if n not in first]
        s, W, G = saved[l], Wl[l], {}
        dx, dxb, dg, G['ffn2_gate'], G['ffn2_up'], G['ffn2_down'] = _ffn_bwd(
            f"ffn2_{l}", dx, dxb, s['ffn2'], w['norm_ffn2'][l], W['ffn2_gate'], W['ffn2_up'], W['ffn2_down'], tile,
            swap_hook(f"ffn2_{l}"))
        gs['norm_ffn2'][l] = dg.reshape(D)
        dx, dxb, dgq, dgkv = _attn_bwd(l, dx, dxb, s['xa'], mems, w['norm_xq'][l], w['norm_xkv'][l], W, G, tile)
        gs['norm_xq'][l], gs['norm_xkv'][l] = dgq.reshape(D), dgkv.reshape(D)
        dx, dxb, _ = start_exchange(f"a{l}", l, first, G, pending, dx, dxb,
                                    keep=EXCHANGES_IN_FLIGHT if l > 0 else 1)
        if l % 2 == 0:
            dx, dxb, dgm, small = _even_bwd(l, l // 2, dx, dxb, s['mix'], w['norm_mix'][l], p, W, G, tile)
        else:
            dx, dxb, dgm, small = _odd_bwd(l, l // 2, dx, dxb, s['mix'], w['norm_mix'][l], p, W, G, tile)
        for n, val in small.items():
            gs[n][l // 2] = val
        gs['norm_mix'][l] = dgm.reshape(D)
        dx, dxb, dg, G['ffn1_gate'], G['ffn1_up'], G['ffn1_down'] = _ffn_bwd(
            f"ffn1_{l}", dx, dxb, s['ffn1'], w['norm_ffn1'][l], W['ffn1_gate'], W['ffn1_up'], W['ffn1_down'], tile,
            swap_hook(f"ffn1_{l}"))
        gs['norm_ffn1'][l] = dg.reshape(D)
        dx, dxb, tok = start_exchange(f"b{l}", l, second, G, pending, dx, dxb)
    grad_x = dx[None]
    wait_joins(tok)

    small_full = {n: jnp.stack(gs[n]) for n in gs}
    small_full['norm_final'] = g_final
    full_shapes = [small_full[n].shape for n in SMALL]
    packed = _pack([small_full[n] for n in SMALL])
    sg_sends, sg_arrivals, sg_thru, _, _ = _split_start(
        "small_grads_start", [packed, lax.empty((N_DEV,) + packed.shape, F32)], tok, N_DEV - 1, _all_copies)

    def flat2(n, t):
        t3 = _as3d(n, t)
        return t3.reshape(-1, t3.shape[-1])

    early, dep = {}, []
    for n in BIG:
        R = w3[n].shape[1]
        lo = 0 if n in ODD_ONLY else R
        early[n] = _adam_rows(f"adam_early_{n}", flat2(n, w[n]), flat2(n, gfull[n]), flat2(n, m[n]), flat2(n, v[n]),
                              lo, w3[n].shape[0] * R, R, after=tok)
        dep.append(early[n][1][-1, 0])
    dep = jnp.stack(dep)

    packed, slots8 = _split_wait("small_grads_wait", sg_thru, sg_sends, sg_arrivals, dep, _all_copies)
    summed = _sum_slots("sum_small", slots8, packed, 4 * cx + 2 * cy + cc)
    g_small = dict(zip(SMALL, _unpack(summed, full_shapes)))
    for n in SMALL_SHARDED:
        width = w[n].shape[-1]
        g_small[n] = lax.dynamic_slice_in_dim(g_small[n], chip * width, width, axis=g_small[n].ndim - 1).reshape(w[n].shape)

    for group in pending:
        finish_exchange(group, summed)
    grads, delta, new_m, new_v = {}, {}, {}, {}
    late, after = dict(early), summed
    while joins:
        ns = joins[0][1]
        wait_joins(after, 1)
        for n in ns:
            if n in ODD_ONLY:
                continue
            R = w3[n].shape[1]
            late[n] = _adam_rows(f"adam_late_{n}", flat2(n, w[n]), flat2(n, gfull[n]), flat2(n, m[n]), flat2(n, v[n]),
                                 0, R, R, prev=early[n])
        after = late[ns[-1]][1]
    for n in BIG:
        grads[n], delta[n], new_m[n], new_v[n] = (t.reshape(w[n].shape) for t in late[n])
    small_shapes = [w[n].shape for n in SMALL]
    d2, m2, v2 = _adam("adam_small", _pack([w[n] for n in SMALL]), _pack([g_small[n] for n in SMALL]),
                       _pack([m[n] for n in SMALL]), _pack([v[n] for n in SMALL]))
    for n, dn, mn_, vn_ in zip(SMALL, _unpack(d2, small_shapes), _unpack(m2, small_shapes), _unpack(v2, small_shapes)):
        grads[n], delta[n], new_m[n], new_v[n] = g_small[n].reshape(w[n].shape), dn, mn_, vn_

    return (loss, grad_x, *[grads[n] for n in WEIGHTS], *[delta[n] for n in WEIGHTS],
            *[new_m[n] for n in WEIGHTS], *[new_v[n] for n in WEIGHTS])
```

```python
import jax
import jax.numpy as jnp
from jax import lax
from jax.experimental import pallas as pl
from jax.experimental.pallas import tpu as pltpu

F32, BF16 = jnp.float32, jnp.bfloat16
EPS = 1e-6
N_MEM_HEADS = 4
A_HEADS = 8
GMLP_BLOCK = 128
CHUNK = 64
POOL_WINDOWS = (2, 4, 8, 16)
N_CHIPS = 4
N_DEV = 8
HALO = 32
LANES = 128
TM, TN, TK = 512, 1024, 512
TM_BIG, TM_MID = 1024, 512
TN_BIG, TN_MID, TN_SMALL = 1024, 512, 256
EPI_ROWS = 256
EXCHANGES_IN_FLIGHT = 2
ROW_TILE = 256
PACK_ROWS = 512
VMEM_LIMIT = 48 * 1024 * 1024
VMEM_LIMIT_BIG = 56 * 1024 * 1024
ADAM_LR, ADAM_B1, ADAM_B2, ADAM_EPS, ADAM_WD, ADAM_STEP = 0.001, 0.9, 0.999, 1e-08, 0.01, 10
MESH = pl.DeviceIdType.MESH
HBM = pl.BlockSpec(memory_space=pltpu.HBM)
SEM = pl.BlockSpec(memory_space=pltpu.SEMAPHORE)
ANY = pl.BlockSpec(memory_space=pl.ANY)
EFFECT = pltpu.SideEffectType.DATAFLOW_SIDE_EFFECTING

WEIGHTS = ['norm_ffn1', 'ffn1_gate', 'ffn1_up', 'ffn1_down', 'norm_mix', 'ab_w_in', 'ab_b_in', 'gmlp_w_s',
           'gmlp_b_s', 'gmlp_ln_g', 'gmlp_ln_b', 'conv_w', 'conv_b', 'conv_ln_g', 'conv_ln_b', 'ab_w_out',
           'ab_b_out', 'pool_w', 'pool_b', 'pool_scale', 'norm_xq', 'norm_xkv', 'xattn_wq', 'xattn_wk',
           'xattn_wv', 'xattn_wo', 'norm_ffn2', 'ffn2_gate', 'ffn2_up', 'ffn2_down', 'norm_final']
BIG = ['ffn1_gate', 'ffn1_up', 'ffn1_down', 'ab_w_in', 'ab_w_out', 'pool_w', 'xattn_wq', 'xattn_wk', 'xattn_wv',
       'xattn_wo', 'ffn2_gate', 'ffn2_up', 'ffn2_down']
EVEN_ONLY, ODD_ONLY = ['ab_w_in', 'ab_w_out'], ['pool_w']
SMALL = [n for n in WEIGHTS if n not in BIG]
SMALL_SHARDED = ['conv_w', 'pool_b', 'pool_scale']

NN = (((1,), (0,)), ((), ()))
NT = (((1,), (1,)), ((), ()))
TN_ = (((0,), (0,)), ((), ()))
_DIMS = {'nn': NN, 'nt': NT, 'tn': TN_}


def _pick(n, pref, unit=LANES):
    if n <= pref:
        return n
    t = (pref // unit) * unit
    while t >= unit:
        if n % t == 0:
            return t
        t -= unit
    return n


def _sds(shape, dtype):
    return jax.ShapeDtypeStruct(tuple(shape), dtype)


def _layer_names(l):
    mix = EVEN_ONLY if l % 2 == 0 else ODD_ONLY
    return ['ffn1_gate', 'ffn1_up', 'ffn1_down'] + mix + ['xattn_wq', 'xattn_wk', 'xattn_wv', 'xattn_wo',
                                                          'ffn2_gate', 'ffn2_up', 'ffn2_down']


def _stack_index(name, l):
    return l // 2 if name in EVEN_ONLY + ODD_ONLY else l


def _mm(name, grid, ins, pairs, outs, acc_shapes, epilogue, extras=()):
    n_in, n_out = len(ins), len(outs)
    nk = grid[2]

    def body(*refs):
        in_refs, out_refs, acc_refs = refs[:n_in], refs[n_in:n_in + n_out], refs[n_in + n_out:]
        k = pl.program_id(2)

        @pl.when(k == 0)
        def _():
            for acc in acc_refs:
                acc[...] = jnp.zeros_like(acc)

        for ai, bi, mode, ci in pairs:
            a = in_refs[ai][...].astype(BF16)
            b = in_refs[bi][...].astype(BF16)
            acc_refs[ci][...] += lax.dot_general(a, b, _DIMS[mode], preferred_element_type=F32)

        @pl.when(k == nk - 1)
        def _():
            res = epilogue([acc[...] for acc in acc_refs], [in_refs[e][...] for e in extras])
            for o, r in zip(out_refs, res):
                o[...] = r.astype(o.dtype)

    return pl.pallas_call(
        body, name=name, grid=grid,
        in_specs=[s for _, s in ins], out_specs=[s for _, s in outs], out_shape=[s for s, _ in outs],
        scratch_shapes=[pltpu.VMEM(s, F32) for s in acc_shapes],
        compiler_params=pltpu.CompilerParams(dimension_semantics=("parallel", "parallel", "arbitrary"),
                                             vmem_limit_bytes=VMEM_LIMIT),
    )(*[a for a, _ in ins])


def _bs(shape, fn):
    return pl.BlockSpec(shape, fn)


def _mm1(name, grid, ins, outs, compute, vmem=None):
    n_in = len(ins)

    def body(*refs):
        compute(refs[:n_in], refs[n_in:])

    return pl.pallas_call(
        body, name=name, grid=grid,
        in_specs=[s for _, s in ins], out_specs=[s for _, s in outs], out_shape=[s for s, _ in outs],
        compiler_params=pltpu.CompilerParams(dimension_semantics=("parallel", "parallel"),
                                             vmem_limit_bytes=vmem or VMEM_LIMIT),
    )(*[a for a, _ in ins])


def _dot(a, b, mode):
    return lax.dot_general(a.astype(BF16), b.astype(BF16), _DIMS[mode], preferred_element_type=F32)


def _row_chunks(rows):
    step = min(rows, EPI_ROWS)
    return [slice(r, r + step) for r in range(0, rows, step)]


def _rowwise(name, fn, ins, row_outs, acc_outs, tile):
    T = next(a.shape[0] for k, a in ins if k == 'row')
    n = T // tile
    per = tile // HALO if tile % HALO == 0 else 1
    last = T // HALO - 1
    in_specs = []
    for kind, a in ins:
        if kind == 'row':
            in_specs.append(pl.BlockSpec((tile, a.shape[1]), lambda i: (i, 0)))
        elif kind == 'prev':
            in_specs.append(pl.BlockSpec((HALO, a.shape[1]), lambda i: (jnp.maximum(i * per - 1, 0), 0)))
        elif kind == 'next':
            in_specs.append(pl.BlockSpec((HALO, a.shape[1]), lambda i: (jnp.minimum((i + 1) * per, last), 0)))
        else:
            in_specs.append(pl.BlockSpec(a.shape, lambda i, nd=a.ndim: (0,) * nd))
    n_in, n_row = len(ins), len(row_outs)
    out_shape = [_sds((T, c), dt) for c, dt in row_outs] + [_sds(s, F32) for s in acc_outs]
    out_specs = [pl.BlockSpec((tile, c), lambda i: (i, 0)) for c, _ in row_outs]
    out_specs += [pl.BlockSpec(s, lambda i, nd=len(s): (0,) * nd) for s in acc_outs]
    kinds = [k for k, _ in ins]

    def body(*refs):
        i = pl.program_id(0)
        vals = [r if k == 'cref' else r[...] for k, r in zip(kinds, refs[:n_in])]
        ro, ao = fn(i, *vals)
        for r, v in zip(refs[n_in:n_in + n_row], ro):
            r[...] = v.astype(r.dtype)
        for r, v in zip(refs[n_in + n_row:], ao):
            @pl.when(i == 0)
            def _(r=r, v=v):
                r[...] = v

            @pl.when(i > 0)
            def _(r=r, v=v):
                r[...] += v

    return pl.pallas_call(
        body, name=name, grid=(n,), in_specs=in_specs, out_specs=out_specs, out_shape=out_shape,
        compiler_params=pltpu.CompilerParams(dimension_semantics=("arbitrary",), vmem_limit_bytes=VMEM_LIMIT),
    )(*[a for _, a in ins])


def _rms(x, g):
    return x * lax.rsqrt(jnp.mean(x * x, axis=-1, keepdims=True) + EPS) * g


def _ln(x, g, b):
    mu = jnp.mean(x, axis=-1, keepdims=True)
    xc = x - mu
    var = jnp.mean(xc * xc, axis=-1, keepdims=True)
    return xc * lax.rsqrt(var + EPS) * g + b


def _gelu(x):
    return 0.5 * x * (1.0 + jnp.tanh(0.7978845608028654 * (x + 0.044715 * (x * x * x))))


def _silu(x):
    return x * jax.nn.sigmoid(x)


def _glu(a, g):
    return a * jax.nn.sigmoid(g)


def _row_ids(i, tile, rows):
    return i * tile + lax.broadcasted_iota(jnp.int32, (rows, 1), 0)


def _mesh_pos():
    return lax.axis_index("x"), lax.axis_index("y"), lax.axis_index("c")


def _other_chips(x, y):
    return [(1 - x, y), (x, 1 - y), (1 - x, 1 - y)]


def _remote(src, dst, send_sems, recv_sems, s, to):
    return pltpu.make_async_remote_copy(src_ref=src, dst_ref=dst, send_sem=send_sems.at[s], recv_sem=recv_sems.at[s],
                                        device_id=to, device_id_type=MESH)


def _gather_near_copies(refs, send_sems, recv_sems):
    x, y, c = _mesh_pos()
    me = 2 * x + y
    out = []
    for t, ref in enumerate(refs):
        rh = ref.shape[1] // 2
        half = pl.ds(c * rh, rh)
        for k, (cx, cy) in enumerate(_other_chips(x, y)[:2]):
            mine, theirs = ref.at[me, half], ref.at[2 * cx + cy, half]
            out.append((_remote(mine, mine, send_sems, recv_sems, 2 * t + k, (cx, cy, c)),
                        _remote(theirs, theirs, send_sems, recv_sems, 2 * t + k, (cx, cy, c))))
    return out


def _gather_far_copies(refs, send_sems, recv_sems):
    x, y, c = _mesh_pos()
    xn, yn, diag = 2 * (1 - x) + y, 2 * x + (1 - y), 2 * (1 - x) + (1 - y)
    out = []
    for t, ref in enumerate(refs):
        rq = ref.shape[1] // 4
        q0, q1 = pl.ds(2 * c * rq, rq), pl.ds((2 * c + 1) * rq, rq)
        out.append((_remote(ref.at[yn, q1], ref.at[yn, q1], send_sems, recv_sems, 2 * t, (1 - x, y, c)),
                    _remote(ref.at[diag, q1], ref.at[diag, q1], send_sems, recv_sems, 2 * t, (1 - x, y, c))))
        out.append((_remote(ref.at[xn, q0], ref.at[xn, q0], send_sems, recv_sems, 2 * t + 1, (x, 1 - y, c)),
                    _remote(ref.at[diag, q0], ref.at[diag, q0], send_sems, recv_sems, 2 * t + 1, (x, 1 - y, c))))
    return out


def _exchange_copies(refs, send_sems, recv_sems):
    x, y, c = _mesh_pos()
    n = len(refs) // 2
    out = []
    for t in range(n):
        part, land = refs[t], refs[n + t]
        for k, (cx, cy) in enumerate(_other_chips(x, y)):
            out.append((_remote(part.at[2 * cx + cy], land.at[k], send_sems, recv_sems, 3 * t + k, (cx, cy, c)),
                        _remote(land.at[k], land.at[k], send_sems, recv_sems, 3 * t + k, (cx, cy, c))))
    return out


def _split_start(name, thru, after, n_sems, copies, carry=()):
    n, nc = len(thru), len(carry)
    both = list(thru) + list(carry)

    def body(*refs):
        outs = refs[n + nc + 1:]
        send_sems, recv_sems, thru_refs, token = outs[0], outs[1], outs[2:2 + n], outs[2 + n + nc]
        for send, _ in copies(thru_refs, send_sems, recv_sems):
            send.start()
        token[...] = jnp.zeros_like(token)

    res = pl.pallas_call(
        body, name=name,
        out_shape=(pltpu.SemaphoreType.DMA((n_sems,)), pltpu.SemaphoreType.DMA((n_sems,)),
                   *[pltpu.HBM(b.shape, b.dtype) for b in both], _sds((8, LANES), F32)),
        in_specs=[HBM] * (n + nc) + [ANY],
        out_specs=(SEM, SEM, *[HBM] * (n + nc), pl.BlockSpec(memory_space=pltpu.VMEM)),
        input_output_aliases={i: 2 + i for i in range(n + nc)},
        compiler_params=pltpu.CompilerParams(has_side_effects=EFFECT),
    )(*[pltpu.with_memory_space_constraint(b, pltpu.HBM) for b in both], after)
    return res[0], res[1], list(res[2:2 + n]), list(res[2 + n:2 + n + nc]), res[2 + n + nc]


def _split_wait(name, thru, send_sems, recv_sems, after, copies):
    n = len(thru)

    def body(*refs):
        sends, recvs, outs = refs[n], refs[n + 1], refs[n + 3:]
        for send, arrival in copies(outs, sends, recvs):
            send.wait_send()
            arrival.wait_recv()

    res = pl.pallas_call(
        body, name=name, out_shape=tuple(pltpu.HBM(b.shape, b.dtype) for b in thru),
        in_specs=[HBM] * n + [SEM, SEM, ANY], out_specs=tuple([HBM] * n),
        input_output_aliases={i: i for i in range(n)},
        compiler_params=pltpu.CompilerParams(has_side_effects=EFFECT),
    )(*thru, send_sems, recv_sems, after)
    return list(res)


def _near_slabs(x, y):
    return [2 * (1 - x) + y, 2 * x + (1 - y)]


def _far_slabs(x, y):
    return [2 * (1 - x) + (1 - y)]


def _forward_copies(slabs_of):
    def copies(refs, send_sems, recv_sems):
        x, y, c = _mesh_pos()
        slabs = slabs_of(x, y)
        out = []
        for t, ref in enumerate(refs):
            rh = ref.shape[1] // 2
            mine, other = pl.ds(c * rh, rh), pl.ds((1 - c) * rh, rh)
            for k, j in enumerate(slabs):
                s = len(slabs) * t + k
                out.append((_remote(ref.at[j, mine], ref.at[j, mine], send_sems, recv_sems, s, (x, y, 1 - c)),
                            _remote(ref.at[j, other], ref.at[j, other], send_sems, recv_sems, s, (x, y, 1 - c))))
        return out
    return copies


def _forward_halves(name, bufs, slabs_of, per):
    n = len(bufs)

    def body(*refs):
        outs, send_sems, recv_sems = refs[n:2 * n], refs[2 * n], refs[2 * n + 1]
        pairs = _forward_copies(slabs_of)(outs, send_sems, recv_sems)
        for send, _ in pairs:
            send.start()
        for _, arrival in pairs:
            arrival.wait_recv()
        for send, _ in pairs:
            send.wait_send()

    res = pl.pallas_call(
        body, name=name, out_shape=tuple(_sds(b.shape, b.dtype) for b in bufs),
        in_specs=[HBM] * n, out_specs=tuple([HBM] * n), input_output_aliases={i: i for i in range(n)},
        scratch_shapes=[pltpu.SemaphoreType.DMA((per * n,)), pltpu.SemaphoreType.DMA((per * n,))],
    )(*bufs)
    return list(res)


def _swap_halves(name, gs):
    n = len(gs)

    def body(*refs):
        ins, outs, send_sems, recv_sems = refs[:n], refs[n:2 * n], refs[2 * n], refs[2 * n + 1]
        x, y, c = _mesh_pos()
        cps = []
        for t, (g_ref, o_ref) in enumerate(zip(ins, outs)):
            rh = g_ref.shape[1] // 2
            cp = _remote(g_ref.at[:, pl.ds((1 - c) * rh, rh)], o_ref, send_sems, recv_sems, t, (x, y, 1 - c))
            cp.start()
            cps.append(cp)
        for cp in cps:
            cp.wait_recv()
        for cp in cps:
            cp.wait_send()

    res = pl.pallas_call(
        body, name=name, out_shape=tuple(_sds((g.shape[0], g.shape[1] // 2, g.shape[2]), g.dtype) for g in gs),
        in_specs=[HBM] * n, out_specs=tuple([HBM] * n),
        scratch_shapes=[pltpu.SemaphoreType.DMA((n,)), pltpu.SemaphoreType.DMA((n,))],
    )(*gs)
    return list(res)


def _swap_copies(refs, send_sems, recv_sems):
    x, y, c = _mesh_pos()
    n = len(refs) // 2
    out = []
    for t in range(n):
        g_ref, land = refs[t], refs[n + t]
        rh = g_ref.shape[1] // 2
        out.append((_remote(g_ref.at[:, pl.ds((1 - c) * rh, rh)], land, send_sems, recv_sems, t, (x, y, 1 - c)),
                    _remote(land, land, send_sems, recv_sems, t, (x, y, 1 - c))))
    return out


def _join_copies(idx):
    def copies(refs, send_sems, recv_sems):
        x, y, c = _mesh_pos()
        out = []
        for t, ref in enumerate(refs):
            rh = ref.shape[1] // 2
            mine, theirs = ref.at[idx[t], pl.ds(c * rh, rh)], ref.at[idx[t], pl.ds((1 - c) * rh, rh)]
            out.append((_remote(mine, mine, send_sems, recv_sems, t, (x, y, 1 - c)),
                        _remote(theirs, theirs, send_sems, recv_sems, t, (x, y, 1 - c))))
        return out
    return copies


def _gather_all(name, buf, after):
    def body(b_ref, after_ref, out_ref, send_sems, recv_sems, local_sem):
        x, y, c = _mesh_pos()
        me = 4 * x + 2 * y + c
        local = pltpu.make_async_copy(b_ref, out_ref.at[me], local_sem)
        local.start()
        peers = []
        for k in range(1, N_DEV):
            peers.append((1 - x if k & 4 else x, 1 - y if k & 2 else y, 1 - c if k & 1 else c))
        sends = []
        for k, peer in enumerate(peers):
            cp = _remote(b_ref, out_ref.at[me], send_sems, recv_sems, k, peer)
            cp.start()
            sends.append(cp)
        for k, (px, py, pc) in enumerate(peers):
            slot = out_ref.at[4 * px + 2 * py + pc]
            _remote(slot, slot, send_sems, recv_sems, k, (px, py, pc)).wait_recv()
        for cp in sends:
            cp.wait_send()
        local.wait()

    return pl.pallas_call(
        body, name=name, out_shape=_sds((N_DEV,) + buf.shape, buf.dtype), in_specs=[HBM, ANY], out_specs=HBM,
        scratch_shapes=[pltpu.SemaphoreType.DMA((N_DEV - 1,)), pltpu.SemaphoreType.DMA((N_DEV - 1,)),
                        pltpu.SemaphoreType.DMA],
    )(buf, after)


def _scalars(*vals):
    return jnp.stack([jnp.asarray(v, jnp.int32) for v in vals])


def _cast_slab(name, w3, li, chip, after):
    _, R, C = w3.shape
    tr = _pick(R, ROW_TILE, 16)

    def body(s_ref, w_ref, after_ref, o_ref):
        o_ref[...] = w_ref[...].astype(o_ref.dtype)

    grid_spec = pltpu.PrefetchScalarGridSpec(
        num_scalar_prefetch=1, grid=(R // tr,),
        in_specs=[pl.BlockSpec((None, tr, C), lambda r, s: (li, r, 0)), ANY],
        out_specs=pl.BlockSpec((None, tr, C), lambda r, s: (s[0], r, 0)))
    return pl.pallas_call(
        body, name=name, grid_spec=grid_spec, out_shape=_sds((N_CHIPS, R, C), BF16),
        compiler_params=pltpu.CompilerParams(dimension_semantics=("arbitrary",), vmem_limit_bytes=VMEM_LIMIT),
    )(_scalars(chip), w3, after)


def _add_halves(name, g, recv, c):
    _, R, C = g.shape
    rh = R // 2
    tr = _pick(rh, 512, 16)
    nr = rh // tr

    def body(s_ref, g_ref, a_ref, o_ref):
        o_ref[...] = (g_ref[...].astype(F32) + a_ref[...].astype(F32)).astype(o_ref.dtype)

    blk = (None, tr, C)
    grid_spec = pltpu.PrefetchScalarGridSpec(
        num_scalar_prefetch=1, grid=(N_CHIPS, nr),
        in_specs=[pl.BlockSpec(blk, lambda j, r, s: (j, s[0] * nr + r, 0)),
                  pl.BlockSpec(blk, lambda j, r, s: (j, r, 0))],
        out_specs=pl.BlockSpec(blk, lambda j, r, s: (j, r, 0)))
    return pl.pallas_call(
        body, name=name, grid_spec=grid_spec, out_shape=_sds((N_CHIPS, rh, C), g.dtype),
        compiler_params=pltpu.CompilerParams(dimension_semantics=("arbitrary",) * 2, vmem_limit_bytes=VMEM_LIMIT),
    )(_scalars(c), g, recv)


def _sum_into(name, p, recv, gfull, li, chip, c):
    _, rh, C = p.shape
    tr = _pick(rh, 512, 16)
    nr = rh // tr

    def body(s_ref, p_ref, r_ref, g_ref, o_ref):
        acc = p_ref[...].astype(F32)
        for k in range(3):
            acc = acc + r_ref[k].astype(F32)
        o_ref[...] = acc

    grid_spec = pltpu.PrefetchScalarGridSpec(
        num_scalar_prefetch=1, grid=(nr,),
        in_specs=[pl.BlockSpec((None, tr, C), lambda r, s: (s[0], r, 0)),
                  pl.BlockSpec((3, tr, C), lambda r, s: (0, r, 0)), HBM],
        out_specs=pl.BlockSpec((None, tr, C), lambda r, s: (li, s[1] * nr + r, 0)))
    return pl.pallas_call(
        body, name=name, grid_spec=grid_spec, out_shape=_sds(gfull.shape, F32), input_output_aliases={3: 0},
        compiler_params=pltpu.CompilerParams(dimension_semantics=("arbitrary",), vmem_limit_bytes=VMEM_LIMIT),
    )(_scalars(chip, c), p, recv, gfull)


def _all_copies(refs, send_sems, recv_sems):
    b_ref, out_ref = refs
    x, y, c = _mesh_pos()
    me = 4 * x + 2 * y + c
    out = []
    for k in range(1, N_DEV):
        px, py, pc = 1 - x if k & 4 else x, 1 - y if k & 2 else y, 1 - c if k & 1 else c
        slot = out_ref.at[4 * px + 2 * py + pc]
        out.append((_remote(b_ref, out_ref.at[me], send_sems, recv_sems, k - 1, (px, py, pc)),
                    _remote(slot, slot, send_sems, recv_sems, k - 1, (px, py, pc))))
    return out


def _sum_slots(name, slots, own, me):
    _, n, _ = slots.shape

    def body(s_ref, b_ref, own_ref, o_ref):
        acc = None
        for k in range(N_DEV):
            term = jnp.where(s_ref[0] == k, own_ref[...], b_ref[k])
            acc = term if acc is None else acc + term
        o_ref[...] = acc

    grid_spec = pltpu.PrefetchScalarGridSpec(
        num_scalar_prefetch=1, grid=(n // PACK_ROWS,),
        in_specs=[pl.BlockSpec((N_DEV, PACK_ROWS, LANES), lambda i, s: (0, i, 0)),
                  pl.BlockSpec((PACK_ROWS, LANES), lambda i, s: (i, 0))],
        out_specs=pl.BlockSpec((PACK_ROWS, LANES), lambda i, s: (i, 0)))
    return pl.pallas_call(body, name=name, grid_spec=grid_spec, out_shape=_sds((n, LANES), F32))(_scalars(me), slots, own)


def _adam_tile(i, w, g, m, v):
    m = ADAM_B1 * m + (1.0 - ADAM_B1) * g
    v = ADAM_B2 * v + (1.0 - ADAM_B2) * (g * g)
    m_hat = m / (1.0 - ADAM_B1 ** ADAM_STEP)
    v_hat = v / (1.0 - ADAM_B2 ** ADAM_STEP)
    delta = -ADAM_LR * (m_hat / (jnp.sqrt(v_hat) + ADAM_EPS) + ADAM_WD * w)
    return [delta, m, v], []


def _adam(name, w, g, m, v):
    rows, C = w.shape
    tile = _pick(rows, ROW_TILE, 8)
    return _rowwise(name, _adam_tile, [('row', w), ('row', g), ('row', m), ('row', v)], [(C, F32)] * 3, [], tile)


def _adam_rows(name, w, g, m, v, lo, hi, unit, prev=None, after=None):
    rows, C = w.shape
    tile = _pick(unit, ROW_TILE, 8)
    first = lo // tile
    spec = pl.BlockSpec((tile, C), lambda i: (i + first, 0))
    ins, in_specs = [w, g, m, v], [spec] * 4
    if prev is not None:
        ins, in_specs = ins + list(prev), in_specs + [ANY] * 4
    if after is not None:
        ins, in_specs = ins + [after], in_specs + [ANY]

    def body(*refs):
        n_in = len(ins)
        outs, _ = _adam_tile(0, *[r[...] for r in refs[:4]])
        refs[n_in][...] = refs[1][...]
        for o, val in zip(refs[n_in + 1:n_in + 4], outs):
            o[...] = val

    return pl.pallas_call(
        body, name=name, grid=((hi - lo) // tile,), in_specs=in_specs, out_specs=[spec] * 4,
        out_shape=[_sds((rows, C), F32)] * 4,
        input_output_aliases={4 + k: k for k in range(4)} if prev is not None else {},
        compiler_params=pltpu.CompilerParams(dimension_semantics=("arbitrary",), vmem_limit_bytes=VMEM_LIMIT),
    )(*ins)


def _pack(arrs):
    flat = jnp.concatenate([a.reshape(-1).astype(F32) for a in arrs])
    unit = PACK_ROWS * LANES
    n = -(-flat.shape[0] // unit) * unit
    return jnp.pad(flat, (0, n - flat.shape[0])).reshape(-1, LANES)


def _unpack(buf, shapes):
    flat = buf.reshape(-1)
    out, off = [], 0
    for s in shapes:
        n = 1
        for d in s:
            n *= d
        out.append(flat[off:off + n].reshape(s))
        off += n
    return out


def _norm_fwd(name, x, g, tile):
    D = x.shape[1]
    return _rowwise(name, lambda i, xv, gv: ([_rms(xv, gv)], []), [('row', x), ('const', g.reshape(1, D))],
                    [(D, BF16)], [], tile)[0]


def _norm_bwd(name, x, g, dh, dxo, tile):
    D = x.shape[1]
    if dxo is None:
        def fn(i, xv, dhv, gv):
            _, vjp = jax.vjp(_rms, xv, gv)
            return [], [vjp(dhv)[1]]
        return _rowwise(name, fn, [('row', x), ('row', dh), ('const', g.reshape(1, D))], [], [(1, D)], tile)[0]

    def fn(i, xv, dhv, dxv, gv):
        _, vjp = jax.vjp(_rms, xv, gv)
        dx, dg = vjp(dhv)
        return [dxv + dx, dxv + dx], [dg]
    return _rowwise(name, fn, [('row', x), ('row', dh), ('row', dxo), ('const', g.reshape(1, D))],
                    [(D, F32), (D, BF16)], [(1, D)], tile)


def _ffn_fwd(tag, x, g, wg, wu, wd, tile):
    T, D = x.shape
    fs = wg.shape[2]
    F = N_CHIPS * fs
    tm, tn = _pick(T, TM_BIG), _pick(D, TN_SMALL)
    h = _norm_fwd(f"{tag}_norm", x, g, tile)
    hspec = _bs((tm, D), lambda j, i: (i, 0))
    wspec = _bs((None, D, fs), lambda j, i: (j, 0, 0))
    ospec = _bs((tm, fs), lambda j, i: (i, j))

    def gate(ins, outs):
        outs[0][...] = _dot(ins[0][...], ins[1][...], 'nn').astype(BF16)

    a = _mm1(f"{tag}_gate", (N_CHIPS, T // tm), [(h, hspec), (wg, wspec)], [(_sds((T, F), BF16), ospec)], gate)[0]

    def up(ins, outs):
        bv = _dot(ins[0][...], ins[1][...], 'nn')
        for rows in _row_chunks(tm):
            bb = bv[rows]
            outs[0][rows, :] = bb.astype(BF16)
            outs[1][rows, :] = (_silu(ins[2][rows, :].astype(F32)) * bb).astype(BF16)

    b, s = _mm1(f"{tag}_up", (N_CHIPS, T // tm), [(h, hspec), (wu, wspec), (a, ospec)],
                [(_sds((T, F), BF16), ospec)] * 2, up)

    def down(ins, outs):
        outs[0][...] = ins[2][...] + 0.5 * _dot(ins[0][...], ins[1][...].reshape(F, tn), 'nn')

    xspec = _bs((tm, tn), lambda i, j: (i, j))
    xo = _mm1(f"{tag}_down", (T // tm, D // tn),
              [(s, _bs((tm, F), lambda i, j: (i, 0))), (wd, _bs((N_CHIPS, fs, tn), lambda i, j: (0, 0, j))), (x, xspec)],
              [(_sds((T, D), F32), xspec)], down)[0]
    return xo, (x, h, a, b, s)


def _ffn_bwd(tag, dxo, dxb, saved, g, wg, wu, wd, tile, on_grads=None):
    x, h, a, b, s = saved
    T, D = x.shape
    fs = wg.shape[2]
    F = N_CHIPS * fs
    tm = _pick(T, TM_BIG)
    tspec = _bs((tm, fs), lambda j, i: (i, j))

    def ds_fn(ins, outs):
        d = _dot(ins[0][...], ins[1][...], 'nt')
        for rows in _row_chunks(tm):
            ds = 0.5 * d[rows]
            av, bv = ins[2][rows, :].astype(F32), ins[3][rows, :].astype(F32)
            sig = jax.nn.sigmoid(av)
            outs[0][rows, :] = (ds * bv * (sig * (1.0 + av * (1.0 - sig)))).astype(BF16)
            outs[1][rows, :] = (ds * (av * sig)).astype(BF16)

    da, db = _mm1(f"{tag}_ds", (N_CHIPS, T // tm),
                  [(dxb, _bs((tm, D), lambda j, i: (i, 0))), (wd, _bs((None, fs, D), lambda j, i: (j, 0, 0))),
                   (a, tspec), (b, tspec)], [(_sds((T, F), BF16), tspec)] * 2, ds_fn, vmem=VMEM_LIMIT_BIG)

    tn = _pick(D, TN_BIG)

    def dwd_fn(ins, outs):
        outs[0][...] = (0.5 * _dot(ins[0][...], ins[1][...], 'tn')).astype(BF16)

    gd = _mm1(f"{tag}_dwd", (N_CHIPS, D // tn),
              [(s, _bs((T, fs), lambda i, j: (0, i))), (dxb, _bs((T, tn), lambda i, j: (0, j)))],
              [(_sds(wd.shape, BF16), _bs((None, fs, tn), lambda i, j: (i, 0, j)))], dwd_fn)[0]

    tmd = _pick(D, TM_BIG)

    def dw_fn(ins, outs):
        outs[0][...] = _dot(ins[0][...], ins[1][...], 'tn').astype(BF16)

    def dw(name, dy):
        return _mm1(name, (N_CHIPS, D // tmd),
                    [(h, _bs((T, tmd), lambda j, i: (0, i))), (dy, _bs((T, fs), lambda j, i: (0, j)))],
                    [(_sds(wg.shape, BF16), _bs((None, tmd, fs), lambda j, i: (j, i, 0)))], dw_fn)[0]

    gg, gu = dw(f"{tag}_dwg", da), dw(f"{tag}_dwu", db)
    if on_grads is not None:
        (gg, gu, gd), (da, db) = on_grads([gg, gu, gd], (da, db))

    tmb, tnb = _pick(T, TM_BIG), _pick(D, TN_BIG)
    aspec = _bs((tmb, fs), lambda i, j, k: (i, k))
    wtspec = _bs((None, tnb, fs), lambda i, j, k: (k, j, 0))
    dh = _mm(f"{tag}_dh", (T // tmb, D // tnb, N_CHIPS), [(da, aspec), (wg, wtspec), (db, aspec), (wu, wtspec)],
             [(0, 1, 'nt', 0), (2, 3, 'nt', 0)], [(_sds((T, D), F32), _bs((tmb, tnb), lambda i, j, k: (i, j)))],
             [(tmb, tnb)], lambda accs, _: accs)[0]
    dx, dxb2, dg = _norm_bwd(f"{tag}_norm_bwd", x, g, dh, dxo, tile)
    return dx, dxb2, dg, gg, gu, gd


def _proj_rows(name, a, w, out_dtype, extras=(), epilogue=None):
    M, K = a.shape
    ks, N = w.shape[1], w.shape[2]
    tm, tn = _pick(M, TM_BIG), _pick(N, TN_MID)
    ins = [(a, _bs((tm, K), lambda i, j: (i, 0))), (w, _bs((N_CHIPS, ks, tn), lambda i, j: (0, 0, j)))]
    for e in extras:
        if e.shape[0] == 1:
            ins.append((e, _bs((1, tn), lambda i, j: (0, j))))
        else:
            ins.append((e, _bs((tm, tn), lambda i, j: (i, j))))

    def fn(refs, outs):
        acc = _dot(refs[0][...], refs[1][...].reshape(K, tn), 'nn')
        if epilogue is not None:
            acc = epilogue(acc, [r[...] for r in refs[2:]])
        outs[0][...] = acc.astype(out_dtype)

    return _mm1(name, (M // tm, N // tn), ins, [(_sds((M, N), out_dtype), _bs((tm, tn), lambda i, j: (i, j)))], fn)[0]


def _proj_rows_t(name, pairs, out_dtype):
    dy0, w0 = pairs[0]
    M, N = dy0.shape
    ks = w0.shape[1]
    tm = _pick(M, TM_BIG)
    ins = []
    for dy, w in pairs:
        ins.append((dy, _bs((tm, N), lambda i, j: (i, 0))))
        ins.append((w, _bs((None, ks, N), lambda i, j: (j, 0, 0))))

    def fn(refs, outs):
        acc = _dot(refs[0][...], refs[1][...], 'nt')
        for p in range(1, len(pairs)):
            acc = acc + _dot(refs[2 * p][...], refs[2 * p + 1][...], 'nt')
        outs[0][...] = acc.astype(out_dtype)

    return _mm1(name, (M // tm, N_CHIPS), ins,
                [(_sds((M, N_CHIPS * ks), out_dtype), _bs((tm, ks), lambda i, j: (i, j)))], fn)[0]


def _grad_rows(name, a, dys):
    T, K = a.shape
    N = dys[0].shape[1]
    ks = K // N_CHIPS
    tn = _pick(N, TN_BIG)
    ins = [(a, _bs((T, ks), lambda i, j: (0, i)))] + [(dy, _bs((T, tn), lambda i, j: (0, j))) for dy in dys]

    def fn(refs, outs):
        av = refs[0][...]
        for p in range(len(dys)):
            outs[p][...] = _dot(av, refs[1 + p][...], 'tn').astype(BF16)

    gspec = _bs((None, ks, tn), lambda i, j: (i, 0, j))
    return _mm1(name, (N_CHIPS, N // tn), ins, [(_sds((N_CHIPS, ks, N), BF16), gspec)] * len(dys), fn)


def _softmax_rows(s):
    s = s - jnp.max(s, axis=-1, keepdims=True)
    p = jnp.exp(s)
    return p / jnp.sum(p, axis=-1, keepdims=True)


def _attn_fwd_tile(hd, scale):
    def fn(i, q, k, v):
        outs = []
        for h in range(N_MEM_HEADS):
            sl = slice(h * hd, (h + 1) * hd)
            p = _softmax_rows(lax.dot_general(q[:, sl], k[:, sl], NT, preferred_element_type=F32) * scale)
            outs.append(lax.dot_general(p.astype(BF16), v[:, sl], NN, preferred_element_type=F32))
        return [jnp.concatenate(outs, axis=1)], []
    return fn


def _attn_bwd_tile(hd, scale):
    def fn(i, q, do, k, v):
        dqs, dks, dvs = [], [], []
        for h in range(N_MEM_HEADS):
            sl = slice(h * hd, (h + 1) * hd)
            qh, kh, vh, doh = q[:, sl], k[:, sl], v[:, sl], do[:, sl]
            p = _softmax_rows(lax.dot_general(qh, kh, NT, preferred_element_type=F32) * scale)
            dvs.append(lax.dot_general(p.astype(BF16), doh, TN_, preferred_element_type=F32))
            dp = lax.dot_general(doh, vh, NT, preferred_element_type=F32)
            ds = (p * (dp - jnp.sum(dp * p, axis=-1, keepdims=True)) * scale).astype(BF16)
            dqs.append(lax.dot_general(ds, kh, NN, preferred_element_type=F32))
            dks.append(lax.dot_general(ds, qh, TN_, preferred_element_type=F32))
        return [jnp.concatenate(dqs, axis=1)], [jnp.concatenate(dks, axis=1), jnp.concatenate(dvs, axis=1)]
    return fn


def _attn_fwd(l, x, mem, gq, gkv, W, tile):
    T, D = x.shape
    M = mem.shape[0]
    hd = D // N_MEM_HEADS
    hq = _norm_fwd(f"xa{l}_normq", x, gq, tile)
    mn = _norm_fwd(f"xa{l}_normkv", mem, gkv, _pick(M, tile, 16))
    q = _proj_rows(f"xa{l}_q", hq, W['xattn_wq'], BF16)
    k = _proj_rows(f"xa{l}_k", mn, W['xattn_wk'], BF16)
    v = _proj_rows(f"xa{l}_v", mn, W['xattn_wv'], BF16)
    o = _rowwise(f"xa{l}_attn", _attn_fwd_tile(hd, hd ** -0.5), [('row', q), ('const', k), ('const', v)],
                 [(D, BF16)], [], tile)[0]
    xo = _proj_rows(f"xa{l}_o", o, W['xattn_wo'], F32, extras=(x,), epilogue=lambda acc, ex: ex[0] + acc)
    return xo, (x, hq, mn, q, k, v, o)


def _attn_bwd(l, dxo, dxb, saved, mem, gq, gkv, W, G, tile):
    x, hq, mn, q, k, v, o = saved
    T, D = x.shape
    M = mem.shape[0]
    hd = D // N_MEM_HEADS
    do = _proj_rows_t(f"xa{l}_do", [(dxb, W['xattn_wo'])], BF16)
    G['xattn_wo'] = _grad_rows(f"xa{l}_dwo", o, [dxb])[0]
    dq, dk, dv = _rowwise(f"xa{l}_attn_bwd", _attn_bwd_tile(hd, hd ** -0.5),
                          [('row', q), ('row', do), ('const', k), ('const', v)], [(D, BF16)], [(M, D), (M, D)], tile)
    dhq = _proj_rows_t(f"xa{l}_dhq", [(dq, W['xattn_wq'])], F32)
    G['xattn_wq'] = _grad_rows(f"xa{l}_dwq", hq, [dq])[0]
    dmn = _proj_rows_t(f"xa{l}_dmn", [(dk, W['xattn_wk']), (dv, W['xattn_wv'])], F32)
    G['xattn_wk'], G['xattn_wv'] = _grad_rows(f"xa{l}_dwkv", mn, [dk, dv])
    dx, dxb2, dgq = _norm_bwd(f"xa{l}_normq_bwd", x, gq, dhq, dxo, tile)
    dgkv = _norm_bwd(f"xa{l}_normkv_bwd", mem, gkv, dmn, None, _pick(M, tile, 16))
    return dx, dxb2, dgq, dgkv


def _chunk_mask():
    p = lax.broadcasted_iota(jnp.int32, (GMLP_BLOCK, GMLP_BLOCK), 0)
    q = lax.broadcasted_iota(jnp.int32, (GMLP_BLOCK, GMLP_BLOCK), 1)
    return (q // CHUNK) <= (p // CHUNK)


def _spatial_fwd(vn, ws_ref, bsf, mask, hd):
    vb = vn.astype(BF16)
    wsm = [jnp.where(mask, ws_ref[h], 0.0).astype(BF16) for h in range(A_HEADS)]
    rows = []
    for n in range(vn.shape[0] // GMLP_BLOCK):
        blk = vb[n * GMLP_BLOCK:(n + 1) * GMLP_BLOCK]
        cols = [lax.dot_general(wsm[h], blk[:, h * hd:(h + 1) * hd], NN, preferred_element_type=F32)
                for h in range(A_HEADS)]
        rows.append(jnp.concatenate(cols, axis=1) + bsf)
    return jnp.concatenate(rows, axis=0)


def _spatial_bwd(dsp, vn, ws_ref, mask, hd):
    vb, db16 = vn.astype(BF16), dsp.astype(BF16)
    wsm = [jnp.where(mask, ws_ref[h], 0.0).astype(BF16) for h in range(A_HEADS)]
    dws = [jnp.zeros((GMLP_BLOCK, GMLP_BLOCK), F32) for _ in range(A_HEADS)]
    dbs = jnp.zeros((GMLP_BLOCK, vn.shape[1]), F32)
    rows = []
    for n in range(vn.shape[0] // GMLP_BLOCK):
        sl = slice(n * GMLP_BLOCK, (n + 1) * GMLP_BLOCK)
        cols = []
        for h in range(A_HEADS):
            hs = slice(h * hd, (h + 1) * hd)
            cols.append(lax.dot_general(wsm[h], db16[sl, hs], TN_, preferred_element_type=F32))
            dws[h] = dws[h] + lax.dot_general(db16[sl, hs], vb[sl, hs], NT, preferred_element_type=F32)
        rows.append(jnp.concatenate(cols, axis=1))
        dbs = dbs + dsp[sl]
    dws = [jnp.where(mask, d, 0.0) for d in dws]
    return jnp.concatenate(rows, axis=0), dws, dbs


def _conv_taps(cat, cw_ref, kw, tile):
    acc = jnp.zeros((tile, cat.shape[1]), F32)
    for k in range(kw):
        sh = kw - 1 - k
        r = cat if sh == 0 else pltpu.roll(cat, sh, 0)
        acc = acc + r[HALO:] * cw_ref[k:k + 1, :]
    return acc


def _mix_fwd_tile(A, B, kw, tile):
    hd = A // A_HEADS

    def fn(i, z, zp, ws_ref, bsf, glg, glb, cw_ref, cb, clg, clb):
        mask = _chunk_mask()
        u = _gelu(z[:, :A])
        vn = _ln(_gelu(z[:, A:2 * A]), glg, glb)
        ya = u * _spatial_fwd(vn, ws_ref, bsf, mask, hd)
        hb = _glu(z[:, 2 * A:2 * A + B], z[:, 2 * A + B:])
        hp = jnp.where(i > 0, _glu(zp[:, 2 * A:2 * A + B], zp[:, 2 * A + B:]), 0.0)
        conv = _conv_taps(jnp.concatenate([hp, hb], axis=0), cw_ref, kw, tile) + cb
        yb = _silu(_ln(conv, clg, clb))
        return [jnp.concatenate([ya, yb], axis=1)], []
    return fn


def _mix_bwd1_tile(A, B, kw, tile):
    hd = A // A_HEADS

    def fn(i, z, zp, dy, dxo, ws_ref, bsf, glg, glb, cw_ref, cb, clg, clb):
        mask = _chunk_mask()
        dya, dyb = dy[:, :A], dy[:, A:]
        zu, zv = z[:, :A], z[:, A:2 * A]
        u, vjp_u = jax.vjp(_gelu, zu)
        vn, vjp_v = jax.vjp(lambda t, g, b: _ln(_gelu(t), g, b), zv, glg, glb)
        sp = _spatial_fwd(vn, ws_ref, bsf, mask, hd)
        dzu = vjp_u(dya * sp)[0]
        dvn, dws, dbs = _spatial_bwd(dya * u, vn, ws_ref, mask, hd)
        dzv, dglg, dglb = vjp_v(dvn)
        hb = _glu(z[:, 2 * A:2 * A + B], z[:, 2 * A + B:])
        hp = jnp.where(i > 0, _glu(zp[:, 2 * A:2 * A + B], zp[:, 2 * A + B:]), 0.0)
        cat = jnp.concatenate([hp, hb], axis=0)
        conv = _conv_taps(cat, cw_ref, kw, tile) + cb
        _, vjp_c = jax.vjp(lambda t, g, b: _silu(_ln(t, g, b)), conv, clg, clb)
        dconv, dclg, dclb = vjp_c(dyb)
        tap = lax.broadcasted_iota(jnp.int32, (HALO, 1), 0)
        dcw = jnp.zeros((HALO, B), F32)
        for k in range(kw):
            sh = kw - 1 - k
            r = cat if sh == 0 else pltpu.roll(cat, sh, 0)
            dcw = dcw + jnp.where(tap == k, jnp.sum(dconv * r[HALO:], axis=0, keepdims=True), 0.0)
        dcb = jnp.sum(dconv, axis=0, keepdims=True)
        dbo = jnp.sum(dxo, axis=0, keepdims=True)
        dws = jnp.concatenate([d[None] for d in dws], axis=0)
        return [jnp.concatenate([dzu, dzv], axis=1), dconv], [dws, dbs, dglg, dglb, dcw, dcb, dclg, dclb, dbo]
    return fn


def _mix_bwd2_tile(A, B, kw, tile, n_tiles):
    def fn(i, z, dza, dc, dcn, cw_ref):
        dcn = jnp.where(i < n_tiles - 1, dcn, 0.0)
        cat = jnp.concatenate([dc, dcn], axis=0)
        n = tile + HALO
        dhb = jnp.zeros((tile, B), F32)
        for k in range(kw):
            sh = kw - 1 - k
            r = cat if sh == 0 else pltpu.roll(cat, n - sh, 0)
            dhb = dhb + r[:tile] * cw_ref[k:k + 1, :]
        _, vjp_g = jax.vjp(_glu, z[:, 2 * A:2 * A + B], z[:, 2 * A + B:])
        da, dg = vjp_g(dhb)
        dz = jnp.concatenate([dza, da, dg], axis=1)
        return [dz], [jnp.sum(dz, axis=0, keepdims=True)]
    return fn


def _even_consts(p, e, A, B, kw):
    hd = A // A_HEADS
    bsf = jnp.repeat(p['gmlp_b_s'][e].T, hd, axis=1)
    cw = jnp.pad(p['conv_w_full'][e], ((0, HALO - kw), (0, 0)))
    return [('cref', p['gmlp_w_s'][e]), ('const', bsf), ('const', p['gmlp_ln_g'][e].reshape(1, A)),
            ('const', p['gmlp_ln_b'][e].reshape(1, A)), ('cref', cw), ('const', p['conv_b'][e].reshape(1, B)),
            ('const', p['conv_ln_g'][e].reshape(1, B)), ('const', p['conv_ln_b'][e].reshape(1, B))]


def _even_fwd(l, e, x, gm, p, W, tile):
    T, D = x.shape
    w_in, w_out = W['ab_w_in'], W['ab_w_out']
    zs = w_in.shape[2]
    Z = N_CHIPS * zs
    A = p['gmlp_ln_g'].shape[1]
    B = p['conv_b'].shape[1]
    kw = p['conv_w_full'].shape[1]
    tm = _pick(T, TM_BIG)
    h = _norm_fwd(f"mix{l}_norm", x, gm, tile)

    def in_fn(refs, outs):
        outs[0][...] = _dot(refs[0][...], refs[1][...], 'nn') + refs[2][...]

    z = _mm1(f"mix{l}_in", (N_CHIPS, T // tm),
             [(h, _bs((tm, D), lambda j, i: (i, 0))), (w_in, _bs((None, D, zs), lambda j, i: (j, 0, 0))),
              (p['ab_b_in'][e].reshape(1, Z), _bs((1, zs), lambda j, i: (0, j)))],
             [(_sds((T, Z), F32), _bs((tm, zs), lambda j, i: (i, j)))], in_fn)[0]
    consts = _even_consts(p, e, A, B, kw)
    ycat = _rowwise(f"mix{l}_mid", _mix_fwd_tile(A, B, kw, tile), [('row', z), ('prev', z)] + consts,
                    [(A + B, BF16)], [], tile)[0]
    xo = _proj_rows(f"mix{l}_out", ycat, w_out, F32, extras=(x, p['ab_b_out'][e].reshape(1, D)),
                    epilogue=lambda acc, ex: ex[0] + acc + ex[1])
    return xo, (x, h, z, ycat)


def _even_bwd(l, e, dxo, dxb, saved, gm, p, W, G, tile):
    x, h, z, ycat = saved
    T, D = x.shape
    w_in, w_out = W['ab_w_in'], W['ab_w_out']
    zs = w_in.shape[2]
    Z = N_CHIPS * zs
    A = p['gmlp_ln_g'].shape[1]
    B = p['conv_b'].shape[1]
    kw = p['conv_w_full'].shape[1]
    hd = A // A_HEADS
    dycat = _proj_rows_t(f"mix{l}_dycat", [(dxb, w_out)], F32)
    G['ab_w_out'] = _grad_rows(f"mix{l}_dwout", ycat, [dxb])[0]
    consts = _even_consts(p, e, A, B, kw)
    accs = [(A_HEADS, GMLP_BLOCK, GMLP_BLOCK), (GMLP_BLOCK, A), (1, A), (1, A), (HALO, B), (1, B), (1, B), (1, B),
            (1, D)]
    dza, dconv, dws, dbs, dglg, dglb, dcw, dcb, dclg, dclb, dbo = _rowwise(
        f"mix{l}_mid_bwd1", _mix_bwd1_tile(A, B, kw, tile),
        [('row', z), ('prev', z), ('row', dycat), ('row', dxo)] + consts, [(2 * A, F32), (B, F32)], accs, tile)
    dz, dbin = _rowwise(f"mix{l}_mid_bwd2", _mix_bwd2_tile(A, B, kw, tile, T // tile),
                        [('row', z), ('row', dza), ('row', dconv), ('next', dconv), consts[4]],
                        [(Z, BF16)], [(1, Z)], tile)
    tmd = _pick(D, TM_BIG)

    def dwin_fn(refs, outs):
        outs[0][...] = _dot(refs[0][...], refs[1][...], 'tn').astype(BF16)

    G['ab_w_in'] = _mm1(f"mix{l}_dwin", (N_CHIPS, D // tmd),
                        [(h, _bs((T, tmd), lambda j, i: (0, i))), (dz, _bs((T, zs), lambda j, i: (0, j)))],
                        [(_sds(w_in.shape, BF16), _bs((None, tmd, zs), lambda j, i: (j, i, 0)))], dwin_fn)[0]
    tm, tn = _pick(T, TM_MID), _pick(D, TN_SMALL)

    def dh_fn(refs, outs):
        acc = None
        for j in range(N_CHIPS):
            t = _dot(refs[0][:, j * zs:(j + 1) * zs], refs[1][j], 'nt')
            acc = t if acc is None else acc + t
        outs[0][...] = acc

    dh = _mm1(f"mix{l}_dh", (T // tm, D // tn),
              [(dz, _bs((tm, Z), lambda i, j: (i, 0))), (w_in, _bs((N_CHIPS, tn, zs), lambda i, j: (0, j, 0)))],
              [(_sds((T, D), F32), _bs((tm, tn), lambda i, j: (i, j)))], dh_fn)[0]
    dx, dxb2, dgm = _norm_bwd(f"mix{l}_norm_bwd", x, gm, dh, dxo, tile)
    small = {'ab_b_in': dbin.reshape(Z), 'gmlp_w_s': dws, 'gmlp_b_s': dbs.reshape(GMLP_BLOCK, A_HEADS, hd).sum(-1).T,
             'gmlp_ln_g': dglg.reshape(A), 'gmlp_ln_b': dglb.reshape(A), 'conv_w': dcw[:kw], 'conv_b': dcb.reshape(B),
             'conv_ln_g': dclg.reshape(B), 'conv_ln_b': dclb.reshape(B), 'ab_b_out': dbo.reshape(D)}
    return dx, dxb2, dgm, small


def _pool_counts(t, cg):
    return jnp.concatenate([jnp.broadcast_to(jnp.minimum(t + 1, w).astype(F32), (t.shape[0], cg))
                            for w in POOL_WINDOWS], axis=1)


def _window_sums(cat, cg, back):
    n = cat.shape[0]
    outs = []
    for gi, w in enumerate(POOL_WINDOWS):
        s = cat[:, gi * cg:(gi + 1) * cg]
        step = 1
        while step < w:
            s = s + pltpu.roll(s, step if back else n - step, 0)
            step *= 2
        outs.append(s)
    return jnp.concatenate(outs, axis=1)


def _pool_fwd_tile(D, tile):
    cg = D // len(POOL_WINDOWS)

    def fn(i, x, xp, g):
        h = _rms(x, g)
        hp = jnp.where(i > 0, _rms(xp, g), 0.0)
        sums = _window_sums(jnp.concatenate([hp, h], axis=0), cg, True)[HALO:]
        return [sums / _pool_counts(_row_ids(i, tile, tile), cg) - h], []
    return fn


def _pool_bwd_tile(D, tile, n_tiles):
    cg = D // len(POOL_WINDOWS)

    def fn(i, dd, ddn, x, dxo, g):
        e = dd / _pool_counts(_row_ids(i, tile, tile), cg)
        en = jnp.where(i < n_tiles - 1, ddn / _pool_counts(_row_ids(i + 1, tile, HALO), cg), 0.0)
        dh = _window_sums(jnp.concatenate([e, en], axis=0), cg, False)[:tile] - dd
        _, vjp = jax.vjp(_rms, x, g)
        dx, dg = vjp(dh)
        return [dxo + dx, dxo + dx], [dg]
    return fn


def _odd_fwd(l, o, x, gm, p, W, tile):
    T, D = x.shape
    wc = W['pool_w']
    cg = wc.shape[2]
    cs = cg // N_CHIPS
    ng = len(POOL_WINDOWS)
    tm = _pick(T, TM)
    d = _rowwise(f"mix{l}_pool", _pool_fwd_tile(D, tile), [('row', x), ('prev', x), ('const', gm.reshape(1, D))],
                 [(D, BF16)], [], tile)[0]
    gspec = _bs((tm, cg), lambda i, j, k: (i, j))
    vspec = _bs((1, cg), lambda i, j, k: (0, j))

    def epi(accs, ex):
        pre = accs[0] + ex[0]
        return [ex[2] + pre * ex[1], pre]

    xo, pre = _mm(f"mix{l}_poolmm", (T // tm, ng, N_CHIPS),
                  [(d, _bs((tm, cs), lambda i, j, k: (i, j * N_CHIPS + k))),
                   (wc, _bs((None, cs, cg), lambda i, j, k: (k, j, 0))),
                   (p['pool_b_full'][o].reshape(1, D), vspec), (p['pool_scale_full'][o].reshape(1, D), vspec),
                   (x, gspec)],
                  [(0, 1, 'nn', 0)], [(_sds((T, D), F32), gspec)] * 2, [(tm, cg)], epi, extras=(2, 3, 4))
    return xo, (x, d, pre)


def _odd_bwd(l, o, dxo, dxb, saved, gm, p, W, G, tile):
    x, d, pre = saved
    T, D = x.shape
    wc = W['pool_w']
    cg = wc.shape[2]
    cs = cg // N_CHIPS
    ng = len(POOL_WINDOWS)
    tm, tkt = _pick(T, TM), _pick(T, TK)

    def fn(i, dxv, prev, sc):
        return [dxv * sc], [jnp.sum(dxv * prev, axis=0, keepdims=True), jnp.sum(dxv * sc, axis=0, keepdims=True)]

    do, dscale, dbc = _rowwise(f"mix{l}_pool_bwd1", fn,
                               [('row', dxo), ('row', pre), ('const', p['pool_scale_full'][o].reshape(1, D))],
                               [(D, BF16)], [(1, D), (1, D)], tile)
    nb = ng * N_CHIPS
    dd = _mm(f"mix{l}_pool_dd", (T // tm, nb, 1),
             [(do, _bs((tm, cg), lambda i, j, k: (i, j // N_CHIPS))),
              (wc, _bs((None, cs, cg), lambda i, j, k: (j % N_CHIPS, j // N_CHIPS, 0)))],
             [(0, 1, 'nt', 0)], [(_sds((T, D), F32), _bs((tm, cs), lambda i, j, k: (i, j)))], [(tm, cs)],
             lambda a, _: a)[0]
    G['pool_w'] = _mm(f"mix{l}_pool_dw", (nb, 1, T // tkt),
                      [(d, _bs((tkt, cs), lambda i, j, k: (k, i))), (do, _bs((tkt, cg), lambda i, j, k: (k, i // N_CHIPS)))],
                      [(0, 1, 'tn', 0)],
                      [(_sds(wc.shape, BF16), _bs((None, cs, cg), lambda i, j, k: (i % N_CHIPS, i // N_CHIPS, 0)))],
                      [(cs, cg)], lambda a, _: a)[0]
    dx, dxb2, dgm = _rowwise(f"mix{l}_pool_bwd2", _pool_bwd_tile(D, tile, T // tile),
                             [('row', dd), ('next', dd), ('row', x), ('row', dxo), ('const', gm.reshape(1, D))],
                             [(D, F32), (D, BF16)], [(1, D)], tile)
    small = {'pool_b': dbc.reshape(ng, cg), 'pool_scale': dscale.reshape(D)}
    return dx, dxb2, dgm, small


def _final(x, g, tgt, tile):
    T, D = x.shape

    def fn(i, xv, tv, gv):
        y, vjp = jax.vjp(_rms, xv, gv)
        err = y - tv
        dx, dg = vjp(err / D)
        loss = 0.5 * jnp.sum(jnp.mean(err * err, axis=-1, keepdims=True), axis=0, keepdims=True)
        return [dx, dx], [dg, jnp.broadcast_to(loss, (1, LANES))]

    dx, dxb, dg, loss = _rowwise("final", fn, [('row', x), ('row', tgt), ('const', g.reshape(1, D))],
                                 [(D, F32), (D, BF16)], [(1, D), (1, LANES)], tile)
    return dx, dxb, dg.reshape(D), loss[0, 0]


def _as3d(name, w):
    return w.reshape(w.shape[0], -1, w.shape[-1]) if name == 'pool_w' else w


def kernel(x, mem, norm_ffn1, ffn1_gate, ffn1_up, ffn1_down, norm_mix, ab_w_in, ab_b_in, gmlp_w_s, gmlp_b_s, gmlp_ln_g, gmlp_ln_b, conv_w, conv_b, conv_ln_g, conv_ln_b, ab_w_out, ab_b_out, pool_w, pool_b, pool_scale, norm_xq, norm_xkv, xattn_wq, xattn_wk, xattn_wv, xattn_wo, norm_ffn2, ffn2_gate, ffn2_up, ffn2_down, norm_final, loss_target, m_norm_ffn1, m_ffn1_gate, m_ffn1_up, m_ffn1_down, m_norm_mix, m_ab_w_in, m_ab_b_in, m_gmlp_w_s, m_gmlp_b_s, m_gmlp_ln_g, m_gmlp_ln_b, m_conv_w, m_conv_b, m_conv_ln_g, m_conv_ln_b, m_ab_w_out, m_ab_b_out, m_pool_w, m_pool_b, m_pool_scale, m_norm_xq, m_norm_xkv, m_xattn_wq, m_xattn_wk, m_xattn_wv, m_xattn_wo, m_norm_ffn2, m_ffn2_gate, m_ffn2_up, m_ffn2_down, m_norm_final, v_norm_ffn1, v_ffn1_gate, v_ffn1_up, v_ffn1_down, v_norm_mix, v_ab_w_in, v_ab_b_in, v_gmlp_w_s, v_gmlp_b_s, v_gmlp_ln_g, v_gmlp_ln_b, v_conv_w, v_conv_b, v_conv_ln_g, v_conv_ln_b, v_ab_w_out, v_ab_b_out, v_pool_w, v_pool_b, v_pool_scale, v_norm_xq, v_norm_xkv, v_xattn_wq, v_xattn_wk, v_xattn_wv, v_xattn_wo, v_norm_ffn2, v_ffn2_gate, v_ffn2_up, v_ffn2_down, v_norm_final):
    w = dict(zip(WEIGHTS, [norm_ffn1, ffn1_gate, ffn1_up, ffn1_down, norm_mix, ab_w_in, ab_b_in, gmlp_w_s, gmlp_b_s, gmlp_ln_g, gmlp_ln_b, conv_w, conv_b, conv_ln_g, conv_ln_b, ab_w_out, ab_b_out, pool_w, pool_b, pool_scale, norm_xq, norm_xkv, xattn_wq, xattn_wk, xattn_wv, xattn_wo, norm_ffn2, ffn2_gate, ffn2_up, ffn2_down, norm_final]))
    m = dict(zip(WEIGHTS, [m_norm_ffn1, m_ffn1_gate, m_ffn1_up, m_ffn1_down, m_norm_mix, m_ab_w_in, m_ab_b_in, m_gmlp_w_s, m_gmlp_b_s, m_gmlp_ln_g, m_gmlp_ln_b, m_conv_w, m_conv_b, m_conv_ln_g, m_conv_ln_b, m_ab_w_out, m_ab_b_out, m_pool_w, m_pool_b, m_pool_scale, m_norm_xq, m_norm_xkv, m_xattn_wq, m_xattn_wk, m_xattn_wv, m_xattn_wo, m_norm_ffn2, m_ffn2_gate, m_ffn2_up, m_ffn2_down, m_norm_final]))
    v = dict(zip(WEIGHTS, [v_norm_ffn1, v_ffn1_gate, v_ffn1_up, v_ffn1_down, v_norm_mix, v_ab_w_in, v_ab_b_in, v_gmlp_w_s, v_gmlp_b_s, v_gmlp_ln_g, v_gmlp_ln_b, v_conv_w, v_conv_b, v_conv_ln_g, v_conv_ln_b, v_ab_w_out, v_ab_b_out, v_pool_w, v_pool_b, v_pool_scale, v_norm_xq, v_norm_xkv, v_xattn_wq, v_xattn_wk, v_xattn_wv, v_xattn_wo, v_norm_ffn2, v_ffn2_gate, v_ffn2_up, v_ffn2_down, v_norm_final]))

    xs, mems, tgt = x[0], mem[0], loss_target[0]
    T, D = xs.shape
    L = norm_ffn1.shape[0]
    tile = _pick(T, ROW_TILE)
    cx, cy, cc = _mesh_pos()
    chip = 2 * cx + cy
    w3 = {n: _as3d(n, w[n]) for n in BIG}
    names = [_layer_names(l) for l in range(L)]

    sh_shapes = [w[n].shape for n in SMALL_SHARDED]
    slots = _gather_all("gather_small_shards", _pack([w[n] for n in SMALL_SHARDED]), jnp.zeros((8, LANES), F32))
    per_chip = [_unpack(slots[2 * j], sh_shapes) for j in range(N_CHIPS)]
    full = {n: jnp.concatenate([per_chip[j][k] for j in range(N_CHIPS)], axis=-1) for k, n in enumerate(SMALL_SHARDED)}
    p = dict(w)
    p['conv_w_full'] = full['conv_w'].reshape(full['conv_w'].shape[0], full['conv_w'].shape[1], -1)
    p['pool_b_full'] = full['pool_b']
    p['pool_scale_full'] = full['pool_scale']

    first_ffn = [n for n in names[0] if n.startswith('ffn1')]
    units = {'0a': (0, first_ffn), '0b': (0, [n for n in names[0] if n not in first_ffn])}
    units.update({str(l): (l, names[l]) for l in range(1, L)})
    cast, near, far = {}, {}, {}

    def cast_unit(u, after):
        l, ns = units[u]
        cast[u] = [_cast_slab(f"cast_{n}_{l}", w3[n], _stack_index(n, l), chip, after) for n in ns]

    def start_near(u, after, carry=()):
        slabs = cast.pop(u)
        sends, arrivals, slabs, carry, tok = _split_start(f"gather_near_start_{u}", slabs, after, 2 * len(slabs),
                                                          _gather_near_copies, carry)
        near[u] = (sends, arrivals, slabs)
        return carry, tok

    def start_far(u, after, carry=()):
        sends, arrivals, slabs = near.pop(u)
        slabs = _split_wait(f"gather_near_wait_{u}", slabs, sends, arrivals, after, _gather_near_copies)
        sends, arrivals, slabs, carry, tok = _split_start(f"gather_far_start_{u}", slabs, jnp.zeros((8, LANES), F32),
                                                          2 * len(slabs), _gather_far_copies, carry)
        d2d = _split_start(f"gather_fwd_near_start_{u}", slabs, jnp.zeros((8, LANES), F32), 2 * len(slabs),
                           _forward_copies(_near_slabs), carry)
        far[u] = (sends, arrivals, d2d[0], d2d[1], d2d[2])
        return d2d[3], tok

    def finish_gather(u, after):
        sends, arrivals, d2d_sends, d2d_arrivals, slabs = far.pop(u)
        slabs = _split_wait(f"gather_far_wait_{u}", slabs, sends, arrivals, after, _gather_far_copies)
        slabs = _split_wait(f"gather_fwd_near_wait_{u}", slabs, d2d_sends, d2d_arrivals, after,
                            _forward_copies(_near_slabs))
        return dict(zip(units[u][1], _forward_halves(f"gather_fwd_{u}", slabs, _far_slabs, 1)))

    cast_unit('0a', slots)
    _, tok = start_near('0a', slots)
    for u in units:
        if u != '0a':
            cast_unit(u, tok)
    casts_done = jnp.stack([s[chip, 0, 0] for u in cast for s in cast[u]]).astype(F32)
    _, tok = start_far('0a', casts_done)
    _, tok = start_near('0b', tok)

    saved, Wl = [], []
    xc = xs + tok[0, 0]
    W = finish_gather('0a', xc)
    for l in range(L):
        Wl.append(W)
        s = {}
        xc, s['ffn1'] = _ffn_fwd(f"ffn1_{l}", xc, w['norm_ffn1'][l], W['ffn1_gate'], W['ffn1_up'], W['ffn1_down'], tile)
        if l == 0:
            (xc,), tok = start_far('0b', xc, (xc,))
            if L > 1:
                (xc,), tok = start_near('1', tok, (xc,))
            W.update(finish_gather('0b', xc))
        if l % 2 == 0:
            xc, s['mix'] = _even_fwd(l, l // 2, xc, w['norm_mix'][l], p, W, tile)
        else:
            xc, s['mix'] = _odd_fwd(l, l // 2, xc, w['norm_mix'][l], p, W, tile)
        if l + 1 < L:
            (xc,), tok = start_far(str(l + 1), xc, (xc,))
            if l + 2 < L:
                (xc,), tok = start_near(str(l + 2), tok, (xc,))
        xc, s['xa'] = _attn_fwd(l, xc, mems, w['norm_xq'][l], w['norm_xkv'][l], W, tile)
        xc, s['ffn2'] = _ffn_fwd(f"ffn2_{l}", xc, w['norm_ffn2'][l], W['ffn2_gate'], W['ffn2_up'], W['ffn2_down'], tile)
        saved.append(s)
        if l + 1 < L:
            W = finish_gather(str(l + 1), xc)

    dx, dxb, g_final, loss_local = _final(xc, w['norm_final'], tgt, tile)
    loss = lax.psum(loss_local, ("x", "y", "c"))
    gfull = {n: lax.empty(w3[n].shape, F32) for n in BIG}
    gs = {n: [None] * w[n].shape[0] for n in SMALL if n != 'norm_final'}

    def finish_exchange(pending, after):
        tag, l, ns, sends, arrivals, thru = pending
        thru = _split_wait(f"rs_wait_{tag}", thru, sends, arrivals, after, _exchange_copies)
        parts, lands = thru[:len(ns)], thru[len(ns):]
        for n, part, land in zip(ns, parts, lands):
            gfull[n] = _sum_into(f"rs_sum_{n}_{l}", part, land, gfull[n], _stack_index(n, l), chip, cc)
        idx = [_stack_index(n, l) for n in ns]
        sends, arrivals, thru, _, _ = _split_start(f"rs_join_start_{tag}", [gfull[n] for n in ns],
                                                   jnp.zeros((8, LANES), F32), len(ns), _join_copies(idx))
        gfull.update(zip(ns, thru))
        joins.append((tag, ns, idx, sends, arrivals))

    def wait_joins(after, limit=None):
        for _ in range(len(joins) if limit is None else limit):
            tag, ns, idx, sends, arrivals = joins.pop(0)
            thru = _split_wait(f"rs_join_wait_{tag}", [gfull[n] for n in ns], sends, arrivals, after, _join_copies(idx))
            gfull.update(zip(ns, thru))

    def swap_hook(key):
        def hook(grads, carry):
            lands = [lax.empty((g.shape[0], g.shape[1] // 2, g.shape[2]), BF16) for g in grads]
            sends, arrivals, thru, carry, _ = _split_start(f"rs_swap_start_{key}", list(grads) + lands,
                                                           jnp.zeros((8, LANES), F32), len(grads), _swap_copies, carry)
            swaps[key] = (sends, arrivals, thru[len(grads):])
            return thru[:len(grads)], carry
        return hook

    def start_exchange(tag, l, ns, G, pending, dx, dxb, keep=EXCHANGES_IN_FLIGHT):
        while len(pending) >= max(keep, 1):
            finish_exchange(pending.pop(0), dx)
        ffn = [n for n in ns if n.startswith('ffn')]
        rest = [n for n in ns if n not in ffn]
        key = f"{ffn[0][:4]}_{l}"
        sends, arrivals, lands = swaps.pop(key)
        thru = _split_wait(f"rs_swap_wait_{key}", [G[n] for n in ffn] + lands, sends, arrivals, dx, _swap_copies)
        got = dict(zip(ffn, thru[len(ffn):]))
        G.update(zip(ffn, thru[:len(ffn)]))
        got.update(zip(rest, _swap_halves(f"rs_swap_{tag}", [G[n] for n in rest])))
        grads_g = [G[n] for n in ns]
        parts = [_add_halves(f"rs_add_{n}_{l}", G[n], got[n], cc) for n in ns]
        lands = [lax.empty((3,) + part.shape[1:], BF16) for part in parts]
        sends, arrivals, thru, (dx, dxb), tok = _split_start(
            f"rs_start_{tag}", parts + lands, jnp.zeros((8, LANES), F32), 3 * len(parts), _exchange_copies,
            carry=(dx, dxb))
        pending.append((tag, l, ns, sends, arrivals, thru))
        return dx, dxb, tok

    pending, joins, swaps = [], [], {}
    for l in reversed(range(L)):
        first = [n for n in names[l] if n.startswith(('ffn2', 'xattn'))]
        second = [n for n in names[l] if n not in first]
        s, W, G = saved[l], Wl[l], {}
        dx, dxb, dg, G['ffn2_gate'], G['ffn2_up'], G['ffn2_down'] = _ffn_bwd(
            f"ffn2_{l}", dx, dxb, s['ffn2'], w['norm_ffn2'][l], W['ffn2_gate'], W['ffn2_up'], W['ffn2_down'], tile,
            swap_hook(f"ffn2_{l}"))
        gs['norm_ffn2'][l] = dg.reshape(D)
        dx, dxb, dgq, dgkv = _attn_bwd(l, dx, dxb, s['xa'], mems, w['norm_xq'][l], w['norm_xkv'][l], W, G, tile)
        gs['norm_xq'][l], gs['norm_xkv'][l] = dgq.reshape(D), dgkv.reshape(D)
        dx, dxb, _ = start_exchange(f"a{l}", l, first, G, pending, dx, dxb,
                                    keep=EXCHANGES_IN_FLIGHT if l > 0 else 1)
        if l % 2 == 0:
            dx, dxb, dgm, small = _even_bwd(l, l // 2, dx, dxb, s['mix'], w['norm_mix'][l], p, W, G, tile)
        else:
            dx, dxb, dgm, small = _odd_bwd(l, l // 2, dx, dxb, s['mix'], w['norm_mix'][l], p, W, G, tile)
        for n, val in small.items():
            gs[n][l // 2] = val
        gs['norm_mix'][l] = dgm.reshape(D)
        dx, dxb, dg, G['ffn1_gate'], G['ffn1_up'], G['ffn1_down'] = _ffn_bwd(
            f"ffn1_{l}", dx, dxb, s['ffn1'], w['norm_ffn1'][l], W['ffn1_gate'], W['ffn1_up'], W['ffn1_down'], tile,
            swap_hook(f"ffn1_{l}"))
        gs['norm_ffn1'][l] = dg.reshape(D)
        dx, dxb, tok = start_exchange(f"b{l}", l, second, G, pending, dx, dxb)
    grad_x = dx[None]
    whole = pending[0][2] if len(pending) > 1 else []
    if whole:
        finish_exchange(pending.pop(0), tok)
    wait_joins(tok, len(joins) - (1 if whole else 0))

    small_full = {n: jnp.stack(gs[n]) for n in gs}
    small_full['norm_final'] = g_final
    full_shapes = [small_full[n].shape for n in SMALL]
    packed = _pack([small_full[n] for n in SMALL])
    sg_sends, sg_arrivals, sg_thru, _, _ = _split_start(
        "small_grads_start", [packed, lax.empty((N_DEV,) + packed.shape, F32)], tok, N_DEV - 1, _all_copies)

    def flat2(n, t):
        t3 = _as3d(n, t)
        return t3.reshape(-1, t3.shape[-1])

    early, dep = {}, []
    for n in BIG:
        if n in whole:
            continue
        R = w3[n].shape[1]
        lo = 0 if n in ODD_ONLY else R
        early[n] = _adam_rows(f"adam_early_{n}", flat2(n, w[n]), flat2(n, gfull[n]), flat2(n, m[n]), flat2(n, v[n]),
                              lo, w3[n].shape[0] * R, R, after=tok)
        dep.append(early[n][1][-1, 0])
    if whole:
        wait_joins(jnp.stack(dep), 1)
    for n in whole:
        R = w3[n].shape[1]
        early[n] = _adam_rows(f"adam_whole_{n}", flat2(n, w[n]), flat2(n, gfull[n]), flat2(n, m[n]), flat2(n, v[n]),
                              0, w3[n].shape[0] * R, R, after=tok)
        dep.append(early[n][1][-1, 0])
    dep = jnp.stack(dep)

    packed, slots8 = _split_wait("small_grads_wait", sg_thru, sg_sends, sg_arrivals, dep, _all_copies)
    summed = _sum_slots("sum_small", slots8, packed, 4 * cx + 2 * cy + cc)
    g_small = dict(zip(SMALL, _unpack(summed, full_shapes)))
    for n in SMALL_SHARDED:
        width = w[n].shape[-1]
        g_small[n] = lax.dynamic_slice_in_dim(g_small[n], chip * width, width, axis=g_small[n].ndim - 1).reshape(w[n].shape)

    for group in pending:
        finish_exchange(group, summed)
    grads, delta, new_m, new_v = {}, {}, {}, {}
    late, after = dict(early), summed
    while joins:
        ns = joins[0][1]
        wait_joins(after, 1)
        for n in ns:
            if n in ODD_ONLY:
                continue
            R = w3[n].shape[1]
            late[n] = _adam_rows(f"adam_late_{n}", flat2(n, w[n]), flat2(n, gfull[n]), flat2(n, m[n]), flat2(n, v[n]),
                                 0, R, R, prev=early[n])
        after = late[ns[-1]][1]
    for n in BIG:
        grads[n], delta[n], new_m[n], new_v[n] = (t.reshape(w[n].shape) for t in late[n])
    small_shapes = [w[n].shape for n in SMALL]
    d2, m2, v2 = _adam("adam_small", _pack([w[n] for n in SMALL]), _pack([g_small[n] for n in SMALL]),
                       _pack([m[n] for n in SMALL]), _pack([v[n] for n in SMALL]))
    for n, dn, mn_, vn_ in zip(SMALL, _unpack(d2, small_shapes), _unpack(m2, small_shapes), _unpack(v2, small_shapes)):
        grads[n], delta[n], new_m[n], new_v[n] = g_small[n].reshape(w[n].shape), dn, mn_, vn_

    return (loss, grad_x, *[grads[n] for n in WEIGHTS], *[delta[n] for n in WEIGHTS],
            *[new_m[n] for n in WEIGHTS], *[new_v[n] for n in WEIGHTS])
```
